```python
import math
import jax, jax.numpy as jnp
from jax import lax
import numpy as np

D_MODEL = 2048
BATCH = 8
SEQ = 8192
DEPTH = 2

CHUNK = 64
Q_BLOCK = 128
EPS = 1e-6

MLA_HEADS = 16
MLA_Q_RANK = 512
MLA_KV_RANK = 512
MLA_NOPE = 128
MLA_ROPE = 64
MLA_V = 128
ROPE_THETA = 10000.0

HG_HEADS = 16
HG_DK = 128
HG_DV = 128
HG_WIDTH = HG_HEADS * HG_DK

SSM_EXPAND = 2
SSM_INNER = SSM_EXPAND * D_MODEL
SSM_HEADDIM = 64
SSM_HEADS = SSM_INNER // SSM_HEADDIM
SSM_GROUPS = 8
SSM_STATE = 128
SSM_CONV = 4
SSM_CONV_DIM = SSM_INNER + 2 * SSM_GROUPS * SSM_STATE

D_FF = 5632
N_BRANCH = 3

IN_SIZES = (MLA_Q_RANK, MLA_KV_RANK, MLA_ROPE,
            HG_WIDTH, HG_WIDTH, HG_WIDTH, HG_WIDTH,
            SSM_INNER, SSM_CONV_DIM, SSM_HEADS,
            N_BRANCH * D_MODEL)
IN_DIM = int(sum(IN_SIZES))
IN_SPLITS = tuple(int(v) for v in np.cumsum(IN_SIZES)[:-1])

kernel_name = "hybrid_mla_hgrn2_mamba2_macaron"


def rmsnorm(x, w):
    xf = x.astype(jnp.float32)
    y = xf * lax.rsqrt(jnp.mean(xf * xf, axis=-1, keepdims=True) + EPS)
    return (y * w.astype(jnp.float32)).astype(x.dtype)


def swiglu(x, w_gate_up, w_down):
    g, u = jnp.split(x @ w_gate_up, 2, axis=-1)
    return (jax.nn.silu(g) * u) @ w_down


def rope_tables(seq, dim):
    inv = 1.0 / (ROPE_THETA ** (jnp.arange(0, dim, 2, dtype=jnp.float32) / dim))
    ang = jnp.arange(seq, dtype=jnp.float32)[:, None] * inv[None, :]
    return jnp.cos(ang), jnp.sin(ang)


def apply_rope(x, cos, sin):
    x1, x2 = jnp.split(x.astype(jnp.float32), 2, axis=-1)
    return jnp.concatenate([x1 * cos - x2 * sin, x1 * sin + x2 * cos], axis=-1).astype(x.dtype)


def tril_mask():
    return jnp.tril(jnp.ones((CHUNK, CHUNK), dtype=bool))


def mla_branch(q_lat, kv_lat, k_pe, q_norm_w, w_uq, kv_norm_w, w_ukv, cos, sin):
    B, S, _ = q_lat.shape
    q = (rmsnorm(q_lat, q_norm_w) @ w_uq).reshape(B, S, MLA_HEADS, MLA_NOPE + MLA_ROPE)
    q_nope = q[..., :MLA_NOPE]
    q_pe = apply_rope(q[..., MLA_NOPE:], cos[:, None, :], sin[:, None, :])
    kv = (rmsnorm(kv_lat, kv_norm_w) @ w_ukv).reshape(B, S, MLA_HEADS, MLA_NOPE + MLA_V)
    k_nope, v = kv[..., :MLA_NOPE], kv[..., MLA_NOPE:]
    k_rot = apply_rope(k_pe, cos, sin)
    scale = (MLA_NOPE + MLA_ROPE) ** -0.5
    n_blk = S // Q_BLOCK
    qn_b = q_nope.reshape(B, n_blk, Q_BLOCK, MLA_HEADS, MLA_NOPE).transpose(1, 0, 2, 3, 4)
    qp_b = q_pe.reshape(B, n_blk, Q_BLOCK, MLA_HEADS, MLA_ROPE).transpose(1, 0, 2, 3, 4)
    key_chunk = jnp.arange(S) // CHUNK

    def block(args):
        i, qn, qp = args
        s = (jnp.einsum('bqhd,bkhd->bhqk', qn, k_nope, preferred_element_type=jnp.float32)
             + jnp.einsum('bqhd,bkd->bhqk', qp, k_rot, preferred_element_type=jnp.float32)) * scale
        q_chunk = (i * Q_BLOCK + jnp.arange(Q_BLOCK)) // CHUNK
        mask = key_chunk[None, :] <= q_chunk[:, None]
        p = jax.nn.softmax(jnp.where(mask, s, -jnp.inf), axis=-1).astype(v.dtype)
        return jnp.einsum('bhqk,bkhd->bqhd', p, v)

    o = lax.map(block, (jnp.arange(n_blk), qn_b, qp_b))
    return o.transpose(1, 0, 2, 3, 4).reshape(B, S, MLA_HEADS * MLA_V)


def hgrn2_branch(q_in, f_in, i_in, g_in, lb, norm_w):
    B, S, _ = q_in.shape
    nc = S // CHUNK
    f32 = jnp.float32
    tril = tril_mask()

    def chunks(t, d):
        return t.astype(f32).reshape(B, nc, CHUNK, HG_HEADS, d).transpose(1, 0, 3, 2, 4)

    f = lb + (1.0 - lb) * jax.nn.sigmoid(f_in.astype(f32))
    q = chunks(jax.nn.silu(q_in.astype(f32)) * HG_DK ** -0.5, HG_DK)
    k = chunks(1.0 - f, HG_DK)
    v = chunks(i_in, HG_DV)
    b = jnp.cumsum(chunks(jnp.log(f), HG_DK), axis=3)

    def step(state, inp):
        qc, kc, vc, bc = inp
        b_last = bc[:, :, -1:, :]
        o_inter = jnp.einsum('bhtk,bhkv->bhtv', qc * jnp.exp(bc), state)
        diff = jnp.where(tril[:, :, None], bc[:, :, :, None, :] - bc[:, :, None, :, :], -jnp.inf)
        att = jnp.einsum('bhtk,bhsk,bhtsk->bhts', qc, kc, jnp.exp(diff))
        o = o_inter + jnp.einsum('bhts,bhsv->bhtv', att, vc)
        state = (jnp.exp(b_last[:, :, 0, :, None]) * state
                 + jnp.einsum('bhsk,bhsv->bhkv', kc * jnp.exp(b_last - bc), vc))
        return state, o

    s0 = jnp.zeros((B, HG_HEADS, HG_DK, HG_DV), f32)
    _, o = lax.scan(step, s0, (q, k, v, b))
    o = o.transpose(1, 0, 3, 2, 4).reshape(B, S, HG_HEADS, HG_DV)
    o = o * lax.rsqrt(jnp.mean(o * o, axis=-1, keepdims=True) + EPS)
    o = o.reshape(B, S, HG_WIDTH) * norm_w.astype(f32) * jax.nn.silu(g_in.astype(f32))
    return o.astype(q_in.dtype)


def mamba2_branch(z, xbc, dt_raw, conv_w, conv_b, a_log, dt_bias, d_skip, norm_w):
    B, S, _ = xbc.shape
    nc = S // CHUNK
    hg = SSM_HEADS // SSM_GROUPS
    f32 = jnp.float32
    tril = tril_mask()
    xbc = lax.conv_general_dilated(xbc, conv_w[:, None, :].astype(xbc.dtype), window_strides=(1,),
                                   padding=[(SSM_CONV - 1, 0)], dimension_numbers=('NWC', 'WIO', 'NWC'),
                                   feature_group_count=SSM_CONV_DIM) + conv_b
    xbc = jax.nn.silu(xbc)
    xs, bm, cm = jnp.split(xbc, [SSM_INNER, SSM_INNER + SSM_GROUPS * SSM_STATE], axis=-1)
    xs = xs.astype(f32).reshape(B, nc, CHUNK, SSM_GROUPS, hg, SSM_HEADDIM)
    bm = bm.astype(f32).reshape(B, nc, CHUNK, SSM_GROUPS, SSM_STATE)
    cm = cm.astype(f32).reshape(B, nc, CHUNK, SSM_GROUPS, SSM_STATE)
    dt = jax.nn.softplus(dt_raw.astype(f32) + dt_bias.astype(f32)).reshape(B, nc, CHUNK, SSM_GROUPS, hg)
    a = -jnp.exp(a_log.astype(f32)).reshape(SSM_GROUPS, hg)
    a_cum = jnp.cumsum((dt * a).transpose(0, 3, 4, 1, 2), axis=-1)
    xdt = xs * dt[..., None]
    seg = a_cum[..., :, None] - a_cum[..., None, :]
    decay = jnp.exp(jnp.where(tril, seg, -jnp.inf))
    cb = jnp.einsum('bclgn,bcsgn->bgcls', cm, bm)
    y_diag = jnp.einsum('bgcls,bghcls,bcsghp->bclghp', cb, decay, xdt)
    decay_states = jnp.exp(a_cum[..., -1:] - a_cum)
    states = jnp.einsum('bcsgn,bghcs,bcsghp->cbghpn', bm, decay_states, xdt)
    chunk_decay = jnp.exp(a_cum[..., -1]).transpose(3, 0, 1, 2)

    def step(h, inp):
        st, dec = inp
        return dec[..., None, None] * h + st, h

    _, h_prev = lax.scan(step, jnp.zeros(states.shape[1:], f32), (states, chunk_decay))
    y_off = jnp.einsum('bclgn,cbghpn,bghcl->bclghp', cm, h_prev, jnp.exp(a_cum))
    y = y_diag + y_off + xs * d_skip.astype(f32).reshape(SSM_GROUPS, hg)[:, :, None]
    y = y.reshape(B, S, SSM_INNER) * jax.nn.silu(z.astype(f32))
    y = y.reshape(B, S, SSM_GROUPS, SSM_INNER // SSM_GROUPS)
    y = y * lax.rsqrt(jnp.mean(y * y, axis=-1, keepdims=True) + EPS)
    y = y.reshape(B, S, SSM_INNER) * norm_w.astype(f32)
    return y.astype(z.dtype)


def _fwd_setup_inputs(seed: int = 0) -> dict:
    key = jax.random.key(seed)
    ks = iter(jax.random.split(key, 32))
    L = DEPTH

    def nrm(shape, fan_in):
        return jax.random.normal(next(ks), shape, jnp.float32) * fan_in ** -0.5

    def gain(shape):
        return 1.0 + 0.01 * jax.random.normal(next(ks), shape, jnp.float32)

    x = jax.random.normal(next(ks), (BATCH, SEQ, D_MODEL), jnp.float32)
    ffn1_norm = gain((L, D_MODEL))
    ffn1_wi = nrm((L, D_MODEL, 2 * D_FF), D_MODEL)
    ffn1_wo = nrm((L, D_FF, D_MODEL), D_FF)
    mix_norm = gain((L, D_MODEL))
    w_in = nrm((L, D_MODEL, IN_DIM), D_MODEL)
    mla_q_norm = gain((L, MLA_Q_RANK))
    mla_w_uq = nrm((L, MLA_Q_RANK, MLA_HEADS * (MLA_NOPE + MLA_ROPE)), MLA_Q_RANK)
    mla_kv_norm = gain((L, MLA_KV_RANK))
    mla_w_ukv = nrm((L, MLA_KV_RANK, MLA_HEADS * (MLA_NOPE + MLA_V)), MLA_KV_RANK)
    hgrn_lb_logits = 0.5 * jax.random.normal(next(ks), (L, HG_WIDTH), jnp.float32)
    hgrn_norm = gain((L, HG_WIDTH))
    ssm_conv_w = nrm((L, SSM_CONV, SSM_CONV_DIM), SSM_CONV)
    ssm_conv_b = 0.01 * jax.random.normal(next(ks), (L, SSM_CONV_DIM), jnp.float32)
    ssm_a_log = jnp.log(jax.random.uniform(next(ks), (L, SSM_HEADS), jnp.float32, 1.0, 16.0))
    dt0 = jnp.exp(jax.random.uniform(next(ks), (L, SSM_HEADS), jnp.float32, math.log(1e-3), math.log(1e-1)))
    ssm_dt_bias = dt0 + jnp.log(-jnp.expm1(-dt0))
    ssm_d = gain((L, SSM_HEADS))
    ssm_norm = gain((L, SSM_INNER))
    w_o_mla = nrm((L, MLA_HEADS * MLA_V, D_MODEL), MLA_HEADS * MLA_V)
    w_o_hgrn = nrm((L, HG_WIDTH, D_MODEL), HG_WIDTH)
    w_o_ssm = nrm((L, SSM_INNER, D_MODEL), SSM_INNER)
    w_out = nrm((L, D_MODEL, D_MODEL), D_MODEL)
    ffn2_norm = gain((L, D_MODEL))
    ffn2_wi = nrm((L, D_MODEL, 2 * D_FF), D_MODEL)
    ffn2_wo = nrm((L, D_FF, D_MODEL), D_FF)
    final_norm = gain((D_MODEL,))
    return {"x": x, "ffn1_norm": ffn1_norm, "ffn1_wi": ffn1_wi, "ffn1_wo": ffn1_wo,
            "mix_norm": mix_norm, "w_in": w_in, "mla_q_norm": mla_q_norm, "mla_w_uq": mla_w_uq,
            "mla_kv_norm": mla_kv_norm, "mla_w_ukv": mla_w_ukv, "hgrn_lb_logits": hgrn_lb_logits,
            "hgrn_norm": hgrn_norm, "ssm_conv_w": ssm_conv_w, "ssm_conv_b": ssm_conv_b,
            "ssm_a_log": ssm_a_log, "ssm_dt_bias": ssm_dt_bias, "ssm_d": ssm_d, "ssm_norm": ssm_norm,
            "w_o_mla": w_o_mla, "w_o_hgrn": w_o_hgrn, "w_o_ssm": w_o_ssm, "w_out": w_out,
            "ffn2_norm": ffn2_norm, "ffn2_wi": ffn2_wi, "ffn2_wo": ffn2_wo, "final_norm": final_norm}


def _fwd_reference(x, ffn1_norm, ffn1_wi, ffn1_wo, mix_norm, w_in, mla_q_norm, mla_w_uq, mla_kv_norm,
              mla_w_ukv, hgrn_lb_logits, hgrn_norm, ssm_conv_w, ssm_conv_b, ssm_a_log, ssm_dt_bias,
              ssm_d, ssm_norm, w_o_mla, w_o_hgrn, w_o_ssm, w_out, ffn2_norm, ffn2_wi, ffn2_wo,
              final_norm):
    S = x.shape[1]
    cos, sin = rope_tables(S, MLA_ROPE)
    p = jax.nn.softmax(hgrn_lb_logits.astype(jnp.float32), axis=0)
    lower_bounds = jnp.cumsum(p, axis=0) - p[0:1]
    for l in range(DEPTH):
        x = x + 0.5 * swiglu(rmsnorm(x, ffn1_norm[l]), ffn1_wi[l], ffn1_wo[l])
        h = rmsnorm(x, mix_norm[l])
        (q_lat, kv_lat, k_pe, hq, hf, hi, hgate, z, xbc, dt_raw, gates) = jnp.split(h @ w_in[l], IN_SPLITS, axis=-1)
        y_a = mla_branch(q_lat, kv_lat, k_pe, mla_q_norm[l], mla_w_uq[l], mla_kv_norm[l], mla_w_ukv[l], cos, sin) @ w_o_mla[l]
        y_b = hgrn2_branch(hq, hf, hi, hgate, lower_bounds[l], hgrn_norm[l]) @ w_o_hgrn[l]
        y_c = mamba2_branch(z, xbc, dt_raw, ssm_conv_w[l], ssm_conv_b[l], ssm_a_log[l], ssm_dt_bias[l],
                            ssm_d[l], ssm_norm[l]) @ w_o_ssm[l]
        g_a, g_b, g_c = jnp.split(jax.nn.sigmoid(gates), N_BRANCH, axis=-1)
        x = x + (g_a * y_a + g_b * y_b + g_c * y_c) @ w_out[l]
        x = x + 0.5 * swiglu(rmsnorm(x, ffn2_norm[l]), ffn2_wi[l], ffn2_wo[l])
    return rmsnorm(x, final_norm)


import jax as _jax
import jax.numpy as _jnp

TWIN_FORMAT = 'train_step'
FWD_PARAMS = ['x', 'ffn1_norm', 'ffn1_wi', 'ffn1_wo', 'mix_norm', 'w_in', 'mla_q_norm', 'mla_w_uq', 'mla_kv_norm', 'mla_w_ukv', 'hgrn_lb_logits', 'hgrn_norm', 'ssm_conv_w', 'ssm_conv_b', 'ssm_a_log', 'ssm_dt_bias', 'ssm_d', 'ssm_norm', 'w_o_mla', 'w_o_hgrn', 'w_o_ssm', 'w_out', 'ffn2_norm', 'ffn2_wi', 'ffn2_wo', 'final_norm']
TWIN_WEIGHTS = ['ffn1_norm', 'ffn1_wi', 'ffn1_wo', 'mix_norm', 'w_in', 'mla_q_norm', 'mla_w_uq', 'mla_kv_norm', 'mla_w_ukv', 'hgrn_lb_logits', 'hgrn_norm', 'ssm_conv_w', 'ssm_conv_b', 'ssm_a_log', 'ssm_dt_bias', 'ssm_d', 'ssm_norm', 'w_o_mla', 'w_o_hgrn', 'w_o_ssm', 'w_out', 'ffn2_norm', 'ffn2_wi', 'ffn2_wo', 'final_norm']
TWIN_DIFF_INPUT = 'x'
TWIN_INPUTS = ['x', 'ffn1_norm', 'ffn1_wi', 'ffn1_wo', 'mix_norm', 'w_in', 'mla_q_norm', 'mla_w_uq', 'mla_kv_norm', 'mla_w_ukv', 'hgrn_lb_logits', 'hgrn_norm', 'ssm_conv_w', 'ssm_conv_b', 'ssm_a_log', 'ssm_dt_bias', 'ssm_d', 'ssm_norm', 'w_o_mla', 'w_o_hgrn', 'w_o_ssm', 'w_out', 'ffn2_norm', 'ffn2_wi', 'ffn2_wo', 'final_norm', 'loss_target', 'm_ffn1_norm', 'm_ffn1_wi', 'm_ffn1_wo', 'm_mix_norm', 'm_w_in', 'm_mla_q_norm', 'm_mla_w_uq', 'm_mla_kv_norm', 'm_mla_w_ukv', 'm_hgrn_lb_logits', 'm_hgrn_norm', 'm_ssm_conv_w', 'm_ssm_conv_b', 'm_ssm_a_log', 'm_ssm_dt_bias', 'm_ssm_d', 'm_ssm_norm', 'm_w_o_mla', 'm_w_o_hgrn', 'm_w_o_ssm', 'm_w_out', 'm_ffn2_norm', 'm_ffn2_wi', 'm_ffn2_wo', 'm_final_norm', 'v_ffn1_norm', 'v_ffn1_wi', 'v_ffn1_wo', 'v_mix_norm', 'v_w_in', 'v_mla_q_norm', 'v_mla_w_uq', 'v_mla_kv_norm', 'v_mla_w_ukv', 'v_hgrn_lb_logits', 'v_hgrn_norm', 'v_ssm_conv_w', 'v_ssm_conv_b', 'v_ssm_a_log', 'v_ssm_dt_bias', 'v_ssm_d', 'v_ssm_norm', 'v_w_o_mla', 'v_w_o_hgrn', 'v_w_o_ssm', 'v_w_out', 'v_ffn2_norm', 'v_ffn2_wi', 'v_ffn2_wo', 'v_final_norm']
TWIN_OUTPUTS = ['loss', 'grad_x', 'grad_ffn1_norm', 'grad_ffn1_wi', 'grad_ffn1_wo', 'grad_mix_norm', 'grad_w_in', 'grad_mla_q_norm', 'grad_mla_w_uq', 'grad_mla_kv_norm', 'grad_mla_w_ukv', 'grad_hgrn_lb_logits', 'grad_hgrn_norm', 'grad_ssm_conv_w', 'grad_ssm_conv_b', 'grad_ssm_a_log', 'grad_ssm_dt_bias', 'grad_ssm_d', 'grad_ssm_norm', 'grad_w_o_mla', 'grad_w_o_hgrn', 'grad_w_o_ssm', 'grad_w_out', 'grad_ffn2_norm', 'grad_ffn2_wi', 'grad_ffn2_wo', 'grad_final_norm', 'delta_ffn1_norm', 'delta_ffn1_wi', 'delta_ffn1_wo', 'delta_mix_norm', 'delta_w_in', 'delta_mla_q_norm', 'delta_mla_w_uq', 'delta_mla_kv_norm', 'delta_mla_w_ukv', 'delta_hgrn_lb_logits', 'delta_hgrn_norm', 'delta_ssm_conv_w', 'delta_ssm_conv_b', 'delta_ssm_a_log', 'delta_ssm_dt_bias', 'delta_ssm_d', 'delta_ssm_norm', 'delta_w_o_mla', 'delta_w_o_hgrn', 'delta_w_o_ssm', 'delta_w_out', 'delta_ffn2_norm', 'delta_ffn2_wi', 'delta_ffn2_wo', 'delta_final_norm', 'new_m_ffn1_norm', 'new_m_ffn1_wi', 'new_m_ffn1_wo', 'new_m_mix_norm', 'new_m_w_in', 'new_m_mla_q_norm', 'new_m_mla_w_uq', 'new_m_mla_kv_norm', 'new_m_mla_w_ukv', 'new_m_hgrn_lb_logits', 'new_m_hgrn_norm', 'new_m_ssm_conv_w', 'new_m_ssm_conv_b', 'new_m_ssm_a_log', 'new_m_ssm_dt_bias', 'new_m_ssm_d', 'new_m_ssm_norm', 'new_m_w_o_mla', 'new_m_w_o_hgrn', 'new_m_w_o_ssm', 'new_m_w_out', 'new_m_ffn2_norm', 'new_m_ffn2_wi', 'new_m_ffn2_wo', 'new_m_final_norm', 'new_v_ffn1_norm', 'new_v_ffn1_wi', 'new_v_ffn1_wo', 'new_v_mix_norm', 'new_v_w_in', 'new_v_mla_q_norm', 'new_v_mla_w_uq', 'new_v_mla_kv_norm', 'new_v_mla_w_ukv', 'new_v_hgrn_lb_logits', 'new_v_hgrn_norm', 'new_v_ssm_conv_w', 'new_v_ssm_conv_b', 'new_v_ssm_a_log', 'new_v_ssm_dt_bias', 'new_v_ssm_d', 'new_v_ssm_norm', 'new_v_w_o_mla', 'new_v_w_o_hgrn', 'new_v_w_o_ssm', 'new_v_w_out', 'new_v_ffn2_norm', 'new_v_ffn2_wi', 'new_v_ffn2_wo', 'new_v_final_norm']
TWIN_LEAF_KINDS = {'loss': 'loss', 'grad_x': 'grad_x', 'grad_ffn1_norm': 'grad_w', 'grad_ffn1_wi': 'grad_w', 'grad_ffn1_wo': 'grad_w', 'grad_mix_norm': 'grad_w', 'grad_w_in': 'grad_w', 'grad_mla_q_norm': 'grad_w', 'grad_mla_w_uq': 'grad_w', 'grad_mla_kv_norm': 'grad_w', 'grad_mla_w_ukv': 'grad_w', 'grad_hgrn_lb_logits': 'grad_w', 'grad_hgrn_norm': 'grad_w', 'grad_ssm_conv_w': 'grad_w', 'grad_ssm_conv_b': 'grad_w', 'grad_ssm_a_log': 'grad_w', 'grad_ssm_dt_bias': 'grad_w', 'grad_ssm_d': 'grad_w', 'grad_ssm_norm': 'grad_w', 'grad_w_o_mla': 'grad_w', 'grad_w_o_hgrn': 'grad_w', 'grad_w_o_ssm': 'grad_w', 'grad_w_out': 'grad_w', 'grad_ffn2_norm': 'grad_w', 'grad_ffn2_wi': 'grad_w', 'grad_ffn2_wo': 'grad_w', 'grad_final_norm': 'grad_w', 'delta_ffn1_norm': 'delta_w', 'delta_ffn1_wi': 'delta_w', 'delta_ffn1_wo': 'delta_w', 'delta_mix_norm': 'delta_w', 'delta_w_in': 'delta_w', 'delta_mla_q_norm': 'delta_w', 'delta_mla_w_uq': 'delta_w', 'delta_mla_kv_norm': 'delta_w', 'delta_mla_w_ukv': 'delta_w', 'delta_hgrn_lb_logits': 'delta_w', 'delta_hgrn_norm': 'delta_w', 'delta_ssm_conv_w': 'delta_w', 'delta_ssm_conv_b': 'delta_w', 'delta_ssm_a_log': 'delta_w', 'delta_ssm_dt_bias': 'delta_w', 'delta_ssm_d': 'delta_w', 'delta_ssm_norm': 'delta_w', 'delta_w_o_mla': 'delta_w', 'delta_w_o_hgrn': 'delta_w', 'delta_w_o_ssm': 'delta_w', 'delta_w_out': 'delta_w', 'delta_ffn2_norm': 'delta_w', 'delta_ffn2_wi': 'delta_w', 'delta_ffn2_wo': 'delta_w', 'delta_final_norm': 'delta_w', 'new_m_ffn1_norm': 'new_m', 'new_m_ffn1_wi': 'new_m', 'new_m_ffn1_wo': 'new_m', 'new_m_mix_norm': 'new_m', 'new_m_w_in': 'new_m', 'new_m_mla_q_norm': 'new_m', 'new_m_mla_w_uq': 'new_m', 'new_m_mla_kv_norm': 'new_m', 'new_m_mla_w_ukv': 'new_m', 'new_m_hgrn_lb_logits': 'new_m', 'new_m_hgrn_norm': 'new_m', 'new_m_ssm_conv_w': 'new_m', 'new_m_ssm_conv_b': 'new_m', 'new_m_ssm_a_log': 'new_m', 'new_m_ssm_dt_bias': 'new_m', 'new_m_ssm_d': 'new_m', 'new_m_ssm_norm': 'new_m', 'new_m_w_o_mla': 'new_m', 'new_m_w_o_hgrn': 'new_m', 'new_m_w_o_ssm': 'new_m', 'new_m_w_out': 'new_m', 'new_m_ffn2_norm': 'new_m', 'new_m_ffn2_wi': 'new_m', 'new_m_ffn2_wo': 'new_m', 'new_m_final_norm': 'new_m', 'new_v_ffn1_norm': 'new_v', 'new_v_ffn1_wi': 'new_v', 'new_v_ffn1_wo': 'new_v', 'new_v_mix_norm': 'new_v', 'new_v_w_in': 'new_v', 'new_v_mla_q_norm': 'new_v', 'new_v_mla_w_uq': 'new_v', 'new_v_mla_kv_norm': 'new_v', 'new_v_mla_w_ukv': 'new_v', 'new_v_hgrn_lb_logits': 'new_v', 'new_v_hgrn_norm': 'new_v', 'new_v_ssm_conv_w': 'new_v', 'new_v_ssm_conv_b': 'new_v', 'new_v_ssm_a_log': 'new_v', 'new_v_ssm_dt_bias': 'new_v', 'new_v_ssm_d': 'new_v', 'new_v_ssm_norm': 'new_v', 'new_v_w_o_mla': 'new_v', 'new_v_w_o_hgrn': 'new_v', 'new_v_w_o_ssm': 'new_v', 'new_v_w_out': 'new_v', 'new_v_ffn2_norm': 'new_v', 'new_v_ffn2_wi': 'new_v', 'new_v_ffn2_wo': 'new_v', 'new_v_final_norm': 'new_v'}


def _forward(args):
    return _fwd_reference(*[args[k] for k in FWD_PARAMS])


def _output_shape():
    def fwd():
        inp = _fwd_setup_inputs(0)
        return _fwd_reference(*[inp[k] for k in FWD_PARAMS])
    out = _jax.eval_shape(fwd)
    return out.shape, out.dtype

N_MICROBATCH = 1
ADAM_LR = 0.001
ADAM_B1 = 0.9
ADAM_B2 = 0.999
ADAM_EPS = 1e-08
ADAM_WD = 0.01
ADAM_STEP = 10
PER_EXAMPLE_BATCH_AXIS = {'x': 0, 'loss_target': 0}
SHARED_INPUTS = []
_WEIGHT_DTYPES = {'ffn1_norm': _jnp.float32, 'ffn1_wi': _jnp.float32, 'ffn1_wo': _jnp.float32, 'mix_norm': _jnp.float32, 'w_in': _jnp.float32, 'mla_q_norm': _jnp.float32, 'mla_w_uq': _jnp.float32, 'mla_kv_norm': _jnp.float32, 'mla_w_ukv': _jnp.float32, 'hgrn_lb_logits': _jnp.float32, 'hgrn_norm': _jnp.float32, 'ssm_conv_w': _jnp.float32, 'ssm_conv_b': _jnp.float32, 'ssm_a_log': _jnp.float32, 'ssm_dt_bias': _jnp.float32, 'ssm_d': _jnp.float32, 'ssm_norm': _jnp.float32, 'w_o_mla': _jnp.float32, 'w_o_hgrn': _jnp.float32, 'w_o_ssm': _jnp.float32, 'w_out': _jnp.float32, 'ffn2_norm': _jnp.float32, 'ffn2_wi': _jnp.float32, 'ffn2_wo': _jnp.float32, 'final_norm': _jnp.float32}
MOMENT_SCALE = {'ffn1_norm': 6.166511e-02, 'ffn1_wi': 2.642870e-02, 'ffn1_wo': 4.313175e-02, 'mix_norm': 1.048939e-01, 'w_in': 3.004932e-02, 'mla_q_norm': 1.658438e-02, 'mla_w_uq': 6.826199e-03, 'mla_kv_norm': 2.480205e-02, 'mla_w_ukv': 8.330353e-03, 'hgrn_lb_logits': 2.471908e-03, 'hgrn_norm': 3.488711e-02, 'ssm_conv_w': 3.624966e-02, 'ssm_conv_b': 4.898106e-02, 'ssm_a_log': 1.336602e-01, 'ssm_dt_bias': 8.749593e-02, 'ssm_d': 2.757526e-01, 'ssm_norm': 4.196481e-02, 'w_o_mla': 9.489562e-03, 'w_o_hgrn': 3.502928e-02, 'w_o_ssm': 5.883549e-02, 'w_out': 6.909042e-02, 'ffn2_norm': 4.385100e-02, 'ffn2_wi': 1.875676e-02, 'ffn2_wo': 3.059865e-02, 'final_norm': 3.197417e+01}


def _to_microbatches(a, axis):
    t = _jnp.moveaxis(a, axis, 0)
    t = t.reshape((N_MICROBATCH, t.shape[0] // N_MICROBATCH) + t.shape[1:])
    return _jnp.moveaxis(t, 1, axis + 1)


def setup_inputs(seed: int = 0) -> dict:
    inp = _fwd_setup_inputs(seed)
    key = _jax.random.fold_in(_jax.random.key(seed), 7919)
    shape, _ = _output_shape()
    out = dict(inp)
    out["loss_target"] = _jax.random.normal(_jax.random.fold_in(key, 0), shape, _jnp.float32)
    for i, name in enumerate(TWIN_WEIGHTS):
        w = inp[name].astype(_jnp.float32)
        if MOMENT_SCALE is None:
            s = _jnp.sqrt(_jnp.mean(_jnp.square(w)) + 1e-30)
        else:
            s = MOMENT_SCALE[name]
        km, kv = _jax.random.split(_jax.random.fold_in(key, i + 1))
        out[name] = w
        out["m_" + name] = s * _jax.random.normal(km, w.shape, _jnp.float32)
        out["v_" + name] = (s * s) * _jax.random.uniform(kv, w.shape, _jnp.float32, 0.5, 1.5)
    if N_MICROBATCH > 1:
        for name, axis in PER_EXAMPLE_BATCH_AXIS.items():
            out[name] = _to_microbatches(out[name], axis)
    return {'x': out['x'], 'ffn1_norm': out['ffn1_norm'], 'ffn1_wi': out['ffn1_wi'], 'ffn1_wo': out['ffn1_wo'], 'mix_norm': out['mix_norm'], 'w_in': out['w_in'], 'mla_q_norm': out['mla_q_norm'], 'mla_w_uq': out['mla_w_uq'], 'mla_kv_norm': out['mla_kv_norm'], 'mla_w_ukv': out['mla_w_ukv'], 'hgrn_lb_logits': out['hgrn_lb_logits'], 'hgrn_norm': out['hgrn_norm'], 'ssm_conv_w': out['ssm_conv_w'], 'ssm_conv_b': out['ssm_conv_b'], 'ssm_a_log': out['ssm_a_log'], 'ssm_dt_bias': out['ssm_dt_bias'], 'ssm_d': out['ssm_d'], 'ssm_norm': out['ssm_norm'], 'w_o_mla': out['w_o_mla'], 'w_o_hgrn': out['w_o_hgrn'], 'w_o_ssm': out['w_o_ssm'], 'w_out': out['w_out'], 'ffn2_norm': out['ffn2_norm'], 'ffn2_wi': out['ffn2_wi'], 'ffn2_wo': out['ffn2_wo'], 'final_norm': out['final_norm'], 'loss_target': out['loss_target'], 'm_ffn1_norm': out['m_ffn1_norm'], 'm_ffn1_wi': out['m_ffn1_wi'], 'm_ffn1_wo': out['m_ffn1_wo'], 'm_mix_norm': out['m_mix_norm'], 'm_w_in': out['m_w_in'], 'm_mla_q_norm': out['m_mla_q_norm'], 'm_mla_w_uq': out['m_mla_w_uq'], 'm_mla_kv_norm': out['m_mla_kv_norm'], 'm_mla_w_ukv': out['m_mla_w_ukv'], 'm_hgrn_lb_logits': out['m_hgrn_lb_logits'], 'm_hgrn_norm': out['m_hgrn_norm'], 'm_ssm_conv_w': out['m_ssm_conv_w'], 'm_ssm_conv_b': out['m_ssm_conv_b'], 'm_ssm_a_log': out['m_ssm_a_log'], 'm_ssm_dt_bias': out['m_ssm_dt_bias'], 'm_ssm_d': out['m_ssm_d'], 'm_ssm_norm': out['m_ssm_norm'], 'm_w_o_mla': out['m_w_o_mla'], 'm_w_o_hgrn': out['m_w_o_hgrn'], 'm_w_o_ssm': out['m_w_o_ssm'], 'm_w_out': out['m_w_out'], 'm_ffn2_norm': out['m_ffn2_norm'], 'm_ffn2_wi': out['m_ffn2_wi'], 'm_ffn2_wo': out['m_ffn2_wo'], 'm_final_norm': out['m_final_norm'], 'v_ffn1_norm': out['v_ffn1_norm'], 'v_ffn1_wi': out['v_ffn1_wi'], 'v_ffn1_wo': out['v_ffn1_wo'], 'v_mix_norm': out['v_mix_norm'], 'v_w_in': out['v_w_in'], 'v_mla_q_norm': out['v_mla_q_norm'], 'v_mla_w_uq': out['v_mla_w_uq'], 'v_mla_kv_norm': out['v_mla_kv_norm'], 'v_mla_w_ukv': out['v_mla_w_ukv'], 'v_hgrn_lb_logits': out['v_hgrn_lb_logits'], 'v_hgrn_norm': out['v_hgrn_norm'], 'v_ssm_conv_w': out['v_ssm_conv_w'], 'v_ssm_conv_b': out['v_ssm_conv_b'], 'v_ssm_a_log': out['v_ssm_a_log'], 'v_ssm_dt_bias': out['v_ssm_dt_bias'], 'v_ssm_d': out['v_ssm_d'], 'v_ssm_norm': out['v_ssm_norm'], 'v_w_o_mla': out['v_w_o_mla'], 'v_w_o_hgrn': out['v_w_o_hgrn'], 'v_w_o_ssm': out['v_w_o_ssm'], 'v_w_out': out['v_w_out'], 'v_ffn2_norm': out['v_ffn2_norm'], 'v_ffn2_wi': out['v_ffn2_wi'], 'v_ffn2_wo': out['v_ffn2_wo'], 'v_final_norm': out['v_final_norm']}


def _loss(weights, diff, rest, loss_target):
    with _jax.named_scope("forward"):
        args = {**rest, TWIN_DIFF_INPUT: diff, **{k: w.astype(_WEIGHT_DTYPES[k]) for k, w in weights.items()}}
        y = _forward(args)
    with _jax.named_scope("loss_head"):
        err = _jnp.square(y.astype(_jnp.float32) - loss_target)
        return 0.5 * _jnp.sum(_jnp.mean(err, axis=-1)) if err.ndim else 0.5 * err


def _adamw(w, g, m, v):
    m = ADAM_B1 * m + (1.0 - ADAM_B1) * g
    v = ADAM_B2 * v + (1.0 - ADAM_B2) * _jnp.square(g)
    m_hat = m / (1.0 - ADAM_B1 ** ADAM_STEP)
    v_hat = v / (1.0 - ADAM_B2 ** ADAM_STEP)
    delta = -ADAM_LR * (m_hat / (_jnp.sqrt(v_hat) + ADAM_EPS) + ADAM_WD * w)
    return delta, m, v


def reference(x, ffn1_norm, ffn1_wi, ffn1_wo, mix_norm, w_in, mla_q_norm, mla_w_uq, mla_kv_norm, mla_w_ukv, hgrn_lb_logits, hgrn_norm, ssm_conv_w, ssm_conv_b, ssm_a_log, ssm_dt_bias, ssm_d, ssm_norm, w_o_mla, w_o_hgrn, w_o_ssm, w_out, ffn2_norm, ffn2_wi, ffn2_wo, final_norm, loss_target, m_ffn1_norm, m_ffn1_wi, m_ffn1_wo, m_mix_norm, m_w_in, m_mla_q_norm, m_mla_w_uq, m_mla_kv_norm, m_mla_w_ukv, m_hgrn_lb_logits, m_hgrn_norm, m_ssm_conv_w, m_ssm_conv_b, m_ssm_a_log, m_ssm_dt_bias, m_ssm_d, m_ssm_norm, m_w_o_mla, m_w_o_hgrn, m_w_o_ssm, m_w_out, m_ffn2_norm, m_ffn2_wi, m_ffn2_wo, m_final_norm, v_ffn1_norm, v_ffn1_wi, v_ffn1_wo, v_mix_norm, v_w_in, v_mla_q_norm, v_mla_w_uq, v_mla_kv_norm, v_mla_w_ukv, v_hgrn_lb_logits, v_hgrn_norm, v_ssm_conv_w, v_ssm_conv_b, v_ssm_a_log, v_ssm_dt_bias, v_ssm_d, v_ssm_norm, v_w_o_mla, v_w_o_hgrn, v_w_o_ssm, v_w_out, v_ffn2_norm, v_ffn2_wi, v_ffn2_wo, v_final_norm):
    given = dict(x=x, ffn1_norm=ffn1_norm, ffn1_wi=ffn1_wi, ffn1_wo=ffn1_wo, mix_norm=mix_norm, w_in=w_in, mla_q_norm=mla_q_norm, mla_w_uq=mla_w_uq, mla_kv_norm=mla_kv_norm, mla_w_ukv=mla_w_ukv, hgrn_lb_logits=hgrn_lb_logits, hgrn_norm=hgrn_norm, ssm_conv_w=ssm_conv_w, ssm_conv_b=ssm_conv_b, ssm_a_log=ssm_a_log, ssm_dt_bias=ssm_dt_bias, ssm_d=ssm_d, ssm_norm=ssm_norm, w_o_mla=w_o_mla, w_o_hgrn=w_o_hgrn, w_o_ssm=w_o_ssm, w_out=w_out, ffn2_norm=ffn2_norm, ffn2_wi=ffn2_wi, ffn2_wo=ffn2_wo, final_norm=final_norm, loss_target=loss_target, m_ffn1_norm=m_ffn1_norm, m_ffn1_wi=m_ffn1_wi, m_ffn1_wo=m_ffn1_wo, m_mix_norm=m_mix_norm, m_w_in=m_w_in, m_mla_q_norm=m_mla_q_norm, m_mla_w_uq=m_mla_w_uq, m_mla_kv_norm=m_mla_kv_norm, m_mla_w_ukv=m_mla_w_ukv, m_hgrn_lb_logits=m_hgrn_lb_logits, m_hgrn_norm=m_hgrn_norm, m_ssm_conv_w=m_ssm_conv_w, m_ssm_conv_b=m_ssm_conv_b, m_ssm_a_log=m_ssm_a_log, m_ssm_dt_bias=m_ssm_dt_bias, m_ssm_d=m_ssm_d, m_ssm_norm=m_ssm_norm, m_w_o_mla=m_w_o_mla, m_w_o_hgrn=m_w_o_hgrn, m_w_o_ssm=m_w_o_ssm, m_w_out=m_w_out, m_ffn2_norm=m_ffn2_norm, m_ffn2_wi=m_ffn2_wi, m_ffn2_wo=m_ffn2_wo, m_final_norm=m_final_norm, v_ffn1_norm=v_ffn1_norm, v_ffn1_wi=v_ffn1_wi, v_ffn1_wo=v_ffn1_wo, v_mix_norm=v_mix_norm, v_w_in=v_w_in, v_mla_q_norm=v_mla_q_norm, v_mla_w_uq=v_mla_w_uq, v_mla_kv_norm=v_mla_kv_norm, v_mla_w_ukv=v_mla_w_ukv, v_hgrn_lb_logits=v_hgrn_lb_logits, v_hgrn_norm=v_hgrn_norm, v_ssm_conv_w=v_ssm_conv_w, v_ssm_conv_b=v_ssm_conv_b, v_ssm_a_log=v_ssm_a_log, v_ssm_dt_bias=v_ssm_dt_bias, v_ssm_d=v_ssm_d, v_ssm_norm=v_ssm_norm, v_w_o_mla=v_w_o_mla, v_w_o_hgrn=v_w_o_hgrn, v_w_o_ssm=v_w_o_ssm, v_w_out=v_w_out, v_ffn2_norm=v_ffn2_norm, v_ffn2_wi=v_ffn2_wi, v_ffn2_wo=v_ffn2_wo, v_final_norm=v_final_norm)
    weights = {n: given[n] for n in TWIN_WEIGHTS}
    shared = {n: given[n] for n in SHARED_INPUTS}
    per_example = {n: given[n] for n in ['x']}
    grad_fn = _jax.value_and_grad(_loss, argnums=(0, 1))

    def one_microbatch(ex, loss_target):
        ex = dict(ex)
        diff = ex.pop(TWIN_DIFF_INPUT)
        return grad_fn(weights, diff, {**shared, **ex}, loss_target)

    if N_MICROBATCH == 1:
        loss, (grad_w, grad_x) = one_microbatch(per_example, given["loss_target"])
    else:
        def body(carry, xs):
            loss_sum, grad_sum = carry
            l_k, (gw_k, gx_k) = one_microbatch(xs[0], xs[1])
            with _jax.named_scope("update"):
                return (loss_sum + l_k, _jax.tree.map(_jnp.add, grad_sum, gw_k)), gx_k

        init = (_jnp.zeros((), _jnp.float32), _jax.tree.map(_jnp.zeros_like, weights))
        (loss, grad_w), grad_x = _jax.lax.scan(body, init, (per_example, given["loss_target"]))
    with _jax.named_scope("update"):
        delta_w, new_m, new_v = {}, {}, {}
        for n in TWIN_WEIGHTS:
            delta_w[n], new_m[n], new_v[n] = _adamw(weights[n], grad_w[n], given["m_" + n], given["v_" + n])
    return (loss, grad_x, *[grad_w[n] for n in TWIN_WEIGHTS], *[delta_w[n] for n in TWIN_WEIGHTS],
            *[new_m[n] for n in TWIN_WEIGHTS], *[new_v[n] for n in TWIN_WEIGHTS])
```

```python
import functools
import math

import jax
import jax.numpy as jnp
import numpy as np
from jax import lax
from jax.experimental import pallas as pl
from jax.experimental.pallas import tpu as pltpu

F32 = jnp.float32
BF16 = jnp.bfloat16
MESH = pl.DeviceIdType.MESH

D_MODEL = 2048
DEPTH = 2
CHUNK = 64
EPS = 1e-6
MLA_HEADS, MLA_Q_RANK, MLA_KV_RANK, MLA_NOPE, MLA_ROPE, MLA_V = 16, 512, 512, 128, 64, 128
ROPE_THETA = 10000.0
HG_HEADS, HG_DK = 16, 128
HG_WIDTH = HG_HEADS * HG_DK
SSM_INNER, SSM_HEADDIM, SSM_HEADS, SSM_GROUPS, SSM_STATE, SSM_CONV = 4096, 64, 64, 8, 128, 4
SSM_CONV_DIM = SSM_INNER + 2 * SSM_GROUPS * SSM_STATE
D_FF = 5632
IN_DIM = 25728
ADAM_LR, ADAM_B1, ADAM_B2, ADAM_EPS, ADAM_WD, ADAM_STEP = 0.001, 0.9, 0.999, 1e-08, 0.01, 10

P_QLAT, P_KVLAT, P_HQ, P_HF, P_HI, P_HGATE, P_Z, P_XBC, P_GATES, P_KPEDT = (
    0, 512, 1024, 3072, 5120, 7168, 9216, 13312, 19456, 25600)
IN_PAD = 26624

VMEM_LIMIT_V7X = 48 << 20
SSD_Q = 256
HG_HB = 4
NEG = -1e30


def _cp(*sem):
    return pltpu.CompilerParams(dimension_semantics=sem, vmem_limit_bytes=VMEM_LIMIT_V7X)


def _fit(n, pref):
    if n <= pref:
        return n
    for t in range(pref, 0, -128):
        if n % t == 0:
            return t
    raise ValueError((n, pref))


_DIMS = {"nn": (((1,), (0,)), ((), ())), "nt": (((1,), (1,)), ((), ())), "tn": (((0,), (0,)), ((), ()))}


def _dot(a, b, mode):
    return lax.dot_general(a.astype(BF16), b.astype(BF16), _DIMS[mode], preferred_element_type=F32)


def matmul(name, a, b, mode="nn", out_dtype=F32, alpha=1.0, res=None, tm=1024, tn=1024, tk=512):
    if mode == "nn":
        (M, K), (K2, N) = a.shape, b.shape
    elif mode == "nt":
        (M, K), (N, K2) = a.shape, b.shape
    else:
        (K, M), (K2, N) = a.shape, b.shape
    assert K == K2, (name, a.shape, b.shape, mode)
    tm, tn, tk = _fit(M, tm), _fit(N, tn), _fit(K, tk)
    nk = K // tk
    a_spec = pl.BlockSpec((tk, tm), lambda i, j, k: (k, i)) if mode == "tn" else pl.BlockSpec((tm, tk), lambda i, j, k: (i, k))
    b_spec = pl.BlockSpec((tn, tk), lambda i, j, k: (j, k)) if mode == "nt" else pl.BlockSpec((tk, tn), lambda i, j, k: (k, j))
    o_spec = pl.BlockSpec((tm, tn), lambda i, j, k: (i, j))
    has_res = res is not None

    def body(*refs):
        a_ref, b_ref = refs[0], refs[1]
        o_ref, acc = refs[-2], refs[-1]
        k = pl.program_id(2)

        @pl.when(k == 0)
        def _():
            acc[...] = jnp.zeros_like(acc)

        acc[...] += _dot(a_ref[...], b_ref[...], mode)

        @pl.when(k == nk - 1)
        def _():
            v = acc[...]
            if alpha != 1.0:
                v = v * alpha
            if has_res:
                v = v + refs[2][...].astype(F32)
            o_ref[...] = v.astype(o_ref.dtype)

    ins = [a, b] + ([res] if has_res else [])
    return pl.pallas_call(
        body, name=name, grid=(M // tm, N // tn, nk),
        in_specs=[a_spec, b_spec] + ([o_spec] if has_res else []), out_specs=o_spec,
        out_shape=jax.ShapeDtypeStruct((M, N), out_dtype),
        scratch_shapes=[pltpu.VMEM((tm, tn), F32)],
        compiler_params=_cp("parallel", "parallel", "arbitrary"),
    )(*ins)


def _row_specs(rows, consts, tm):
    specs = []
    for arr, w, off in rows:
        specs.append(pl.BlockSpec((tm, w), functools.partial(lambda j, i, off: (i, off + j), off=off)))
    for arr, w, off in consts:
        specs.append(pl.BlockSpec((arr.shape[0], w), functools.partial(lambda j, i, off: (0, off + j), off=off)))
    return specs


def rowwise(name, fn, rows, consts, outs, n_groups=1, tm=512):
    S = rows[0][0].shape[0]
    tm = _fit(S, tm)
    n_in = len(rows) + len(consts)

    def body(*refs):
        vals = fn(*[r[...].astype(F32) for r in refs[:n_in]])
        for o_ref, v in zip(refs[n_in:], vals):
            o_ref[...] = v.astype(o_ref.dtype)

    return pl.pallas_call(
        body, name=name, grid=(n_groups, S // tm),
        in_specs=_row_specs(rows, consts, tm),
        out_specs=[pl.BlockSpec((tm, w), lambda j, i: (i, j)) for _, w, _ in outs],
        out_shape=[jax.ShapeDtypeStruct((S, W), dt) for W, _, dt in outs],
        compiler_params=_cp("parallel", "parallel"),
    )(*[r[0] for r in rows], *[c[0] for c in consts])


def rowwise_bwd(name, fn, rows, consts, cts, row_grads, const_grads, adds=(), n_groups=1, tm=512):
    S = rows[0][0].shape[0]
    tm = _fit(S, tm)
    n_r, n_c, n_ct, n_add = len(rows), len(consts), len(cts), len(adds)

    def body(*refs):
        ins = [r[...].astype(F32) for r in refs[:n_r + n_c]]
        ct = tuple(r[...].astype(F32) for r in refs[n_r + n_c:n_r + n_c + n_ct])
        add_refs = refs[n_r + n_c + n_ct:n_r + n_c + n_ct + n_add]
        out_refs = refs[n_r + n_c + n_ct + n_add:]
        _, vjp = jax.vjp(fn, *ins)
        g = vjp(ct)
        for q, (idx, _, _) in enumerate(row_grads):
            v = g[idx]
            if q < n_add:
                v = v + add_refs[q][...].astype(F32)
            out_refs[q][...] = v.astype(out_refs[q].dtype)
        first = pl.program_id(1) == 0
        for q, idx in enumerate(const_grads):
            o_ref = out_refs[len(row_grads) + q]

            @pl.when(first)
            def _(o_ref=o_ref):
                o_ref[...] = jnp.zeros_like(o_ref)

            o_ref[...] += g[n_r + idx]

    in_specs = _row_specs(rows, consts, tm)
    in_specs += [pl.BlockSpec((tm, w), lambda j, i: (i, j)) for _, w in cts]
    in_specs += [pl.BlockSpec((tm, rows[idx][1]), lambda j, i: (i, j)) for (idx, _, _) in row_grads[:n_add]]
    out_specs = [pl.BlockSpec((tm, rows[idx][1]), lambda j, i: (i, j)) for idx, _, _ in row_grads]
    out_specs += [pl.BlockSpec((consts[idx][0].shape[0], consts[idx][1]), lambda j, i: (0, j)) for idx in const_grads]
    out_shape = [jax.ShapeDtypeStruct((S, W), dt) for _, W, dt in row_grads]
    out_shape += [jax.ShapeDtypeStruct((consts[idx][0].shape[0], consts[idx][1] * n_groups), F32) for idx in const_grads]
    return pl.pallas_call(
        body, name=name, grid=(n_groups, S // tm), in_specs=in_specs, out_specs=out_specs, out_shape=out_shape,
        compiler_params=_cp("parallel", "arbitrary"),
    )(*[r[0] for r in rows], *[c[0] for c in consts], *[c[0] for c in cts], *adds)


def f_rmsnorm(x, w):
    return (x * lax.rsqrt(jnp.mean(x * x, axis=-1, keepdims=True) + EPS) * w,)


def f_swiglu(g, u):
    return (jax.nn.silu(g) * u,)


def f_hgrn_out(o, g, w):
    return (o * lax.rsqrt(jnp.mean(o * o, axis=-1, keepdims=True) + EPS) * w * jax.nn.silu(g),)


def f_ssm_out(y, z, w):
    y = y * jax.nn.silu(z)
    return (y * lax.rsqrt(jnp.mean(y * y, axis=-1, keepdims=True) + EPS) * w,)


def f_merge(ga, gb, gc, ya, yb, yc):
    return (jax.nn.sigmoid(ga) * ya + jax.nn.sigmoid(gb) * yb + jax.nn.sigmoid(gc) * yc,)


def f_lower_bounds(logits):
    p = jax.nn.softmax(logits, axis=0)
    rows = [jnp.zeros_like(p[0:1])]
    for l in range(1, DEPTH):
        rows.append(rows[-1] + p[l:l + 1])
    return (jnp.concatenate(rows, axis=0),)


def loss_head(x, target, w, tm=512):
    S, D = x.shape
    tm = _fit(S, tm)

    def loss_fn(xb, wb, tb):
        (y,) = f_rmsnorm(xb, wb)
        return 0.5 * jnp.sum(jnp.mean(jnp.square(y - tb), axis=-1))

    def body(x_ref, t_ref, w_ref, dx_ref, dw_ref, loss_ref):
        @pl.when(pl.program_id(0) == 0)
        def _():
            dw_ref[...] = jnp.zeros_like(dw_ref)
            loss_ref[...] = jnp.zeros_like(loss_ref)

        l, (dx, dw) = jax.value_and_grad(loss_fn, argnums=(0, 1))(x_ref[...], w_ref[...], t_ref[...])
        dx_ref[...] = dx
        dw_ref[...] += dw
        loss_ref[...] += jnp.full(loss_ref.shape, l, F32)

    row = pl.BlockSpec((tm, D), lambda i: (i, 0))
    vec = pl.BlockSpec((1, D), lambda i: (0, 0))
    return pl.pallas_call(
        body, name="loss_head", grid=(S // tm,), in_specs=[row, row, vec],
        out_specs=[row, vec, pl.BlockSpec((1, 128), lambda i: (0, 0))],
        out_shape=[jax.ShapeDtypeStruct((S, D), F32), jax.ShapeDtypeStruct((1, D), F32), jax.ShapeDtypeStruct((1, 128), F32)],
        compiler_params=_cp("arbitrary"),
    )(x, target, w)


def rope_tables(S):
    inv = 1.0 / (ROPE_THETA ** (jnp.arange(0, MLA_ROPE, 2, dtype=F32) / MLA_ROPE))
    ang = jnp.arange(S, dtype=F32)[:, None] * inv[None, :]
    c, s = jnp.cos(ang), jnp.sin(ang)
    return jnp.tile(c, (1, 4)), jnp.concatenate([-s, s, -s, s], axis=1)


def _rope128(x, cosf, sinf):
    lane = lax.broadcasted_iota(jnp.int32, x.shape, 1)
    swapped = jnp.where((lane & 32) == 0, pltpu.roll(x, 96, 1), pltpu.roll(x, 32, 1))
    return x * cosf + swapped * sinf


def mla_q_prep(q, cosf, sinf, tm=512):
    S = q.shape[0]
    tm = _fit(S, tm)

    def body(q_ref, c_ref, s_ref, o_ref):
        x = q_ref[...]
        r = _rope128(x[:, 256:384], c_ref[...], s_ref[...])
        lane = lax.broadcasted_iota(jnp.int32, r.shape, 1)
        z = jnp.zeros_like(r)
        o_ref[...] = jnp.concatenate(
            [x[:, 0:128], jnp.where(lane < 64, r, z), x[:, 128:256], jnp.where(lane >= 64, r, z)], axis=1).astype(BF16)

    tab = pl.BlockSpec((tm, 128), lambda j, i: (i, 0))
    return pl.pallas_call(
        body, name="mla_q_prep", grid=(8, S // tm),
        in_specs=[pl.BlockSpec((tm, 384), lambda j, i: (i, j)), tab, tab],
        out_specs=pl.BlockSpec((tm, 512), lambda j, i: (i, j)),
        out_shape=jax.ShapeDtypeStruct((S, 4096), BF16), compiler_params=_cp("parallel", "parallel"),
    )(q, cosf, sinf)


def mla_q_prep_bwd(dqc, cosf, sinf, tm=512):
    S = dqc.shape[0]
    tm = _fit(S, tm)

    def body(d_ref, c_ref, s_ref, o_ref):
        d = d_ref[...]
        lane = lax.broadcasted_iota(jnp.int32, (tm, 128), 1)
        dr = jnp.where(lane < 64, d[:, 128:256], d[:, 384:512])
        o_ref[...] = jnp.concatenate([d[:, 0:128], d[:, 256:384], _rope128(dr, c_ref[...], -s_ref[...])], axis=1).astype(BF16)

    tab = pl.BlockSpec((tm, 128), lambda j, i: (i, 0))
    return pl.pallas_call(
        body, name="mla_q_prep_bwd", grid=(8, S // tm),
        in_specs=[pl.BlockSpec((tm, 512), lambda j, i: (i, j)), tab, tab],
        out_specs=pl.BlockSpec((tm, 384), lambda j, i: (i, j)),
        out_shape=jax.ShapeDtypeStruct((S, 3072), BF16), compiler_params=_cp("parallel", "parallel"),
    )(dqc, cosf, sinf)


def mla_k_prep(kv, proj, cosf, sinf, tm=512):
    S = kv.shape[0]
    tm = _fit(S, tm)

    def body(kv_ref, pe_ref, c_ref, s_ref, k_ref, v_ref):
        x = kv_ref[...]
        r = _rope128(pe_ref[...], c_ref[...], s_ref[...])
        lane = lax.broadcasted_iota(jnp.int32, r.shape, 1)
        r2 = jnp.where(lane < 64, r, pltpu.roll(r, 64, 1))
        k_ref[...] = jnp.concatenate([x[:, 0:128], r2, x[:, 256:384], r2], axis=1).astype(BF16)
        v_ref[...] = jnp.concatenate([x[:, 128:256], x[:, 384:512]], axis=1).astype(BF16)

    tab = pl.BlockSpec((tm, 128), lambda j, i: (i, 0))
    return pl.pallas_call(
        body, name="mla_k_prep", grid=(8, S // tm),
        in_specs=[pl.BlockSpec((tm, 512), lambda j, i: (i, j)), pl.BlockSpec((tm, 128), lambda j, i: (i, P_KPEDT // 128)), tab, tab],
        out_specs=[pl.BlockSpec((tm, 512), lambda j, i: (i, j)), pl.BlockSpec((tm, 256), lambda j, i: (i, j))],
        out_shape=[jax.ShapeDtypeStruct((S, 4096), BF16), jax.ShapeDtypeStruct((S, 2048), BF16)],
        compiler_params=_cp("parallel", "parallel"),
    )(kv, proj, cosf, sinf)


def mla_k_prep_bwd(dkc, dv, cosf, sinf, tm=512):
    S = dkc.shape[0]
    tm = _fit(S, tm)

    def body(dk_ref, dv_ref, c_ref, s_ref, dkv_ref, dpe_ref):
        dk, dvv = dk_ref[...], dv_ref[...]
        dkv_ref[...] = jnp.concatenate([dk[:, 0:128], dvv[:, 0:128], dk[:, 256:384], dvv[:, 128:256]], axis=1).astype(BF16)
        d2 = dk[:, 128:256] + dk[:, 384:512]
        lane = lax.broadcasted_iota(jnp.int32, d2.shape, 1)
        dr = jnp.where(lane < 64, d2 + pltpu.roll(d2, 64, 1), 0.0)
        dpe = jnp.where(lane < 64, _rope128(dr, c_ref[...], -s_ref[...]), 0.0)

        @pl.when(pl.program_id(1) == 0)
        def _():
            dpe_ref[...] = jnp.zeros_like(dpe_ref)

        dpe_ref[...] += dpe

    tab = pl.BlockSpec((tm, 128), lambda i, j: (i, 0))
    return pl.pallas_call(
        body, name="mla_k_prep_bwd", grid=(S // tm, 8),
        in_specs=[pl.BlockSpec((tm, 512), lambda i, j: (i, j)), pl.BlockSpec((tm, 256), lambda i, j: (i, j)), tab, tab],
        out_specs=[pl.BlockSpec((tm, 512), lambda i, j: (i, j)), tab],
        out_shape=[jax.ShapeDtypeStruct((S, 4096), BF16), jax.ShapeDtypeStruct((S, 128), F32)],
        compiler_params=_cp("parallel", "arbitrary"),
    )(dkc, dv, cosf, sinf)


ATT_SCALE = (MLA_NOPE + MLA_ROPE) ** -0.5


def _att_scores(q, k, qi, ki, t):
    s = _dot(q, k, "nt") * ATT_SCALE
    rows = qi * t + lax.broadcasted_iota(jnp.int32, (t, t), 0)
    cols = ki * t + lax.broadcasted_iota(jnp.int32, (t, t), 1)
    shift = CHUNK.bit_length() - 1
    return jnp.where((cols >> shift) <= (rows >> shift), s, NEG)


def attention_fwd(qc, kc, vb, t=1024):
    S = qc.shape[0]
    t = _fit(S, t)
    n = S // t

    def body(q_ref, k_ref, v_ref, o_ref, lse_ref, m_s, l_s, acc_s):
        qi, ki = pl.program_id(1), pl.program_id(2)

        @pl.when(ki == 0)
        def _():
            m_s[...] = jnp.full_like(m_s, NEG)
            l_s[...] = jnp.zeros_like(l_s)
            acc_s[...] = jnp.zeros_like(acc_s)

        @pl.when(ki <= qi)
        def _():
            s = _att_scores(q_ref[...], k_ref[...], qi, ki, t)
            m_prev = m_s[...]
            m_new = jnp.maximum(m_prev, jnp.max(s, axis=1, keepdims=True))
            alpha = jnp.exp(m_prev - m_new)
            p = jnp.exp(s - m_new)
            l_s[...] = alpha * l_s[...] + jnp.sum(p, axis=1, keepdims=True)
            acc_s[...] = alpha * acc_s[...] + _dot(p, v_ref[...], "nn")
            m_s[...] = m_new

        @pl.when(ki == qi)
        def _():
            o_ref[...] = (acc_s[...] / l_s[...]).astype(o_ref.dtype)
            lse_ref[...] = m_s[...] + jnp.log(l_s[...])

    return pl.pallas_call(
        body, name="attention_fwd", grid=(MLA_HEADS, n, n),
        in_specs=[pl.BlockSpec((t, 256), lambda h, i, j: (i, h)),
                  pl.BlockSpec((t, 256), lambda h, i, j: (jnp.minimum(i, j), h)),
                  pl.BlockSpec((t, 128), lambda h, i, j: (jnp.minimum(i, j), h))],
        out_specs=[pl.BlockSpec((t, 128), lambda h, i, j: (i, h)), pl.BlockSpec((None, t, 1), lambda h, i, j: (h, i, 0))],
        out_shape=[jax.ShapeDtypeStruct((S, 2048), BF16), jax.ShapeDtypeStruct((MLA_HEADS, S, 1), F32)],
        scratch_shapes=[pltpu.VMEM((t, 1), F32), pltpu.VMEM((t, 1), F32), pltpu.VMEM((t, 128), F32)],
        compiler_params=_cp("parallel", "parallel", "arbitrary"),
    )(qc, kc, vb)


def attention_delta(o, do, t=1024):
    S = o.shape[0]
    t = _fit(S, t)

    def body(o_ref, do_ref, d_ref):
        d_ref[...] = jnp.sum(o_ref[...].astype(F32) * do_ref[...].astype(F32), axis=1, keepdims=True)

    blk = pl.BlockSpec((t, 128), lambda h, i: (i, h))
    return pl.pallas_call(
        body, name="attention_delta", grid=(MLA_HEADS, S // t), in_specs=[blk, blk],
        out_specs=pl.BlockSpec((None, t, 1), lambda h, i: (h, i, 0)),
        out_shape=jax.ShapeDtypeStruct((MLA_HEADS, S, 1), F32), compiler_params=_cp("parallel", "parallel"),
    )(o, do)


def attention_bwd_dq(qc, kc, vb, do, lse, delta, t=1024):
    S = qc.shape[0]
    t = _fit(S, t)
    n = S // t

    def body(q_ref, k_ref, v_ref, do_ref, lse_ref, dl_ref, dq_ref, acc):
        qi, ki = pl.program_id(1), pl.program_id(2)

        @pl.when(ki == 0)
        def _():
            acc[...] = jnp.zeros_like(acc)

        @pl.when(ki <= qi)
        def _():
            p = jnp.exp(_att_scores(q_ref[...], k_ref[...], qi, ki, t) - lse_ref[...])
            dp = _dot(do_ref[...], v_ref[...], "nt")
            ds = p * (dp - dl_ref[...]) * ATT_SCALE
            acc[...] += _dot(ds, k_ref[...], "nn")

        @pl.when(ki == qi)
        def _():
            dq_ref[...] = acc[...]

    stat = pl.BlockSpec((None, t, 1), lambda h, i, j: (h, i, 0))
    return pl.pallas_call(
        body, name="attention_bwd_dq", grid=(MLA_HEADS, n, n),
        in_specs=[pl.BlockSpec((t, 256), lambda h, i, j: (i, h)),
                  pl.BlockSpec((t, 256), lambda h, i, j: (jnp.minimum(i, j), h)),
                  pl.BlockSpec((t, 128), lambda h, i, j: (jnp.minimum(i, j), h)),
                  pl.BlockSpec((t, 128), lambda h, i, j: (i, h)), stat, stat],
        out_specs=pl.BlockSpec((t, 256), lambda h, i, j: (i, h)),
        out_shape=jax.ShapeDtypeStruct((S, 4096), F32),
        scratch_shapes=[pltpu.VMEM((t, 256), F32)],
        compiler_params=_cp("parallel", "parallel", "arbitrary"),
    )(qc, kc, vb, do, lse, delta)


def attention_bwd_dkv(qc, kc, vb, do, lse, delta, t=1024):
    S = qc.shape[0]
    t = _fit(S, t)
    n = S // t

    def body(q_ref, k_ref, v_ref, do_ref, lse_ref, dl_ref, dk_ref, dv_ref, dk_acc, dv_acc):
        ki, qi = pl.program_id(1), pl.program_id(2)

        @pl.when(qi == 0)
        def _():
            dk_acc[...] = jnp.zeros_like(dk_acc)
            dv_acc[...] = jnp.zeros_like(dv_acc)

        @pl.when(qi >= ki)
        def _():
            p = jnp.exp(_att_scores(q_ref[...], k_ref[...], qi, ki, t) - lse_ref[...])
            dv_acc[...] += _dot(p, do_ref[...], "tn")
            dp = _dot(do_ref[...], v_ref[...], "nt")
            ds = p * (dp - dl_ref[...]) * ATT_SCALE
            dk_acc[...] += _dot(ds, q_ref[...], "tn")

        @pl.when(qi == n - 1)
        def _():
            dk_ref[...] = dk_acc[...]
            dv_ref[...] = dv_acc[...]

    stat = pl.BlockSpec((None, t, 1), lambda h, j, i: (h, jnp.maximum(i, j), 0))
    return pl.pallas_call(
        body, name="attention_bwd_dkv", grid=(MLA_HEADS, n, n),
        in_specs=[pl.BlockSpec((t, 256), lambda h, j, i: (jnp.maximum(i, j), h)),
                  pl.BlockSpec((t, 256), lambda h, j, i: (j, h)),
                  pl.BlockSpec((t, 128), lambda h, j, i: (j, h)),
                  pl.BlockSpec((t, 128), lambda h, j, i: (jnp.maximum(i, j), h)), stat, stat],
        out_specs=[pl.BlockSpec((t, 256), lambda h, j, i: (j, h)), pl.BlockSpec((t, 128), lambda h, j, i: (j, h))],
        out_shape=[jax.ShapeDtypeStruct((S, 4096), F32), jax.ShapeDtypeStruct((S, 2048), F32)],
        scratch_shapes=[pltpu.VMEM((t, 256), F32), pltpu.VMEM((t, 128), F32)],
        compiler_params=_cp("parallel", "parallel", "arbitrary"),
    )(qc, kc, vb, do, lse, delta)


def _scan_rows(x, reverse):
    n = x.shape[0]
    row = lax.broadcasted_iota(jnp.int32, x.shape, 0)
    d = 1
    while d < n:
        if reverse:
            x = x + jnp.where(row < n - d, pltpu.roll(x, n - d, 0), 0.0)
        else:
            x = x + jnp.where(row >= d, pltpu.roll(x, d, 0), 0.0)
        d *= 2
    return x


@jax.custom_vjp
def cumsum_rows(x):
    return _scan_rows(x, False)


cumsum_rows.defvjp(lambda x: (_scan_rows(x, False), None), lambda _, g: (_scan_rows(g, True),))


def hgrn_chunk(q_in, f_in, v, lb, state_t):
    f = lb + (1.0 - lb) * jax.nn.sigmoid(f_in)
    q = jax.nn.silu(q_in) * HG_DK ** -0.5
    k = 1.0 - f
    b = cumsum_rows(jnp.log(f))
    b_last = b[CHUNK - 1:CHUNK]
    b_mid = b[CHUNK // 2 - 1:CHUNK // 2]
    r = lax.broadcasted_iota(jnp.int32, (CHUNK, CHUNK), 0)
    c = lax.broadcasted_iota(jnp.int32, (CHUNK, CHUNK), 1)
    att = jnp.where(c <= r, _dot(q * jnp.exp(b - b_mid), k * jnp.exp(b_mid - b), "nt"), 0.0)
    o = _dot(q * jnp.exp(b), state_t, "nt") + _dot(att, v, "nn")
    new_state_t = state_t * jnp.exp(b_last) + _dot(v, k * jnp.exp(b_last - b), "tn")
    return o, new_state_t


def hgrn_scan_fwd(proj, lb):
    S = proj.shape[0]
    nc = S // CHUNK
    W = HG_HB * 128

    def body(q_ref, f_ref, v_ref, lb_ref, o_ref, hst_ref, st):
        @pl.when(pl.program_id(1) == 0)
        def _():
            st[...] = jnp.zeros_like(st)

        for h in range(HG_HB):
            cs = slice(h * 128, (h + 1) * 128)
            hst_ref[h] = st[h]
            o, new = hgrn_chunk(q_ref[:, cs], f_ref[:, cs], v_ref[:, cs], lb_ref[:, cs], st[h])
            o_ref[:, cs] = o
            st[h] = new

    def seg(off):
        return pl.BlockSpec((CHUNK, W), functools.partial(lambda g, c, off: (c, off + g), off=off // W))

    return pl.pallas_call(
        body, name="hgrn_scan_fwd", grid=(HG_HEADS // HG_HB, nc),
        in_specs=[seg(P_HQ), seg(P_HF), seg(P_HI), pl.BlockSpec((1, W), lambda g, c: (0, g))],
        out_specs=[pl.BlockSpec((CHUNK, W), lambda g, c: (c, g)), pl.BlockSpec((None, HG_HB, 128, 128), lambda g, c: (c, g, 0, 0))],
        out_shape=[jax.ShapeDtypeStruct((S, HG_WIDTH), F32), jax.ShapeDtypeStruct((nc, HG_HEADS, 128, 128), F32)],
        scratch_shapes=[pltpu.VMEM((HG_HB, 128, 128), F32)],
        compiler_params=_cp("parallel", "arbitrary"),
    )(proj, proj, proj, lb)


def hgrn_scan_bwd(proj, lb, hst, do):
    S = proj.shape[0]
    nc = S // CHUNK
    W = HG_HB * 128

    def body(q_ref, f_ref, v_ref, lb_ref, hst_ref, do_ref, dq_ref, df_ref, dv_ref, dlb_ref, dst):
        @pl.when(pl.program_id(1) == 0)
        def _():
            dst[...] = jnp.zeros_like(dst)
            dlb_ref[...] = jnp.zeros_like(dlb_ref)

        for h in range(HG_HB):
            cs = slice(h * 128, (h + 1) * 128)
            _, vjp = jax.vjp(hgrn_chunk, q_ref[:, cs], f_ref[:, cs], v_ref[:, cs], lb_ref[:, cs], hst_ref[h])
            dq, df, dv, dlb, dstate = vjp((do_ref[:, cs], dst[h]))
            dq_ref[:, cs] = dq.astype(dq_ref.dtype)
            df_ref[:, cs] = df.astype(df_ref.dtype)
            dv_ref[:, cs] = dv.astype(dv_ref.dtype)
            dlb_ref[:, cs] += dlb
            dst[h] = dstate

    def seg(off):
        return pl.BlockSpec((CHUNK, W), functools.partial(lambda g, c, off: (nc - 1 - c, off + g), off=off // W))

    row = pl.BlockSpec((CHUNK, W), lambda g, c: (nc - 1 - c, g))
    vec = pl.BlockSpec((1, W), lambda g, c: (0, g))
    return pl.pallas_call(
        body, name="hgrn_scan_bwd", grid=(HG_HEADS // HG_HB, nc),
        in_specs=[seg(P_HQ), seg(P_HF), seg(P_HI), vec,
                  pl.BlockSpec((None, HG_HB, 128, 128), lambda g, c: (nc - 1 - c, g, 0, 0)), row],
        out_specs=[row, row, row, vec],
        out_shape=[jax.ShapeDtypeStruct((S, HG_WIDTH), BF16)] * 3 + [jax.ShapeDtypeStruct((1, HG_WIDTH), F32)],
        scratch_shapes=[pltpu.VMEM((HG_HB, 128, 128), F32)],
        compiler_params=_cp("parallel", "arbitrary"),
    )(proj, proj, proj, lb, hst, do)


def _silu_grad(x):
    s = jax.nn.sigmoid(x)
    return s * (1.0 + x * (1.0 - s))


def conv_fwd(proj, w, b, tm=512):
    S = proj.shape[0]
    tm = _fit(S, tm)
    G = 512
    off = P_XBC // G

    def body(cur_ref, prev_ref, w_ref, b_ref, act_ref, pre_ref):
        prev = prev_ref[...] * (pl.program_id(1) > 0).astype(F32)
        ext = jnp.concatenate([prev, cur_ref[...]], axis=0)
        n = tm + 8
        acc = b_ref[...] + jnp.zeros((tm, G), F32)
        for j in range(SSM_CONV):
            acc = acc + w_ref[j:j + 1, :] * pltpu.roll(ext, (n - 5 - j) % n, 0)[0:tm]
        pre_ref[...] = acc
        act_ref[...] = jax.nn.silu(acc)

    out = pl.BlockSpec((tm, G), lambda j, i: (i, j))
    return pl.pallas_call(
        body, name="conv_fwd", grid=(SSM_CONV_DIM // G, S // tm),
        in_specs=[pl.BlockSpec((tm, G), lambda j, i: (i, off + j)),
                  pl.BlockSpec((8, G), lambda j, i: (jnp.maximum(i * (tm // 8) - 1, 0), off + j)),
                  pl.BlockSpec((SSM_CONV, G), lambda j, i: (0, j)), pl.BlockSpec((1, G), lambda j, i: (0, j))],
        out_specs=[out, out], out_shape=[jax.ShapeDtypeStruct((S, SSM_CONV_DIM), F32)] * 2,
        compiler_params=_cp("parallel", "parallel"),
    )(proj, proj, w, b)


def conv_bwd(proj, pre, dact, w, tm=512):
    S = proj.shape[0]
    tm = _fit(S, tm)
    G = 512
    off = P_XBC // G
    nb = S // tm

    def body(x_ref, xp_ref, pre_ref, pren_ref, d_ref, dn_ref, w_ref, dx_ref, dw_ref, db_ref):
        i = pl.program_id(1)
        n = tm + 8
        dpre = d_ref[...] * _silu_grad(pre_ref[...])
        dpre_next = dn_ref[...] * _silu_grad(pren_ref[...]) * (i < nb - 1).astype(F32)
        dext = jnp.concatenate([dpre, dpre_next], axis=0)
        xext = jnp.concatenate([xp_ref[...] * (i > 0).astype(F32), x_ref[...]], axis=0)
        dx = jnp.zeros((tm, G), F32)
        dws = []
        for j in range(SSM_CONV):
            dx = dx + w_ref[j:j + 1, :] * pltpu.roll(dext, (n - (3 - j)) % n, 0)[0:tm]
            dws.append(jnp.sum(dpre * pltpu.roll(xext, (n - 5 - j) % n, 0)[0:tm], axis=0, keepdims=True))
        dx_ref[...] = dx.astype(dx_ref.dtype)

        @pl.when(i == 0)
        def _():
            dw_ref[...] = jnp.zeros_like(dw_ref)
            db_ref[...] = jnp.zeros_like(db_ref)

        dw_ref[...] += jnp.concatenate(dws, axis=0)
        db_ref[...] += jnp.sum(dpre, axis=0, keepdims=True)

    cur = pl.BlockSpec((tm, G), lambda j, i: (i, j))
    nxt = pl.BlockSpec((8, G), lambda j, i: (jnp.minimum((i + 1) * (tm // 8), S // 8 - 1), j))
    return pl.pallas_call(
        body, name="conv_bwd", grid=(SSM_CONV_DIM // G, nb),
        in_specs=[pl.BlockSpec((tm, G), lambda j, i: (i, off + j)),
                  pl.BlockSpec((8, G), lambda j, i: (jnp.maximum(i * (tm // 8) - 1, 0), off + j)),
                  cur, nxt, cur, nxt, pl.BlockSpec((SSM_CONV, G), lambda j, i: (0, j))],
        out_specs=[cur, pl.BlockSpec((SSM_CONV, G), lambda j, i: (0, j)), pl.BlockSpec((1, G), lambda j, i: (0, j))],
        out_shape=[jax.ShapeDtypeStruct((S, SSM_CONV_DIM), BF16), jax.ShapeDtypeStruct((SSM_CONV, SSM_CONV_DIM), F32),
                   jax.ShapeDtypeStruct((1, SSM_CONV_DIM), F32)],
        compiler_params=_cp("parallel", "arbitrary"),
    )(proj, proj, pre, pre, dact, dact, w)


def ssd_head(x, bm, cm, dtr, bias, alog, dsk, h_prev):
    Q = x.shape[0]
    dt = jax.nn.softplus(dtr + bias)
    a = -jnp.exp(alog)
    acum = cumsum_rows(jnp.broadcast_to(dt * a, (Q, Q)))
    r = lax.broadcasted_iota(jnp.int32, (Q, Q), 0)
    c = lax.broadcasted_iota(jnp.int32, (Q, Q), 1)
    arow = jnp.sum(jnp.where(r == c, acum, 0.0), axis=0, keepdims=True)
    decay = jnp.where(c <= r, jnp.exp(jnp.minimum(acum - arow, 0.0)), 0.0)
    xdt = x * dt
    y_diag = _dot(_dot(cm, bm, "nt") * decay, xdt, "nn")
    a64 = acum[:, 0:SSM_HEADDIM]
    a_last = a64[Q - 1:Q]
    states = _dot(xdt * jnp.exp(a_last - a64), bm, "tn")
    h_new = h_prev * jnp.exp(a_last[:, 0:1]) + states
    y_off = _dot(cm, h_prev, "nt") * jnp.exp(a64)
    return y_diag + y_off + x * dsk, h_new


def _ssd_specs(S, rev):
    Q = _fit(S, SSD_Q)
    nc = S // Q
    ci = (lambda c: nc - 1 - c) if rev else (lambda c: c)
    x = pl.BlockSpec((Q, 512), lambda g, c: (ci(c), g))
    bm = pl.BlockSpec((Q, 128), lambda g, c: (ci(c), SSM_INNER // 128 + g))
    cm = pl.BlockSpec((Q, 128), lambda g, c: (ci(c), SSM_INNER // 128 + SSM_GROUPS + g))
    dtr = pl.BlockSpec((None, Q, 8), lambda g, c: (g, ci(c), 0))
    par = pl.BlockSpec((None, 1, 8), lambda g, c: (g, 0, 0))
    hs = pl.BlockSpec((None, 8, SSM_HEADDIM, SSM_STATE), lambda g, c: (ci(c), g, 0, 0))
    return Q, nc, x, bm, cm, dtr, par, hs


def ssd_scan_fwd(act, dtr, bias, alog, dsk):
    S = act.shape[0]
    Q, nc, x_s, bm_s, cm_s, dtr_s, par_s, hs_s = _ssd_specs(S, False)

    def body(x_ref, bm_ref, cm_ref, dtr_ref, b_ref, a_ref, d_ref, y_ref, hs_ref, st):
        @pl.when(pl.program_id(1) == 0)
        def _():
            st[...] = jnp.zeros_like(st)

        for j in range(8):
            hs_ref[j] = st[j]
            y, h_new = ssd_head(x_ref[:, j * 64:(j + 1) * 64], bm_ref[...], cm_ref[...], dtr_ref[:, j:j + 1],
                                b_ref[:, j:j + 1], a_ref[:, j:j + 1], d_ref[:, j:j + 1], st[j])
            y_ref[:, j * 64:(j + 1) * 64] = y
            st[j] = h_new

    return pl.pallas_call(
        body, name="ssd_scan_fwd", grid=(SSM_GROUPS, nc),
        in_specs=[x_s, bm_s, cm_s, dtr_s, par_s, par_s, par_s], out_specs=[x_s, hs_s],
        out_shape=[jax.ShapeDtypeStruct((S, SSM_INNER), F32), jax.ShapeDtypeStruct((nc, SSM_HEADS, SSM_HEADDIM, SSM_STATE), F32)],
        scratch_shapes=[pltpu.VMEM((8, SSM_HEADDIM, SSM_STATE), F32)],
        compiler_params=_cp("parallel", "arbitrary"),
    )(act, act, act, dtr, bias, alog, dsk)


def ssd_scan_bwd(act, dtr, bias, alog, dsk, hs, dy):
    S = act.shape[0]
    Q, nc, x_s, bm_s, cm_s, dtr_s, par_s, hs_s = _ssd_specs(S, True)
    g_s = pl.BlockSpec((Q, 128), lambda g, c: (nc - 1 - c, g))

    def body(x_ref, bm_ref, cm_ref, dtr_ref, b_ref, a_ref, d_ref, hs_ref, dy_ref,
             dx_ref, dbm_ref, dcm_ref, ddtr_ref, db_ref, da_ref, dd_ref, dst):
        @pl.when(pl.program_id(1) == 0)
        def _():
            dst[...] = jnp.zeros_like(dst)
            db_ref[...] = jnp.zeros_like(db_ref)
            da_ref[...] = jnp.zeros_like(da_ref)
            dd_ref[...] = jnp.zeros_like(dd_ref)

        dbm = jnp.zeros((Q, 128), F32)
        dcm = jnp.zeros((Q, 128), F32)
        for j in range(8):
            cs = slice(j * 64, (j + 1) * 64)
            one = slice(j, j + 1)
            _, vjp = jax.vjp(ssd_head, x_ref[:, cs], bm_ref[...], cm_ref[...], dtr_ref[:, one],
                             b_ref[:, one], a_ref[:, one], d_ref[:, one], hs_ref[j])
            dx, gb, gc, gdt, gbias, galog, gdsk, gh = vjp((dy_ref[:, cs], dst[j]))
            dx_ref[:, cs] = dx
            dbm = dbm + gb
            dcm = dcm + gc
            ddtr_ref[:, one] = gdt
            db_ref[:, one] += gbias
            da_ref[:, one] += galog
            dd_ref[:, one] += gdsk
            dst[j] = gh
        dbm_ref[...] = dbm
        dcm_ref[...] = dcm

    return pl.pallas_call(
        body, name="ssd_scan_bwd", grid=(SSM_GROUPS, nc),
        in_specs=[x_s, bm_s, cm_s, dtr_s, par_s, par_s, par_s, hs_s, x_s],
        out_specs=[x_s, g_s, g_s, dtr_s, par_s, par_s, par_s],
        out_shape=[jax.ShapeDtypeStruct((S, SSM_INNER), F32), jax.ShapeDtypeStruct((S, 1024), F32), jax.ShapeDtypeStruct((S, 1024), F32),
                   jax.ShapeDtypeStruct((SSM_GROUPS, S, 8), F32)] + [jax.ShapeDtypeStruct((SSM_GROUPS, 1, 8), F32)] * 3,
        scratch_shapes=[pltpu.VMEM((8, SSM_HEADDIM, SSM_STATE), F32)],
        compiler_params=_cp("parallel", "arbitrary"),
    )(act, act, act, dtr, bias, alog, dsk, hs, dy)


def _w_in_to_local(w):
    parts = [w[..., 0:1024], w[..., 1088:19520], w[..., 19584:25728], w[..., 1024:1088], w[..., 19520:19584],
             jnp.zeros(w.shape[:-1] + (IN_PAD - IN_DIM,), w.dtype)]
    return jnp.concatenate(parts, axis=-1)


def _w_in_from_local(g):
    return jnp.concatenate([g[..., 0:1024], g[..., 25600:25664], g[..., 1024:19456], g[..., 25664:25728], g[..., 19456:25600]], axis=-1)


def _w_uq_to_local(w):
    lead = w.shape[:-1]
    w = w.reshape(lead + (MLA_HEADS, 192))
    nope = w[..., :128].reshape(lead + (8, 256))
    rope = w[..., 128:].reshape(lead + (8, 128))
    return jnp.concatenate([nope, rope], axis=-1).reshape(lead + (3072,))


def _w_uq_from_local(g):
    lead = g.shape[:-1]
    g = g.reshape(lead + (8, 384))
    nope = g[..., :256].reshape(lead + (MLA_HEADS, 128))
    rope = g[..., 256:].reshape(lead + (MLA_HEADS, 64))
    return jnp.concatenate([nope, rope], axis=-1).reshape(lead + (3072,))


BIG = ("ffn1_wi", "ffn1_wo", "w_in", "mla_w_uq", "mla_w_ukv", "w_o_mla", "w_o_hgrn", "w_o_ssm", "w_out", "ffn2_wi", "ffn2_wo")
COL_SHARDED = ("ffn1_wi", "w_in", "mla_w_uq", "mla_w_ukv", "ffn2_wi")
SMALL = ("ffn1_norm", "mix_norm", "mla_q_norm", "mla_kv_norm", "hgrn_lb_logits", "hgrn_norm", "ssm_conv_b", "ssm_a_log",
         "ssm_dt_bias", "ssm_d", "ssm_norm", "ffn2_norm")


def _ffn_fwd(tag, x, norm_w, wi, wo):
    (h,) = rowwise(tag + "_norm", f_rmsnorm, [(x, D_MODEL, 0)], [(norm_w, D_MODEL, 0)], [(D_MODEL, D_MODEL, BF16)])
    gu = matmul(tag + "_wi", h, wi)
    (a,) = rowwise(tag + "_act", f_swiglu, [(gu, 512, 0), (gu, 512, D_FF // 512)], [], [(D_FF, 512, BF16)], n_groups=D_FF // 512)
    out = matmul(tag + "_wo", a, wo, alpha=0.5, res=x)
    return out, (x, h, gu, a)


def _ffn_bwd(tag, dx, saved, norm_w, wi, wo):
    x, h, gu, a = saved
    da = matmul(tag + "_da", dx, wo, "nt", alpha=0.5)
    dwo = matmul(tag + "_dwo", a, dx, "tn", alpha=0.5)
    dg, du = rowwise_bwd(tag + "_act_bwd", f_swiglu, [(gu, 512, 0), (gu, 512, D_FF // 512)], [], [(da, 512)],
                         [(0, D_FF, BF16), (1, D_FF, BF16)], [], n_groups=D_FF // 512)
    dgu = jnp.concatenate([dg, du], axis=1)
    dh = matmul(tag + "_dh", dgu, wi, "nt")
    dwi = matmul(tag + "_dwi", h, dgu, "tn")
    dx_in, dnorm = rowwise_bwd(tag + "_norm_bwd", f_rmsnorm, [(x, D_MODEL, 0)], [(norm_w, D_MODEL, 0)], [(dh, D_MODEL)],
                               [(0, D_MODEL, F32)], [0], adds=[dx])
    return dx_in, dnorm, dwi, dwo


def _ssm_params(W):
    return [W[k].reshape(SSM_GROUPS, 1, 8) for k in ("ssm_dt_bias", "ssm_a_log", "ssm_d")]


def _dt_cols(proj):
    S = proj.shape[0]
    return proj[:, P_KPEDT + 64:P_KPEDT + 128].reshape(S, SSM_GROUPS, 8).transpose(1, 0, 2)


def _mix_fwd(x, W, lb, tabs):
    cosf, sinf = tabs
    (h,) = rowwise("mix_norm", f_rmsnorm, [(x, D_MODEL, 0)], [(W["mix_norm"], D_MODEL, 0)], [(D_MODEL, D_MODEL, BF16)])
    proj = matmul("w_in", h, W["w_in"])
    (qn,) = rowwise("q_norm", f_rmsnorm, [(proj, 512, 0)], [(W["mla_q_norm"], 512, 0)], [(512, 512, BF16)])
    (kvn,) = rowwise("kv_norm", f_rmsnorm, [(proj, 512, 1)], [(W["mla_kv_norm"], 512, 0)], [(512, 512, BF16)])
    q = matmul("w_uq", qn, W["mla_w_uq"])
    kv = matmul("w_ukv", kvn, W["mla_w_ukv"])
    qc = mla_q_prep(q, cosf, sinf)
    kc, vb = mla_k_prep(kv, proj, cosf, sinf)
    o_a, lse = attention_fwd(qc, kc, vb)
    y_a = matmul("w_o_mla", o_a, W["w_o_mla"])
    o_h, hst = hgrn_scan_fwd(proj, lb)
    (pre_b,) = rowwise("hgrn_out", f_hgrn_out, [(o_h, 128, 0), (proj, 128, P_HGATE // 128)], [(W["hgrn_norm"], 128, 0)],
                       [(HG_WIDTH, 128, BF16)], n_groups=HG_HEADS, tm=1024)
    y_b = matmul("w_o_hgrn", pre_b, W["w_o_hgrn"])
    act, pre = conv_fwd(proj, W["ssm_conv_w"], W["ssm_conv_b"])
    dtr = _dt_cols(proj)
    y_s, hs = ssd_scan_fwd(act, dtr, *_ssm_params(W))
    (pre_c,) = rowwise("ssm_out", f_ssm_out, [(y_s, 512, 0), (proj, 512, P_Z // 512)], [(W["ssm_norm"], 512, 0)],
                       [(SSM_INNER, 512, BF16)], n_groups=SSM_GROUPS)
    y_c = matmul("w_o_ssm", pre_c, W["w_o_ssm"])
    g0 = P_GATES // 512
    (merged,) = rowwise("merge", f_merge, [(proj, 512, g0), (proj, 512, g0 + 4), (proj, 512, g0 + 8), (y_a, 512, 0), (y_b, 512, 0), (y_c, 512, 0)],
                        [], [(D_MODEL, 512, BF16)], n_groups=4)
    out = matmul("w_out", merged, W["w_out"], res=x)
    return out, (x, h, proj, qn, kvn, qc, kc, vb, o_a, lse, y_a, o_h, hst, pre_b, y_b, act, pre, dtr, y_s, hs, pre_c, y_c, merged)


def _mix_bwd(dx, saved, W, lb, tabs):
    cosf, sinf = tabs
    (x, h, proj, qn, kvn, qc, kc, vb, o_a, lse, y_a, o_h, hst, pre_b, y_b, act, pre, dtr, y_s, hs, pre_c, y_c, merged) = saved
    S = x.shape[0]
    g = {}
    dmerged = matmul("d_merged", dx, W["w_out"], "nt")
    g["w_out"] = matmul("dw_out", merged, dx, "tn")
    g0 = P_GATES // 512
    dga, dgb, dgc, dya, dyb, dyc = rowwise_bwd(
        "merge_bwd", f_merge, [(proj, 512, g0), (proj, 512, g0 + 4), (proj, 512, g0 + 8), (y_a, 512, 0), (y_b, 512, 0), (y_c, 512, 0)],
        [], [(dmerged, 512)], [(k, D_MODEL, BF16) for k in range(6)], [], n_groups=4)
    do_a = matmul("d_o_mla", dya, W["w_o_mla"], "nt", out_dtype=BF16)
    g["w_o_mla"] = matmul("dw_o_mla", o_a, dya, "tn")
    delta = attention_delta(o_a, do_a)
    dqc = attention_bwd_dq(qc, kc, vb, do_a, lse, delta)
    dkc, dv = attention_bwd_dkv(qc, kc, vb, do_a, lse, delta)
    dq = mla_q_prep_bwd(dqc, cosf, sinf)
    dkv, dpe = mla_k_prep_bwd(dkc, dv, cosf, sinf)
    dqn = matmul("d_qn", dq, W["mla_w_uq"], "nt")
    g["mla_w_uq"] = matmul("dw_uq", qn, dq, "tn")
    dkvn = matmul("d_kvn", dkv, W["mla_w_ukv"], "nt")
    g["mla_w_ukv"] = matmul("dw_ukv", kvn, dkv, "tn")
    dq_lat, g["mla_q_norm"] = rowwise_bwd("q_norm_bwd", f_rmsnorm, [(proj, 512, 0)], [(W["mla_q_norm"], 512, 0)], [(dqn, 512)],
                                          [(0, 512, BF16)], [0])
    dkv_lat, g["mla_kv_norm"] = rowwise_bwd("kv_norm_bwd", f_rmsnorm, [(proj, 512, 1)], [(W["mla_kv_norm"], 512, 0)], [(dkvn, 512)],
                                            [(0, 512, BF16)], [0])
    do_b = matmul("d_o_hgrn", dyb, W["w_o_hgrn"], "nt")
    g["w_o_hgrn"] = matmul("dw_o_hgrn", pre_b, dyb, "tn")
    do_h, dhgate, g["hgrn_norm"] = rowwise_bwd(
        "hgrn_out_bwd", f_hgrn_out, [(o_h, 128, 0), (proj, 128, P_HGATE // 128)], [(W["hgrn_norm"], 128, 0)], [(do_b, 128)],
        [(0, HG_WIDTH, F32), (1, HG_WIDTH, BF16)], [0], n_groups=HG_HEADS, tm=1024)
    dhq, dhf, dhi, dlb = hgrn_scan_bwd(proj, lb, hst, do_h)
    do_c = matmul("d_o_ssm", dyc, W["w_o_ssm"], "nt")
    g["w_o_ssm"] = matmul("dw_o_ssm", pre_c, dyc, "tn")
    dy_s, dz, g["ssm_norm"] = rowwise_bwd(
        "ssm_out_bwd", f_ssm_out, [(y_s, 512, 0), (proj, 512, P_Z // 512)], [(W["ssm_norm"], 512, 0)], [(do_c, 512)],
        [(0, SSM_INNER, F32), (1, SSM_INNER, BF16)], [0], n_groups=SSM_GROUPS)
    dxs, dbm, dcm, ddtr, dbias, dalog, ddsk = ssd_scan_bwd(act, dtr, *_ssm_params(W), hs, dy_s)
    g["ssm_dt_bias"], g["ssm_a_log"], g["ssm_d"] = (v.reshape(1, SSM_HEADS) for v in (dbias, dalog, ddsk))
    dxbc, g["ssm_conv_w"], g["ssm_conv_b"] = conv_bwd(proj, pre, jnp.concatenate([dxs, dbm, dcm], axis=1), W["ssm_conv_w"])
    ddt = ddtr.transpose(1, 0, 2).reshape(S, SSM_HEADS)
    dproj = jnp.concatenate([dq_lat, dkv_lat, dhq, dhf, dhi, dhgate, dz, dxbc, dga, dgb, dgc, dpe[:, :64].astype(BF16),
                             ddt.astype(BF16), jnp.zeros((S, IN_PAD - IN_DIM), BF16)], axis=1)
    dh = matmul("d_h_mix", dproj, W["w_in"], "nt")
    g["w_in"] = matmul("dw_in", h, dproj, "tn")
    dx_in, g["mix_norm"] = rowwise_bwd("mix_norm_bwd", f_rmsnorm, [(x, D_MODEL, 0)], [(W["mix_norm"], D_MODEL, 0)], [(dh, D_MODEL)],
                                       [(0, D_MODEL, F32)], [0], adds=[dx])
    return dx_in, g, dlb


def local_step(x, target, Wl, small, final_norm):
    S = x.shape[0]
    tabs = rope_tables(S)
    (lbs,) = rowwise("lower_bounds", f_lower_bounds, [(small["hgrn_lb_logits"], HG_WIDTH, 0)], [], [(HG_WIDTH, HG_WIDTH, F32)])
    saved = []
    for l in range(DEPTH):
        W = dict(Wl[l])
        for k in SMALL:
            W[k] = small[k][l:l + 1]
        lb = lbs[l:l + 1]
        x, s1 = _ffn_fwd("ffn1", x, W["ffn1_norm"], W["ffn1_wi"], W["ffn1_wo"])
        x, s2 = _mix_fwd(x, W, lb, tabs)
        x, s3 = _ffn_fwd("ffn2", x, W["ffn2_norm"], W["ffn2_wi"], W["ffn2_wo"])
        saved.append((W, lb, s1, s2, s3))
    dx, dfinal, loss = loss_head(x, target, final_norm.reshape(1, D_MODEL))
    grads = [None] * DEPTH
    dlbs = [None] * DEPTH
    for l in reversed(range(DEPTH)):
        W, lb, s1, s2, s3 = saved[l]
        g = {}
        dx, g["ffn2_norm"], g["ffn2_wi"], g["ffn2_wo"] = _ffn_bwd("ffn2", dx, s3, W["ffn2_norm"], W["ffn2_wi"], W["ffn2_wo"])
        dx, gm, dlbs[l] = _mix_bwd(dx, s2, W, lb, tabs)
        g.update(gm)
        dx, g["ffn1_norm"], g["ffn1_wi"], g["ffn1_wo"] = _ffn_bwd("ffn1", dx, s1, W["ffn1_norm"], W["ffn1_wi"], W["ffn1_wo"])
        grads[l] = g
    (dlogits,) = rowwise_bwd("lower_bounds_bwd", f_lower_bounds, [(small["hgrn_lb_logits"], HG_WIDTH, 0)], [],
                             [(jnp.concatenate(dlbs, axis=0), HG_WIDTH)], [(0, HG_WIDTH, F32)], [])
    return loss, dx, grads, dlogits, dfinal


ANY = pl.BlockSpec(memory_space=pl.ANY)


def _coords():
    return lax.axis_index("x"), lax.axis_index("y"), lax.axis_index("c")


def _exchange_call(name, body, src, out_shape, n_copies):
    return pl.pallas_call(
        body, name=name, in_specs=[ANY], out_specs=ANY, out_shape=out_shape,
        scratch_shapes=[pltpu.SemaphoreType.DMA((n_copies,)), pltpu.SemaphoreType.DMA((n_copies,)), pltpu.SemaphoreType.DMA],
    )(src)


def chip_allgather(name, src):
    def body(src_ref, out_ref, send_sems, recv_sems, local_sem):
        x, y, c = _coords()
        me = 2 * x + y
        peers = [(1 - x, y), (x, 1 - y), (1 - x, 1 - y)]

        def copy(k, chip):
            return pltpu.make_async_remote_copy(src_ref, out_ref.at[chip], send_sems.at[k], recv_sems.at[k],
                                                device_id=(*peers[k], c), device_id_type=MESH)

        local = pltpu.make_async_copy(src_ref, out_ref.at[me], local_sem)
        local.start()
        sends = [copy(k, me) for k in range(3)]
        for s in sends:
            s.start()
        for k, (px, py) in enumerate(peers):
            copy(k, 2 * px + py).wait_recv()
        for s in sends:
            s.wait_send()
        local.wait()

    return _exchange_call(name, body, src, jax.ShapeDtypeStruct((4,) + src.shape, src.dtype), 3)


def sibling_exchange(name, src, pick_half):
    shape = src.shape[1:] if pick_half else src.shape

    def body(src_ref, out_ref, send_sems, recv_sems, local_sem):
        x, y, c = _coords()
        cp = pltpu.make_async_remote_copy(src_ref.at[1 - c] if pick_half else src_ref, out_ref, send_sems.at[0], recv_sems.at[0],
                                          device_id=(x, y, 1 - c), device_id_type=MESH)
        cp.start()
        cp.wait_recv()
        cp.wait_send()

    return _exchange_call(name, body, src, jax.ShapeDtypeStruct(shape, src.dtype), 1)


def chip_exchange(name, src):
    def body(src_ref, out_ref, send_sems, recv_sems, local_sem):
        x, y, c = _coords()
        peers = [(1 - x, y), (x, 1 - y), (1 - x, 1 - y)]
        copies = [pltpu.make_async_remote_copy(src_ref.at[2 * px + py], out_ref.at[k], send_sems.at[k], recv_sems.at[k],
                                               device_id=(px, py, c), device_id_type=MESH) for k, (px, py) in enumerate(peers)]
        for cp in copies:
            cp.start()
        for cp in copies:
            cp.wait_recv()
        for cp in copies:
            cp.wait_send()

    return _exchange_call(name, body, src, jax.ShapeDtypeStruct((3,) + src.shape[1:], src.dtype), 3)


def device_allgather(name, src):
    def body(src_ref, out_ref, send_sems, recv_sems, local_sem):
        x, y, c = _coords()
        me = 4 * x + 2 * y + c
        peers = [(x ^ (m >> 2), y ^ ((m >> 1) & 1), c ^ (m & 1)) for m in range(1, 8)]

        def copy(k, dev):
            return pltpu.make_async_remote_copy(src_ref, out_ref.at[dev], send_sems.at[k], recv_sems.at[k],
                                                device_id=peers[k], device_id_type=MESH)

        local = pltpu.make_async_copy(src_ref, out_ref.at[me], local_sem)
        local.start()
        sends = [copy(k, me) for k in range(7)]
        for s in sends:
            s.start()
        for k, (px, py, pc) in enumerate(peers):
            copy(k, 4 * px + 2 * py + pc).wait_recv()
        for s in sends:
            s.wait_send()
        local.wait()

    return _exchange_call(name, body, src, jax.ShapeDtypeStruct((8,) + src.shape, src.dtype), 7)


PACK_COLS = 1024
PACK_TR = 832


def add_halves(g, recv, c):
    _, n, R, C = g.shape
    tr = _fit_rows(R, PACK_TR)

    def body(c_ref, g_ref, r_ref, o_ref):
        o_ref[...] = (g_ref[...].astype(F32) + r_ref[...].astype(F32)).astype(o_ref.dtype)

    blk = pl.BlockSpec((None, tr, C), lambda j, i, c_ref: (j, i, 0))
    return pl.pallas_call(
        body, name="add_halves", out_shape=jax.ShapeDtypeStruct((n, R, C), g.dtype),
        grid_spec=pltpu.PrefetchScalarGridSpec(
            num_scalar_prefetch=1, grid=(n, R // tr),
            in_specs=[pl.BlockSpec((None, None, tr, C), lambda j, i, c_ref: (c_ref[0], j, i, 0)), blk], out_specs=blk),
        compiler_params=_cp("parallel", "parallel"),
    )(c, g, recv)


def sum_chips(own, recv, me):
    _, R, C = own.shape
    tr = _fit_rows(R, PACK_TR)

    def body(me_ref, o_ref, r_ref, out_ref):
        acc = o_ref[...].astype(F32)
        for k in range(3):
            acc = acc + r_ref[k].astype(F32)
        out_ref[...] = acc

    return pl.pallas_call(
        body, name="sum_chips", out_shape=jax.ShapeDtypeStruct((R, C), F32),
        grid_spec=pltpu.PrefetchScalarGridSpec(
            num_scalar_prefetch=1, grid=(R // tr,),
            in_specs=[pl.BlockSpec((None, tr, C), lambda i, me_ref: (me_ref[0], i, 0)), pl.BlockSpec((3, tr, C), lambda i, me_ref: (0, i, 0))],
            out_specs=pl.BlockSpec((tr, C), lambda i, me_ref: (i, 0))),
        compiler_params=_cp("parallel"),
    )(me, own, recv)


def sum_devices(parts):
    _, R, C = parts.shape

    def body(p_ref, o_ref):
        acc = p_ref[0]
        for k in range(1, 8):
            acc = acc + p_ref[k]
        o_ref[...] = acc

    return pl.pallas_call(body, name="sum_devices", out_shape=jax.ShapeDtypeStruct((R, C), F32))(parts)


def _fit_rows(R, pref):
    for t in range(min(pref, R), 0, -1):
        if R % t == 0 and (t % 16 == 0 or t == R):
            return t
    raise ValueError((R, pref))


def adamw(name, w, g, m, v):
    shape = w.shape
    C = shape[-1]
    w2, g2, m2, v2 = (a.reshape(-1, C) for a in (w, g, m, v))
    R = w2.shape[0]
    tr = _fit_rows(R, max(8, (1 << 18) // C)) if R * C > (1 << 18) else R
    c1 = 1.0 - ADAM_B1 ** ADAM_STEP
    c2 = 1.0 - ADAM_B2 ** ADAM_STEP

    def body(w_ref, g_ref, m_ref, v_ref, d_ref, mo_ref, vo_ref):
        gg = g_ref[...]
        mn = ADAM_B1 * m_ref[...] + (1.0 - ADAM_B1) * gg
        vn = ADAM_B2 * v_ref[...] + (1.0 - ADAM_B2) * jnp.square(gg)
        d_ref[...] = -ADAM_LR * ((mn / c1) / (jnp.sqrt(vn / c2) + ADAM_EPS) + ADAM_WD * w_ref[...])
        mo_ref[...] = mn
        vo_ref[...] = vn

    blk = pl.BlockSpec((tr, C), lambda i: (i, 0))
    outs = pl.pallas_call(
        body, name=name, grid=(R // tr,), in_specs=[blk] * 4, out_specs=[blk] * 3,
        out_shape=[jax.ShapeDtypeStruct((R, C), F32)] * 3, compiler_params=_cp("parallel"),
    )(w2, g2, m2, v2)
    return tuple(o.reshape(shape) for o in outs)


def _unshard(name, parts):
    n, L, r, c = parts.shape
    if name in COL_SHARDED:
        return parts.transpose(1, 2, 0, 3).reshape(L, r, n * c)
    return parts.transpose(1, 0, 2, 3).reshape(L, n * r, c)


def _shard(name, full):
    L, R, C = full.shape
    if name in COL_SHARDED:
        return full.reshape(L, R, 4, C // 4).transpose(2, 0, 1, 3)
    return full.reshape(L, 4, R // 4, C).transpose(1, 0, 2, 3)


def _to_local(name, w):
    if name == "w_in":
        return _w_in_to_local(w)
    if name == "mla_w_uq":
        return _w_uq_to_local(w)
    return w


def _from_local(name, g):
    if name == "w_in":
        return _w_in_from_local(g)
    if name == "mla_w_uq":
        return _w_uq_from_local(g)
    return g


TWIN_WEIGHTS = ("ffn1_norm", "ffn1_wi", "ffn1_wo", "mix_norm", "w_in", "mla_q_norm", "mla_w_uq", "mla_kv_norm", "mla_w_ukv",
                "hgrn_lb_logits", "hgrn_norm", "ssm_conv_w", "ssm_conv_b", "ssm_a_log", "ssm_dt_bias", "ssm_d", "ssm_norm",
                "w_o_mla", "w_o_hgrn", "w_o_ssm", "w_out", "ffn2_norm", "ffn2_wi", "ffn2_wo", "final_norm")
SMALL_PACK = SMALL + ("ssm_conv_w", "final_norm")


def _pad_rows(flat, cols):
    n = flat.shape[0]
    rows = -(-n // cols)
    rows = -(-rows // 8) * 8
    return jnp.concatenate([flat, jnp.zeros((rows * cols - n,), flat.dtype)]).reshape(rows, cols)


def kernel(x, ffn1_norm, ffn1_wi, ffn1_wo, mix_norm, w_in, mla_q_norm, mla_w_uq, mla_kv_norm, mla_w_ukv, hgrn_lb_logits, hgrn_norm, ssm_conv_w, ssm_conv_b, ssm_a_log, ssm_dt_bias, ssm_d, ssm_norm, w_o_mla, w_o_hgrn, w_o_ssm, w_out, ffn2_norm, ffn2_wi, ffn2_wo, final_norm, loss_target, m_ffn1_norm, m_ffn1_wi, m_ffn1_wo, m_mix_norm, m_w_in, m_mla_q_norm, m_mla_w_uq, m_mla_kv_norm, m_mla_w_ukv, m_hgrn_lb_logits, m_hgrn_norm, m_ssm_conv_w, m_ssm_conv_b, m_ssm_a_log, m_ssm_dt_bias, m_ssm_d, m_ssm_norm, m_w_o_mla, m_w_o_hgrn, m_w_o_ssm, m_w_out, m_ffn2_norm, m_ffn2_wi, m_ffn2_wo, m_final_norm, v_ffn1_norm, v_ffn1_wi, v_ffn1_wo, v_mix_norm, v_w_in, v_mla_q_norm, v_mla_w_uq, v_mla_kv_norm, v_mla_w_ukv, v_hgrn_lb_logits, v_hgrn_norm, v_ssm_conv_w, v_ssm_conv_b, v_ssm_a_log, v_ssm_dt_bias, v_ssm_d, v_ssm_norm, v_w_o_mla, v_w_o_hgrn, v_w_o_ssm, v_w_out, v_ffn2_norm, v_ffn2_wi, v_ffn2_wo, v_final_norm):
    args = dict(locals())
    w = {n: args[n] for n in TWIN_WEIGHTS}
    m = {n: args["m_" + n] for n in TWIN_WEIGHTS}
    v = {n: args["v_" + n] for n in TWIN_WEIGHTS}
    xi, yi, ci = _coords()
    chip = 2 * xi + yi

    pack = jnp.concatenate([w[n].astype(BF16).reshape(-1) for n in BIG]).reshape(-1, PACK_COLS)
    gathered = chip_allgather("weights_allgather", pack).reshape(4, -1)
    full, off = {}, 0
    for n in BIG:
        size = w[n].size
        full[n] = _to_local(n, _unshard(n, gathered[:, off:off + size].reshape((4,) + w[n].shape)))
        off += size
    conv_parts = device_allgather("conv_allgather", _pad_rows(w["ssm_conv_w"].reshape(-1), 128))
    conv_full = jnp.concatenate(
        [conv_parts[2 * j].reshape(-1)[:w["ssm_conv_w"].size].reshape(w["ssm_conv_w"].shape) for j in range(4)], axis=-1)
    Wl = [dict({n: full[n][l] for n in BIG}, ssm_conv_w=conv_full[l]) for l in range(DEPTH)]
    small = {n: w[n] for n in SMALL}

    loss, grad_x, grads, dlogits, dfinal = local_step(x[0], loss_target[0], Wl, small, w["final_norm"])
    loss = lax.psum(loss[0, 0], ("x", "y", "c"))

    gpack = jnp.concatenate(
        [_shard(n, _from_local(n, jnp.stack([grads[l][n] for l in range(DEPTH)]))).reshape(4, -1) for n in BIG], axis=1)
    rows_half = gpack.shape[1] // (2 * PACK_COLS)
    gpack = gpack.astype(BF16).reshape(4, 2, rows_half, PACK_COLS).transpose(1, 0, 2, 3)
    from_sibling = sibling_exchange("grad_halves_exchange", gpack, True)
    pair_sum = add_halves(gpack, from_sibling, ci.astype(jnp.int32).reshape(1))
    from_chips = chip_exchange("grad_chip_exchange", pair_sum)
    mine = sum_chips(pair_sum, from_chips, chip.astype(jnp.int32).reshape(1))
    other = sibling_exchange("grad_reduced_exchange", mine, False)
    halves = [jnp.where(ci == 0, mine, other), jnp.where(ci == 0, other, mine)]
    reduced = jnp.concatenate(halves, axis=0).reshape(-1)
    g, off = {}, 0
    for n in BIG:
        g[n] = reduced[off:off + w[n].size].reshape(w[n].shape)
        off += w[n].size

    sg = {n: jnp.concatenate([grads[l][n] for l in range(DEPTH)], axis=0) for n in SMALL if n != "hgrn_lb_logits"}
    sg["hgrn_lb_logits"] = dlogits
    sg["ssm_conv_w"] = jnp.stack([grads[l]["ssm_conv_w"] for l in range(DEPTH)])
    sg["final_norm"] = dfinal
    spack = _pad_rows(jnp.concatenate([sg[n].reshape(-1) for n in SMALL_PACK]), 128)
    ssum = sum_devices(device_allgather("small_grads_allgather", spack)).reshape(-1)
    off = 0
    for n in SMALL_PACK:
        size = sg[n].size
        g[n] = ssum[off:off + size].reshape(sg[n].shape)
        off += size
    shard_cols = w["ssm_conv_w"].shape[-1]
    g["ssm_conv_w"] = lax.dynamic_slice_in_dim(g["ssm_conv_w"], chip * shard_cols, shard_cols, axis=2)
    g = {n: g[n].reshape(w[n].shape) for n in TWIN_WEIGHTS}

    upd = {n: adamw("adamw_" + n, w[n], g[n], m[n], v[n]) for n in TWIN_WEIGHTS}
    return (loss, grad_x[None], *[g[n] for n in TWIN_WEIGHTS], *[upd[n][0] for n in TWIN_WEIGHTS],
            *[upd[n][1] for n in TWIN_WEIGHTS], *[upd[n][2] for n in TWIN_WEIGHTS])
```

```python
import functools
import math

import jax
import jax.numpy as jnp
import numpy as np
from jax import lax
from jax.experimental import pallas as pl
from jax.experimental.pallas import tpu as pltpu

F32 = jnp.float32
BF16 = jnp.bfloat16
MESH = pl.DeviceIdType.MESH

D_MODEL = 2048
DEPTH = 2
CHUNK = 64
EPS = 1e-6
MLA_HEADS, MLA_Q_RANK, MLA_KV_RANK, MLA_NOPE, MLA_ROPE, MLA_V = 16, 512, 512, 128, 64, 128
ROPE_THETA = 10000.0
HG_HEADS, HG_DK = 16, 128
HG_WIDTH = HG_HEADS * HG_DK
SSM_INNER, SSM_HEADDIM, SSM_HEADS, SSM_GROUPS, SSM_STATE, SSM_CONV = 4096, 64, 64, 8, 128, 4
SSM_CONV_DIM = SSM_INNER + 2 * SSM_GROUPS * SSM_STATE
D_FF = 5632
IN_DIM = 25728
ADAM_LR, ADAM_B1, ADAM_B2, ADAM_EPS, ADAM_WD, ADAM_STEP = 0.001, 0.9, 0.999, 1e-08, 0.01, 10

P_QLAT, P_KVLAT, P_HQ, P_HF, P_HI, P_HGATE, P_Z, P_XBC, P_GATES, P_KPEDT = (
    0, 512, 1024, 3072, 5120, 7168, 9216, 13312, 19456, 25600)
IN_PAD = 26624

VMEM_LIMIT_V7X = 48 << 20
SSD_Q = 256
HG_HB = 4
NEG = -1e30


def _cp(*sem):
    return pltpu.CompilerParams(dimension_semantics=sem, vmem_limit_bytes=VMEM_LIMIT_V7X)


def _fit(n, pref):
    if n <= pref:
        return n
    for t in range(pref, 0, -128):
        if n % t == 0:
            return t
    raise ValueError((n, pref))


_DIMS = {"nn": (((1,), (0,)), ((), ())), "nt": (((1,), (1,)), ((), ())), "tn": (((0,), (0,)), ((), ()))}


def _dot(a, b, mode):
    return lax.dot_general(a.astype(BF16), b.astype(BF16), _DIMS[mode], preferred_element_type=F32)


def matmul(name, a, b, mode="nn", out_dtype=F32, alpha=1.0, res=None, tm=1024, tn=1024, tk=2048):
    if mode == "nn":
        (M, K), (K2, N) = a.shape, b.shape
    elif mode == "nt":
        (M, K), (N, K2) = a.shape, b.shape
    else:
        (K, M), (K2, N) = a.shape, b.shape
    assert K == K2, (name, a.shape, b.shape, mode)
    tm, tn, tk = _fit(M, tm), _fit(N, tn), _fit(K, tk)
    nk = K // tk
    a_spec = pl.BlockSpec((tk, tm), lambda i, j, k: (k, i)) if mode == "tn" else pl.BlockSpec((tm, tk), lambda i, j, k: (i, k))
    b_spec = pl.BlockSpec((tn, tk), lambda i, j, k: (j, k)) if mode == "nt" else pl.BlockSpec((tk, tn), lambda i, j, k: (k, j))
    o_spec = pl.BlockSpec((tm, tn), lambda i, j, k: (i, j))
    has_res = res is not None

    def body(*refs):
        a_ref, b_ref = refs[0], refs[1]
        o_ref = refs[3] if has_res else refs[2]

        def finish(v):
            if alpha != 1.0:
                v = v * alpha
            if has_res:
                v = v + refs[2][...].astype(F32)
            o_ref[...] = v.astype(o_ref.dtype)

        if nk == 1:
            finish(_dot(a_ref[...], b_ref[...], mode))
            return
        acc = refs[-1]
        k = pl.program_id(2)

        @pl.when(k == 0)
        def _():
            acc[...] = _dot(a_ref[...], b_ref[...], mode)

        @pl.when(jnp.logical_and(k > 0, k < nk - 1))
        def _():
            acc[...] += _dot(a_ref[...], b_ref[...], mode)

        @pl.when(k == nk - 1)
        def _():
            finish(acc[...] + _dot(a_ref[...], b_ref[...], mode))

    ins = [a, b] + ([res] if has_res else [])
    return pl.pallas_call(
        body, name=name, grid=(M // tm, N // tn, nk),
        in_specs=[a_spec, b_spec] + ([o_spec] if has_res else []), out_specs=o_spec,
        out_shape=jax.ShapeDtypeStruct((M, N), out_dtype),
        scratch_shapes=[pltpu.VMEM((tm, tn), F32)] if nk > 1 else [],
        compiler_params=_cp("parallel", "parallel", "arbitrary"),
    )(*ins)


def _row_specs(rows, consts, tm):
    specs = []
    for arr, w, off in rows:
        specs.append(pl.BlockSpec((tm, w), functools.partial(lambda j, i, off: (i, off + j), off=off)))
    for arr, w, off in consts:
        specs.append(pl.BlockSpec((arr.shape[0], w), functools.partial(lambda j, i, off: (0, off + j), off=off)))
    return specs


def rowwise(name, fn, rows, consts, outs, n_groups=1, tm=512):
    S = rows[0][0].shape[0]
    tm = _fit(S, tm)
    n_in = len(rows) + len(consts)

    def body(*refs):
        vals = fn(*[r[...].astype(F32) for r in refs[:n_in]])
        for o_ref, v in zip(refs[n_in:], vals):
            o_ref[...] = v.astype(o_ref.dtype)

    return pl.pallas_call(
        body, name=name, grid=(n_groups, S // tm),
        in_specs=_row_specs(rows, consts, tm),
        out_specs=[pl.BlockSpec((tm, w), lambda j, i: (i, j)) for _, w, _ in outs],
        out_shape=[jax.ShapeDtypeStruct((S, W), dt) for W, _, dt in outs],
        compiler_params=_cp("parallel", "parallel"),
    )(*[r[0] for r in rows], *[c[0] for c in consts])


def rowwise_bwd(name, fn, rows, consts, cts, row_grads, const_grads, adds=None, n_groups=1, tm=512):
    S = rows[0][0].shape[0]
    tm = _fit(S, tm)
    adds = adds or {}
    add_idx = list(adds)
    n_r, n_c, n_ct, n_add = len(rows), len(consts), len(cts), len(adds)

    def body(*refs):
        ins = [r[...].astype(F32) for r in refs[:n_r + n_c]]
        ct = tuple(r[...].astype(F32) for r in refs[n_r + n_c:n_r + n_c + n_ct])
        add_refs = refs[n_r + n_c + n_ct:n_r + n_c + n_ct + n_add]
        out_refs = refs[n_r + n_c + n_ct + n_add:]
        _, vjp = jax.vjp(fn, *ins)
        g = list(vjp(ct))
        for a_ref, idx in zip(add_refs, add_idx):
            g[idx] = g[idx] + a_ref[...].astype(F32)
        for q, (idx, _, _) in enumerate(row_grads):
            out_refs[q][...] = g[idx].astype(out_refs[q].dtype)
        first = pl.program_id(1) == 0
        for q, idx in enumerate(const_grads):
            o_ref = out_refs[len(row_grads) + q]

            @pl.when(first)
            def _(o_ref=o_ref):
                o_ref[...] = jnp.zeros_like(o_ref)

            o_ref[...] += g[n_r + idx]

    in_specs = _row_specs(rows, consts, tm)
    in_specs += [pl.BlockSpec((tm, w), lambda j, i: (i, j)) for _, w in cts]
    in_specs += [pl.BlockSpec((tm, rows[idx][1]), lambda j, i: (i, j)) for idx in add_idx]
    out_specs = [pl.BlockSpec((tm, rows[idx][1]), lambda j, i: (i, j)) for idx, _, _ in row_grads]
    out_specs += [pl.BlockSpec((consts[idx][0].shape[0], consts[idx][1]), lambda j, i: (0, j)) for idx in const_grads]
    out_shape = [jax.ShapeDtypeStruct((S, W), dt) for _, W, dt in row_grads]
    out_shape += [jax.ShapeDtypeStruct((consts[idx][0].shape[0], consts[idx][1] * n_groups), F32) for idx in const_grads]
    return pl.pallas_call(
        body, name=name, grid=(n_groups, S // tm), in_specs=in_specs, out_specs=out_specs, out_shape=out_shape,
        compiler_params=_cp("parallel", "arbitrary"),
    )(*[r[0] for r in rows], *[c[0] for c in consts], *[c[0] for c in cts], *adds.values())


def f_rmsnorm(x, w):
    return (x * lax.rsqrt(jnp.mean(x * x, axis=-1, keepdims=True) + EPS) * w,)


def f_swiglu(g, u):
    return (jax.nn.silu(g) * u,)


def f_hgrn_out(o, g, w):
    return (o * lax.rsqrt(jnp.mean(o * o, axis=-1, keepdims=True) + EPS) * w * jax.nn.silu(g),)


def f_ssm_out(y, z, w):
    y = y * jax.nn.silu(z)
    return (y * lax.rsqrt(jnp.mean(y * y, axis=-1, keepdims=True) + EPS) * w,)


def f_merge(ga, gb, gc, ya, yb, yc):
    return (jax.nn.sigmoid(ga) * ya + jax.nn.sigmoid(gb) * yb + jax.nn.sigmoid(gc) * yc,)


def f_lower_bounds(logits):
    p = jax.nn.softmax(logits, axis=0)
    rows = [jnp.zeros_like(p[0:1])]
    for l in range(1, DEPTH):
        rows.append(rows[-1] + p[l:l + 1])
    return (jnp.concatenate(rows, axis=0),)


def loss_head(x, target, w, tm=512):
    S, D = x.shape
    tm = _fit(S, tm)

    def loss_fn(xb, wb, tb):
        (y,) = f_rmsnorm(xb, wb)
        return 0.5 * jnp.sum(jnp.mean(jnp.square(y - tb), axis=-1))

    def body(x_ref, t_ref, w_ref, dx_ref, dxb_ref, dw_ref, loss_ref):
        @pl.when(pl.program_id(0) == 0)
        def _():
            dw_ref[...] = jnp.zeros_like(dw_ref)
            loss_ref[...] = jnp.zeros_like(loss_ref)

        l, (dx, dw) = jax.value_and_grad(loss_fn, argnums=(0, 1))(x_ref[...], w_ref[...], t_ref[...])
        dx_ref[...] = dx
        dxb_ref[...] = dx.astype(BF16)
        dw_ref[...] += dw
        loss_ref[...] += jnp.full(loss_ref.shape, l, F32)

    row = pl.BlockSpec((tm, D), lambda i: (i, 0))
    vec = pl.BlockSpec((1, D), lambda i: (0, 0))
    return pl.pallas_call(
        body, name="loss_head", grid=(S // tm,), in_specs=[row, row, vec],
        out_specs=[row, row, vec, pl.BlockSpec((1, 128), lambda i: (0, 0))],
        out_shape=[jax.ShapeDtypeStruct((S, D), F32), jax.ShapeDtypeStruct((S, D), BF16), jax.ShapeDtypeStruct((1, D), F32),
                   jax.ShapeDtypeStruct((1, 128), F32)],
        compiler_params=_cp("arbitrary"),
    )(x, target, w)


def rope_tables(S):
    inv = 1.0 / (ROPE_THETA ** (jnp.arange(0, MLA_ROPE, 2, dtype=F32) / MLA_ROPE))
    ang = jnp.arange(S, dtype=F32)[:, None] * inv[None, :]
    c, s = jnp.cos(ang), jnp.sin(ang)
    return jnp.tile(c, (1, 4)), jnp.concatenate([-s, s, -s, s], axis=1)


def _rope128(x, cosf, sinf):
    lane = lax.broadcasted_iota(jnp.int32, x.shape, 1)
    swapped = jnp.where((lane & 32) == 0, pltpu.roll(x, 96, 1), pltpu.roll(x, 32, 1))
    return x * cosf + swapped * sinf


def mla_q_prep(q, cosf, sinf, tm=512):
    S = q.shape[0]
    tm = _fit(S, tm)

    def body(q_ref, c_ref, s_ref, o_ref):
        x = q_ref[...]
        r = _rope128(x[:, 256:384], c_ref[...], s_ref[...])
        lane = lax.broadcasted_iota(jnp.int32, r.shape, 1)
        z = jnp.zeros_like(r)
        o_ref[...] = jnp.concatenate(
            [x[:, 0:128], jnp.where(lane < 64, r, z), x[:, 128:256], jnp.where(lane >= 64, r, z)], axis=1).astype(BF16)

    tab = pl.BlockSpec((tm, 128), lambda j, i: (i, 0))
    return pl.pallas_call(
        body, name="mla_q_prep", grid=(8, S // tm),
        in_specs=[pl.BlockSpec((tm, 384), lambda j, i: (i, j)), tab, tab],
        out_specs=pl.BlockSpec((tm, 512), lambda j, i: (i, j)),
        out_shape=jax.ShapeDtypeStruct((S, 4096), BF16), compiler_params=_cp("parallel", "parallel"),
    )(q, cosf, sinf)


def mla_q_prep_bwd(dqc, cosf, sinf, tm=512):
    S = dqc.shape[0]
    tm = _fit(S, tm)

    def body(d_ref, c_ref, s_ref, o_ref):
        d = d_ref[...]
        lane = lax.broadcasted_iota(jnp.int32, (tm, 128), 1)
        dr = jnp.where(lane < 64, d[:, 128:256], d[:, 384:512])
        o_ref[...] = jnp.concatenate([d[:, 0:128], d[:, 256:384], _rope128(dr, c_ref[...], -s_ref[...])], axis=1).astype(BF16)

    tab = pl.BlockSpec((tm, 128), lambda j, i: (i, 0))
    return pl.pallas_call(
        body, name="mla_q_prep_bwd", grid=(8, S // tm),
        in_specs=[pl.BlockSpec((tm, 512), lambda j, i: (i, j)), tab, tab],
        out_specs=pl.BlockSpec((tm, 384), lambda j, i: (i, j)),
        out_shape=jax.ShapeDtypeStruct((S, 3072), BF16), compiler_params=_cp("parallel", "parallel"),
    )(dqc, cosf, sinf)


def mla_k_prep(kv, proj, cosf, sinf, tm=512):
    S = kv.shape[0]
    tm = _fit(S, tm)

    def body(kv_ref, pe_ref, c_ref, s_ref, k_ref, v_ref):
        x = kv_ref[...]
        r = _rope128(pe_ref[...], c_ref[...], s_ref[...])
        lane = lax.broadcasted_iota(jnp.int32, r.shape, 1)
        r2 = jnp.where(lane < 64, r, pltpu.roll(r, 64, 1))
        k_ref[...] = jnp.concatenate([x[:, 0:128], r2, x[:, 256:384], r2], axis=1).astype(BF16)
        v_ref[...] = jnp.concatenate([x[:, 128:256], x[:, 384:512]], axis=1).astype(BF16)

    tab = pl.BlockSpec((tm, 128), lambda j, i: (i, 0))
    return pl.pallas_call(
        body, name="mla_k_prep", grid=(8, S // tm),
        in_specs=[pl.BlockSpec((tm, 512), lambda j, i: (i, j)), pl.BlockSpec((tm, 128), lambda j, i: (i, P_KPEDT // 128)), tab, tab],
        out_specs=[pl.BlockSpec((tm, 512), lambda j, i: (i, j)), pl.BlockSpec((tm, 256), lambda j, i: (i, j))],
        out_shape=[jax.ShapeDtypeStruct((S, 4096), BF16), jax.ShapeDtypeStruct((S, 2048), BF16)],
        compiler_params=_cp("parallel", "parallel"),
    )(kv, proj, cosf, sinf)


def mla_k_prep_bwd(dkc, dv, cosf, sinf, tm=512):
    S = dkc.shape[0]
    tm = _fit(S, tm)

    def body(dk_ref, dv_ref, c_ref, s_ref, dkv_ref, dpe_ref):
        dk, dvv = dk_ref[...], dv_ref[...]
        dkv_ref[...] = jnp.concatenate([dk[:, 0:128], dvv[:, 0:128], dk[:, 256:384], dvv[:, 128:256]], axis=1).astype(BF16)
        d2 = dk[:, 128:256] + dk[:, 384:512]
        lane = lax.broadcasted_iota(jnp.int32, d2.shape, 1)
        dr = jnp.where(lane < 64, d2 + pltpu.roll(d2, 64, 1), 0.0)
        dpe = jnp.where(lane < 64, _rope128(dr, c_ref[...], -s_ref[...]), 0.0)

        @pl.when(pl.program_id(1) == 0)
        def _():
            dpe_ref[...] = jnp.zeros_like(dpe_ref)

        dpe_ref[...] += dpe

    tab = pl.BlockSpec((tm, 128), lambda i, j: (i, 0))
    return pl.pallas_call(
        body, name="mla_k_prep_bwd", grid=(S // tm, 8),
        in_specs=[pl.BlockSpec((tm, 512), lambda i, j: (i, j)), pl.BlockSpec((tm, 256), lambda i, j: (i, j)), tab, tab],
        out_specs=[pl.BlockSpec((tm, 512), lambda i, j: (i, j)), tab],
        out_shape=[jax.ShapeDtypeStruct((S, 4096), BF16), jax.ShapeDtypeStruct((S, 128), F32)],
        compiler_params=_cp("parallel", "arbitrary"),
    )(dkc, dv, cosf, sinf)


ATT_SCALE = (MLA_NOPE + MLA_ROPE) ** -0.5


def _att_scores(q, k, qi, ki, t):
    s = _dot(q, k, "nt") * ATT_SCALE
    rows = qi * t + lax.broadcasted_iota(jnp.int32, (t, t), 0)
    cols = ki * t + lax.broadcasted_iota(jnp.int32, (t, t), 1)
    shift = CHUNK.bit_length() - 1
    return jnp.where((cols >> shift) <= (rows >> shift), s, NEG)


def attention_fwd(qc, kc, vb, t=1024):
    S = qc.shape[0]
    t = _fit(S, t)
    n = S // t

    def body(q_ref, k_ref, v_ref, o_ref, lse_ref, m_s, l_s, acc_s):
        qi, ki = pl.program_id(1), pl.program_id(2)

        @pl.when(ki == 0)
        def _():
            m_s[...] = jnp.full_like(m_s, NEG)
            l_s[...] = jnp.zeros_like(l_s)
            acc_s[...] = jnp.zeros_like(acc_s)

        @pl.when(ki <= qi)
        def _():
            s = _att_scores(q_ref[...], k_ref[...], qi, ki, t)
            m_prev = m_s[...]
            m_new = jnp.maximum(m_prev, jnp.max(s, axis=1, keepdims=True))
            alpha = jnp.exp(m_prev - m_new)
            p = jnp.exp(s - m_new)
            l_s[...] = alpha * l_s[...] + jnp.sum(p, axis=1, keepdims=True)
            acc_s[...] = alpha * acc_s[...] + _dot(p, v_ref[...], "nn")
            m_s[...] = m_new

        @pl.when(ki == qi)
        def _():
            o_ref[...] = (acc_s[...] / l_s[...]).astype(o_ref.dtype)
            lse_ref[...] = m_s[...] + jnp.log(l_s[...])

    return pl.pallas_call(
        body, name="attention_fwd", grid=(MLA_HEADS, n, n),
        in_specs=[pl.BlockSpec((t, 256), lambda h, i, j: (i, h)),
                  pl.BlockSpec((t, 256), lambda h, i, j: (jnp.minimum(i, j), h)),
                  pl.BlockSpec((t, 128), lambda h, i, j: (jnp.minimum(i, j), h))],
        out_specs=[pl.BlockSpec((t, 128), lambda h, i, j: (i, h)), pl.BlockSpec((None, t, 1), lambda h, i, j: (h, i, 0))],
        out_shape=[jax.ShapeDtypeStruct((S, 2048), BF16), jax.ShapeDtypeStruct((MLA_HEADS, S, 1), F32)],
        scratch_shapes=[pltpu.VMEM((t, 1), F32), pltpu.VMEM((t, 1), F32), pltpu.VMEM((t, 128), F32)],
        compiler_params=_cp("parallel", "parallel", "arbitrary"),
    )(qc, kc, vb)


def attention_delta(o, do, t=1024):
    S = o.shape[0]
    t = _fit(S, t)

    def body(o_ref, do_ref, d_ref):
        d_ref[...] = jnp.sum(o_ref[...].astype(F32) * do_ref[...].astype(F32), axis=1, keepdims=True)

    blk = pl.BlockSpec((t, 128), lambda h, i: (i, h))
    return pl.pallas_call(
        body, name="attention_delta", grid=(MLA_HEADS, S // t), in_specs=[blk, blk],
        out_specs=pl.BlockSpec((None, t, 1), lambda h, i: (h, i, 0)),
        out_shape=jax.ShapeDtypeStruct((MLA_HEADS, S, 1), F32), compiler_params=_cp("parallel", "parallel"),
    )(o, do)


def attention_bwd_dq(qc, kc, vb, do, lse, delta, t=1024):
    S = qc.shape[0]
    t = _fit(S, t)
    n = S // t

    def body(q_ref, k_ref, v_ref, do_ref, lse_ref, dl_ref, dq_ref, acc):
        qi, ki = pl.program_id(1), pl.program_id(2)

        @pl.when(ki == 0)
        def _():
            acc[...] = jnp.zeros_like(acc)

        @pl.when(ki <= qi)
        def _():
            p = jnp.exp(_att_scores(q_ref[...], k_ref[...], qi, ki, t) - lse_ref[...])
            dp = _dot(do_ref[...], v_ref[...], "nt")
            ds = p * (dp - dl_ref[...]) * ATT_SCALE
            acc[...] += _dot(ds, k_ref[...], "nn")

        @pl.when(ki == qi)
        def _():
            dq_ref[...] = acc[...]

    stat = pl.BlockSpec((None, t, 1), lambda h, i, j: (h, i, 0))
    return pl.pallas_call(
        body, name="attention_bwd_dq", grid=(MLA_HEADS, n, n),
        in_specs=[pl.BlockSpec((t, 256), lambda h, i, j: (i, h)),
                  pl.BlockSpec((t, 256), lambda h, i, j: (jnp.minimum(i, j), h)),
                  pl.BlockSpec((t, 128), lambda h, i, j: (jnp.minimum(i, j), h)),
                  pl.BlockSpec((t, 128), lambda h, i, j: (i, h)), stat, stat],
        out_specs=pl.BlockSpec((t, 256), lambda h, i, j: (i, h)),
        out_shape=jax.ShapeDtypeStruct((S, 4096), F32),
        scratch_shapes=[pltpu.VMEM((t, 256), F32)],
        compiler_params=_cp("parallel", "parallel", "arbitrary"),
    )(qc, kc, vb, do, lse, delta)


def attention_bwd_dkv(qc, kc, vb, do, lse, delta, t=1024):
    S = qc.shape[0]
    t = _fit(S, t)
    n = S // t

    def body(q_ref, k_ref, v_ref, do_ref, lse_ref, dl_ref, dk_ref, dv_ref, dk_acc, dv_acc):
        ki, qi = pl.program_id(1), pl.program_id(2)

        @pl.when(qi == 0)
        def _():
            dk_acc[...] = jnp.zeros_like(dk_acc)
            dv_acc[...] = jnp.zeros_like(dv_acc)

        @pl.when(qi >= ki)
        def _():
            p = jnp.exp(_att_scores(q_ref[...], k_ref[...], qi, ki, t) - lse_ref[...])
            dv_acc[...] += _dot(p, do_ref[...], "tn")
            dp = _dot(do_ref[...], v_ref[...], "nt")
            ds = p * (dp - dl_ref[...]) * ATT_SCALE
            dk_acc[...] += _dot(ds, q_ref[...], "tn")

        @pl.when(qi == n - 1)
        def _():
            dk_ref[...] = dk_acc[...]
            dv_ref[...] = dv_acc[...]

    stat = pl.BlockSpec((None, t, 1), lambda h, j, i: (h, jnp.maximum(i, j), 0))
    return pl.pallas_call(
        body, name="attention_bwd_dkv", grid=(MLA_HEADS, n, n),
        in_specs=[pl.BlockSpec((t, 256), lambda h, j, i: (jnp.maximum(i, j), h)),
                  pl.BlockSpec((t, 256), lambda h, j, i: (j, h)),
                  pl.BlockSpec((t, 128), lambda h, j, i: (j, h)),
                  pl.BlockSpec((t, 128), lambda h, j, i: (jnp.maximum(i, j), h)), stat, stat],
        out_specs=[pl.BlockSpec((t, 256), lambda h, j, i: (j, h)), pl.BlockSpec((t, 128), lambda h, j, i: (j, h))],
        out_shape=[jax.ShapeDtypeStruct((S, 4096), F32), jax.ShapeDtypeStruct((S, 2048), F32)],
        scratch_shapes=[pltpu.VMEM((t, 256), F32), pltpu.VMEM((t, 128), F32)],
        compiler_params=_cp("parallel", "parallel", "arbitrary"),
    )(qc, kc, vb, do, lse, delta)


def _scan_rows(x, reverse):
    n = x.shape[0]
    row = lax.broadcasted_iota(jnp.int32, x.shape, 0)
    d = 1
    while d < n:
        if reverse:
            x = x + jnp.where(row < n - d, pltpu.roll(x, n - d, 0), 0.0)
        else:
            x = x + jnp.where(row >= d, pltpu.roll(x, d, 0), 0.0)
        d *= 2
    return x


@jax.custom_vjp
def cumsum_rows(x):
    return _scan_rows(x, False)


cumsum_rows.defvjp(lambda x: (_scan_rows(x, False), None), lambda _, g: (_scan_rows(g, True),))


def hgrn_chunk(q_in, f_in, v, lb, state_t):
    f = lb + (1.0 - lb) * jax.nn.sigmoid(f_in)
    q = jax.nn.silu(q_in) * HG_DK ** -0.5
    k = 1.0 - f
    b = cumsum_rows(jnp.log(f))
    b_last = b[CHUNK - 1:CHUNK]
    b_mid = b[CHUNK // 2 - 1:CHUNK // 2]
    r = lax.broadcasted_iota(jnp.int32, (CHUNK, CHUNK), 0)
    c = lax.broadcasted_iota(jnp.int32, (CHUNK, CHUNK), 1)
    att = jnp.where(c <= r, _dot(q * jnp.exp(b - b_mid), k * jnp.exp(b_mid - b), "nt"), 0.0)
    o = _dot(q * jnp.exp(b), state_t, "nt") + _dot(att, v, "nn")
    new_state_t = state_t * jnp.exp(b_last) + _dot(v, k * jnp.exp(b_last - b), "tn")
    return o, new_state_t


def hgrn_scan_fwd(proj, lb):
    S = proj.shape[0]
    nc = S // CHUNK
    W = HG_HB * 128

    def body(q_ref, f_ref, v_ref, lb_ref, o_ref, hst_ref, st):
        @pl.when(pl.program_id(1) == 0)
        def _():
            st[...] = jnp.zeros_like(st)

        for h in range(HG_HB):
            cs = slice(h * 128, (h + 1) * 128)
            hst_ref[h] = st[h]
            o, new = hgrn_chunk(q_ref[:, cs], f_ref[:, cs], v_ref[:, cs], lb_ref[:, cs], st[h])
            o_ref[:, cs] = o
            st[h] = new

    def seg(off):
        return pl.BlockSpec((CHUNK, W), functools.partial(lambda g, c, off: (c, off + g), off=off // W))

    return pl.pallas_call(
        body, name="hgrn_scan_fwd", grid=(HG_HEADS // HG_HB, nc),
        in_specs=[seg(P_HQ), seg(P_HF), seg(P_HI), pl.BlockSpec((1, W), lambda g, c: (0, g))],
        out_specs=[pl.BlockSpec((CHUNK, W), lambda g, c: (c, g)), pl.BlockSpec((None, HG_HB, 128, 128), lambda g, c: (c, g, 0, 0))],
        out_shape=[jax.ShapeDtypeStruct((S, HG_WIDTH), F32), jax.ShapeDtypeStruct((nc, HG_HEADS, 128, 128), F32)],
        scratch_shapes=[pltpu.VMEM((HG_HB, 128, 128), F32)],
        compiler_params=_cp("parallel", "arbitrary"),
    )(proj, proj, proj, lb)


def hgrn_scan_bwd(proj, lb, hst, do):
    S = proj.shape[0]
    nc = S // CHUNK
    W = HG_HB * 128

    def body(q_ref, f_ref, v_ref, lb_ref, hst_ref, do_ref, dq_ref, df_ref, dv_ref, dlb_ref, dst):
        @pl.when(pl.program_id(1) == 0)
        def _():
            dst[...] = jnp.zeros_like(dst)
            dlb_ref[...] = jnp.zeros_like(dlb_ref)

        for h in range(HG_HB):
            cs = slice(h * 128, (h + 1) * 128)
            _, vjp = jax.vjp(hgrn_chunk, q_ref[:, cs], f_ref[:, cs], v_ref[:, cs], lb_ref[:, cs], hst_ref[h])
            dq, df, dv, dlb, dstate = vjp((do_ref[:, cs], dst[h]))
            dq_ref[:, cs] = dq.astype(dq_ref.dtype)
            df_ref[:, cs] = df.astype(df_ref.dtype)
            dv_ref[:, cs] = dv.astype(dv_ref.dtype)
            dlb_ref[:, cs] += dlb
            dst[h] = dstate

    def seg(off):
        return pl.BlockSpec((CHUNK, W), functools.partial(lambda g, c, off: (nc - 1 - c, off + g), off=off // W))

    row = pl.BlockSpec((CHUNK, W), lambda g, c: (nc - 1 - c, g))
    vec = pl.BlockSpec((1, W), lambda g, c: (0, g))
    return pl.pallas_call(
        body, name="hgrn_scan_bwd", grid=(HG_HEADS // HG_HB, nc),
        in_specs=[seg(P_HQ), seg(P_HF), seg(P_HI), vec,
                  pl.BlockSpec((None, HG_HB, 128, 128), lambda g, c: (nc - 1 - c, g, 0, 0)), row],
        out_specs=[row, row, row, vec],
        out_shape=[jax.ShapeDtypeStruct((S, HG_WIDTH), BF16)] * 3 + [jax.ShapeDtypeStruct((1, HG_WIDTH), F32)],
        scratch_shapes=[pltpu.VMEM((HG_HB, 128, 128), F32)],
        compiler_params=_cp("parallel", "arbitrary"),
    )(proj, proj, proj, lb, hst, do)


def _silu_grad(x):
    s = jax.nn.sigmoid(x)
    return s * (1.0 + x * (1.0 - s))


def conv_fwd(proj, w, b, tm=512):
    S = proj.shape[0]
    tm = _fit(S, tm)
    G = 512
    off = P_XBC // G

    def body(cur_ref, prev_ref, w_ref, b_ref, act_ref, pre_ref):
        prev = prev_ref[...] * (pl.program_id(1) > 0).astype(F32)
        ext = jnp.concatenate([prev, cur_ref[...]], axis=0)
        n = tm + 8
        acc = b_ref[...] + jnp.zeros((tm, G), F32)
        for j in range(SSM_CONV):
            acc = acc + w_ref[j:j + 1, :] * pltpu.roll(ext, (n - 5 - j) % n, 0)[0:tm]
        pre_ref[...] = acc
        act_ref[...] = jax.nn.silu(acc)

    out = pl.BlockSpec((tm, G), lambda j, i: (i, j))
    return pl.pallas_call(
        body, name="conv_fwd", grid=(SSM_CONV_DIM // G, S // tm),
        in_specs=[pl.BlockSpec((tm, G), lambda j, i: (i, off + j)),
                  pl.BlockSpec((8, G), lambda j, i: (jnp.maximum(i * (tm // 8) - 1, 0), off + j)),
                  pl.BlockSpec((SSM_CONV, G), lambda j, i: (0, j)), pl.BlockSpec((1, G), lambda j, i: (0, j))],
        out_specs=[out, out], out_shape=[jax.ShapeDtypeStruct((S, SSM_CONV_DIM), F32)] * 2,
        compiler_params=_cp("parallel", "parallel"),
    )(proj, proj, w, b)


def conv_bwd(proj, pre, dact, w, tm=512):
    S = proj.shape[0]
    tm = _fit(S, tm)
    G = 512
    off = P_XBC // G
    nb = S // tm

    def body(x_ref, xp_ref, pre_ref, pren_ref, d_ref, dn_ref, w_ref, dx_ref, dw_ref, db_ref):
        i = pl.program_id(1)
        n = tm + 8
        dpre = d_ref[...] * _silu_grad(pre_ref[...])
        dpre_next = dn_ref[...] * _silu_grad(pren_ref[...]) * (i < nb - 1).astype(F32)
        dext = jnp.concatenate([dpre, dpre_next], axis=0)
        xext = jnp.concatenate([xp_ref[...] * (i > 0).astype(F32), x_ref[...]], axis=0)
        dx = jnp.zeros((tm, G), F32)
        dws = []
        for j in range(SSM_CONV):
            dx = dx + w_ref[j:j + 1, :] * pltpu.roll(dext, (n - (3 - j)) % n, 0)[0:tm]
            dws.append(jnp.sum(dpre * pltpu.roll(xext, (n - 5 - j) % n, 0)[0:tm], axis=0, keepdims=True))
        dx_ref[...] = dx.astype(dx_ref.dtype)

        @pl.when(i == 0)
        def _():
            dw_ref[...] = jnp.zeros_like(dw_ref)
            db_ref[...] = jnp.zeros_like(db_ref)

        dw_ref[...] += jnp.concatenate(dws, axis=0)
        db_ref[...] += jnp.sum(dpre, axis=0, keepdims=True)

    cur = pl.BlockSpec((tm, G), lambda j, i: (i, j))
    nxt = pl.BlockSpec((8, G), lambda j, i: (jnp.minimum((i + 1) * (tm // 8), S // 8 - 1), j))
    return pl.pallas_call(
        body, name="conv_bwd", grid=(SSM_CONV_DIM // G, nb),
        in_specs=[pl.BlockSpec((tm, G), lambda j, i: (i, off + j)),
                  pl.BlockSpec((8, G), lambda j, i: (jnp.maximum(i * (tm // 8) - 1, 0), off + j)),
                  cur, nxt, cur, nxt, pl.BlockSpec((SSM_CONV, G), lambda j, i: (0, j))],
        out_specs=[cur, pl.BlockSpec((SSM_CONV, G), lambda j, i: (0, j)), pl.BlockSpec((1, G), lambda j, i: (0, j))],
        out_shape=[jax.ShapeDtypeStruct((S, SSM_CONV_DIM), BF16), jax.ShapeDtypeStruct((SSM_CONV, SSM_CONV_DIM), F32),
                   jax.ShapeDtypeStruct((1, SSM_CONV_DIM), F32)],
        compiler_params=_cp("parallel", "arbitrary"),
    )(proj, proj, pre, pre, dact, dact, w)


def ssd_head(x, bm, cm, dtr, bias, alog, dsk, h_prev):
    Q = x.shape[0]
    dt = jax.nn.softplus(dtr + bias)
    a = -jnp.exp(alog)
    acum = cumsum_rows(jnp.broadcast_to(dt * a, (Q, Q)))
    r = lax.broadcasted_iota(jnp.int32, (Q, Q), 0)
    c = lax.broadcasted_iota(jnp.int32, (Q, Q), 1)
    arow = jnp.sum(jnp.where(r == c, acum, 0.0), axis=0, keepdims=True)
    decay = jnp.where(c <= r, jnp.exp(jnp.minimum(acum - arow, 0.0)), 0.0)
    xdt = x * dt
    y_diag = _dot(_dot(cm, bm, "nt") * decay, xdt, "nn")
    a64 = acum[:, 0:SSM_HEADDIM]
    a_last = a64[Q - 1:Q]
    states = _dot(xdt * jnp.exp(a_last - a64), bm, "tn")
    h_new = h_prev * jnp.exp(a_last[:, 0:1]) + states
    y_off = _dot(cm, h_prev, "nt") * jnp.exp(a64)
    return y_diag + y_off + x * dsk, h_new


def _ssd_specs(S, rev):
    Q = _fit(S, SSD_Q)
    nc = S // Q
    ci = (lambda c: nc - 1 - c) if rev else (lambda c: c)
    x = pl.BlockSpec((Q, 512), lambda g, c: (ci(c), g))
    bm = pl.BlockSpec((Q, 128), lambda g, c: (ci(c), SSM_INNER // 128 + g))
    cm = pl.BlockSpec((Q, 128), lambda g, c: (ci(c), SSM_INNER // 128 + SSM_GROUPS + g))
    dtr = pl.BlockSpec((None, Q, 8), lambda g, c: (g, ci(c), 0))
    par = pl.BlockSpec((None, 1, 8), lambda g, c: (g, 0, 0))
    hs = pl.BlockSpec((None, 8, SSM_HEADDIM, SSM_STATE), lambda g, c: (ci(c), g, 0, 0))
    return Q, nc, x, bm, cm, dtr, par, hs


def ssd_scan_fwd(act, dtr, bias, alog, dsk):
    S = act.shape[0]
    Q, nc, x_s, bm_s, cm_s, dtr_s, par_s, hs_s = _ssd_specs(S, False)

    def body(x_ref, bm_ref, cm_ref, dtr_ref, b_ref, a_ref, d_ref, y_ref, hs_ref, st):
        @pl.when(pl.program_id(1) == 0)
        def _():
            st[...] = jnp.zeros_like(st)

        for j in range(8):
            hs_ref[j] = st[j]
            y, h_new = ssd_head(x_ref[:, j * 64:(j + 1) * 64], bm_ref[...], cm_ref[...], dtr_ref[:, j:j + 1],
                                b_ref[:, j:j + 1], a_ref[:, j:j + 1], d_ref[:, j:j + 1], st[j])
            y_ref[:, j * 64:(j + 1) * 64] = y
            st[j] = h_new

    return pl.pallas_call(
        body, name="ssd_scan_fwd", grid=(SSM_GROUPS, nc),
        in_specs=[x_s, bm_s, cm_s, dtr_s, par_s, par_s, par_s], out_specs=[x_s, hs_s],
        out_shape=[jax.ShapeDtypeStruct((S, SSM_INNER), F32), jax.ShapeDtypeStruct((nc, SSM_HEADS, SSM_HEADDIM, SSM_STATE), F32)],
        scratch_shapes=[pltpu.VMEM((8, SSM_HEADDIM, SSM_STATE), F32)],
        compiler_params=_cp("parallel", "arbitrary"),
    )(act, act, act, dtr, bias, alog, dsk)


def ssd_scan_bwd(act, dtr, bias, alog, dsk, hs, dy):
    S = act.shape[0]
    Q, nc, x_s, bm_s, cm_s, dtr_s, par_s, hs_s = _ssd_specs(S, True)
    g_s = pl.BlockSpec((Q, 128), lambda g, c: (nc - 1 - c, g))

    def body(x_ref, bm_ref, cm_ref, dtr_ref, b_ref, a_ref, d_ref, hs_ref, dy_ref,
             dx_ref, dbm_ref, dcm_ref, ddtr_ref, db_ref, da_ref, dd_ref, dst):
        @pl.when(pl.program_id(1) == 0)
        def _():
            dst[...] = jnp.zeros_like(dst)
            db_ref[...] = jnp.zeros_like(db_ref)
            da_ref[...] = jnp.zeros_like(da_ref)
            dd_ref[...] = jnp.zeros_like(dd_ref)

        dbm = jnp.zeros((Q, 128), F32)
        dcm = jnp.zeros((Q, 128), F32)
        for j in range(8):
            cs = slice(j * 64, (j + 1) * 64)
            one = slice(j, j + 1)
            _, vjp = jax.vjp(ssd_head, x_ref[:, cs], bm_ref[...], cm_ref[...], dtr_ref[:, one],
                             b_ref[:, one], a_ref[:, one], d_ref[:, one], hs_ref[j])
            dx, gb, gc, gdt, gbias, galog, gdsk, gh = vjp((dy_ref[:, cs], dst[j]))
            dx_ref[:, cs] = dx
            dbm = dbm + gb
            dcm = dcm + gc
            ddtr_ref[:, one] = gdt
            db_ref[:, one] += gbias
            da_ref[:, one] += galog
            dd_ref[:, one] += gdsk
            dst[j] = gh
        dbm_ref[...] = dbm
        dcm_ref[...] = dcm

    return pl.pallas_call(
        body, name="ssd_scan_bwd", grid=(SSM_GROUPS, nc),
        in_specs=[x_s, bm_s, cm_s, dtr_s, par_s, par_s, par_s, hs_s, x_s],
        out_specs=[x_s, g_s, g_s, dtr_s, par_s, par_s, par_s],
        out_shape=[jax.ShapeDtypeStruct((S, SSM_INNER), F32), jax.ShapeDtypeStruct((S, 1024), F32), jax.ShapeDtypeStruct((S, 1024), F32),
                   jax.ShapeDtypeStruct((SSM_GROUPS, S, 8), F32)] + [jax.ShapeDtypeStruct((SSM_GROUPS, 1, 8), F32)] * 3,
        scratch_shapes=[pltpu.VMEM((8, SSM_HEADDIM, SSM_STATE), F32)],
        compiler_params=_cp("parallel", "arbitrary"),
    )(act, act, act, dtr, bias, alog, dsk, hs, dy)


def _w_in_to_local(w):
    parts = [w[..., 0:1024], w[..., 1088:19520], w[..., 19584:25728], w[..., 1024:1088], w[..., 19520:19584],
             jnp.zeros(w.shape[:-1] + (IN_PAD - IN_DIM,), w.dtype)]
    return jnp.concatenate(parts, axis=-1)


def _w_in_from_local(g):
    return jnp.concatenate([g[..., 0:1024], g[..., 25600:25664], g[..., 1024:19456], g[..., 25664:25728], g[..., 19456:25600]], axis=-1)


def _w_uq_to_local(w):
    lead = w.shape[:-1]
    w = w.reshape(lead + (MLA_HEADS, 192))
    nope = w[..., :128].reshape(lead + (8, 256))
    rope = w[..., 128:].reshape(lead + (8, 128))
    return jnp.concatenate([nope, rope], axis=-1).reshape(lead + (3072,))


def _w_uq_from_local(g):
    lead = g.shape[:-1]
    g = g.reshape(lead + (8, 384))
    nope = g[..., :256].reshape(lead + (MLA_HEADS, 128))
    rope = g[..., 256:].reshape(lead + (MLA_HEADS, 64))
    return jnp.concatenate([nope, rope], axis=-1).reshape(lead + (3072,))


BIG = ("ffn1_wi", "ffn1_wo", "w_in", "mla_w_uq", "mla_w_ukv", "w_o_mla", "w_o_hgrn", "w_o_ssm", "w_out", "ffn2_wi", "ffn2_wo")
COL_SHARDED = ("ffn1_wi", "w_in", "mla_w_uq", "mla_w_ukv", "ffn2_wi")
SMALL = ("ffn1_norm", "mix_norm", "mla_q_norm", "mla_kv_norm", "hgrn_lb_logits", "hgrn_norm", "ssm_conv_b", "ssm_a_log",
         "ssm_dt_bias", "ssm_d", "ssm_norm", "ffn2_norm")


def _ffn_fwd(tag, x, norm_w, wi, wo):
    (h,) = rowwise(tag + "_norm", f_rmsnorm, [(x, D_MODEL, 0)], [(norm_w, D_MODEL, 0)], [(D_MODEL, D_MODEL, BF16)])
    gu = matmul(tag + "_wi", h, wi)
    (a,) = rowwise(tag + "_act", f_swiglu, [(gu, 512, 0), (gu, 512, D_FF // 512)], [], [(D_FF, 512, BF16)], n_groups=D_FF // 512)
    out = matmul(tag + "_wo", a, wo, alpha=0.5, res=x)
    return out, (x, h, gu, a)


def _ffn_bwd(tag, dx, dxb, saved, norm_w, wi, wo):
    x, h, gu, a = saved
    da = matmul(tag + "_da", dxb, wo, "nt", alpha=0.5)
    dwo = matmul(tag + "_dwo", a, dxb, "tn", alpha=0.5)
    dg, du = rowwise_bwd(tag + "_act_bwd", f_swiglu, [(gu, 512, 0), (gu, 512, D_FF // 512)], [], [(da, 512)],
                         [(0, D_FF, BF16), (1, D_FF, BF16)], [], n_groups=D_FF // 512)
    dgu = jnp.concatenate([dg, du], axis=1)
    dh = matmul(tag + "_dh", dgu, wi, "nt")
    dwi = matmul(tag + "_dwi", h, dgu, "tn")
    dx_in, dxb_in, dnorm = rowwise_bwd(tag + "_norm_bwd", f_rmsnorm, [(x, D_MODEL, 0)], [(norm_w, D_MODEL, 0)], [(dh, D_MODEL)],
                                       [(0, D_MODEL, F32), (0, D_MODEL, BF16)], [0], adds={0: dx}, tm=256)
    return dx_in, dxb_in, dnorm, dwi, dwo


def _ssm_params(W):
    return [W[k].reshape(SSM_GROUPS, 1, 8) for k in ("ssm_dt_bias", "ssm_a_log", "ssm_d")]


def _dt_cols(proj):
    S = proj.shape[0]
    return proj[:, P_KPEDT + 64:P_KPEDT + 128].reshape(S, SSM_GROUPS, 8).transpose(1, 0, 2)


def _mix_fwd(x, W, lb, tabs):
    cosf, sinf = tabs
    (h,) = rowwise("mix_norm", f_rmsnorm, [(x, D_MODEL, 0)], [(W["mix_norm"], D_MODEL, 0)], [(D_MODEL, D_MODEL, BF16)])
    proj = matmul("w_in", h, W["w_in"])
    (qn,) = rowwise("q_norm", f_rmsnorm, [(proj, 512, 0)], [(W["mla_q_norm"], 512, 0)], [(512, 512, BF16)])
    (kvn,) = rowwise("kv_norm", f_rmsnorm, [(proj, 512, 1)], [(W["mla_kv_norm"], 512, 0)], [(512, 512, BF16)])
    q = matmul("w_uq", qn, W["mla_w_uq"])
    kv = matmul("w_ukv", kvn, W["mla_w_ukv"])
    qc = mla_q_prep(q, cosf, sinf)
    kc, vb = mla_k_prep(kv, proj, cosf, sinf)
    o_a, lse = attention_fwd(qc, kc, vb)
    y_a = matmul("w_o_mla", o_a, W["w_o_mla"])
    o_h, hst = hgrn_scan_fwd(proj, lb)
    (pre_b,) = rowwise("hgrn_out", f_hgrn_out, [(o_h, 128, 0), (proj, 128, P_HGATE // 128)], [(W["hgrn_norm"], 128, 0)],
                       [(HG_WIDTH, 128, BF16)], n_groups=HG_HEADS, tm=1024)
    y_b = matmul("w_o_hgrn", pre_b, W["w_o_hgrn"])
    act, pre = conv_fwd(proj, W["ssm_conv_w"], W["ssm_conv_b"])
    dtr = _dt_cols(proj)
    y_s, hs = ssd_scan_fwd(act, dtr, *_ssm_params(W))
    (pre_c,) = rowwise("ssm_out", f_ssm_out, [(y_s, 512, 0), (proj, 512, P_Z // 512)], [(W["ssm_norm"], 512, 0)],
                       [(SSM_INNER, 512, BF16)], n_groups=SSM_GROUPS)
    y_c = matmul("w_o_ssm", pre_c, W["w_o_ssm"])
    g0 = P_GATES // 512
    (merged,) = rowwise("merge", f_merge, [(proj, 512, g0), (proj, 512, g0 + 4), (proj, 512, g0 + 8), (y_a, 512, 0), (y_b, 512, 0), (y_c, 512, 0)],
                        [], [(D_MODEL, 512, BF16)], n_groups=4)
    out = matmul("w_out", merged, W["w_out"], res=x)
    return out, (x, h, proj, qn, kvn, qc, kc, vb, o_a, lse, y_a, o_h, hst, pre_b, y_b, act, pre, dtr, y_s, hs, pre_c, y_c, merged)


def _mix_bwd(dx, dxb, saved, W, lb, tabs):
    cosf, sinf = tabs
    (x, h, proj, qn, kvn, qc, kc, vb, o_a, lse, y_a, o_h, hst, pre_b, y_b, act, pre, dtr, y_s, hs, pre_c, y_c, merged) = saved
    S = x.shape[0]
    g = {}
    dmerged = matmul("d_merged", dxb, W["w_out"], "nt")
    g["w_out"] = matmul("dw_out", merged, dxb, "tn")
    g0 = P_GATES // 512
    dga, dgb, dgc, dya, dyb, dyc = rowwise_bwd(
        "merge_bwd", f_merge, [(proj, 512, g0), (proj, 512, g0 + 4), (proj, 512, g0 + 8), (y_a, 512, 0), (y_b, 512, 0), (y_c, 512, 0)],
        [], [(dmerged, 512)], [(k, D_MODEL, BF16) for k in range(6)], [], n_groups=4)
    do_a = matmul("d_o_mla", dya, W["w_o_mla"], "nt", out_dtype=BF16)
    g["w_o_mla"] = matmul("dw_o_mla", o_a, dya, "tn")
    delta = attention_delta(o_a, do_a)
    dqc = attention_bwd_dq(qc, kc, vb, do_a, lse, delta)
    dkc, dv = attention_bwd_dkv(qc, kc, vb, do_a, lse, delta)
    dq = mla_q_prep_bwd(dqc, cosf, sinf)
    dkv, dpe = mla_k_prep_bwd(dkc, dv, cosf, sinf)
    dqn = matmul("d_qn", dq, W["mla_w_uq"], "nt")
    g["mla_w_uq"] = matmul("dw_uq", qn, dq, "tn")
    dkvn = matmul("d_kvn", dkv, W["mla_w_ukv"], "nt")
    g["mla_w_ukv"] = matmul("dw_ukv", kvn, dkv, "tn")
    dq_lat, g["mla_q_norm"] = rowwise_bwd("q_norm_bwd", f_rmsnorm, [(proj, 512, 0)], [(W["mla_q_norm"], 512, 0)], [(dqn, 512)],
                                          [(0, 512, BF16)], [0])
    dkv_lat, g["mla_kv_norm"] = rowwise_bwd("kv_norm_bwd", f_rmsnorm, [(proj, 512, 1)], [(W["mla_kv_norm"], 512, 0)], [(dkvn, 512)],
                                            [(0, 512, BF16)], [0])
    do_b = matmul("d_o_hgrn", dyb, W["w_o_hgrn"], "nt")
    g["w_o_hgrn"] = matmul("dw_o_hgrn", pre_b, dyb, "tn")
    do_h, dhgate, g["hgrn_norm"] = rowwise_bwd(
        "hgrn_out_bwd", f_hgrn_out, [(o_h, 128, 0), (proj, 128, P_HGATE // 128)], [(W["hgrn_norm"], 128, 0)], [(do_b, 128)],
        [(0, HG_WIDTH, F32), (1, HG_WIDTH, BF16)], [0], n_groups=HG_HEADS, tm=1024)
    dhq, dhf, dhi, dlb = hgrn_scan_bwd(proj, lb, hst, do_h)
    do_c = matmul("d_o_ssm", dyc, W["w_o_ssm"], "nt")
    g["w_o_ssm"] = matmul("dw_o_ssm", pre_c, dyc, "tn")
    dy_s, dz, g["ssm_norm"] = rowwise_bwd(
        "ssm_out_bwd", f_ssm_out, [(y_s, 512, 0), (proj, 512, P_Z // 512)], [(W["ssm_norm"], 512, 0)], [(do_c, 512)],
        [(0, SSM_INNER, F32), (1, SSM_INNER, BF16)], [0], n_groups=SSM_GROUPS)
    dxs, dbm, dcm, ddtr, dbias, dalog, ddsk = ssd_scan_bwd(act, dtr, *_ssm_params(W), hs, dy_s)
    g["ssm_dt_bias"], g["ssm_a_log"], g["ssm_d"] = (v.reshape(1, SSM_HEADS) for v in (dbias, dalog, ddsk))
    dxbc, g["ssm_conv_w"], g["ssm_conv_b"] = conv_bwd(proj, pre, jnp.concatenate([dxs, dbm, dcm], axis=1), W["ssm_conv_w"])
    ddt = ddtr.transpose(1, 0, 2).reshape(S, SSM_HEADS)
    dproj = jnp.concatenate([dq_lat, dkv_lat, dhq, dhf, dhi, dhgate, dz, dxbc, dga, dgb, dgc, dpe[:, :64].astype(BF16),
                             ddt.astype(BF16), jnp.zeros((S, IN_PAD - IN_DIM), BF16)], axis=1)
    dh = matmul("d_h_mix", dproj, W["w_in"], "nt")
    g["w_in"] = matmul("dw_in", h, dproj, "tn")
    dx_in, dxb_in, g["mix_norm"] = rowwise_bwd("mix_norm_bwd", f_rmsnorm, [(x, D_MODEL, 0)], [(W["mix_norm"], D_MODEL, 0)], [(dh, D_MODEL)],
                                               [(0, D_MODEL, F32), (0, D_MODEL, BF16)], [0], adds={0: dx}, tm=256)
    return dx_in, dxb_in, g, dlb


def local_step(x, target, Wl, small, final_norm):
    S = x.shape[0]
    tabs = rope_tables(S)
    (lbs,) = rowwise("lower_bounds", f_lower_bounds, [(small["hgrn_lb_logits"], HG_WIDTH, 0)], [], [(HG_WIDTH, HG_WIDTH, F32)])
    saved = []
    for l in range(DEPTH):
        W = dict(Wl[l])
        for k in SMALL:
            W[k] = small[k][l:l + 1]
        lb = lbs[l:l + 1]
        x, s1 = _ffn_fwd("ffn1", x, W["ffn1_norm"], W["ffn1_wi"], W["ffn1_wo"])
        x, s2 = _mix_fwd(x, W, lb, tabs)
        x, s3 = _ffn_fwd("ffn2", x, W["ffn2_norm"], W["ffn2_wi"], W["ffn2_wo"])
        saved.append((W, lb, s1, s2, s3))
    dx, dxb, dfinal, loss = loss_head(x, target, final_norm.reshape(1, D_MODEL))
    grads = [None] * DEPTH
    dlbs = [None] * DEPTH
    for l in reversed(range(DEPTH)):
        W, lb, s1, s2, s3 = saved[l]
        g = {}
        dx, dxb, g["ffn2_norm"], g["ffn2_wi"], g["ffn2_wo"] = _ffn_bwd("ffn2", dx, dxb, s3, W["ffn2_norm"], W["ffn2_wi"], W["ffn2_wo"])
        dx, dxb, gm, dlbs[l] = _mix_bwd(dx, dxb, s2, W, lb, tabs)
        g.update(gm)
        dx, dxb, g["ffn1_norm"], g["ffn1_wi"], g["ffn1_wo"] = _ffn_bwd("ffn1", dx, dxb, s1, W["ffn1_norm"], W["ffn1_wi"], W["ffn1_wo"])
        grads[l] = g
    (dlogits,) = rowwise_bwd("lower_bounds_bwd", f_lower_bounds, [(small["hgrn_lb_logits"], HG_WIDTH, 0)], [],
                             [(jnp.concatenate(dlbs, axis=0), HG_WIDTH)], [(0, HG_WIDTH, F32)], [])
    return loss, dx, grads, dlogits, dfinal


ANY = pl.BlockSpec(memory_space=pl.ANY)


def _coords():
    return lax.axis_index("x"), lax.axis_index("y"), lax.axis_index("c")


def _exchange_call(name, body, src, out_shape, n_copies):
    return pl.pallas_call(
        body, name=name, in_specs=[ANY], out_specs=ANY, out_shape=out_shape,
        scratch_shapes=[pltpu.SemaphoreType.DMA((n_copies,)), pltpu.SemaphoreType.DMA((n_copies,)), pltpu.SemaphoreType.DMA],
    )(src)


def _multi_exchange(name, body, srcs, out_shapes, sem_counts):
    n = len(srcs)
    return pl.pallas_call(
        body, name=name, in_specs=[ANY] * n, out_specs=[ANY] * len(out_shapes), out_shape=out_shapes,
        scratch_shapes=[pltpu.SemaphoreType.DMA((k,)) for k in sem_counts],
    )(*srcs)


def _chip_peers(x, y):
    return [(1 - x, y), (x, 1 - y), (1 - x, 1 - y)]


def weights_allgather(srcs):
    n = len(srcs)

    def body(*refs):
        src, out = refs[:n], refs[n:2 * n]
        send_sems, recv_sems, fsend_sems, frecv_sems, local_sems = refs[2 * n:]
        x, y, c = _coords()
        me = 2 * x + y
        peers = _chip_peers(x, y)

        def ici(i, k, chip):
            return pltpu.make_async_remote_copy(src[i].at[c], out[i].at[chip, c], send_sems.at[3 * i + k], recv_sems.at[3 * i + k],
                                                device_id=(*peers[k], c), device_id_type=MESH)

        def fwd(i, k, chip, layer):
            return pltpu.make_async_remote_copy(out[i].at[chip, layer], out[i].at[chip, layer], fsend_sems.at[3 * i + k],
                                                frecv_sems.at[3 * i + k], device_id=(x, y, 1 - c), device_id_type=MESH)

        own = [pltpu.make_async_copy(src[i], out[i].at[me], local_sems.at[i]) for i in range(n)]
        sends = [ici(i, k, me) for k in range(3) for i in range(n)]
        for cp in own + sends:
            cp.start()
        passed = []
        for k, (px, py) in enumerate(peers):
            for i in range(n):
                ici(i, k, 2 * px + py).wait_recv()
                passed.append(fwd(i, k, 2 * px + py, c))
                passed[-1].start()
        for k, (px, py) in enumerate(peers):
            for i in range(n):
                fwd(i, k, 2 * px + py, 1 - c).wait_recv()
        for cp in sends + passed:
            cp.wait_send()
        for cp in own:
            cp.wait()

    out_shapes = [jax.ShapeDtypeStruct((4,) + s.shape, s.dtype) for s in srcs]
    return _multi_exchange("weights_allgather", body, srcs, out_shapes, (3 * n, 3 * n, 3 * n, 3 * n, n))


def grad_layer_exchange(gs):
    n = len(gs)

    def body(*refs):
        src, out = refs[:n], refs[n:2 * n]
        send_sems, recv_sems = refs[2 * n:]
        x, y, c = _coords()
        copies = [pltpu.make_async_remote_copy(src[i].at[1 - c], out[i], send_sems.at[i], recv_sems.at[i],
                                               device_id=(x, y, 1 - c), device_id_type=MESH) for i in range(n)]
        for cp in copies:
            cp.start()
        for cp in copies:
            cp.wait_recv()
        for cp in copies:
            cp.wait_send()

    return _multi_exchange("grad_layer_exchange", body, gs, [jax.ShapeDtypeStruct(g.shape[1:], g.dtype) for g in gs], (n, n))


def grad_chip_exchange(ps):
    n = len(ps)

    def body(*refs):
        src, out = refs[:n], refs[n:2 * n]
        send_sems, recv_sems = refs[2 * n:]
        x, y, c = _coords()
        copies = [pltpu.make_async_remote_copy(src[i].at[2 * px + py], out[i].at[k], send_sems.at[3 * i + k], recv_sems.at[3 * i + k],
                                               device_id=(px, py, c), device_id_type=MESH)
                  for k, (px, py) in enumerate(_chip_peers(x, y)) for i in range(n)]
        for cp in copies:
            cp.start()
        for cp in copies:
            cp.wait_recv()
        for cp in copies:
            cp.wait_send()

    return _multi_exchange("grad_chip_exchange", body, ps, [jax.ShapeDtypeStruct((3,) + p.shape[1:], p.dtype) for p in ps], (3 * n, 3 * n))


def grad_reduced_exchange(rs):
    n = len(rs)

    def body(*refs):
        src, out = refs[:n], refs[n:2 * n]
        send_sems, recv_sems, local_sems = refs[2 * n:]
        x, y, c = _coords()
        own = [pltpu.make_async_copy(src[i], out[i].at[c], local_sems.at[i]) for i in range(n)]
        copies = [pltpu.make_async_remote_copy(src[i], out[i].at[c], send_sems.at[i], recv_sems.at[i],
                                               device_id=(x, y, 1 - c), device_id_type=MESH) for i in range(n)]
        for cp in own + copies:
            cp.start()
        for i in range(n):
            pltpu.make_async_remote_copy(src[i], out[i].at[1 - c], send_sems.at[i], recv_sems.at[i],
                                         device_id=(x, y, 1 - c), device_id_type=MESH).wait_recv()
        for cp in copies:
            cp.wait_send()
        for cp in own:
            cp.wait()

    return _multi_exchange("grad_reduced_exchange", body, rs, [jax.ShapeDtypeStruct((DEPTH,) + r.shape, r.dtype) for r in rs], (n, n, n))


def device_allgather(name, src):
    def body(src_ref, out_ref, send_sems, recv_sems, local_sem):
        x, y, c = _coords()
        me = 4 * x + 2 * y + c
        peers = [(x ^ (m >> 2), y ^ ((m >> 1) & 1), c ^ (m & 1)) for m in range(1, 8)]

        def copy(k, dev):
            return pltpu.make_async_remote_copy(src_ref, out_ref.at[dev], send_sems.at[k], recv_sems.at[k],
                                                device_id=peers[k], device_id_type=MESH)

        local = pltpu.make_async_copy(src_ref, out_ref.at[me], local_sem)
        local.start()
        sends = [copy(k, me) for k in range(7)]
        for s in sends:
            s.start()
        for k, (px, py, pc) in enumerate(peers):
            copy(k, 4 * px + 2 * py + pc).wait_recv()
        for s in sends:
            s.wait_send()
        local.wait()

    return _exchange_call(name, body, src, jax.ShapeDtypeStruct((8,) + src.shape, src.dtype), 7)


BLOCK_ELEMS = 1 << 19


def add_pair(name, g, recv, c):
    _, n, R, C = g.shape
    tr = _fit_rows(R, max(16, BLOCK_ELEMS // C))

    def body(c_ref, g_ref, r_ref, o_ref):
        o_ref[...] = (g_ref[...].astype(F32) + r_ref[...].astype(F32)).astype(o_ref.dtype)

    blk = pl.BlockSpec((None, tr, C), lambda j, i, c_ref: (j, i, 0))
    return pl.pallas_call(
        body, name=name, out_shape=jax.ShapeDtypeStruct((n, R, C), g.dtype),
        grid_spec=pltpu.PrefetchScalarGridSpec(
            num_scalar_prefetch=1, grid=(n, R // tr),
            in_specs=[pl.BlockSpec((None, None, tr, C), lambda j, i, c_ref: (c_ref[0], j, i, 0)), blk], out_specs=blk),
        compiler_params=_cp("parallel", "parallel"),
    )(c, g, recv)


def sum_chips(name, own, recv, me):
    _, R, C = own.shape
    tr = _fit_rows(R, max(16, BLOCK_ELEMS // (2 * C)))

    def body(me_ref, o_ref, r_ref, out_ref):
        acc = o_ref[...].astype(F32)
        for k in range(3):
            acc = acc + r_ref[k].astype(F32)
        out_ref[...] = acc

    return pl.pallas_call(
        body, name=name, out_shape=jax.ShapeDtypeStruct((R, C), F32),
        grid_spec=pltpu.PrefetchScalarGridSpec(
            num_scalar_prefetch=1, grid=(R // tr,),
            in_specs=[pl.BlockSpec((None, tr, C), lambda i, me_ref: (me_ref[0], i, 0)), pl.BlockSpec((3, tr, C), lambda i, me_ref: (0, i, 0))],
            out_specs=pl.BlockSpec((tr, C), lambda i, me_ref: (i, 0))),
        compiler_params=_cp("parallel"),
    )(me, own, recv)


def sum_devices(parts):
    _, R, C = parts.shape

    def body(p_ref, o_ref):
        acc = p_ref[0]
        for k in range(1, 8):
            acc = acc + p_ref[k]
        o_ref[...] = acc

    return pl.pallas_call(body, name="sum_devices", out_shape=jax.ShapeDtypeStruct((R, C), F32))(parts)


def _fit_rows(R, pref):
    for t in range(min(pref, R), 0, -1):
        if R % t == 0 and (t % 16 == 0 or t == R):
            return t
    raise ValueError((R, pref))


def adamw(name, w, g, m, v):
    shape = w.shape
    C = shape[-1]
    w2, g2, m2, v2 = (a.reshape(-1, C) for a in (w, g, m, v))
    R = w2.shape[0]
    tr = _fit_rows(R, max(8, (1 << 18) // C)) if R * C > (1 << 18) else R
    c1 = 1.0 - ADAM_B1 ** ADAM_STEP
    c2 = 1.0 - ADAM_B2 ** ADAM_STEP

    def body(w_ref, g_ref, m_ref, v_ref, d_ref, mo_ref, vo_ref):
        gg = g_ref[...]
        mn = ADAM_B1 * m_ref[...] + (1.0 - ADAM_B1) * gg
        vn = ADAM_B2 * v_ref[...] + (1.0 - ADAM_B2) * jnp.square(gg)
        d_ref[...] = -ADAM_LR * ((mn / c1) / (jnp.sqrt(vn / c2) + ADAM_EPS) + ADAM_WD * w_ref[...])
        mo_ref[...] = mn
        vo_ref[...] = vn

    blk = pl.BlockSpec((tr, C), lambda i: (i, 0))
    outs = pl.pallas_call(
        body, name=name, grid=(R // tr,), in_specs=[blk] * 4, out_specs=[blk] * 3,
        out_shape=[jax.ShapeDtypeStruct((R, C), F32)] * 3, compiler_params=_cp("parallel"),
    )(w2, g2, m2, v2)
    return tuple(o.reshape(shape) for o in outs)


def _unshard(name, parts):
    n, L, r, c = parts.shape
    if name in COL_SHARDED:
        return parts.transpose(1, 2, 0, 3).reshape(L, r, n * c)
    return parts.transpose(1, 0, 2, 3).reshape(L, n * r, c)


def _shard(name, full):
    L, R, C = full.shape
    if name in COL_SHARDED:
        return full.reshape(L, R, 4, C // 4).transpose(0, 2, 1, 3)
    return full.reshape(L, 4, R // 4, C)


def _to_local(name, w):
    if name == "w_in":
        return _w_in_to_local(w)
    if name == "mla_w_uq":
        return _w_uq_to_local(w)
    return w


def _from_local(name, g):
    if name == "w_in":
        return _w_in_from_local(g)
    if name == "mla_w_uq":
        return _w_uq_from_local(g)
    return g


TWIN_WEIGHTS = ("ffn1_norm", "ffn1_wi", "ffn1_wo", "mix_norm", "w_in", "mla_q_norm", "mla_w_uq", "mla_kv_norm", "mla_w_ukv",
                "hgrn_lb_logits", "hgrn_norm", "ssm_conv_w", "ssm_conv_b", "ssm_a_log", "ssm_dt_bias", "ssm_d", "ssm_norm",
                "w_o_mla", "w_o_hgrn", "w_o_ssm", "w_out", "ffn2_norm", "ffn2_wi", "ffn2_wo", "final_norm")
SMALL_PACK = SMALL + ("ssm_conv_w", "final_norm")


def _pad_rows(flat, cols):
    n = flat.shape[0]
    rows = -(-n // cols)
    rows = -(-rows // 8) * 8
    return jnp.concatenate([flat, jnp.zeros((rows * cols - n,), flat.dtype)]).reshape(rows, cols)


def kernel(x, ffn1_norm, ffn1_wi, ffn1_wo, mix_norm, w_in, mla_q_norm, mla_w_uq, mla_kv_norm, mla_w_ukv, hgrn_lb_logits, hgrn_norm, ssm_conv_w, ssm_conv_b, ssm_a_log, ssm_dt_bias, ssm_d, ssm_norm, w_o_mla, w_o_hgrn, w_o_ssm, w_out, ffn2_norm, ffn2_wi, ffn2_wo, final_norm, loss_target, m_ffn1_norm, m_ffn1_wi, m_ffn1_wo, m_mix_norm, m_w_in, m_mla_q_norm, m_mla_w_uq, m_mla_kv_norm, m_mla_w_ukv, m_hgrn_lb_logits, m_hgrn_norm, m_ssm_conv_w, m_ssm_conv_b, m_ssm_a_log, m_ssm_dt_bias, m_ssm_d, m_ssm_norm, m_w_o_mla, m_w_o_hgrn, m_w_o_ssm, m_w_out, m_ffn2_norm, m_ffn2_wi, m_ffn2_wo, m_final_norm, v_ffn1_norm, v_ffn1_wi, v_ffn1_wo, v_mix_norm, v_w_in, v_mla_q_norm, v_mla_w_uq, v_mla_kv_norm, v_mla_w_ukv, v_hgrn_lb_logits, v_hgrn_norm, v_ssm_conv_w, v_ssm_conv_b, v_ssm_a_log, v_ssm_dt_bias, v_ssm_d, v_ssm_norm, v_w_o_mla, v_w_o_hgrn, v_w_o_ssm, v_w_out, v_ffn2_norm, v_ffn2_wi, v_ffn2_wo, v_final_norm):
    args = dict(locals())
    w = {n: args[n] for n in TWIN_WEIGHTS}
    m = {n: args["m_" + n] for n in TWIN_WEIGHTS}
    v = {n: args["v_" + n] for n in TWIN_WEIGHTS}
    xi, yi, ci = _coords()
    chip = 2 * xi + yi

    gathered = weights_allgather([w[n].astype(BF16) for n in BIG])
    full = {n: _to_local(n, _unshard(n, parts)) for n, parts in zip(BIG, gathered)}
    conv_parts = device_allgather("conv_allgather", _pad_rows(w["ssm_conv_w"].reshape(-1), 128))
    conv_full = jnp.concatenate(
        [conv_parts[2 * j].reshape(-1)[:w["ssm_conv_w"].size].reshape(w["ssm_conv_w"].shape) for j in range(4)], axis=-1)
    Wl = [dict({n: full[n][l] for n in BIG}, ssm_conv_w=conv_full[l]) for l in range(DEPTH)]
    small = {n: w[n] for n in SMALL}

    loss, grad_x, grads, dlogits, dfinal = local_step(x[0], loss_target[0], Wl, small, w["final_norm"])
    loss = lax.psum(loss[0, 0], ("x", "y", "c"))

    c_idx, chip_idx = ci.astype(jnp.int32).reshape(1), chip.astype(jnp.int32).reshape(1)
    gs = [_shard(n, _from_local(n, jnp.stack([grads[l][n] for l in range(DEPTH)]))).astype(BF16) for n in BIG]
    from_sibling = grad_layer_exchange(gs)
    pair_sums = [add_pair("add_pair_" + n, a, b, c_idx) for n, a, b in zip(BIG, gs, from_sibling)]
    from_chips = grad_chip_exchange(pair_sums)
    reduced = [sum_chips("sum_chips_" + n, a, b, chip_idx) for n, a, b in zip(BIG, pair_sums, from_chips)]
    g = dict(zip(BIG, grad_reduced_exchange(reduced)))

    sg = {n: jnp.concatenate([grads[l][n] for l in range(DEPTH)], axis=0) for n in SMALL if n != "hgrn_lb_logits"}
    sg["hgrn_lb_logits"] = dlogits
    sg["ssm_conv_w"] = jnp.stack([grads[l]["ssm_conv_w"] for l in range(DEPTH)])
    sg["final_norm"] = dfinal
    spack = _pad_rows(jnp.concatenate([sg[n].reshape(-1) for n in SMALL_PACK]), 128)
    ssum = sum_devices(device_allgather("small_grads_allgather", spack)).reshape(-1)
    off = 0
    for n in SMALL_PACK:
        size = sg[n].size
        g[n] = ssum[off:off + size].reshape(sg[n].shape)
        off += size
    shard_cols = w["ssm_conv_w"].shape[-1]
    g["ssm_conv_w"] = lax.dynamic_slice_in_dim(g["ssm_conv_w"], chip * shard_cols, shard_cols, axis=2)
    g = {n: g[n].reshape(w[n].shape) for n in TWIN_WEIGHTS}

    upd = {n: adamw("adamw_" + n, w[n], g[n], m[n], v[n]) for n in TWIN_WEIGHTS}
    return (loss, grad_x[None], *[g[n] for n in TWIN_WEIGHTS], *[upd[n][0] for n in TWIN_WEIGHTS],
            *[upd[n][1] for n in TWIN_WEIGHTS], *[upd[n][2] for n in TWIN_WEIGHTS])
```

```python
import functools
import math

import jax
import jax.numpy as jnp
import numpy as np
from jax import lax
from jax.experimental import pallas as pl
from jax.experimental.pallas import tpu as pltpu

F32 = jnp.float32
BF16 = jnp.bfloat16
MESH = pl.DeviceIdType.MESH

D_MODEL = 2048
DEPTH = 2
CHUNK = 64
EPS = 1e-6
MLA_HEADS, MLA_Q_RANK, MLA_KV_RANK, MLA_NOPE, MLA_ROPE, MLA_V = 16, 512, 512, 128, 64, 128
ROPE_THETA = 10000.0
HG_HEADS, HG_DK = 16, 128
HG_WIDTH = HG_HEADS * HG_DK
SSM_INNER, SSM_HEADDIM, SSM_HEADS, SSM_GROUPS, SSM_STATE, SSM_CONV = 4096, 64, 64, 8, 128, 4
SSM_CONV_DIM = SSM_INNER + 2 * SSM_GROUPS * SSM_STATE
D_FF = 5632
IN_DIM = 25728
ADAM_LR, ADAM_B1, ADAM_B2, ADAM_EPS, ADAM_WD, ADAM_STEP = 0.001, 0.9, 0.999, 1e-08, 0.01, 10

P_QLAT, P_KVLAT, P_HQ, P_HF, P_HI, P_HGATE, P_Z, P_XBC, P_GATES, P_KPEDT = (
    0, 512, 1024, 3072, 5120, 7168, 9216, 13312, 19456, 25600)
IN_PAD = 26624

VMEM_LIMIT_V7X = 48 << 20
SSD_Q = 256
HG_HB = 4
NEG = -1e30


def _cp(*sem):
    return pltpu.CompilerParams(dimension_semantics=sem, vmem_limit_bytes=VMEM_LIMIT_V7X)


def _fit(n, pref):
    if n <= pref:
        return n
    for t in range(pref, 0, -128):
        if n % t == 0:
            return t
    raise ValueError((n, pref))


_DIMS = {"nn": (((1,), (0,)), ((), ())), "nt": (((1,), (1,)), ((), ())), "tn": (((0,), (0,)), ((), ()))}


def _dot(a, b, mode):
    return lax.dot_general(a.astype(BF16), b.astype(BF16), _DIMS[mode], preferred_element_type=F32)


def matmul(name, a, b, mode="nn", out_dtype=F32, alpha=1.0, res=None, tm=1024, tn=1024, tk=2048):
    if mode == "nn":
        (M, K), (K2, N) = a.shape, b.shape
    elif mode == "nt":
        (M, K), (N, K2) = a.shape, b.shape
    else:
        (K, M), (K2, N) = a.shape, b.shape
    assert K == K2, (name, a.shape, b.shape, mode)
    tm, tn, tk = _fit(M, tm), _fit(N, tn), _fit(K, tk)
    nk = K // tk
    a_spec = pl.BlockSpec((tk, tm), lambda i, j, k: (k, i)) if mode == "tn" else pl.BlockSpec((tm, tk), lambda i, j, k: (i, k))
    b_spec = pl.BlockSpec((tn, tk), lambda i, j, k: (j, k)) if mode == "nt" else pl.BlockSpec((tk, tn), lambda i, j, k: (k, j))
    o_spec = pl.BlockSpec((tm, tn), lambda i, j, k: (i, j))
    has_res = res is not None

    def body(*refs):
        a_ref, b_ref = refs[0], refs[1]
        o_ref = refs[3] if has_res else refs[2]

        def finish(v):
            if alpha != 1.0:
                v = v * alpha
            if has_res:
                v = v + refs[2][...].astype(F32)
            o_ref[...] = v.astype(o_ref.dtype)

        if nk == 1:
            finish(_dot(a_ref[...], b_ref[...], mode))
            return
        acc = refs[-1]
        k = pl.program_id(2)

        @pl.when(k == 0)
        def _():
            acc[...] = _dot(a_ref[...], b_ref[...], mode)

        @pl.when(jnp.logical_and(k > 0, k < nk - 1))
        def _():
            acc[...] += _dot(a_ref[...], b_ref[...], mode)

        @pl.when(k == nk - 1)
        def _():
            finish(acc[...] + _dot(a_ref[...], b_ref[...], mode))

    ins = [a, b] + ([res] if has_res else [])
    return pl.pallas_call(
        body, name=name, grid=(M // tm, N // tn, nk),
        in_specs=[a_spec, b_spec] + ([o_spec] if has_res else []), out_specs=o_spec,
        out_shape=jax.ShapeDtypeStruct((M, N), out_dtype),
        scratch_shapes=[pltpu.VMEM((tm, tn), F32)] if nk > 1 else [],
        compiler_params=_cp("parallel", "parallel", "arbitrary"),
    )(*ins)


def _row_specs(rows, consts, tm):
    specs = []
    for arr, w, off in rows:
        specs.append(pl.BlockSpec((tm, w), functools.partial(lambda j, i, off: (i, off + j), off=off)))
    for arr, w, off in consts:
        specs.append(pl.BlockSpec((arr.shape[0], w), functools.partial(lambda j, i, off: (0, off + j), off=off)))
    return specs


def rowwise(name, fn, rows, consts, outs, n_groups=1, tm=512):
    S = rows[0][0].shape[0]
    tm = _fit(S, tm)
    n_in = len(rows) + len(consts)

    def body(*refs):
        vals = fn(*[r[...].astype(F32) for r in refs[:n_in]])
        for o_ref, v in zip(refs[n_in:], vals):
            o_ref[...] = v.astype(o_ref.dtype)

    return pl.pallas_call(
        body, name=name, grid=(n_groups, S // tm),
        in_specs=_row_specs(rows, consts, tm),
        out_specs=[pl.BlockSpec((tm, w), lambda j, i: (i, j)) for _, w, _ in outs],
        out_shape=[jax.ShapeDtypeStruct((S, W), dt) for W, _, dt in outs],
        compiler_params=_cp("parallel", "parallel"),
    )(*[r[0] for r in rows], *[c[0] for c in consts])


def rowwise_bwd(name, fn, rows, consts, cts, row_grads, const_grads, adds=None, n_groups=1, tm=512):
    S = rows[0][0].shape[0]
    tm = _fit(S, tm)
    adds = adds or {}
    add_idx = list(adds)
    n_r, n_c, n_ct, n_add = len(rows), len(consts), len(cts), len(adds)

    def body(*refs):
        ins = [r[...].astype(F32) for r in refs[:n_r + n_c]]
        ct = tuple(r[...].astype(F32) for r in refs[n_r + n_c:n_r + n_c + n_ct])
        add_refs = refs[n_r + n_c + n_ct:n_r + n_c + n_ct + n_add]
        out_refs = refs[n_r + n_c + n_ct + n_add:]
        _, vjp = jax.vjp(fn, *ins)
        g = list(vjp(ct))
        for a_ref, idx in zip(add_refs, add_idx):
            g[idx] = g[idx] + a_ref[...].astype(F32)
        for q, (idx, _, _) in enumerate(row_grads):
            out_refs[q][...] = g[idx].astype(out_refs[q].dtype)
        first = pl.program_id(1) == 0
        for q, idx in enumerate(const_grads):
            o_ref = out_refs[len(row_grads) + q]

            @pl.when(first)
            def _(o_ref=o_ref):
                o_ref[...] = jnp.zeros_like(o_ref)

            o_ref[...] += g[n_r + idx]

    in_specs = _row_specs(rows, consts, tm)
    in_specs += [pl.BlockSpec((tm, w), lambda j, i: (i, j)) for _, w in cts]
    in_specs += [pl.BlockSpec((tm, rows[idx][1]), lambda j, i: (i, j)) for idx in add_idx]
    out_specs = [pl.BlockSpec((tm, rows[idx][1]), lambda j, i: (i, j)) for idx, _, _ in row_grads]
    out_specs += [pl.BlockSpec((consts[idx][0].shape[0], consts[idx][1]), lambda j, i: (0, j)) for idx in const_grads]
    out_shape = [jax.ShapeDtypeStruct((S, W), dt) for _, W, dt in row_grads]
    out_shape += [jax.ShapeDtypeStruct((consts[idx][0].shape[0], consts[idx][1] * n_groups), F32) for idx in const_grads]
    return pl.pallas_call(
        body, name=name, grid=(n_groups, S // tm), in_specs=in_specs, out_specs=out_specs, out_shape=out_shape,
        compiler_params=_cp("parallel", "arbitrary"),
    )(*[r[0] for r in rows], *[c[0] for c in consts], *[c[0] for c in cts], *adds.values())


def f_rmsnorm(x, w):
    return (x * lax.rsqrt(jnp.mean(x * x, axis=-1, keepdims=True) + EPS) * w,)


def f_swiglu(g, u):
    return (jax.nn.silu(g) * u,)


def f_hgrn_out(o, g, w):
    return (o * lax.rsqrt(jnp.mean(o * o, axis=-1, keepdims=True) + EPS) * w * jax.nn.silu(g),)


def f_ssm_out(y, z, w):
    y = y * jax.nn.silu(z)
    return (y * lax.rsqrt(jnp.mean(y * y, axis=-1, keepdims=True) + EPS) * w,)


def f_merge(ga, gb, gc, ya, yb, yc):
    return (jax.nn.sigmoid(ga) * ya + jax.nn.sigmoid(gb) * yb + jax.nn.sigmoid(gc) * yc,)


def f_lower_bounds(logits):
    p = jax.nn.softmax(logits, axis=0)
    rows = [jnp.zeros_like(p[0:1])]
    for l in range(1, DEPTH):
        rows.append(rows[-1] + p[l:l + 1])
    return (jnp.concatenate(rows, axis=0),)


def loss_head(x, target, w, tm=512):
    S, D = x.shape
    tm = _fit(S, tm)

    def loss_fn(xb, wb, tb):
        (y,) = f_rmsnorm(xb, wb)
        return 0.5 * jnp.sum(jnp.mean(jnp.square(y - tb), axis=-1))

    def body(x_ref, t_ref, w_ref, dx_ref, dxb_ref, dw_ref, loss_ref):
        @pl.when(pl.program_id(0) == 0)
        def _():
            dw_ref[...] = jnp.zeros_like(dw_ref)
            loss_ref[...] = jnp.zeros_like(loss_ref)

        l, (dx, dw) = jax.value_and_grad(loss_fn, argnums=(0, 1))(x_ref[...], w_ref[...], t_ref[...])
        dx_ref[...] = dx
        dxb_ref[...] = dx.astype(BF16)
        dw_ref[...] += dw
        loss_ref[...] += jnp.full(loss_ref.shape, l, F32)

    row = pl.BlockSpec((tm, D), lambda i: (i, 0))
    vec = pl.BlockSpec((1, D), lambda i: (0, 0))
    return pl.pallas_call(
        body, name="loss_head", grid=(S // tm,), in_specs=[row, row, vec],
        out_specs=[row, row, vec, pl.BlockSpec((1, 128), lambda i: (0, 0))],
        out_shape=[jax.ShapeDtypeStruct((S, D), F32), jax.ShapeDtypeStruct((S, D), BF16), jax.ShapeDtypeStruct((1, D), F32),
                   jax.ShapeDtypeStruct((1, 128), F32)],
        compiler_params=_cp("arbitrary"),
    )(x, target, w)


def rope_tables(S):
    inv = 1.0 / (ROPE_THETA ** (jnp.arange(0, MLA_ROPE, 2, dtype=F32) / MLA_ROPE))
    ang = jnp.arange(S, dtype=F32)[:, None] * inv[None, :]
    c, s = jnp.cos(ang), jnp.sin(ang)
    return jnp.tile(c, (1, 4)), jnp.concatenate([-s, s, -s, s], axis=1)


def _rope128(x, cosf, sinf):
    lane = lax.broadcasted_iota(jnp.int32, x.shape, 1)
    swapped = jnp.where((lane & 32) == 0, pltpu.roll(x, 96, 1), pltpu.roll(x, 32, 1))
    return x * cosf + swapped * sinf


def mla_q_prep(q, cosf, sinf, tm=512):
    S = q.shape[0]
    tm = _fit(S, tm)

    def body(q_ref, c_ref, s_ref, o_ref):
        x = q_ref[...]
        r = _rope128(x[:, 256:384], c_ref[...], s_ref[...])
        lane = lax.broadcasted_iota(jnp.int32, r.shape, 1)
        z = jnp.zeros_like(r)
        o_ref[...] = jnp.concatenate(
            [x[:, 0:128], jnp.where(lane < 64, r, z), x[:, 128:256], jnp.where(lane >= 64, r, z)], axis=1).astype(BF16)

    tab = pl.BlockSpec((tm, 128), lambda j, i: (i, 0))
    return pl.pallas_call(
        body, name="mla_q_prep", grid=(8, S // tm),
        in_specs=[pl.BlockSpec((tm, 384), lambda j, i: (i, j)), tab, tab],
        out_specs=pl.BlockSpec((tm, 512), lambda j, i: (i, j)),
        out_shape=jax.ShapeDtypeStruct((S, 4096), BF16), compiler_params=_cp("parallel", "parallel"),
    )(q, cosf, sinf)


def mla_q_prep_bwd(dqc, cosf, sinf, tm=512):
    S = dqc.shape[0]
    tm = _fit(S, tm)

    def body(d_ref, c_ref, s_ref, o_ref):
        d = d_ref[...]
        lane = lax.broadcasted_iota(jnp.int32, (tm, 128), 1)
        dr = jnp.where(lane < 64, d[:, 128:256], d[:, 384:512])
        o_ref[...] = jnp.concatenate([d[:, 0:128], d[:, 256:384], _rope128(dr, c_ref[...], -s_ref[...])], axis=1).astype(BF16)

    tab = pl.BlockSpec((tm, 128), lambda j, i: (i, 0))
    return pl.pallas_call(
        body, name="mla_q_prep_bwd", grid=(8, S // tm),
        in_specs=[pl.BlockSpec((tm, 512), lambda j, i: (i, j)), tab, tab],
        out_specs=pl.BlockSpec((tm, 384), lambda j, i: (i, j)),
        out_shape=jax.ShapeDtypeStruct((S, 3072), BF16), compiler_params=_cp("parallel", "parallel"),
    )(dqc, cosf, sinf)


def mla_k_prep(kv, proj, cosf, sinf, tm=512):
    S = kv.shape[0]
    tm = _fit(S, tm)

    def body(kv_ref, pe_ref, c_ref, s_ref, k_ref, v_ref):
        x = kv_ref[...]
        r = _rope128(pe_ref[...], c_ref[...], s_ref[...])
        lane = lax.broadcasted_iota(jnp.int32, r.shape, 1)
        r2 = jnp.where(lane < 64, r, pltpu.roll(r, 64, 1))
        k_ref[...] = jnp.concatenate([x[:, 0:128], r2, x[:, 256:384], r2], axis=1).astype(BF16)
        v_ref[...] = jnp.concatenate([x[:, 128:256], x[:, 384:512]], axis=1).astype(BF16)

    tab = pl.BlockSpec((tm, 128), lambda j, i: (i, 0))
    return pl.pallas_call(
        body, name="mla_k_prep", grid=(8, S // tm),
        in_specs=[pl.BlockSpec((tm, 512), lambda j, i: (i, j)), pl.BlockSpec((tm, 128), lambda j, i: (i, P_KPEDT // 128)), tab, tab],
        out_specs=[pl.BlockSpec((tm, 512), lambda j, i: (i, j)), pl.BlockSpec((tm, 256), lambda j, i: (i, j))],
        out_shape=[jax.ShapeDtypeStruct((S, 4096), BF16), jax.ShapeDtypeStruct((S, 2048), BF16)],
        compiler_params=_cp("parallel", "parallel"),
    )(kv, proj, cosf, sinf)


def mla_k_prep_bwd(dkc, dv, cosf, sinf, tm=512):
    S = dkc.shape[0]
    tm = _fit(S, tm)

    def body(dk_ref, dv_ref, c_ref, s_ref, dkv_ref, dpe_ref):
        dk, dvv = dk_ref[...], dv_ref[...]
        dkv_ref[...] = jnp.concatenate([dk[:, 0:128], dvv[:, 0:128], dk[:, 256:384], dvv[:, 128:256]], axis=1).astype(BF16)
        d2 = dk[:, 128:256] + dk[:, 384:512]
        lane = lax.broadcasted_iota(jnp.int32, d2.shape, 1)
        dr = jnp.where(lane < 64, d2 + pltpu.roll(d2, 64, 1), 0.0)
        dpe = jnp.where(lane < 64, _rope128(dr, c_ref[...], -s_ref[...]), 0.0)

        @pl.when(pl.program_id(1) == 0)
        def _():
            dpe_ref[...] = jnp.zeros_like(dpe_ref)

        dpe_ref[...] += dpe

    tab = pl.BlockSpec((tm, 128), lambda i, j: (i, 0))
    return pl.pallas_call(
        body, name="mla_k_prep_bwd", grid=(S // tm, 8),
        in_specs=[pl.BlockSpec((tm, 512), lambda i, j: (i, j)), pl.BlockSpec((tm, 256), lambda i, j: (i, j)), tab, tab],
        out_specs=[pl.BlockSpec((tm, 512), lambda i, j: (i, j)), tab],
        out_shape=[jax.ShapeDtypeStruct((S, 4096), BF16), jax.ShapeDtypeStruct((S, 128), F32)],
        compiler_params=_cp("parallel", "arbitrary"),
    )(dkc, dv, cosf, sinf)


ATT_SCALE = (MLA_NOPE + MLA_ROPE) ** -0.5


def _att_scores(q, k, qi, ki, t):
    s = _dot(q, k, "nt") * ATT_SCALE
    rows = qi * t + lax.broadcasted_iota(jnp.int32, (t, t), 0)
    cols = ki * t + lax.broadcasted_iota(jnp.int32, (t, t), 1)
    shift = CHUNK.bit_length() - 1
    return jnp.where((cols >> shift) <= (rows >> shift), s, NEG)


def attention_fwd(qc, kc, vb, t=1024):
    S = qc.shape[0]
    t = _fit(S, t)
    n = S // t

    def body(q_ref, k_ref, v_ref, o_ref, lse_ref, m_s, l_s, acc_s):
        qi, ki = pl.program_id(1), pl.program_id(2)

        @pl.when(ki == 0)
        def _():
            m_s[...] = jnp.full_like(m_s, NEG)
            l_s[...] = jnp.zeros_like(l_s)
            acc_s[...] = jnp.zeros_like(acc_s)

        @pl.when(ki <= qi)
        def _():
            s = _att_scores(q_ref[...], k_ref[...], qi, ki, t)
            m_prev = m_s[...]
            m_new = jnp.maximum(m_prev, jnp.max(s, axis=1, keepdims=True))
            alpha = jnp.exp(m_prev - m_new)
            p = jnp.exp(s - m_new)
            l_s[...] = alpha * l_s[...] + jnp.sum(p, axis=1, keepdims=True)
            acc_s[...] = alpha * acc_s[...] + _dot(p, v_ref[...], "nn")
            m_s[...] = m_new

        @pl.when(ki == qi)
        def _():
            o_ref[...] = (acc_s[...] / l_s[...]).astype(o_ref.dtype)
            lse_ref[...] = m_s[...] + jnp.log(l_s[...])

    return pl.pallas_call(
        body, name="attention_fwd", grid=(MLA_HEADS, n, n),
        in_specs=[pl.BlockSpec((t, 256), lambda h, i, j: (i, h)),
                  pl.BlockSpec((t, 256), lambda h, i, j: (jnp.minimum(i, j), h)),
                  pl.BlockSpec((t, 128), lambda h, i, j: (jnp.minimum(i, j), h))],
        out_specs=[pl.BlockSpec((t, 128), lambda h, i, j: (i, h)), pl.BlockSpec((None, t, 1), lambda h, i, j: (h, i, 0))],
        out_shape=[jax.ShapeDtypeStruct((S, 2048), BF16), jax.ShapeDtypeStruct((MLA_HEADS, S, 1), F32)],
        scratch_shapes=[pltpu.VMEM((t, 1), F32), pltpu.VMEM((t, 1), F32), pltpu.VMEM((t, 128), F32)],
        compiler_params=_cp("parallel", "parallel", "arbitrary"),
    )(qc, kc, vb)


def attention_delta(o, do, t=1024):
    S = o.shape[0]
    t = _fit(S, t)

    def body(o_ref, do_ref, d_ref):
        d_ref[...] = jnp.sum(o_ref[...].astype(F32) * do_ref[...].astype(F32), axis=1, keepdims=True)

    blk = pl.BlockSpec((t, 128), lambda h, i: (i, h))
    return pl.pallas_call(
        body, name="attention_delta", grid=(MLA_HEADS, S // t), in_specs=[blk, blk],
        out_specs=pl.BlockSpec((None, t, 1), lambda h, i: (h, i, 0)),
        out_shape=jax.ShapeDtypeStruct((MLA_HEADS, S, 1), F32), compiler_params=_cp("parallel", "parallel"),
    )(o, do)


def attention_bwd_dq(qc, kc, vb, do, lse, delta, t=1024):
    S = qc.shape[0]
    t = _fit(S, t)
    n = S // t

    def body(q_ref, k_ref, v_ref, do_ref, lse_ref, dl_ref, dq_ref, acc):
        qi, ki = pl.program_id(1), pl.program_id(2)

        @pl.when(ki == 0)
        def _():
            acc[...] = jnp.zeros_like(acc)

        @pl.when(ki <= qi)
        def _():
            p = jnp.exp(_att_scores(q_ref[...], k_ref[...], qi, ki, t) - lse_ref[...])
            dp = _dot(do_ref[...], v_ref[...], "nt")
            ds = p * (dp - dl_ref[...]) * ATT_SCALE
            acc[...] += _dot(ds, k_ref[...], "nn")

        @pl.when(ki == qi)
        def _():
            dq_ref[...] = acc[...]

    stat = pl.BlockSpec((None, t, 1), lambda h, i, j: (h, i, 0))
    return pl.pallas_call(
        body, name="attention_bwd_dq", grid=(MLA_HEADS, n, n),
        in_specs=[pl.BlockSpec((t, 256), lambda h, i, j: (i, h)),
                  pl.BlockSpec((t, 256), lambda h, i, j: (jnp.minimum(i, j), h)),
                  pl.BlockSpec((t, 128), lambda h, i, j: (jnp.minimum(i, j), h)),
                  pl.BlockSpec((t, 128), lambda h, i, j: (i, h)), stat, stat],
        out_specs=pl.BlockSpec((t, 256), lambda h, i, j: (i, h)),
        out_shape=jax.ShapeDtypeStruct((S, 4096), F32),
        scratch_shapes=[pltpu.VMEM((t, 256), F32)],
        compiler_params=_cp("parallel", "parallel", "arbitrary"),
    )(qc, kc, vb, do, lse, delta)


def attention_bwd_dkv(qc, kc, vb, do, lse, delta, t=1024):
    S = qc.shape[0]
    t = _fit(S, t)
    n = S // t

    def body(q_ref, k_ref, v_ref, do_ref, lse_ref, dl_ref, dk_ref, dv_ref, dk_acc, dv_acc):
        ki, qi = pl.program_id(1), pl.program_id(2)

        @pl.when(qi == 0)
        def _():
            dk_acc[...] = jnp.zeros_like(dk_acc)
            dv_acc[...] = jnp.zeros_like(dv_acc)

        @pl.when(qi >= ki)
        def _():
            p = jnp.exp(_att_scores(q_ref[...], k_ref[...], qi, ki, t) - lse_ref[...])
            dv_acc[...] += _dot(p, do_ref[...], "tn")
            dp = _dot(do_ref[...], v_ref[...], "nt")
            ds = p * (dp - dl_ref[...]) * ATT_SCALE
            dk_acc[...] += _dot(ds, q_ref[...], "tn")

        @pl.when(qi == n - 1)
        def _():
            dk_ref[...] = dk_acc[...]
            dv_ref[...] = dv_acc[...]

    stat = pl.BlockSpec((None, t, 1), lambda h, j, i: (h, jnp.maximum(i, j), 0))
    return pl.pallas_call(
        body, name="attention_bwd_dkv", grid=(MLA_HEADS, n, n),
        in_specs=[pl.BlockSpec((t, 256), lambda h, j, i: (jnp.maximum(i, j), h)),
                  pl.BlockSpec((t, 256), lambda h, j, i: (j, h)),
                  pl.BlockSpec((t, 128), lambda h, j, i: (j, h)),
                  pl.BlockSpec((t, 128), lambda h, j, i: (jnp.maximum(i, j), h)), stat, stat],
        out_specs=[pl.BlockSpec((t, 256), lambda h, j, i: (j, h)), pl.BlockSpec((t, 128), lambda h, j, i: (j, h))],
        out_shape=[jax.ShapeDtypeStruct((S, 4096), F32), jax.ShapeDtypeStruct((S, 2048), F32)],
        scratch_shapes=[pltpu.VMEM((t, 256), F32), pltpu.VMEM((t, 128), F32)],
        compiler_params=_cp("parallel", "parallel", "arbitrary"),
    )(qc, kc, vb, do, lse, delta)


def _scan_rows(x, reverse):
    n = x.shape[0]
    row = lax.broadcasted_iota(jnp.int32, x.shape, 0)
    d = 1
    while d < n:
        if reverse:
            x = x + jnp.where(row < n - d, pltpu.roll(x, n - d, 0), 0.0)
        else:
            x = x + jnp.where(row >= d, pltpu.roll(x, d, 0), 0.0)
        d *= 2
    return x


@jax.custom_vjp
def cumsum_rows(x):
    return _scan_rows(x, False)


cumsum_rows.defvjp(lambda x: (_scan_rows(x, False), None), lambda _, g: (_scan_rows(g, True),))


def hgrn_chunk(q_in, f_in, v, lb, state_t):
    f = lb + (1.0 - lb) * jax.nn.sigmoid(f_in)
    q = jax.nn.silu(q_in) * HG_DK ** -0.5
    k = 1.0 - f
    b = cumsum_rows(jnp.log(f))
    b_last = b[CHUNK - 1:CHUNK]
    b_mid = b[CHUNK // 2 - 1:CHUNK // 2]
    r = lax.broadcasted_iota(jnp.int32, (CHUNK, CHUNK), 0)
    c = lax.broadcasted_iota(jnp.int32, (CHUNK, CHUNK), 1)
    att = jnp.where(c <= r, _dot(q * jnp.exp(b - b_mid), k * jnp.exp(b_mid - b), "nt"), 0.0)
    o = _dot(q * jnp.exp(b), state_t, "nt") + _dot(att, v, "nn")
    new_state_t = state_t * jnp.exp(b_last) + _dot(v, k * jnp.exp(b_last - b), "tn")
    return o, new_state_t


def hgrn_scan_fwd(proj, lb):
    S = proj.shape[0]
    nc = S // CHUNK
    W = HG_HB * 128

    def body(q_ref, f_ref, v_ref, lb_ref, o_ref, hst_ref, st):
        @pl.when(pl.program_id(1) == 0)
        def _():
            st[...] = jnp.zeros_like(st)

        for h in range(HG_HB):
            cs = slice(h * 128, (h + 1) * 128)
            hst_ref[h] = st[h]
            o, new = hgrn_chunk(q_ref[:, cs], f_ref[:, cs], v_ref[:, cs], lb_ref[:, cs], st[h])
            o_ref[:, cs] = o
            st[h] = new

    def seg(off):
        return pl.BlockSpec((CHUNK, W), functools.partial(lambda g, c, off: (c, off + g), off=off // W))

    return pl.pallas_call(
        body, name="hgrn_scan_fwd", grid=(HG_HEADS // HG_HB, nc),
        in_specs=[seg(P_HQ), seg(P_HF), seg(P_HI), pl.BlockSpec((1, W), lambda g, c: (0, g))],
        out_specs=[pl.BlockSpec((CHUNK, W), lambda g, c: (c, g)), pl.BlockSpec((None, HG_HB, 128, 128), lambda g, c: (c, g, 0, 0))],
        out_shape=[jax.ShapeDtypeStruct((S, HG_WIDTH), F32), jax.ShapeDtypeStruct((nc, HG_HEADS, 128, 128), F32)],
        scratch_shapes=[pltpu.VMEM((HG_HB, 128, 128), F32)],
        compiler_params=_cp("parallel", "arbitrary"),
    )(proj, proj, proj, lb)


def hgrn_scan_bwd(proj, lb, hst, do):
    S = proj.shape[0]
    nc = S // CHUNK
    W = HG_HB * 128

    def body(q_ref, f_ref, v_ref, lb_ref, hst_ref, do_ref, dq_ref, df_ref, dv_ref, dlb_ref, dst):
        @pl.when(pl.program_id(1) == 0)
        def _():
            dst[...] = jnp.zeros_like(dst)
            dlb_ref[...] = jnp.zeros_like(dlb_ref)

        for h in range(HG_HB):
            cs = slice(h * 128, (h + 1) * 128)
            _, vjp = jax.vjp(hgrn_chunk, q_ref[:, cs], f_ref[:, cs], v_ref[:, cs], lb_ref[:, cs], hst_ref[h])
            dq, df, dv, dlb, dstate = vjp((do_ref[:, cs], dst[h]))
            dq_ref[:, cs] = dq.astype(dq_ref.dtype)
            df_ref[:, cs] = df.astype(df_ref.dtype)
            dv_ref[:, cs] = dv.astype(dv_ref.dtype)
            dlb_ref[:, cs] += dlb
            dst[h] = dstate

    def seg(off):
        return pl.BlockSpec((CHUNK, W), functools.partial(lambda g, c, off: (nc - 1 - c, off + g), off=off // W))

    row = pl.BlockSpec((CHUNK, W), lambda g, c: (nc - 1 - c, g))
    vec = pl.BlockSpec((1, W), lambda g, c: (0, g))
    return pl.pallas_call(
        body, name="hgrn_scan_bwd", grid=(HG_HEADS // HG_HB, nc),
        in_specs=[seg(P_HQ), seg(P_HF), seg(P_HI), vec,
                  pl.BlockSpec((None, HG_HB, 128, 128), lambda g, c: (nc - 1 - c, g, 0, 0)), row],
        out_specs=[row, row, row, vec],
        out_shape=[jax.ShapeDtypeStruct((S, HG_WIDTH), BF16)] * 3 + [jax.ShapeDtypeStruct((1, HG_WIDTH), F32)],
        scratch_shapes=[pltpu.VMEM((HG_HB, 128, 128), F32)],
        compiler_params=_cp("parallel", "arbitrary"),
    )(proj, proj, proj, lb, hst, do)


def _silu_grad(x):
    s = jax.nn.sigmoid(x)
    return s * (1.0 + x * (1.0 - s))


def conv_fwd(proj, w, b, tm=512):
    S = proj.shape[0]
    tm = _fit(S, tm)
    G = 512
    off = P_XBC // G

    def body(cur_ref, prev_ref, w_ref, b_ref, act_ref, pre_ref):
        prev = prev_ref[...] * (pl.program_id(1) > 0).astype(F32)
        ext = jnp.concatenate([prev, cur_ref[...]], axis=0)
        n = tm + 8
        acc = b_ref[...] + jnp.zeros((tm, G), F32)
        for j in range(SSM_CONV):
            acc = acc + w_ref[j:j + 1, :] * pltpu.roll(ext, (n - 5 - j) % n, 0)[0:tm]
        pre_ref[...] = acc
        act_ref[...] = jax.nn.silu(acc)

    out = pl.BlockSpec((tm, G), lambda j, i: (i, j))
    return pl.pallas_call(
        body, name="conv_fwd", grid=(SSM_CONV_DIM // G, S // tm),
        in_specs=[pl.BlockSpec((tm, G), lambda j, i: (i, off + j)),
                  pl.BlockSpec((8, G), lambda j, i: (jnp.maximum(i * (tm // 8) - 1, 0), off + j)),
                  pl.BlockSpec((SSM_CONV, G), lambda j, i: (0, j)), pl.BlockSpec((1, G), lambda j, i: (0, j))],
        out_specs=[out, out], out_shape=[jax.ShapeDtypeStruct((S, SSM_CONV_DIM), F32)] * 2,
        compiler_params=_cp("parallel", "parallel"),
    )(proj, proj, w, b)


def conv_bwd(proj, pre, dact, w, tm=512):
    S = proj.shape[0]
    tm = _fit(S, tm)
    G = 512
    off = P_XBC // G
    nb = S // tm

    def body(x_ref, xp_ref, pre_ref, pren_ref, d_ref, dn_ref, w_ref, dx_ref, dw_ref, db_ref):
        i = pl.program_id(1)
        n = tm + 8
        dpre = d_ref[...] * _silu_grad(pre_ref[...])
        dpre_next = dn_ref[...] * _silu_grad(pren_ref[...]) * (i < nb - 1).astype(F32)
        dext = jnp.concatenate([dpre, dpre_next], axis=0)
        xext = jnp.concatenate([xp_ref[...] * (i > 0).astype(F32), x_ref[...]], axis=0)
        dx = jnp.zeros((tm, G), F32)
        dws = []
        for j in range(SSM_CONV):
            dx = dx + w_ref[j:j + 1, :] * pltpu.roll(dext, (n - (3 - j)) % n, 0)[0:tm]
            dws.append(jnp.sum(dpre * pltpu.roll(xext, (n - 5 - j) % n, 0)[0:tm], axis=0, keepdims=True))
        dx_ref[...] = dx.astype(dx_ref.dtype)

        @pl.when(i == 0)
        def _():
            dw_ref[...] = jnp.zeros_like(dw_ref)
            db_ref[...] = jnp.zeros_like(db_ref)

        dw_ref[...] += jnp.concatenate(dws, axis=0)
        db_ref[...] += jnp.sum(dpre, axis=0, keepdims=True)

    cur = pl.BlockSpec((tm, G), lambda j, i: (i, j))
    nxt = pl.BlockSpec((8, G), lambda j, i: (jnp.minimum((i + 1) * (tm // 8), S // 8 - 1), j))
    return pl.pallas_call(
        body, name="conv_bwd", grid=(SSM_CONV_DIM // G, nb),
        in_specs=[pl.BlockSpec((tm, G), lambda j, i: (i, off + j)),
                  pl.BlockSpec((8, G), lambda j, i: (jnp.maximum(i * (tm // 8) - 1, 0), off + j)),
                  cur, nxt, cur, nxt, pl.BlockSpec((SSM_CONV, G), lambda j, i: (0, j))],
        out_specs=[cur, pl.BlockSpec((SSM_CONV, G), lambda j, i: (0, j)), pl.BlockSpec((1, G), lambda j, i: (0, j))],
        out_shape=[jax.ShapeDtypeStruct((S, SSM_CONV_DIM), BF16), jax.ShapeDtypeStruct((SSM_CONV, SSM_CONV_DIM), F32),
                   jax.ShapeDtypeStruct((1, SSM_CONV_DIM), F32)],
        compiler_params=_cp("parallel", "arbitrary"),
    )(proj, proj, pre, pre, dact, dact, w)


def _eye_dot(a, mode):
    Q = a.shape[0] if mode == "tn" else a.shape[1]
    eye = (lax.broadcasted_iota(jnp.int32, (Q, Q), 0) == lax.broadcasted_iota(jnp.int32, (Q, Q), 1)).astype(BF16)
    hi = a.astype(BF16)
    r1 = a - hi.astype(F32)
    mid = r1.astype(BF16)
    lo = (r1 - mid.astype(F32)).astype(BF16)
    if mode == "tn":
        return sum(lax.dot_general(p, eye, _DIMS["tn"], preferred_element_type=F32) for p in (hi, mid, lo))
    return sum(lax.dot_general(eye, p, _DIMS["nt"], preferred_element_type=F32) for p in (hi, mid, lo))


@jax.custom_vjp
def _transpose_exact(a):
    return _eye_dot(a, "tn")


_transpose_exact.defvjp(lambda a: (_eye_dot(a, "tn"), None), lambda _, g: (_eye_dot(g, "nt"),))


def ssd_decay_inputs(dtr, bias, alog):
    dt = jax.nn.softplus(dtr + bias)
    acum = cumsum_rows(dt * -jnp.exp(alog))
    return dt, acum, _transpose_exact(acum)


def ssd_head(x, cb, bm, cm, dt, acum, a_s, dsk, h_prev):
    Q = x.shape[0]
    r = lax.broadcasted_iota(jnp.int32, (Q, Q), 0)
    c = lax.broadcasted_iota(jnp.int32, (Q, Q), 1)
    a_l = jnp.broadcast_to(acum, (Q, Q))
    decay = jnp.where(c <= r, jnp.exp(jnp.minimum(a_l - a_s, 0.0)), 0.0)
    xdt = x * dt
    y_diag = _dot(cb * decay, xdt, "nn")
    a_last = acum[Q - 1:Q]
    states = _dot(xdt * jnp.exp(a_last - acum), bm, "tn")
    h_new = h_prev * jnp.exp(a_last) + states
    y_off = _dot(cm, h_prev, "nt") * jnp.exp(acum)
    return y_diag + y_off + x * dsk, h_new


def _ssd_specs(S, rev):
    Q = _fit(S, SSD_Q)
    nc = S // Q
    ci = (lambda c: nc - 1 - c) if rev else (lambda c: c)
    x = pl.BlockSpec((Q, 512), lambda g, c: (ci(c), g))
    bm = pl.BlockSpec((Q, 128), lambda g, c: (ci(c), SSM_INNER // 128 + g))
    cm = pl.BlockSpec((Q, 128), lambda g, c: (ci(c), SSM_INNER // 128 + SSM_GROUPS + g))
    dtr = pl.BlockSpec((None, Q, 8), lambda g, c: (g, ci(c), 0))
    par = pl.BlockSpec((None, 1, 8), lambda g, c: (g, 0, 0))
    hs = pl.BlockSpec((None, 8, SSM_HEADDIM, SSM_STATE), lambda g, c: (ci(c), g, 0, 0))
    return Q, nc, x, bm, cm, dtr, par, hs


def ssd_scan_fwd(act, dtr, bias, alog, dsk):
    S = act.shape[0]
    Q, nc, x_s, bm_s, cm_s, dtr_s, par_s, hs_s = _ssd_specs(S, False)

    def body(x_ref, bm_ref, cm_ref, dtr_ref, b_ref, a_ref, d_ref, y_ref, hs_ref, st, dt_s, ac_s, act_s):
        @pl.when(pl.program_id(1) == 0)
        def _():
            st[...] = jnp.zeros_like(st)

        dt_s[...], ac_s[...], act_s[...] = ssd_decay_inputs(dtr_ref[...], b_ref[...], a_ref[...])
        bm, cm = bm_ref[...], cm_ref[...]
        cb = _dot(cm, bm, "nt")
        for j in range(8):
            hs_ref[j] = st[j]
            y, h_new = ssd_head(x_ref[:, j * 64:(j + 1) * 64], cb, bm, cm, dt_s[:, j:j + 1], ac_s[:, j:j + 1], act_s[j:j + 1, :],
                                d_ref[:, j:j + 1], st[j])
            y_ref[:, j * 64:(j + 1) * 64] = y
            st[j] = h_new

    return pl.pallas_call(
        body, name="ssd_scan_fwd", grid=(SSM_GROUPS, nc),
        in_specs=[x_s, bm_s, cm_s, dtr_s, par_s, par_s, par_s], out_specs=[x_s, hs_s],
        out_shape=[jax.ShapeDtypeStruct((S, SSM_INNER), F32), jax.ShapeDtypeStruct((nc, SSM_HEADS, SSM_HEADDIM, SSM_STATE), F32)],
        scratch_shapes=[pltpu.VMEM((8, SSM_HEADDIM, SSM_STATE), F32), pltpu.VMEM((Q, 8), F32), pltpu.VMEM((Q, 8), F32),
                        pltpu.VMEM((8, Q), F32)],
        compiler_params=_cp("parallel", "arbitrary"),
    )(act, act, act, dtr, bias, alog, dsk)


def ssd_scan_bwd(act, dtr, bias, alog, dsk, hs, dy):
    S = act.shape[0]
    Q, nc, x_s, bm_s, cm_s, dtr_s, par_s, hs_s = _ssd_specs(S, True)
    g_s = pl.BlockSpec((Q, 128), lambda g, c: (nc - 1 - c, g))

    def body(x_ref, bm_ref, cm_ref, dtr_ref, b_ref, a_ref, d_ref, hs_ref, dy_ref,
             dx_ref, dbm_ref, dcm_ref, ddtr_ref, db_ref, da_ref, dd_ref, dst, dt_s, ac_s, ddt_s, dac_s, act_s, dact_s):
        @pl.when(pl.program_id(1) == 0)
        def _():
            dst[...] = jnp.zeros_like(dst)
            db_ref[...] = jnp.zeros_like(db_ref)
            da_ref[...] = jnp.zeros_like(da_ref)
            dd_ref[...] = jnp.zeros_like(dd_ref)

        (dt_s[...], ac_s[...], act_s[...]), decay_vjp = jax.vjp(ssd_decay_inputs, dtr_ref[...], b_ref[...], a_ref[...])
        bm, cm = bm_ref[...], cm_ref[...]
        cb = _dot(cm, bm, "nt")
        dcb = jnp.zeros((Q, Q), F32)
        dbm = jnp.zeros((Q, 128), F32)
        dcm = jnp.zeros((Q, 128), F32)
        for j in range(8):
            cs = slice(j * 64, (j + 1) * 64)
            one = slice(j, j + 1)
            _, vjp = jax.vjp(ssd_head, x_ref[:, cs], cb, bm, cm, dt_s[:, one], ac_s[:, one], act_s[one, :], d_ref[:, one], hs_ref[j])
            dx, gcb, gb, gc, gdt, gac, gact, gdsk, gh = vjp((dy_ref[:, cs], dst[j]))
            dact_s[one, :] = gact
            dx_ref[:, cs] = dx
            dcb = dcb + gcb
            dbm = dbm + gb
            dcm = dcm + gc
            ddt_s[:, one] = gdt
            dac_s[:, one] = gac
            dd_ref[:, one] += gdsk
            dst[j] = gh
        dbm_ref[...] = dbm + _dot(dcb, cm, "tn")
        dcm_ref[...] = dcm + _dot(dcb, bm, "nn")
        ddtr, dbias, dalog = decay_vjp((ddt_s[...], dac_s[...], dact_s[...]))
        ddtr_ref[...] = ddtr
        db_ref[...] += dbias
        da_ref[...] += dalog

    return pl.pallas_call(
        body, name="ssd_scan_bwd", grid=(SSM_GROUPS, nc),
        in_specs=[x_s, bm_s, cm_s, dtr_s, par_s, par_s, par_s, hs_s, x_s],
        out_specs=[x_s, g_s, g_s, dtr_s, par_s, par_s, par_s],
        out_shape=[jax.ShapeDtypeStruct((S, SSM_INNER), F32), jax.ShapeDtypeStruct((S, 1024), F32), jax.ShapeDtypeStruct((S, 1024), F32),
                   jax.ShapeDtypeStruct((SSM_GROUPS, S, 8), F32)] + [jax.ShapeDtypeStruct((SSM_GROUPS, 1, 8), F32)] * 3,
        scratch_shapes=[pltpu.VMEM((8, SSM_HEADDIM, SSM_STATE), F32)] + [pltpu.VMEM((Q, 8), F32)] * 4 + [pltpu.VMEM((8, Q), F32)] * 2,
        compiler_params=_cp("parallel", "arbitrary"),
    )(act, act, act, dtr, bias, alog, dsk, hs, dy)


def _w_in_to_local(w):
    parts = [w[..., 0:1024], w[..., 1088:19520], w[..., 19584:25728], w[..., 1024:1088], w[..., 19520:19584],
             jnp.zeros(w.shape[:-1] + (IN_PAD - IN_DIM,), w.dtype)]
    return jnp.concatenate(parts, axis=-1)


def _w_in_from_local(g):
    return jnp.concatenate([g[..., 0:1024], g[..., 25600:25664], g[..., 1024:19456], g[..., 25664:25728], g[..., 19456:25600]], axis=-1)


def _w_uq_to_local(w):
    lead = w.shape[:-1]
    w = w.reshape(lead + (MLA_HEADS, 192))
    nope = w[..., :128].reshape(lead + (8, 256))
    rope = w[..., 128:].reshape(lead + (8, 128))
    return jnp.concatenate([nope, rope], axis=-1).reshape(lead + (3072,))


def _w_uq_from_local(g):
    lead = g.shape[:-1]
    g = g.reshape(lead + (8, 384))
    nope = g[..., :256].reshape(lead + (MLA_HEADS, 128))
    rope = g[..., 256:].reshape(lead + (MLA_HEADS, 64))
    return jnp.concatenate([nope, rope], axis=-1).reshape(lead + (3072,))


BIG = ("ffn1_wi", "ffn1_wo", "w_in", "mla_w_uq", "mla_w_ukv", "w_o_mla", "w_o_hgrn", "w_o_ssm", "w_out", "ffn2_wi", "ffn2_wo")
COL_SHARDED = ("ffn1_wi", "w_in", "mla_w_uq", "mla_w_ukv", "ffn2_wi")
SMALL = ("ffn1_norm", "mix_norm", "mla_q_norm", "mla_kv_norm", "hgrn_lb_logits", "hgrn_norm", "ssm_conv_b", "ssm_a_log",
         "ssm_dt_bias", "ssm_d", "ssm_norm", "ffn2_norm")


def _ffn_fwd(tag, x, norm_w, wi, wo):
    (h,) = rowwise(tag + "_norm", f_rmsnorm, [(x, D_MODEL, 0)], [(norm_w, D_MODEL, 0)], [(D_MODEL, D_MODEL, BF16)])
    gu = matmul(tag + "_wi", h, wi)
    (a,) = rowwise(tag + "_act", f_swiglu, [(gu, 512, 0), (gu, 512, D_FF // 512)], [], [(D_FF, 512, BF16)], n_groups=D_FF // 512)
    out = matmul(tag + "_wo", a, wo, alpha=0.5, res=x)
    return out, (x, h, gu, a)


def _ffn_bwd(tag, dx, dxb, saved, norm_w, wi, wo):
    x, h, gu, a = saved
    da = matmul(tag + "_da", dxb, wo, "nt", alpha=0.5)
    dwo = matmul(tag + "_dwo", a, dxb, "tn", alpha=0.5)
    dg, du = rowwise_bwd(tag + "_act_bwd", f_swiglu, [(gu, 512, 0), (gu, 512, D_FF // 512)], [], [(da, 512)],
                         [(0, D_FF, BF16), (1, D_FF, BF16)], [], n_groups=D_FF // 512)
    dgu = jnp.concatenate([dg, du], axis=1)
    dh = matmul(tag + "_dh", dgu, wi, "nt")
    dwi = matmul(tag + "_dwi", h, dgu, "tn")
    dx_in, dxb_in, dnorm = rowwise_bwd(tag + "_norm_bwd", f_rmsnorm, [(x, D_MODEL, 0)], [(norm_w, D_MODEL, 0)], [(dh, D_MODEL)],
                                       [(0, D_MODEL, F32), (0, D_MODEL, BF16)], [0], adds={0: dx}, tm=256)
    return dx_in, dxb_in, dnorm, dwi, dwo


def _ssm_params(W):
    return [W[k].reshape(SSM_GROUPS, 1, 8) for k in ("ssm_dt_bias", "ssm_a_log", "ssm_d")]


def _dt_cols(proj):
    S = proj.shape[0]
    return proj[:, P_KPEDT + 64:P_KPEDT + 128].reshape(S, SSM_GROUPS, 8).transpose(1, 0, 2)


def _mix_fwd(x, W, lb, tabs):
    cosf, sinf = tabs
    (h,) = rowwise("mix_norm", f_rmsnorm, [(x, D_MODEL, 0)], [(W["mix_norm"], D_MODEL, 0)], [(D_MODEL, D_MODEL, BF16)])
    proj = matmul("w_in", h, W["w_in"])
    (qn,) = rowwise("q_norm", f_rmsnorm, [(proj, 512, 0)], [(W["mla_q_norm"], 512, 0)], [(512, 512, BF16)])
    (kvn,) = rowwise("kv_norm", f_rmsnorm, [(proj, 512, 1)], [(W["mla_kv_norm"], 512, 0)], [(512, 512, BF16)])
    q = matmul("w_uq", qn, W["mla_w_uq"])
    kv = matmul("w_ukv", kvn, W["mla_w_ukv"])
    qc = mla_q_prep(q, cosf, sinf)
    kc, vb = mla_k_prep(kv, proj, cosf, sinf)
    o_a, lse = attention_fwd(qc, kc, vb)
    y_a = matmul("w_o_mla", o_a, W["w_o_mla"])
    o_h, hst = hgrn_scan_fwd(proj, lb)
    (pre_b,) = rowwise("hgrn_out", f_hgrn_out, [(o_h, 128, 0), (proj, 128, P_HGATE // 128)], [(W["hgrn_norm"], 128, 0)],
                       [(HG_WIDTH, 128, BF16)], n_groups=HG_HEADS, tm=1024)
    y_b = matmul("w_o_hgrn", pre_b, W["w_o_hgrn"])
    act, pre = conv_fwd(proj, W["ssm_conv_w"], W["ssm_conv_b"])
    dtr = _dt_cols(proj)
    y_s, hs = ssd_scan_fwd(act, dtr, *_ssm_params(W))
    (pre_c,) = rowwise("ssm_out", f_ssm_out, [(y_s, 512, 0), (proj, 512, P_Z // 512)], [(W["ssm_norm"], 512, 0)],
                       [(SSM_INNER, 512, BF16)], n_groups=SSM_GROUPS)
    y_c = matmul("w_o_ssm", pre_c, W["w_o_ssm"])
    g0 = P_GATES // 512
    (merged,) = rowwise("merge", f_merge, [(proj, 512, g0), (proj, 512, g0 + 4), (proj, 512, g0 + 8), (y_a, 512, 0), (y_b, 512, 0), (y_c, 512, 0)],
                        [], [(D_MODEL, 512, BF16)], n_groups=4)
    out = matmul("w_out", merged, W["w_out"], res=x)
    return out, (x, h, proj, qn, kvn, qc, kc, vb, o_a, lse, y_a, o_h, hst, pre_b, y_b, act, pre, dtr, y_s, hs, pre_c, y_c, merged)


def _mix_bwd(dx, dxb, saved, W, lb, tabs):
    cosf, sinf = tabs
    (x, h, proj, qn, kvn, qc, kc, vb, o_a, lse, y_a, o_h, hst, pre_b, y_b, act, pre, dtr, y_s, hs, pre_c, y_c, merged) = saved
    S = x.shape[0]
    g = {}
    dmerged = matmul("d_merged", dxb, W["w_out"], "nt")
    g["w_out"] = matmul("dw_out", merged, dxb, "tn")
    g0 = P_GATES // 512
    dga, dgb, dgc, dya, dyb, dyc = rowwise_bwd(
        "merge_bwd", f_merge, [(proj, 512, g0), (proj, 512, g0 + 4), (proj, 512, g0 + 8), (y_a, 512, 0), (y_b, 512, 0), (y_c, 512, 0)],
        [], [(dmerged, 512)], [(k, D_MODEL, BF16) for k in range(6)], [], n_groups=4)
    do_a = matmul("d_o_mla", dya, W["w_o_mla"], "nt", out_dtype=BF16)
    g["w_o_mla"] = matmul("dw_o_mla", o_a, dya, "tn")
    delta = attention_delta(o_a, do_a)
    dqc = attention_bwd_dq(qc, kc, vb, do_a, lse, delta)
    dkc, dv = attention_bwd_dkv(qc, kc, vb, do_a, lse, delta)
    dq = mla_q_prep_bwd(dqc, cosf, sinf)
    dkv, dpe = mla_k_prep_bwd(dkc, dv, cosf, sinf)
    dqn = matmul("d_qn", dq, W["mla_w_uq"], "nt")
    g["mla_w_uq"] = matmul("dw_uq", qn, dq, "tn")
    dkvn = matmul("d_kvn", dkv, W["mla_w_ukv"], "nt")
    g["mla_w_ukv"] = matmul("dw_ukv", kvn, dkv, "tn")
    dq_lat, g["mla_q_norm"] = rowwise_bwd("q_norm_bwd", f_rmsnorm, [(proj, 512, 0)], [(W["mla_q_norm"], 512, 0)], [(dqn, 512)],
                                          [(0, 512, BF16)], [0])
    dkv_lat, g["mla_kv_norm"] = rowwise_bwd("kv_norm_bwd", f_rmsnorm, [(proj, 512, 1)], [(W["mla_kv_norm"], 512, 0)], [(dkvn, 512)],
                                            [(0, 512, BF16)], [0])
    do_b = matmul("d_o_hgrn", dyb, W["w_o_hgrn"], "nt")
    g["w_o_hgrn"] = matmul("dw_o_hgrn", pre_b, dyb, "tn")
    do_h, dhgate, g["hgrn_norm"] = rowwise_bwd(
        "hgrn_out_bwd", f_hgrn_out, [(o_h, 128, 0), (proj, 128, P_HGATE // 128)], [(W["hgrn_norm"], 128, 0)], [(do_b, 128)],
        [(0, HG_WIDTH, F32), (1, HG_WIDTH, BF16)], [0], n_groups=HG_HEADS, tm=1024)
    dhq, dhf, dhi, dlb = hgrn_scan_bwd(proj, lb, hst, do_h)
    do_c = matmul("d_o_ssm", dyc, W["w_o_ssm"], "nt")
    g["w_o_ssm"] = matmul("dw_o_ssm", pre_c, dyc, "tn")
    dy_s, dz, g["ssm_norm"] = rowwise_bwd(
        "ssm_out_bwd", f_ssm_out, [(y_s, 512, 0), (proj, 512, P_Z // 512)], [(W["ssm_norm"], 512, 0)], [(do_c, 512)],
        [(0, SSM_INNER, F32), (1, SSM_INNER, BF16)], [0], n_groups=SSM_GROUPS)
    dxs, dbm, dcm, ddtr, dbias, dalog, ddsk = ssd_scan_bwd(act, dtr, *_ssm_params(W), hs, dy_s)
    g["ssm_dt_bias"], g["ssm_a_log"], g["ssm_d"] = (v.reshape(1, SSM_HEADS) for v in (dbias, dalog, ddsk))
    dxbc, g["ssm_conv_w"], g["ssm_conv_b"] = conv_bwd(proj, pre, jnp.concatenate([dxs, dbm, dcm], axis=1), W["ssm_conv_w"])
    ddt = ddtr.transpose(1, 0, 2).reshape(S, SSM_HEADS)
    dproj = jnp.concatenate([dq_lat, dkv_lat, dhq, dhf, dhi, dhgate, dz, dxbc, dga, dgb, dgc, dpe[:, :64].astype(BF16),
                             ddt.astype(BF16), jnp.zeros((S, IN_PAD - IN_DIM), BF16)], axis=1)
    dh = matmul("d_h_mix", dproj, W["w_in"], "nt")
    g["w_in"] = matmul("dw_in", h, dproj, "tn")
    dx_in, dxb_in, g["mix_norm"] = rowwise_bwd("mix_norm_bwd", f_rmsnorm, [(x, D_MODEL, 0)], [(W["mix_norm"], D_MODEL, 0)], [(dh, D_MODEL)],
                                               [(0, D_MODEL, F32), (0, D_MODEL, BF16)], [0], adds={0: dx}, tm=256)
    return dx_in, dxb_in, g, dlb


def local_step(x, target, Wl, small, final_norm):
    S = x.shape[0]
    tabs = rope_tables(S)
    (lbs,) = rowwise("lower_bounds", f_lower_bounds, [(small["hgrn_lb_logits"], HG_WIDTH, 0)], [], [(HG_WIDTH, HG_WIDTH, F32)])
    saved = []
    for l in range(DEPTH):
        W = dict(Wl[l])
        for k in SMALL:
            W[k] = small[k][l:l + 1]
        lb = lbs[l:l + 1]
        x, s1 = _ffn_fwd("ffn1", x, W["ffn1_norm"], W["ffn1_wi"], W["ffn1_wo"])
        x, s2 = _mix_fwd(x, W, lb, tabs)
        x, s3 = _ffn_fwd("ffn2", x, W["ffn2_norm"], W["ffn2_wi"], W["ffn2_wo"])
        saved.append((W, lb, s1, s2, s3))
    dx, dxb, dfinal, loss = loss_head(x, target, final_norm.reshape(1, D_MODEL))
    grads = [None] * DEPTH
    dlbs = [None] * DEPTH
    for l in reversed(range(DEPTH)):
        W, lb, s1, s2, s3 = saved[l]
        g = {}
        dx, dxb, g["ffn2_norm"], g["ffn2_wi"], g["ffn2_wo"] = _ffn_bwd("ffn2", dx, dxb, s3, W["ffn2_norm"], W["ffn2_wi"], W["ffn2_wo"])
        dx, dxb, gm, dlbs[l] = _mix_bwd(dx, dxb, s2, W, lb, tabs)
        g.update(gm)
        dx, dxb, g["ffn1_norm"], g["ffn1_wi"], g["ffn1_wo"] = _ffn_bwd("ffn1", dx, dxb, s1, W["ffn1_norm"], W["ffn1_wi"], W["ffn1_wo"])
        grads[l] = g
    (dlogits,) = rowwise_bwd("lower_bounds_bwd", f_lower_bounds, [(small["hgrn_lb_logits"], HG_WIDTH, 0)], [],
                             [(jnp.concatenate(dlbs, axis=0), HG_WIDTH)], [(0, HG_WIDTH, F32)], [])
    return loss, dx, grads, dlogits, dfinal


ANY = pl.BlockSpec(memory_space=pl.ANY)


def _coords():
    return lax.axis_index("x"), lax.axis_index("y"), lax.axis_index("c")


def _exchange_call(name, body, src, out_shape, n_copies):
    return pl.pallas_call(
        body, name=name, in_specs=[ANY], out_specs=ANY, out_shape=out_shape,
        scratch_shapes=[pltpu.SemaphoreType.DMA((n_copies,)), pltpu.SemaphoreType.DMA((n_copies,)), pltpu.SemaphoreType.DMA],
    )(src)


def _multi_exchange(name, body, srcs, out_shapes, sem_counts):
    n = len(srcs)
    return pl.pallas_call(
        body, name=name, in_specs=[ANY] * n, out_specs=[ANY] * len(out_shapes), out_shape=out_shapes,
        scratch_shapes=[pltpu.SemaphoreType.DMA((k,)) for k in sem_counts],
    )(*srcs)


def _chip_peers(x, y):
    return [(1 - x, y), (x, 1 - y), (1 - x, 1 - y)]


def weights_allgather(srcs):
    n = len(srcs)

    def body(*refs):
        src, out = refs[:n], refs[n:2 * n]
        send_sems, recv_sems, fsend_sems, frecv_sems = refs[2 * n:]
        x, y, c = _coords()
        me = 2 * x + y
        peers = _chip_peers(x, y)

        def ici(i, k, chip):
            return pltpu.make_async_remote_copy(src[i].at[c], out[i].at[chip, c], send_sems.at[3 * i + k], recv_sems.at[3 * i + k],
                                                device_id=(*peers[k], c), device_id_type=MESH)

        def fwd(i, k, chip, layer):
            return pltpu.make_async_remote_copy(out[i].at[chip, layer], out[i].at[chip, layer], fsend_sems.at[3 * i + k],
                                                frecv_sems.at[3 * i + k], device_id=(x, y, 1 - c), device_id_type=MESH)

        sends = [ici(i, k, me) for k in range(3) for i in range(n)]
        for cp in sends:
            cp.start()
        passed = []
        for k, (px, py) in enumerate(peers):
            for i in range(n):
                ici(i, k, 2 * px + py).wait_recv()
                passed.append(fwd(i, k, 2 * px + py, c))
                passed[-1].start()
        for k, (px, py) in enumerate(peers):
            for i in range(n):
                fwd(i, k, 2 * px + py, 1 - c).wait_recv()
        for cp in sends + passed:
            cp.wait_send()

    out_shapes = [jax.ShapeDtypeStruct((4,) + s.shape, s.dtype) for s in srcs]
    return _multi_exchange("weights_allgather", body, srcs, out_shapes, (3 * n, 3 * n, 3 * n, 3 * n))


def grad_layer_exchange(gs):
    n = len(gs)

    def body(*refs):
        src, out = refs[:n], refs[n:2 * n]
        send_sems, recv_sems = refs[2 * n:]
        x, y, c = _coords()
        copies = [pltpu.make_async_remote_copy(src[i].at[1 - c], out[i], send_sems.at[i], recv_sems.at[i],
                                               device_id=(x, y, 1 - c), device_id_type=MESH) for i in range(n)]
        for cp in copies:
            cp.start()
        for cp in copies:
            cp.wait_recv()
        for cp in copies:
            cp.wait_send()

    return _multi_exchange("grad_layer_exchange", body, gs, [jax.ShapeDtypeStruct(g.shape[1:], g.dtype) for g in gs], (n, n))


def grad_chip_exchange(ps):
    n = len(ps)

    def body(*refs):
        src, out = refs[:n], refs[n:2 * n]
        send_sems, recv_sems = refs[2 * n:]
        x, y, c = _coords()
        copies = [pltpu.make_async_remote_copy(src[i].at[2 * px + py], out[i].at[k], send_sems.at[3 * i + k], recv_sems.at[3 * i + k],
                                               device_id=(px, py, c), device_id_type=MESH)
                  for k, (px, py) in enumerate(_chip_peers(x, y)) for i in range(n)]
        for cp in copies:
            cp.start()
        for cp in copies:
            cp.wait_recv()
        for cp in copies:
            cp.wait_send()

    return _multi_exchange("grad_chip_exchange", body, ps, [jax.ShapeDtypeStruct((3,) + p.shape[1:], p.dtype) for p in ps], (3 * n, 3 * n))


def grad_reduced_exchange(rs):
    n = len(rs)

    def body(*refs):
        src, out = refs[:n], refs[n:2 * n]
        send_sems, recv_sems = refs[2 * n:]
        x, y, c = _coords()
        copies = [pltpu.make_async_remote_copy(src[i], out[i], send_sems.at[i], recv_sems.at[i],
                                               device_id=(x, y, 1 - c), device_id_type=MESH) for i in range(n)]
        for cp in copies:
            cp.start()
        for cp in copies:
            cp.wait_recv()
        for cp in copies:
            cp.wait_send()

    return _multi_exchange("grad_reduced_exchange", body, rs, [jax.ShapeDtypeStruct(r.shape, r.dtype) for r in rs], (n, n))


def device_allgather(name, src):
    def body(src_ref, out_ref, send_sems, recv_sems, local_sem):
        x, y, c = _coords()
        me = 4 * x + 2 * y + c
        peers = [(x ^ (m >> 2), y ^ ((m >> 1) & 1), c ^ (m & 1)) for m in range(1, 8)]

        def copy(k, dev):
            return pltpu.make_async_remote_copy(src_ref, out_ref.at[dev], send_sems.at[k], recv_sems.at[k],
                                                device_id=peers[k], device_id_type=MESH)

        local = pltpu.make_async_copy(src_ref, out_ref.at[me], local_sem)
        local.start()
        sends = [copy(k, me) for k in range(7)]
        for s in sends:
            s.start()
        for k, (px, py, pc) in enumerate(peers):
            copy(k, 4 * px + 2 * py + pc).wait_recv()
        for s in sends:
            s.wait_send()
        local.wait()

    return _exchange_call(name, body, src, jax.ShapeDtypeStruct((8,) + src.shape, src.dtype), 7)


BLOCK_ELEMS = 1 << 19


def add_pair(name, g, recv, c):
    _, n, R, C = g.shape
    tr = _fit_rows(R, max(16, BLOCK_ELEMS // C))

    def body(c_ref, g_ref, r_ref, o_ref):
        o_ref[...] = (g_ref[...].astype(F32) + r_ref[...].astype(F32)).astype(o_ref.dtype)

    blk = pl.BlockSpec((None, tr, C), lambda j, i, c_ref: (j, i, 0))
    return pl.pallas_call(
        body, name=name, out_shape=jax.ShapeDtypeStruct((n, R, C), g.dtype),
        grid_spec=pltpu.PrefetchScalarGridSpec(
            num_scalar_prefetch=1, grid=(n, R // tr),
            in_specs=[pl.BlockSpec((None, None, tr, C), lambda j, i, c_ref: (c_ref[0], j, i, 0)), blk], out_specs=blk),
        compiler_params=_cp("parallel", "parallel"),
    )(c, g, recv)


def sum_chips(name, own, recv, me):
    _, R, C = own.shape
    tr = _fit_rows(R, max(16, BLOCK_ELEMS // (2 * C)))

    def body(me_ref, o_ref, r_ref, out_ref):
        acc = o_ref[...].astype(F32)
        for k in range(3):
            acc = acc + r_ref[k].astype(F32)
        out_ref[...] = acc

    return pl.pallas_call(
        body, name=name, out_shape=jax.ShapeDtypeStruct((R, C), F32),
        grid_spec=pltpu.PrefetchScalarGridSpec(
            num_scalar_prefetch=1, grid=(R // tr,),
            in_specs=[pl.BlockSpec((None, tr, C), lambda i, me_ref: (me_ref[0], i, 0)), pl.BlockSpec((3, tr, C), lambda i, me_ref: (0, i, 0))],
            out_specs=pl.BlockSpec((tr, C), lambda i, me_ref: (i, 0))),
        compiler_params=_cp("parallel"),
    )(me, own, recv)


def sum_devices(parts):
    _, R, C = parts.shape

    def body(p_ref, o_ref):
        acc = p_ref[0]
        for k in range(1, 8):
            acc = acc + p_ref[k]
        o_ref[...] = acc

    return pl.pallas_call(body, name="sum_devices", out_shape=jax.ShapeDtypeStruct((R, C), F32))(parts)


def _fit_rows(R, pref):
    for t in range(min(pref, R), 0, -1):
        if R % t == 0 and (t % 16 == 0 or t == R):
            return t
    raise ValueError((R, pref))


def adamw(name, w, g, m, v):
    shape = w.shape
    C = shape[-1]
    w2, g2, m2, v2 = (a.reshape(-1, C) for a in (w, g, m, v))
    R = w2.shape[0]
    tr = _fit_rows(R, max(8, (1 << 18) // C)) if R * C > (1 << 18) else R
    c1 = 1.0 - ADAM_B1 ** ADAM_STEP
    c2 = 1.0 - ADAM_B2 ** ADAM_STEP

    def body(w_ref, g_ref, m_ref, v_ref, d_ref, mo_ref, vo_ref):
        gg = g_ref[...]
        mn = ADAM_B1 * m_ref[...] + (1.0 - ADAM_B1) * gg
        vn = ADAM_B2 * v_ref[...] + (1.0 - ADAM_B2) * jnp.square(gg)
        d_ref[...] = -ADAM_LR * ((mn / c1) / (jnp.sqrt(vn / c2) + ADAM_EPS) + ADAM_WD * w_ref[...])
        mo_ref[...] = mn
        vo_ref[...] = vn

    blk = pl.BlockSpec((tr, C), lambda i: (i, 0))
    outs = pl.pallas_call(
        body, name=name, grid=(R // tr,), in_specs=[blk] * 4, out_specs=[blk] * 3,
        out_shape=[jax.ShapeDtypeStruct((R, C), F32)] * 3, compiler_params=_cp("parallel"),
    )(w2, g2, m2, v2)
    return tuple(o.reshape(shape) for o in outs)


def _unshard(name, parts):
    n, L, r, c = parts.shape
    if name in COL_SHARDED:
        return parts.transpose(1, 2, 0, 3).reshape(L, r, n * c)
    return parts.transpose(1, 0, 2, 3).reshape(L, n * r, c)


def _shard(name, full):
    L, R, C = full.shape
    if name in COL_SHARDED:
        return full.reshape(L, R, 4, C // 4).transpose(0, 2, 1, 3)
    return full.reshape(L, 4, R // 4, C)


def _to_local(name, w):
    if name == "w_in":
        return _w_in_to_local(w)
    if name == "mla_w_uq":
        return _w_uq_to_local(w)
    return w


def _from_local(name, g):
    if name == "w_in":
        return _w_in_from_local(g)
    if name == "mla_w_uq":
        return _w_uq_from_local(g)
    return g


TWIN_WEIGHTS = ("ffn1_norm", "ffn1_wi", "ffn1_wo", "mix_norm", "w_in", "mla_q_norm", "mla_w_uq", "mla_kv_norm", "mla_w_ukv",
                "hgrn_lb_logits", "hgrn_norm", "ssm_conv_w", "ssm_conv_b", "ssm_a_log", "ssm_dt_bias", "ssm_d", "ssm_norm",
                "w_o_mla", "w_o_hgrn", "w_o_ssm", "w_out", "ffn2_norm", "ffn2_wi", "ffn2_wo", "final_norm")
SMALL_PACK = SMALL + ("ssm_conv_w", "final_norm")


def _pad_rows(flat, cols):
    n = flat.shape[0]
    rows = -(-n // cols)
    rows = -(-rows // 8) * 8
    return jnp.concatenate([flat, jnp.zeros((rows * cols - n,), flat.dtype)]).reshape(rows, cols)


def kernel(x, ffn1_norm, ffn1_wi, ffn1_wo, mix_norm, w_in, mla_q_norm, mla_w_uq, mla_kv_norm, mla_w_ukv, hgrn_lb_logits, hgrn_norm, ssm_conv_w, ssm_conv_b, ssm_a_log, ssm_dt_bias, ssm_d, ssm_norm, w_o_mla, w_o_hgrn, w_o_ssm, w_out, ffn2_norm, ffn2_wi, ffn2_wo, final_norm, loss_target, m_ffn1_norm, m_ffn1_wi, m_ffn1_wo, m_mix_norm, m_w_in, m_mla_q_norm, m_mla_w_uq, m_mla_kv_norm, m_mla_w_ukv, m_hgrn_lb_logits, m_hgrn_norm, m_ssm_conv_w, m_ssm_conv_b, m_ssm_a_log, m_ssm_dt_bias, m_ssm_d, m_ssm_norm, m_w_o_mla, m_w_o_hgrn, m_w_o_ssm, m_w_out, m_ffn2_norm, m_ffn2_wi, m_ffn2_wo, m_final_norm, v_ffn1_norm, v_ffn1_wi, v_ffn1_wo, v_mix_norm, v_w_in, v_mla_q_norm, v_mla_w_uq, v_mla_kv_norm, v_mla_w_ukv, v_hgrn_lb_logits, v_hgrn_norm, v_ssm_conv_w, v_ssm_conv_b, v_ssm_a_log, v_ssm_dt_bias, v_ssm_d, v_ssm_norm, v_w_o_mla, v_w_o_hgrn, v_w_o_ssm, v_w_out, v_ffn2_norm, v_ffn2_wi, v_ffn2_wo, v_final_norm):
    args = dict(locals())
    w = {n: args[n] for n in TWIN_WEIGHTS}
    m = {n: args["m_" + n] for n in TWIN_WEIGHTS}
    v = {n: args["v_" + n] for n in TWIN_WEIGHTS}
    xi, yi, ci = _coords()
    chip = 2 * xi + yi

    shards = [w[n].astype(BF16) for n in BIG]
    gathered = [lax.dynamic_update_slice(parts, own[None], (chip, 0, 0, 0)) for parts, own in zip(weights_allgather(shards), shards)]
    full = {n: _to_local(n, _unshard(n, parts)) for n, parts in zip(BIG, gathered)}
    conv_parts = device_allgather("conv_allgather", _pad_rows(w["ssm_conv_w"].reshape(-1), 128))
    conv_full = jnp.concatenate(
        [conv_parts[2 * j].reshape(-1)[:w["ssm_conv_w"].size].reshape(w["ssm_conv_w"].shape) for j in range(4)], axis=-1)
    Wl = [dict({n: full[n][l] for n in BIG}, ssm_conv_w=conv_full[l]) for l in range(DEPTH)]
    small = {n: w[n] for n in SMALL}

    loss, grad_x, grads, dlogits, dfinal = local_step(x[0], loss_target[0], Wl, small, w["final_norm"])
    loss = lax.psum(loss[0, 0], ("x", "y", "c"))

    c_idx, chip_idx = ci.astype(jnp.int32).reshape(1), chip.astype(jnp.int32).reshape(1)
    gs = [_shard(n, _from_local(n, jnp.stack([grads[l][n] for l in range(DEPTH)]))).astype(BF16) for n in BIG]
    from_sibling = grad_layer_exchange(gs)
    pair_sums = [add_pair("add_pair_" + n, a, b, c_idx) for n, a, b in zip(BIG, gs, from_sibling)]
    from_chips = grad_chip_exchange(pair_sums)
    reduced = [sum_chips("sum_chips_" + n, a, b, chip_idx) for n, a, b in zip(BIG, pair_sums, from_chips)]
    others = grad_reduced_exchange(reduced)
    g = {n: jnp.where(ci == 0, jnp.stack([mine, other]), jnp.stack([other, mine])) for n, mine, other in zip(BIG, reduced, others)}

    sg = {n: jnp.concatenate([grads[l][n] for l in range(DEPTH)], axis=0) for n in SMALL if n != "hgrn_lb_logits"}
    sg["hgrn_lb_logits"] = dlogits
    sg["ssm_conv_w"] = jnp.stack([grads[l]["ssm_conv_w"] for l in range(DEPTH)])
    sg["final_norm"] = dfinal
    spack = _pad_rows(jnp.concatenate([sg[n].reshape(-1) for n in SMALL_PACK]), 128)
    ssum = sum_devices(device_allgather("small_grads_allgather", spack)).reshape(-1)
    off = 0
    for n in SMALL_PACK:
        size = sg[n].size
        g[n] = ssum[off:off + size].reshape(sg[n].shape)
        off += size
    shard_cols = w["ssm_conv_w"].shape[-1]
    g["ssm_conv_w"] = lax.dynamic_slice_in_dim(g["ssm_conv_w"], chip * shard_cols, shard_cols, axis=2)
    g = {n: g[n].reshape(w[n].shape) for n in TWIN_WEIGHTS}

    upd = {n: adamw("adamw_" + n, w[n], g[n], m[n], v[n]) for n in TWIN_WEIGHTS}
    return (loss, grad_x[None], *[g[n] for n in TWIN_WEIGHTS], *[upd[n][0] for n in TWIN_WEIGHTS],
            *[upd[n][1] for n in TWIN_WEIGHTS], *[upd[n][2] for n in TWIN_WEIGHTS])
```

```python
import functools
import math

import jax
import jax.numpy as jnp
import numpy as np
from jax import lax
from jax.experimental import pallas as pl
from jax.experimental.pallas import tpu as pltpu

F32 = jnp.float32
BF16 = jnp.bfloat16
MESH = pl.DeviceIdType.MESH

D_MODEL = 2048
DEPTH = 2
CHUNK = 64
EPS = 1e-6
MLA_HEADS, MLA_Q_RANK, MLA_KV_RANK, MLA_NOPE, MLA_ROPE, MLA_V = 16, 512, 512, 128, 64, 128
ROPE_THETA = 10000.0
HG_HEADS, HG_DK = 16, 128
HG_WIDTH = HG_HEADS * HG_DK
SSM_INNER, SSM_HEADDIM, SSM_HEADS, SSM_GROUPS, SSM_STATE, SSM_CONV = 4096, 64, 64, 8, 128, 4
SSM_CONV_DIM = SSM_INNER + 2 * SSM_GROUPS * SSM_STATE
D_FF = 5632
IN_DIM = 25728
ADAM_LR, ADAM_B1, ADAM_B2, ADAM_EPS, ADAM_WD, ADAM_STEP = 0.001, 0.9, 0.999, 1e-08, 0.01, 10

P_QLAT, P_KVLAT, P_HQ, P_HF, P_HI, P_HGATE, P_Z, P_XBC, P_GATES, P_KPEDT = (
    0, 512, 1024, 3072, 5120, 7168, 9216, 13312, 19456, 25600)
IN_PAD = 26624

VMEM_LIMIT_V7X = 48 << 20
SSD_Q = 256
HG_HB = 4
NEG = -1e30


def _cp(*sem):
    return pltpu.CompilerParams(dimension_semantics=sem, vmem_limit_bytes=VMEM_LIMIT_V7X)


def _fit(n, pref):
    if n <= pref:
        return n
    for t in range(pref, 0, -128):
        if n % t == 0:
            return t
    raise ValueError((n, pref))


_DIMS = {"nn": (((1,), (0,)), ((), ())), "nt": (((1,), (1,)), ((), ())), "tn": (((0,), (0,)), ((), ()))}


def _dot(a, b, mode):
    return lax.dot_general(a.astype(BF16), b.astype(BF16), _DIMS[mode], preferred_element_type=F32)


def matmul(name, a, b, mode="nn", out_dtype=F32, alpha=1.0, res=None, tm=1024, tn=1024, tk=2048):
    if mode == "nn":
        (M, K), (K2, N) = a.shape, b.shape
    elif mode == "nt":
        (M, K), (N, K2) = a.shape, b.shape
    else:
        (K, M), (K2, N) = a.shape, b.shape
    assert K == K2, (name, a.shape, b.shape, mode)
    tm, tn, tk = _fit(M, tm), _fit(N, tn), _fit(K, tk)
    nk = K // tk
    a_spec = pl.BlockSpec((tk, tm), lambda i, j, k: (k, i)) if mode == "tn" else pl.BlockSpec((tm, tk), lambda i, j, k: (i, k))
    b_spec = pl.BlockSpec((tn, tk), lambda i, j, k: (j, k)) if mode == "nt" else pl.BlockSpec((tk, tn), lambda i, j, k: (k, j))
    o_spec = pl.BlockSpec((tm, tn), lambda i, j, k: (i, j))
    has_res = res is not None

    def body(*refs):
        a_ref, b_ref = refs[0], refs[1]
        o_ref = refs[3] if has_res else refs[2]

        def finish(v):
            if alpha != 1.0:
                v = v * alpha
            if has_res:
                v = v + refs[2][...].astype(F32)
            o_ref[...] = v.astype(o_ref.dtype)

        if nk == 1:
            finish(_dot(a_ref[...], b_ref[...], mode))
            return
        acc = refs[-1]
        k = pl.program_id(2)

        @pl.when(k == 0)
        def _():
            acc[...] = _dot(a_ref[...], b_ref[...], mode)

        @pl.when(jnp.logical_and(k > 0, k < nk - 1))
        def _():
            acc[...] += _dot(a_ref[...], b_ref[...], mode)

        @pl.when(k == nk - 1)
        def _():
            finish(acc[...] + _dot(a_ref[...], b_ref[...], mode))

    ins = [a, b] + ([res] if has_res else [])
    return pl.pallas_call(
        body, name=name, grid=(M // tm, N // tn, nk),
        in_specs=[a_spec, b_spec] + ([o_spec] if has_res else []), out_specs=o_spec,
        out_shape=jax.ShapeDtypeStruct((M, N), out_dtype),
        scratch_shapes=[pltpu.VMEM((tm, tn), F32)] if nk > 1 else [],
        compiler_params=_cp("parallel", "parallel", "arbitrary"),
    )(*ins)


def _row_specs(rows, consts, tm):
    specs = []
    for arr, w, off in rows:
        specs.append(pl.BlockSpec((tm, w), functools.partial(lambda j, i, off: (i, off + j), off=off)))
    for arr, w, off in consts:
        specs.append(pl.BlockSpec((arr.shape[0], w), functools.partial(lambda j, i, off: (0, off + j), off=off)))
    return specs


def rowwise(name, fn, rows, consts, outs, n_groups=1, tm=512):
    S = rows[0][0].shape[0]
    tm = _fit(S, tm)
    n_in = len(rows) + len(consts)

    def body(*refs):
        vals = fn(*[r[...].astype(F32) for r in refs[:n_in]])
        for o_ref, v in zip(refs[n_in:], vals):
            o_ref[...] = v.astype(o_ref.dtype)

    return pl.pallas_call(
        body, name=name, grid=(n_groups, S // tm),
        in_specs=_row_specs(rows, consts, tm),
        out_specs=[pl.BlockSpec((tm, w), lambda j, i: (i, j)) for _, w, _ in outs],
        out_shape=[jax.ShapeDtypeStruct((S, W), dt) for W, _, dt in outs],
        compiler_params=_cp("parallel", "parallel"),
    )(*[r[0] for r in rows], *[c[0] for c in consts])


def rowwise_bwd(name, fn, rows, consts, cts, row_grads, const_grads, adds=None, n_groups=1, tm=512):
    S = rows[0][0].shape[0]
    tm = _fit(S, tm)
    adds = adds or {}
    add_idx = list(adds)
    n_r, n_c, n_ct, n_add = len(rows), len(consts), len(cts), len(adds)

    def body(*refs):
        ins = [r[...].astype(F32) for r in refs[:n_r + n_c]]
        ct = tuple(r[...].astype(F32) for r in refs[n_r + n_c:n_r + n_c + n_ct])
        add_refs = refs[n_r + n_c + n_ct:n_r + n_c + n_ct + n_add]
        out_refs = refs[n_r + n_c + n_ct + n_add:]
        _, vjp = jax.vjp(fn, *ins)
        g = list(vjp(ct))
        for a_ref, idx in zip(add_refs, add_idx):
            g[idx] = g[idx] + a_ref[...].astype(F32)
        for q, (idx, _, _) in enumerate(row_grads):
            out_refs[q][...] = g[idx].astype(out_refs[q].dtype)
        first = pl.program_id(1) == 0
        for q, idx in enumerate(const_grads):
            o_ref = out_refs[len(row_grads) + q]

            @pl.when(first)
            def _(o_ref=o_ref):
                o_ref[...] = jnp.zeros_like(o_ref)

            o_ref[...] += g[n_r + idx]

    in_specs = _row_specs(rows, consts, tm)
    in_specs += [pl.BlockSpec((tm, w), lambda j, i: (i, j)) for _, w in cts]
    in_specs += [pl.BlockSpec((tm, rows[idx][1]), lambda j, i: (i, j)) for idx in add_idx]
    out_specs = [pl.BlockSpec((tm, rows[idx][1]), lambda j, i: (i, j)) for idx, _, _ in row_grads]
    out_specs += [pl.BlockSpec((consts[idx][0].shape[0], consts[idx][1]), lambda j, i: (0, j)) for idx in const_grads]
    out_shape = [jax.ShapeDtypeStruct((S, W), dt) for _, W, dt in row_grads]
    out_shape += [jax.ShapeDtypeStruct((consts[idx][0].shape[0], consts[idx][1] * n_groups), F32) for idx in const_grads]
    return pl.pallas_call(
        body, name=name, grid=(n_groups, S // tm), in_specs=in_specs, out_specs=out_specs, out_shape=out_shape,
        compiler_params=_cp("parallel", "arbitrary"),
    )(*[r[0] for r in rows], *[c[0] for c in consts], *[c[0] for c in cts], *adds.values())


def f_rmsnorm(x, w):
    return (x * lax.rsqrt(jnp.mean(x * x, axis=-1, keepdims=True) + EPS) * w,)


def f_swiglu(g, u):
    return (jax.nn.silu(g) * u,)


def f_hgrn_out(o, g, w):
    return (o * lax.rsqrt(jnp.mean(o * o, axis=-1, keepdims=True) + EPS) * w * jax.nn.silu(g),)


def f_ssm_out(y, z, w):
    y = y * jax.nn.silu(z)
    return (y * lax.rsqrt(jnp.mean(y * y, axis=-1, keepdims=True) + EPS) * w,)


def f_merge(ga, gb, gc, ya, yb, yc):
    return (jax.nn.sigmoid(ga) * ya + jax.nn.sigmoid(gb) * yb + jax.nn.sigmoid(gc) * yc,)


def f_lower_bounds(logits):
    p = jax.nn.softmax(logits, axis=0)
    rows = [jnp.zeros_like(p[0:1])]
    for l in range(1, DEPTH):
        rows.append(rows[-1] + p[l:l + 1])
    return (jnp.concatenate(rows, axis=0),)


def loss_head(x, target, w, tm=512):
    S, D = x.shape
    tm = _fit(S, tm)

    def loss_fn(xb, wb, tb):
        (y,) = f_rmsnorm(xb, wb)
        return 0.5 * jnp.sum(jnp.mean(jnp.square(y - tb), axis=-1))

    def body(x_ref, t_ref, w_ref, dx_ref, dxb_ref, dw_ref, loss_ref):
        @pl.when(pl.program_id(0) == 0)
        def _():
            dw_ref[...] = jnp.zeros_like(dw_ref)
            loss_ref[...] = jnp.zeros_like(loss_ref)

        l, (dx, dw) = jax.value_and_grad(loss_fn, argnums=(0, 1))(x_ref[...], w_ref[...], t_ref[...])
        dx_ref[...] = dx
        dxb_ref[...] = dx.astype(BF16)
        dw_ref[...] += dw
        loss_ref[...] += jnp.full(loss_ref.shape, l, F32)

    row = pl.BlockSpec((tm, D), lambda i: (i, 0))
    vec = pl.BlockSpec((1, D), lambda i: (0, 0))
    return pl.pallas_call(
        body, name="loss_head", grid=(S // tm,), in_specs=[row, row, vec],
        out_specs=[row, row, vec, pl.BlockSpec((1, 128), lambda i: (0, 0))],
        out_shape=[jax.ShapeDtypeStruct((S, D), F32), jax.ShapeDtypeStruct((S, D), BF16), jax.ShapeDtypeStruct((1, D), F32),
                   jax.ShapeDtypeStruct((1, 128), F32)],
        compiler_params=_cp("arbitrary"),
    )(x, target, w)


def rope_tables(S):
    inv = 1.0 / (ROPE_THETA ** (jnp.arange(0, MLA_ROPE, 2, dtype=F32) / MLA_ROPE))
    ang = jnp.arange(S, dtype=F32)[:, None] * inv[None, :]
    c, s = jnp.cos(ang), jnp.sin(ang)
    return jnp.tile(c, (1, 4)), jnp.concatenate([-s, s, -s, s], axis=1)


def _rope128(x, cosf, sinf):
    lane = lax.broadcasted_iota(jnp.int32, x.shape, 1)
    swapped = jnp.where((lane & 32) == 0, pltpu.roll(x, 96, 1), pltpu.roll(x, 32, 1))
    return x * cosf + swapped * sinf


def mla_q_prep(q, cosf, sinf, tm=512):
    S = q.shape[0]
    tm = _fit(S, tm)

    def body(q_ref, c_ref, s_ref, o_ref):
        x = q_ref[...]
        r = _rope128(x[:, 256:384], c_ref[...], s_ref[...])
        lane = lax.broadcasted_iota(jnp.int32, r.shape, 1)
        z = jnp.zeros_like(r)
        o_ref[...] = jnp.concatenate(
            [x[:, 0:128], jnp.where(lane < 64, r, z), x[:, 128:256], jnp.where(lane >= 64, r, z)], axis=1).astype(BF16)

    tab = pl.BlockSpec((tm, 128), lambda j, i: (i, 0))
    return pl.pallas_call(
        body, name="mla_q_prep", grid=(8, S // tm),
        in_specs=[pl.BlockSpec((tm, 384), lambda j, i: (i, j)), tab, tab],
        out_specs=pl.BlockSpec((tm, 512), lambda j, i: (i, j)),
        out_shape=jax.ShapeDtypeStruct((S, 4096), BF16), compiler_params=_cp("parallel", "parallel"),
    )(q, cosf, sinf)


def mla_q_prep_bwd(dqc, cosf, sinf, tm=512):
    S = dqc.shape[0]
    tm = _fit(S, tm)

    def body(d_ref, c_ref, s_ref, o_ref):
        d = d_ref[...]
        lane = lax.broadcasted_iota(jnp.int32, (tm, 128), 1)
        dr = jnp.where(lane < 64, d[:, 128:256], d[:, 384:512])
        o_ref[...] = jnp.concatenate([d[:, 0:128], d[:, 256:384], _rope128(dr, c_ref[...], -s_ref[...])], axis=1).astype(BF16)

    tab = pl.BlockSpec((tm, 128), lambda j, i: (i, 0))
    return pl.pallas_call(
        body, name="mla_q_prep_bwd", grid=(8, S // tm),
        in_specs=[pl.BlockSpec((tm, 512), lambda j, i: (i, j)), tab, tab],
        out_specs=pl.BlockSpec((tm, 384), lambda j, i: (i, j)),
        out_shape=jax.ShapeDtypeStruct((S, 3072), BF16), compiler_params=_cp("parallel", "parallel"),
    )(dqc, cosf, sinf)


def mla_k_prep(kv, proj, cosf, sinf, tm=512):
    S = kv.shape[0]
    tm = _fit(S, tm)

    def body(kv_ref, pe_ref, c_ref, s_ref, k_ref, v_ref):
        x = kv_ref[...]
        r = _rope128(pe_ref[...], c_ref[...], s_ref[...])
        lane = lax.broadcasted_iota(jnp.int32, r.shape, 1)
        r2 = jnp.where(lane < 64, r, pltpu.roll(r, 64, 1))
        k_ref[...] = jnp.concatenate([x[:, 0:128], r2, x[:, 256:384], r2], axis=1).astype(BF16)
        v_ref[...] = jnp.concatenate([x[:, 128:256], x[:, 384:512]], axis=1).astype(BF16)

    tab = pl.BlockSpec((tm, 128), lambda j, i: (i, 0))
    return pl.pallas_call(
        body, name="mla_k_prep", grid=(8, S // tm),
        in_specs=[pl.BlockSpec((tm, 512), lambda j, i: (i, j)), pl.BlockSpec((tm, 128), lambda j, i: (i, P_KPEDT // 128)), tab, tab],
        out_specs=[pl.BlockSpec((tm, 512), lambda j, i: (i, j)), pl.BlockSpec((tm, 256), lambda j, i: (i, j))],
        out_shape=[jax.ShapeDtypeStruct((S, 4096), BF16), jax.ShapeDtypeStruct((S, 2048), BF16)],
        compiler_params=_cp("parallel", "parallel"),
    )(kv, proj, cosf, sinf)


def mla_k_prep_bwd(dkc, dv, cosf, sinf, tm=512):
    S = dkc.shape[0]
    tm = _fit(S, tm)

    def body(dk_ref, dv_ref, c_ref, s_ref, dkv_ref, dpe_ref):
        dk, dvv = dk_ref[...], dv_ref[...]
        dkv_ref[...] = jnp.concatenate([dk[:, 0:128], dvv[:, 0:128], dk[:, 256:384], dvv[:, 128:256]], axis=1).astype(BF16)
        d2 = dk[:, 128:256] + dk[:, 384:512]
        lane = lax.broadcasted_iota(jnp.int32, d2.shape, 1)
        dr = jnp.where(lane < 64, d2 + pltpu.roll(d2, 64, 1), 0.0)
        dpe = jnp.where(lane < 64, _rope128(dr, c_ref[...], -s_ref[...]), 0.0)

        @pl.when(pl.program_id(1) == 0)
        def _():
            dpe_ref[...] = jnp.zeros_like(dpe_ref)

        dpe_ref[...] += dpe

    tab = pl.BlockSpec((tm, 128), lambda i, j: (i, 0))
    return pl.pallas_call(
        body, name="mla_k_prep_bwd", grid=(S // tm, 8),
        in_specs=[pl.BlockSpec((tm, 512), lambda i, j: (i, j)), pl.BlockSpec((tm, 256), lambda i, j: (i, j)), tab, tab],
        out_specs=[pl.BlockSpec((tm, 512), lambda i, j: (i, j)), tab],
        out_shape=[jax.ShapeDtypeStruct((S, 4096), BF16), jax.ShapeDtypeStruct((S, 128), F32)],
        compiler_params=_cp("parallel", "arbitrary"),
    )(dkc, dv, cosf, sinf)


ATT_SCALE = (MLA_NOPE + MLA_ROPE) ** -0.5


def _att_scores(q, k, qi, ki, t):
    s = _dot(q, k, "nt") * ATT_SCALE
    rows = qi * t + lax.broadcasted_iota(jnp.int32, (t, t), 0)
    cols = ki * t + lax.broadcasted_iota(jnp.int32, (t, t), 1)
    shift = CHUNK.bit_length() - 1
    return jnp.where((cols >> shift) <= (rows >> shift), s, NEG)


def attention_fwd(qc, kc, vb, t=1024):
    S = qc.shape[0]
    t = _fit(S, t)
    n = S // t

    def body(q_ref, k_ref, v_ref, o_ref, lse_ref, m_s, l_s, acc_s):
        qi, ki = pl.program_id(1), pl.program_id(2)

        @pl.when(ki == 0)
        def _():
            m_s[...] = jnp.full_like(m_s, NEG)
            l_s[...] = jnp.zeros_like(l_s)
            acc_s[...] = jnp.zeros_like(acc_s)

        @pl.when(ki <= qi)
        def _():
            s = _att_scores(q_ref[...], k_ref[...], qi, ki, t)
            m_prev = m_s[...]
            m_new = jnp.maximum(m_prev, jnp.max(s, axis=1, keepdims=True))
            alpha = jnp.exp(m_prev - m_new)
            p = jnp.exp(s - m_new)
            l_s[...] = alpha * l_s[...] + jnp.sum(p, axis=1, keepdims=True)
            acc_s[...] = alpha * acc_s[...] + _dot(p, v_ref[...], "nn")
            m_s[...] = m_new

        @pl.when(ki == qi)
        def _():
            o_ref[...] = (acc_s[...] / l_s[...]).astype(o_ref.dtype)
            lse_ref[...] = m_s[...] + jnp.log(l_s[...])

    return pl.pallas_call(
        body, name="attention_fwd", grid=(MLA_HEADS, n, n),
        in_specs=[pl.BlockSpec((t, 256), lambda h, i, j: (i, h)),
                  pl.BlockSpec((t, 256), lambda h, i, j: (jnp.minimum(i, j), h)),
                  pl.BlockSpec((t, 128), lambda h, i, j: (jnp.minimum(i, j), h))],
        out_specs=[pl.BlockSpec((t, 128), lambda h, i, j: (i, h)), pl.BlockSpec((None, t, 1), lambda h, i, j: (h, i, 0))],
        out_shape=[jax.ShapeDtypeStruct((S, 2048), BF16), jax.ShapeDtypeStruct((MLA_HEADS, S, 1), F32)],
        scratch_shapes=[pltpu.VMEM((t, 1), F32), pltpu.VMEM((t, 1), F32), pltpu.VMEM((t, 128), F32)],
        compiler_params=_cp("parallel", "parallel", "arbitrary"),
    )(qc, kc, vb)


def attention_delta(o, do, t=1024):
    S = o.shape[0]
    t = _fit(S, t)

    def body(o_ref, do_ref, d_ref):
        d_ref[...] = jnp.sum(o_ref[...].astype(F32) * do_ref[...].astype(F32), axis=1, keepdims=True)

    blk = pl.BlockSpec((t, 128), lambda h, i: (i, h))
    return pl.pallas_call(
        body, name="attention_delta", grid=(MLA_HEADS, S // t), in_specs=[blk, blk],
        out_specs=pl.BlockSpec((None, t, 1), lambda h, i: (h, i, 0)),
        out_shape=jax.ShapeDtypeStruct((MLA_HEADS, S, 1), F32), compiler_params=_cp("parallel", "parallel"),
    )(o, do)


def attention_bwd_dq(qc, kc, vb, do, lse, delta, t=1024):
    S = qc.shape[0]
    t = _fit(S, t)
    n = S // t

    def body(q_ref, k_ref, v_ref, do_ref, lse_ref, dl_ref, dq_ref, acc):
        qi, ki = pl.program_id(1), pl.program_id(2)

        @pl.when(ki == 0)
        def _():
            acc[...] = jnp.zeros_like(acc)

        @pl.when(ki <= qi)
        def _():
            p = jnp.exp(_att_scores(q_ref[...], k_ref[...], qi, ki, t) - lse_ref[...])
            dp = _dot(do_ref[...], v_ref[...], "nt")
            ds = p * (dp - dl_ref[...])
            acc[...] += _dot(ds, k_ref[...], "nn")

        @pl.when(ki == qi)
        def _():
            dq_ref[...] = acc[...] * ATT_SCALE

    stat = pl.BlockSpec((None, t, 1), lambda h, i, j: (h, i, 0))
    return pl.pallas_call(
        body, name="attention_bwd_dq", grid=(MLA_HEADS, n, n),
        in_specs=[pl.BlockSpec((t, 256), lambda h, i, j: (i, h)),
                  pl.BlockSpec((t, 256), lambda h, i, j: (jnp.minimum(i, j), h)),
                  pl.BlockSpec((t, 128), lambda h, i, j: (jnp.minimum(i, j), h)),
                  pl.BlockSpec((t, 128), lambda h, i, j: (i, h)), stat, stat],
        out_specs=pl.BlockSpec((t, 256), lambda h, i, j: (i, h)),
        out_shape=jax.ShapeDtypeStruct((S, 4096), F32),
        scratch_shapes=[pltpu.VMEM((t, 256), F32)],
        compiler_params=_cp("parallel", "parallel", "arbitrary"),
    )(qc, kc, vb, do, lse, delta)


def attention_bwd_dkv(qc, kc, vb, do, lse, delta, t=1024):
    S = qc.shape[0]
    t = _fit(S, t)
    n = S // t

    def body(q_ref, k_ref, v_ref, do_ref, lse_ref, dl_ref, dk_ref, dv_ref, dk_acc, dv_acc):
        ki, qi = pl.program_id(1), pl.program_id(2)

        @pl.when(qi == 0)
        def _():
            dk_acc[...] = jnp.zeros_like(dk_acc)
            dv_acc[...] = jnp.zeros_like(dv_acc)

        @pl.when(qi >= ki)
        def _():
            s = _dot(k_ref[...], q_ref[...], "nt") * ATT_SCALE
            krow = ki * t + lax.broadcasted_iota(jnp.int32, (t, t), 0)
            qcol = qi * t + lax.broadcasted_iota(jnp.int32, (t, t), 1)
            shift = CHUNK.bit_length() - 1
            p = jnp.exp(jnp.where((krow >> shift) <= (qcol >> shift), s, NEG) - lse_ref[...])
            dv_acc[...] += _dot(p, do_ref[...], "nn")
            dp = _dot(v_ref[...], do_ref[...], "nt")
            ds = p * (dp - dl_ref[...])
            dk_acc[...] += _dot(ds, q_ref[...], "nn")

        @pl.when(qi == n - 1)
        def _():
            dk_ref[...] = dk_acc[...] * ATT_SCALE
            dv_ref[...] = dv_acc[...]

    stat = pl.BlockSpec((None, 1, t), lambda h, j, i: (h, 0, jnp.maximum(i, j)))
    return pl.pallas_call(
        body, name="attention_bwd_dkv", grid=(MLA_HEADS, n, n),
        in_specs=[pl.BlockSpec((t, 256), lambda h, j, i: (jnp.maximum(i, j), h)),
                  pl.BlockSpec((t, 256), lambda h, j, i: (j, h)),
                  pl.BlockSpec((t, 128), lambda h, j, i: (j, h)),
                  pl.BlockSpec((t, 128), lambda h, j, i: (jnp.maximum(i, j), h)), stat, stat],
        out_specs=[pl.BlockSpec((t, 256), lambda h, j, i: (j, h)), pl.BlockSpec((t, 128), lambda h, j, i: (j, h))],
        out_shape=[jax.ShapeDtypeStruct((S, 4096), F32), jax.ShapeDtypeStruct((S, 2048), F32)],
        scratch_shapes=[pltpu.VMEM((t, 256), F32), pltpu.VMEM((t, 128), F32)],
        compiler_params=_cp("parallel", "parallel", "arbitrary"),
    )(qc, kc, vb, do, lse.reshape(MLA_HEADS, 1, S), delta.reshape(MLA_HEADS, 1, S))


def _scan_rows(x, reverse):
    n = x.shape[0]
    row = lax.broadcasted_iota(jnp.int32, x.shape, 0)
    d = 1
    while d < n:
        if reverse:
            x = x + jnp.where(row < n - d, pltpu.roll(x, n - d, 0), 0.0)
        else:
            x = x + jnp.where(row >= d, pltpu.roll(x, d, 0), 0.0)
        d *= 2
    return x


@jax.custom_vjp
def cumsum_rows(x):
    return _scan_rows(x, False)


cumsum_rows.defvjp(lambda x: (_scan_rows(x, False), None), lambda _, g: (_scan_rows(g, True),))


def hgrn_chunk(q_in, f_in, v, lb, state_t):
    f = lb + (1.0 - lb) * jax.nn.sigmoid(f_in)
    q = jax.nn.silu(q_in) * HG_DK ** -0.5
    k = 1.0 - f
    b = cumsum_rows(jnp.log(f))
    b_last = b[CHUNK - 1:CHUNK]
    b_mid = b[CHUNK // 2 - 1:CHUNK // 2]
    r = lax.broadcasted_iota(jnp.int32, (CHUNK, CHUNK), 0)
    c = lax.broadcasted_iota(jnp.int32, (CHUNK, CHUNK), 1)
    att = jnp.where(c <= r, _dot(q * jnp.exp(b - b_mid), k * jnp.exp(b_mid - b), "nt"), 0.0)
    o = _dot(q * jnp.exp(b), state_t, "nt") + _dot(att, v, "nn")
    new_state_t = state_t * jnp.exp(b_last) + _dot(v, k * jnp.exp(b_last - b), "tn")
    return o, new_state_t


def hgrn_scan_fwd(proj, lb):
    S = proj.shape[0]
    nc = S // CHUNK
    W = HG_HB * 128

    def body(q_ref, f_ref, v_ref, lb_ref, o_ref, hst_ref, st):
        @pl.when(pl.program_id(1) == 0)
        def _():
            st[...] = jnp.zeros_like(st)

        for h in range(HG_HB):
            cs = slice(h * 128, (h + 1) * 128)
            hst_ref[h] = st[h]
            o, new = hgrn_chunk(q_ref[:, cs], f_ref[:, cs], v_ref[:, cs], lb_ref[:, cs], st[h])
            o_ref[:, cs] = o
            st[h] = new

    def seg(off):
        return pl.BlockSpec((CHUNK, W), functools.partial(lambda g, c, off: (c, off + g), off=off // W))

    return pl.pallas_call(
        body, name="hgrn_scan_fwd", grid=(HG_HEADS // HG_HB, nc),
        in_specs=[seg(P_HQ), seg(P_HF), seg(P_HI), pl.BlockSpec((1, W), lambda g, c: (0, g))],
        out_specs=[pl.BlockSpec((CHUNK, W), lambda g, c: (c, g)), pl.BlockSpec((None, HG_HB, 128, 128), lambda g, c: (c, g, 0, 0))],
        out_shape=[jax.ShapeDtypeStruct((S, HG_WIDTH), F32), jax.ShapeDtypeStruct((nc, HG_HEADS, 128, 128), F32)],
        scratch_shapes=[pltpu.VMEM((HG_HB, 128, 128), F32)],
        compiler_params=_cp("parallel", "arbitrary"),
    )(proj, proj, proj, lb)


def hgrn_scan_bwd(proj, lb, hst, do):
    S = proj.shape[0]
    nc = S // CHUNK
    W = HG_HB * 128

    def body(q_ref, f_ref, v_ref, lb_ref, hst_ref, do_ref, dq_ref, df_ref, dv_ref, dlb_ref, dst):
        @pl.when(pl.program_id(1) == 0)
        def _():
            dst[...] = jnp.zeros_like(dst)
            dlb_ref[...] = jnp.zeros_like(dlb_ref)

        for h in range(HG_HB):
            cs = slice(h * 128, (h + 1) * 128)
            _, vjp = jax.vjp(hgrn_chunk, q_ref[:, cs], f_ref[:, cs], v_ref[:, cs], lb_ref[:, cs], hst_ref[h])
            dq, df, dv, dlb, dstate = vjp((do_ref[:, cs], dst[h]))
            dq_ref[:, cs] = dq.astype(dq_ref.dtype)
            df_ref[:, cs] = df.astype(df_ref.dtype)
            dv_ref[:, cs] = dv.astype(dv_ref.dtype)
            dlb_ref[:, cs] += dlb
            dst[h] = dstate

    def seg(off):
        return pl.BlockSpec((CHUNK, W), functools.partial(lambda g, c, off: (nc - 1 - c, off + g), off=off // W))

    row = pl.BlockSpec((CHUNK, W), lambda g, c: (nc - 1 - c, g))
    vec = pl.BlockSpec((1, W), lambda g, c: (0, g))
    return pl.pallas_call(
        body, name="hgrn_scan_bwd", grid=(HG_HEADS // HG_HB, nc),
        in_specs=[seg(P_HQ), seg(P_HF), seg(P_HI), vec,
                  pl.BlockSpec((None, HG_HB, 128, 128), lambda g, c: (nc - 1 - c, g, 0, 0)), row],
        out_specs=[row, row, row, vec],
        out_shape=[jax.ShapeDtypeStruct((S, HG_WIDTH), BF16)] * 3 + [jax.ShapeDtypeStruct((1, HG_WIDTH), F32)],
        scratch_shapes=[pltpu.VMEM((HG_HB, 128, 128), F32)],
        compiler_params=_cp("parallel", "arbitrary"),
    )(proj, proj, proj, lb, hst, do)


def _silu_grad(x):
    s = jax.nn.sigmoid(x)
    return s * (1.0 + x * (1.0 - s))


def conv_fwd(proj, w, b, tm=512):
    S = proj.shape[0]
    tm = _fit(S, tm)
    G = 512
    off = P_XBC // G

    def body(cur_ref, prev_ref, w_ref, b_ref, act_ref, pre_ref):
        prev = prev_ref[...] * (pl.program_id(1) > 0).astype(F32)
        ext = jnp.concatenate([prev, cur_ref[...]], axis=0)
        n = tm + 8
        acc = b_ref[...] + jnp.zeros((tm, G), F32)
        for j in range(SSM_CONV):
            acc = acc + w_ref[j:j + 1, :] * pltpu.roll(ext, (n - 5 - j) % n, 0)[0:tm]
        pre_ref[...] = acc
        act_ref[...] = jax.nn.silu(acc)

    out = pl.BlockSpec((tm, G), lambda j, i: (i, j))
    return pl.pallas_call(
        body, name="conv_fwd", grid=(SSM_CONV_DIM // G, S // tm),
        in_specs=[pl.BlockSpec((tm, G), lambda j, i: (i, off + j)),
                  pl.BlockSpec((8, G), lambda j, i: (jnp.maximum(i * (tm // 8) - 1, 0), off + j)),
                  pl.BlockSpec((SSM_CONV, G), lambda j, i: (0, j)), pl.BlockSpec((1, G), lambda j, i: (0, j))],
        out_specs=[out, out], out_shape=[jax.ShapeDtypeStruct((S, SSM_CONV_DIM), F32)] * 2,
        compiler_params=_cp("parallel", "parallel"),
    )(proj, proj, w, b)


def conv_bwd(proj, pre, dact, w, tm=512):
    S = proj.shape[0]
    tm = _fit(S, tm)
    G = 512
    off = P_XBC // G
    nb = S // tm

    def body(x_ref, xp_ref, pre_ref, pren_ref, d_ref, dn_ref, w_ref, dx_ref, dw_ref, db_ref):
        i = pl.program_id(1)
        n = tm + 8
        dpre = d_ref[...] * _silu_grad(pre_ref[...])
        dpre_next = dn_ref[...] * _silu_grad(pren_ref[...]) * (i < nb - 1).astype(F32)
        dext = jnp.concatenate([dpre, dpre_next], axis=0)
        xext = jnp.concatenate([xp_ref[...] * (i > 0).astype(F32), x_ref[...]], axis=0)
        dx = jnp.zeros((tm, G), F32)
        dws = []
        for j in range(SSM_CONV):
            dx = dx + w_ref[j:j + 1, :] * pltpu.roll(dext, (n - (3 - j)) % n, 0)[0:tm]
            dws.append(jnp.sum(dpre * pltpu.roll(xext, (n - 5 - j) % n, 0)[0:tm], axis=0, keepdims=True))
        dx_ref[...] = dx.astype(dx_ref.dtype)

        @pl.when(i == 0)
        def _():
            dw_ref[...] = jnp.zeros_like(dw_ref)
            db_ref[...] = jnp.zeros_like(db_ref)

        dw_ref[...] += jnp.concatenate(dws, axis=0)
        db_ref[...] += jnp.sum(dpre, axis=0, keepdims=True)

    cur = pl.BlockSpec((tm, G), lambda j, i: (i, j))
    nxt = pl.BlockSpec((8, G), lambda j, i: (jnp.minimum((i + 1) * (tm // 8), S // 8 - 1), j))
    return pl.pallas_call(
        body, name="conv_bwd", grid=(SSM_CONV_DIM // G, nb),
        in_specs=[pl.BlockSpec((tm, G), lambda j, i: (i, off + j)),
                  pl.BlockSpec((8, G), lambda j, i: (jnp.maximum(i * (tm // 8) - 1, 0), off + j)),
                  cur, nxt, cur, nxt, pl.BlockSpec((SSM_CONV, G), lambda j, i: (0, j))],
        out_specs=[cur, pl.BlockSpec((SSM_CONV, G), lambda j, i: (0, j)), pl.BlockSpec((1, G), lambda j, i: (0, j))],
        out_shape=[jax.ShapeDtypeStruct((S, SSM_CONV_DIM), BF16), jax.ShapeDtypeStruct((SSM_CONV, SSM_CONV_DIM), F32),
                   jax.ShapeDtypeStruct((1, SSM_CONV_DIM), F32)],
        compiler_params=_cp("parallel", "arbitrary"),
    )(proj, proj, pre, pre, dact, dact, w)


def _eye_dot(a, mode):
    Q = a.shape[0] if mode == "tn" else a.shape[1]
    eye = (lax.broadcasted_iota(jnp.int32, (Q, Q), 0) == lax.broadcasted_iota(jnp.int32, (Q, Q), 1)).astype(BF16)
    hi = a.astype(BF16)
    r1 = a - hi.astype(F32)
    mid = r1.astype(BF16)
    lo = (r1 - mid.astype(F32)).astype(BF16)
    if mode == "tn":
        return sum(lax.dot_general(p, eye, _DIMS["tn"], preferred_element_type=F32) for p in (hi, mid, lo))
    return sum(lax.dot_general(eye, p, _DIMS["nt"], preferred_element_type=F32) for p in (hi, mid, lo))


@jax.custom_vjp
def _transpose_exact(a):
    return _eye_dot(a, "tn")


_transpose_exact.defvjp(lambda a: (_eye_dot(a, "tn"), None), lambda _, g: (_eye_dot(g, "nt"),))


def ssd_decay_inputs(dtr, bias, alog):
    dt = jax.nn.softplus(dtr + bias)
    acum = cumsum_rows(dt * -jnp.exp(alog))
    return dt, acum, _transpose_exact(acum)


def ssd_head(x, cb, bm, cm, dt, acum, a_s, dsk, h_prev):
    Q = x.shape[0]
    r = lax.broadcasted_iota(jnp.int32, (Q, Q), 0)
    c = lax.broadcasted_iota(jnp.int32, (Q, Q), 1)
    a_l = jnp.broadcast_to(acum, (Q, Q))
    decay = jnp.where(c <= r, jnp.exp(jnp.minimum(a_l - a_s, 0.0)), 0.0)
    xdt = x * dt
    y_diag = _dot(cb * decay, xdt, "nn")
    a_last = acum[Q - 1:Q]
    states = _dot(xdt * jnp.exp(a_last - acum), bm, "tn")
    h_new = h_prev * jnp.exp(a_last) + states
    y_off = _dot(cm, h_prev, "nt") * jnp.exp(acum)
    return y_diag + y_off + x * dsk, h_new


def _ssd_specs(S, rev):
    Q = _fit(S, SSD_Q)
    nc = S // Q
    ci = (lambda c: nc - 1 - c) if rev else (lambda c: c)
    x = pl.BlockSpec((Q, 512), lambda g, c: (ci(c), g))
    bm = pl.BlockSpec((Q, 128), lambda g, c: (ci(c), SSM_INNER // 128 + g))
    cm = pl.BlockSpec((Q, 128), lambda g, c: (ci(c), SSM_INNER // 128 + SSM_GROUPS + g))
    dtr = pl.BlockSpec((None, Q, 8), lambda g, c: (g, ci(c), 0))
    par = pl.BlockSpec((None, 1, 8), lambda g, c: (g, 0, 0))
    hs = pl.BlockSpec((None, 8, SSM_HEADDIM, SSM_STATE), lambda g, c: (ci(c), g, 0, 0))
    return Q, nc, x, bm, cm, dtr, par, hs


def ssd_scan_fwd(act, dtr, bias, alog, dsk):
    S = act.shape[0]
    Q, nc, x_s, bm_s, cm_s, dtr_s, par_s, hs_s = _ssd_specs(S, False)

    def body(x_ref, bm_ref, cm_ref, dtr_ref, b_ref, a_ref, d_ref, y_ref, hs_ref, st, dt_s, ac_s, act_s):
        @pl.when(pl.program_id(1) == 0)
        def _():
            st[...] = jnp.zeros_like(st)

        dt_s[...], ac_s[...], act_s[...] = ssd_decay_inputs(dtr_ref[...], b_ref[...], a_ref[...])
        bm, cm = bm_ref[...], cm_ref[...]
        cb = _dot(cm, bm, "nt")
        for j in range(8):
            hs_ref[j] = st[j]
            y, h_new = ssd_head(x_ref[:, j * 64:(j + 1) * 64], cb, bm, cm, dt_s[:, j:j + 1], ac_s[:, j:j + 1], act_s[j:j + 1, :],
                                d_ref[:, j:j + 1], st[j])
            y_ref[:, j * 64:(j + 1) * 64] = y
            st[j] = h_new

    return pl.pallas_call(
        body, name="ssd_scan_fwd", grid=(SSM_GROUPS, nc),
        in_specs=[x_s, bm_s, cm_s, dtr_s, par_s, par_s, par_s], out_specs=[x_s, hs_s],
        out_shape=[jax.ShapeDtypeStruct((S, SSM_INNER), F32), jax.ShapeDtypeStruct((nc, SSM_HEADS, SSM_HEADDIM, SSM_STATE), F32)],
        scratch_shapes=[pltpu.VMEM((8, SSM_HEADDIM, SSM_STATE), F32), pltpu.VMEM((Q, 8), F32), pltpu.VMEM((Q, 8), F32),
                        pltpu.VMEM((8, Q), F32)],
        compiler_params=_cp("parallel", "arbitrary"),
    )(act, act, act, dtr, bias, alog, dsk)


def ssd_scan_bwd(act, dtr, bias, alog, dsk, hs, dy):
    S = act.shape[0]
    Q, nc, x_s, bm_s, cm_s, dtr_s, par_s, hs_s = _ssd_specs(S, True)
    g_s = pl.BlockSpec((Q, 128), lambda g, c: (nc - 1 - c, g))

    def body(x_ref, bm_ref, cm_ref, dtr_ref, b_ref, a_ref, d_ref, hs_ref, dy_ref,
             dx_ref, dbm_ref, dcm_ref, ddtr_ref, db_ref, da_ref, dd_ref, dst, dt_s, ac_s, ddt_s, dac_s, act_s, dact_s):
        @pl.when(pl.program_id(1) == 0)
        def _():
            dst[...] = jnp.zeros_like(dst)
            db_ref[...] = jnp.zeros_like(db_ref)
            da_ref[...] = jnp.zeros_like(da_ref)
            dd_ref[...] = jnp.zeros_like(dd_ref)

        (dt_s[...], ac_s[...], act_s[...]), decay_vjp = jax.vjp(ssd_decay_inputs, dtr_ref[...], b_ref[...], a_ref[...])
        bm, cm = bm_ref[...], cm_ref[...]
        cb = _dot(cm, bm, "nt")
        dcb = jnp.zeros((Q, Q), F32)
        dbm = jnp.zeros((Q, 128), F32)
        dcm = jnp.zeros((Q, 128), F32)
        for j in range(8):
            cs = slice(j * 64, (j + 1) * 64)
            one = slice(j, j + 1)
            _, vjp = jax.vjp(ssd_head, x_ref[:, cs], cb, bm, cm, dt_s[:, one], ac_s[:, one], act_s[one, :], d_ref[:, one], hs_ref[j])
            dx, gcb, gb, gc, gdt, gac, gact, gdsk, gh = vjp((dy_ref[:, cs], dst[j]))
            dact_s[one, :] = gact
            dx_ref[:, cs] = dx
            dcb = dcb + gcb
            dbm = dbm + gb
            dcm = dcm + gc
            ddt_s[:, one] = gdt
            dac_s[:, one] = gac
            dd_ref[:, one] += gdsk
            dst[j] = gh
        dbm_ref[...] = dbm + _dot(dcb, cm, "tn")
        dcm_ref[...] = dcm + _dot(dcb, bm, "nn")
        ddtr, dbias, dalog = decay_vjp((ddt_s[...], dac_s[...], dact_s[...]))
        ddtr_ref[...] = ddtr
        db_ref[...] += dbias
        da_ref[...] += dalog

    return pl.pallas_call(
        body, name="ssd_scan_bwd", grid=(SSM_GROUPS, nc),
        in_specs=[x_s, bm_s, cm_s, dtr_s, par_s, par_s, par_s, hs_s, x_s],
        out_specs=[x_s, g_s, g_s, dtr_s, par_s, par_s, par_s],
        out_shape=[jax.ShapeDtypeStruct((S, SSM_INNER), F32), jax.ShapeDtypeStruct((S, 1024), F32), jax.ShapeDtypeStruct((S, 1024), F32),
                   jax.ShapeDtypeStruct((SSM_GROUPS, S, 8), F32)] + [jax.ShapeDtypeStruct((SSM_GROUPS, 1, 8), F32)] * 3,
        scratch_shapes=[pltpu.VMEM((8, SSM_HEADDIM, SSM_STATE), F32)] + [pltpu.VMEM((Q, 8), F32)] * 4 + [pltpu.VMEM((8, Q), F32)] * 2,
        compiler_params=_cp("parallel", "arbitrary"),
    )(act, act, act, dtr, bias, alog, dsk, hs, dy)


def _w_in_to_local(w):
    parts = [w[..., 0:1024], w[..., 1088:19520], w[..., 19584:25728], w[..., 1024:1088], w[..., 19520:19584],
             jnp.zeros(w.shape[:-1] + (IN_PAD - IN_DIM,), w.dtype)]
    return jnp.concatenate(parts, axis=-1)


def _w_in_from_local(g):
    return jnp.concatenate([g[..., 0:1024], g[..., 25600:25664], g[..., 1024:19456], g[..., 25664:25728], g[..., 19456:25600]], axis=-1)


def _w_uq_to_local(w):
    lead = w.shape[:-1]
    w = w.reshape(lead + (MLA_HEADS, 192))
    nope = w[..., :128].reshape(lead + (8, 256))
    rope = w[..., 128:].reshape(lead + (8, 128))
    return jnp.concatenate([nope, rope], axis=-1).reshape(lead + (3072,))


def _w_uq_from_local(g):
    lead = g.shape[:-1]
    g = g.reshape(lead + (8, 384))
    nope = g[..., :256].reshape(lead + (MLA_HEADS, 128))
    rope = g[..., 256:].reshape(lead + (MLA_HEADS, 64))
    return jnp.concatenate([nope, rope], axis=-1).reshape(lead + (3072,))


BIG = ("ffn1_wi", "ffn1_wo", "w_in", "mla_w_uq", "mla_w_ukv", "w_o_mla", "w_o_hgrn", "w_o_ssm", "w_out", "ffn2_wi", "ffn2_wo")
COL_SHARDED = ("ffn1_wi", "w_in", "mla_w_uq", "mla_w_ukv", "ffn2_wi")
SMALL = ("ffn1_norm", "mix_norm", "mla_q_norm", "mla_kv_norm", "hgrn_lb_logits", "hgrn_norm", "ssm_conv_b", "ssm_a_log",
         "ssm_dt_bias", "ssm_d", "ssm_norm", "ffn2_norm")


def _ffn_fwd(tag, x, norm_w, wi, wo):
    (h,) = rowwise(tag + "_norm", f_rmsnorm, [(x, D_MODEL, 0)], [(norm_w, D_MODEL, 0)], [(D_MODEL, D_MODEL, BF16)])
    gu = matmul(tag + "_wi", h, wi, out_dtype=BF16)
    (a,) = rowwise(tag + "_act", f_swiglu, [(gu, 512, 0), (gu, 512, D_FF // 512)], [], [(D_FF, 512, BF16)], n_groups=D_FF // 512)
    out = matmul(tag + "_wo", a, wo, alpha=0.5, res=x)
    return out, (x, h, gu, a)


def _ffn_bwd(tag, dx, dxb, saved, norm_w, wi, wo):
    x, h, gu, a = saved
    da = matmul(tag + "_da", dxb, wo, "nt", out_dtype=BF16, alpha=0.5)
    dwo = matmul(tag + "_dwo", a, dxb, "tn", alpha=0.5)
    dg, du = rowwise_bwd(tag + "_act_bwd", f_swiglu, [(gu, 512, 0), (gu, 512, D_FF // 512)], [], [(da, 512)],
                         [(0, D_FF, BF16), (1, D_FF, BF16)], [], n_groups=D_FF // 512)
    dgu = jnp.concatenate([dg, du], axis=1)
    dh = matmul(tag + "_dh", dgu, wi, "nt")
    dwi = matmul(tag + "_dwi", h, dgu, "tn")
    dx_in, dxb_in, dnorm = rowwise_bwd(tag + "_norm_bwd", f_rmsnorm, [(x, D_MODEL, 0)], [(norm_w, D_MODEL, 0)], [(dh, D_MODEL)],
                                       [(0, D_MODEL, F32), (0, D_MODEL, BF16)], [0], adds={0: dx}, tm=256)
    return dx_in, dxb_in, dnorm, dwi, dwo


def _ssm_params(W):
    return [W[k].reshape(SSM_GROUPS, 1, 8) for k in ("ssm_dt_bias", "ssm_a_log", "ssm_d")]


def _dt_cols(proj):
    S = proj.shape[0]
    return proj[:, P_KPEDT + 64:P_KPEDT + 128].reshape(S, SSM_GROUPS, 8).transpose(1, 0, 2)


def _mix_fwd(x, W, lb, tabs):
    cosf, sinf = tabs
    (h,) = rowwise("mix_norm", f_rmsnorm, [(x, D_MODEL, 0)], [(W["mix_norm"], D_MODEL, 0)], [(D_MODEL, D_MODEL, BF16)])
    proj = matmul("w_in", h, W["w_in"])
    (qn,) = rowwise("q_norm", f_rmsnorm, [(proj, 512, 0)], [(W["mla_q_norm"], 512, 0)], [(512, 512, BF16)])
    (kvn,) = rowwise("kv_norm", f_rmsnorm, [(proj, 512, 1)], [(W["mla_kv_norm"], 512, 0)], [(512, 512, BF16)])
    q = matmul("w_uq", qn, W["mla_w_uq"])
    kv = matmul("w_ukv", kvn, W["mla_w_ukv"])
    qc = mla_q_prep(q, cosf, sinf)
    kc, vb = mla_k_prep(kv, proj, cosf, sinf)
    o_a, lse = attention_fwd(qc, kc, vb)
    y_a = matmul("w_o_mla", o_a, W["w_o_mla"])
    o_h, hst = hgrn_scan_fwd(proj, lb)
    (pre_b,) = rowwise("hgrn_out", f_hgrn_out, [(o_h, 128, 0), (proj, 128, P_HGATE // 128)], [(W["hgrn_norm"], 128, 0)],
                       [(HG_WIDTH, 128, BF16)], n_groups=HG_HEADS, tm=1024)
    y_b = matmul("w_o_hgrn", pre_b, W["w_o_hgrn"])
    act, pre = conv_fwd(proj, W["ssm_conv_w"], W["ssm_conv_b"])
    dtr = _dt_cols(proj)
    y_s, hs = ssd_scan_fwd(act, dtr, *_ssm_params(W))
    (pre_c,) = rowwise("ssm_out", f_ssm_out, [(y_s, 512, 0), (proj, 512, P_Z // 512)], [(W["ssm_norm"], 512, 0)],
                       [(SSM_INNER, 512, BF16)], n_groups=SSM_GROUPS)
    y_c = matmul("w_o_ssm", pre_c, W["w_o_ssm"])
    g0 = P_GATES // 512
    (merged,) = rowwise("merge", f_merge, [(proj, 512, g0), (proj, 512, g0 + 4), (proj, 512, g0 + 8), (y_a, 512, 0), (y_b, 512, 0), (y_c, 512, 0)],
                        [], [(D_MODEL, 512, BF16)], n_groups=4)
    out = matmul("w_out", merged, W["w_out"], res=x)
    return out, (x, h, proj, qn, kvn, qc, kc, vb, o_a, lse, y_a, o_h, hst, pre_b, y_b, act, pre, dtr, y_s, hs, pre_c, y_c, merged)


def _mix_bwd(dx, dxb, saved, W, lb, tabs):
    cosf, sinf = tabs
    (x, h, proj, qn, kvn, qc, kc, vb, o_a, lse, y_a, o_h, hst, pre_b, y_b, act, pre, dtr, y_s, hs, pre_c, y_c, merged) = saved
    S = x.shape[0]
    g = {}
    dmerged = matmul("d_merged", dxb, W["w_out"], "nt")
    g["w_out"] = matmul("dw_out", merged, dxb, "tn")
    g0 = P_GATES // 512
    dga, dgb, dgc, dya, dyb, dyc = rowwise_bwd(
        "merge_bwd", f_merge, [(proj, 512, g0), (proj, 512, g0 + 4), (proj, 512, g0 + 8), (y_a, 512, 0), (y_b, 512, 0), (y_c, 512, 0)],
        [], [(dmerged, 512)], [(k, D_MODEL, BF16) for k in range(6)], [], n_groups=4)
    do_a = matmul("d_o_mla", dya, W["w_o_mla"], "nt", out_dtype=BF16)
    g["w_o_mla"] = matmul("dw_o_mla", o_a, dya, "tn")
    delta = attention_delta(o_a, do_a)
    dqc = attention_bwd_dq(qc, kc, vb, do_a, lse, delta)
    dkc, dv = attention_bwd_dkv(qc, kc, vb, do_a, lse, delta)
    dq = mla_q_prep_bwd(dqc, cosf, sinf)
    dkv, dpe = mla_k_prep_bwd(dkc, dv, cosf, sinf)
    dqn = matmul("d_qn", dq, W["mla_w_uq"], "nt")
    g["mla_w_uq"] = matmul("dw_uq", qn, dq, "tn")
    dkvn = matmul("d_kvn", dkv, W["mla_w_ukv"], "nt")
    g["mla_w_ukv"] = matmul("dw_ukv", kvn, dkv, "tn")
    dq_lat, g["mla_q_norm"] = rowwise_bwd("q_norm_bwd", f_rmsnorm, [(proj, 512, 0)], [(W["mla_q_norm"], 512, 0)], [(dqn, 512)],
                                          [(0, 512, BF16)], [0])
    dkv_lat, g["mla_kv_norm"] = rowwise_bwd("kv_norm_bwd", f_rmsnorm, [(proj, 512, 1)], [(W["mla_kv_norm"], 512, 0)], [(dkvn, 512)],
                                            [(0, 512, BF16)], [0])
    do_b = matmul("d_o_hgrn", dyb, W["w_o_hgrn"], "nt")
    g["w_o_hgrn"] = matmul("dw_o_hgrn", pre_b, dyb, "tn")
    do_h, dhgate, g["hgrn_norm"] = rowwise_bwd(
        "hgrn_out_bwd", f_hgrn_out, [(o_h, 128, 0), (proj, 128, P_HGATE // 128)], [(W["hgrn_norm"], 128, 0)], [(do_b, 128)],
        [(0, HG_WIDTH, F32), (1, HG_WIDTH, BF16)], [0], n_groups=HG_HEADS, tm=1024)
    dhq, dhf, dhi, dlb = hgrn_scan_bwd(proj, lb, hst, do_h)
    do_c = matmul("d_o_ssm", dyc, W["w_o_ssm"], "nt")
    g["w_o_ssm"] = matmul("dw_o_ssm", pre_c, dyc, "tn")
    dy_s, dz, g["ssm_norm"] = rowwise_bwd(
        "ssm_out_bwd", f_ssm_out, [(y_s, 512, 0), (proj, 512, P_Z // 512)], [(W["ssm_norm"], 512, 0)], [(do_c, 512)],
        [(0, SSM_INNER, F32), (1, SSM_INNER, BF16)], [0], n_groups=SSM_GROUPS)
    dxs, dbm, dcm, ddtr, dbias, dalog, ddsk = ssd_scan_bwd(act, dtr, *_ssm_params(W), hs, dy_s)
    g["ssm_dt_bias"], g["ssm_a_log"], g["ssm_d"] = (v.reshape(1, SSM_HEADS) for v in (dbias, dalog, ddsk))
    dxbc, g["ssm_conv_w"], g["ssm_conv_b"] = conv_bwd(proj, pre, jnp.concatenate([dxs, dbm, dcm], axis=1), W["ssm_conv_w"])
    ddt = ddtr.transpose(1, 0, 2).reshape(S, SSM_HEADS)
    dproj = jnp.concatenate([dq_lat, dkv_lat, dhq, dhf, dhi, dhgate, dz, dxbc, dga, dgb, dgc, dpe[:, :64].astype(BF16),
                             ddt.astype(BF16), jnp.zeros((S, IN_PAD - IN_DIM), BF16)], axis=1)
    dh = matmul("d_h_mix", dproj, W["w_in"], "nt")
    g["w_in"] = matmul("dw_in", h, dproj, "tn")
    dx_in, dxb_in, g["mix_norm"] = rowwise_bwd("mix_norm_bwd", f_rmsnorm, [(x, D_MODEL, 0)], [(W["mix_norm"], D_MODEL, 0)], [(dh, D_MODEL)],
                                               [(0, D_MODEL, F32), (0, D_MODEL, BF16)], [0], adds={0: dx}, tm=256)
    return dx_in, dxb_in, g, dlb


def local_step(x, target, Wl, small, final_norm):
    S = x.shape[0]
    tabs = rope_tables(S)
    (lbs,) = rowwise("lower_bounds", f_lower_bounds, [(small["hgrn_lb_logits"], HG_WIDTH, 0)], [], [(HG_WIDTH, HG_WIDTH, F32)])
    saved = []
    for l in range(DEPTH):
        W = dict(Wl[l])
        for k in SMALL:
            W[k] = small[k][l:l + 1]
        lb = lbs[l:l + 1]
        x, s1 = _ffn_fwd("ffn1", x, W["ffn1_norm"], W["ffn1_wi"], W["ffn1_wo"])
        x, s2 = _mix_fwd(x, W, lb, tabs)
        x, s3 = _ffn_fwd("ffn2", x, W["ffn2_norm"], W["ffn2_wi"], W["ffn2_wo"])
        saved.append((W, lb, s1, s2, s3))
    dx, dxb, dfinal, loss = loss_head(x, target, final_norm.reshape(1, D_MODEL))
    grads = [None] * DEPTH
    dlbs = [None] * DEPTH
    for l in reversed(range(DEPTH)):
        W, lb, s1, s2, s3 = saved[l]
        g = {}
        dx, dxb, g["ffn2_norm"], g["ffn2_wi"], g["ffn2_wo"] = _ffn_bwd("ffn2", dx, dxb, s3, W["ffn2_norm"], W["ffn2_wi"], W["ffn2_wo"])
        dx, dxb, gm, dlbs[l] = _mix_bwd(dx, dxb, s2, W, lb, tabs)
        g.update(gm)
        dx, dxb, g["ffn1_norm"], g["ffn1_wi"], g["ffn1_wo"] = _ffn_bwd("ffn1", dx, dxb, s1, W["ffn1_norm"], W["ffn1_wi"], W["ffn1_wo"])
        grads[l] = g
    (dlogits,) = rowwise_bwd("lower_bounds_bwd", f_lower_bounds, [(small["hgrn_lb_logits"], HG_WIDTH, 0)], [],
                             [(jnp.concatenate(dlbs, axis=0), HG_WIDTH)], [(0, HG_WIDTH, F32)], [])
    return loss, dx, grads, dlogits, dfinal


ANY = pl.BlockSpec(memory_space=pl.ANY)


def _coords():
    return lax.axis_index("x"), lax.axis_index("y"), lax.axis_index("c")


def _exchange_call(name, body, src, out_shape, n_copies):
    return pl.pallas_call(
        body, name=name, in_specs=[ANY], out_specs=ANY, out_shape=out_shape,
        scratch_shapes=[pltpu.SemaphoreType.DMA((n_copies,)), pltpu.SemaphoreType.DMA((n_copies,)), pltpu.SemaphoreType.DMA],
    )(src)


def _multi_exchange(name, body, srcs, out_shapes, sem_counts):
    n = len(srcs)
    return pl.pallas_call(
        body, name=name, in_specs=[ANY] * n, out_specs=[ANY] * len(out_shapes), out_shape=out_shapes,
        scratch_shapes=[pltpu.SemaphoreType.DMA((k,)) for k in sem_counts],
    )(*srcs)


def _chip_peers(x, y):
    return [(1 - x, y), (x, 1 - y), (1 - x, 1 - y)]


def weights_allgather(srcs):
    n = len(srcs)

    def body(*refs):
        src, out = refs[:n], refs[n:2 * n]
        send_sems, recv_sems, fsend_sems, frecv_sems = refs[2 * n:]
        x, y, c = _coords()
        me = 2 * x + y
        peers = _chip_peers(x, y)

        def ici(i, k, chip):
            return pltpu.make_async_remote_copy(src[i].at[c], out[i].at[chip, c], send_sems.at[3 * i + k], recv_sems.at[3 * i + k],
                                                device_id=(*peers[k], c), device_id_type=MESH)

        def fwd(i, k, chip, layer):
            return pltpu.make_async_remote_copy(out[i].at[chip, layer], out[i].at[chip, layer], fsend_sems.at[3 * i + k],
                                                frecv_sems.at[3 * i + k], device_id=(x, y, 1 - c), device_id_type=MESH)

        sends = [ici(i, k, me) for k in range(3) for i in range(n)]
        for cp in sends:
            cp.start()
        passed = []
        for k, (px, py) in enumerate(peers):
            for i in range(n):
                ici(i, k, 2 * px + py).wait_recv()
                passed.append(fwd(i, k, 2 * px + py, c))
                passed[-1].start()
        for k, (px, py) in enumerate(peers):
            for i in range(n):
                fwd(i, k, 2 * px + py, 1 - c).wait_recv()
        for cp in sends + passed:
            cp.wait_send()

    out_shapes = [jax.ShapeDtypeStruct((4,) + s.shape, s.dtype) for s in srcs]
    return _multi_exchange("weights_allgather", body, srcs, out_shapes, (3 * n, 3 * n, 3 * n, 3 * n))


def grad_layer_exchange(gs):
    n = len(gs)

    def body(*refs):
        src, out = refs[:n], refs[n:2 * n]
        send_sems, recv_sems = refs[2 * n:]
        x, y, c = _coords()
        copies = [pltpu.make_async_remote_copy(src[i].at[1 - c], out[i], send_sems.at[i], recv_sems.at[i],
                                               device_id=(x, y, 1 - c), device_id_type=MESH) for i in range(n)]
        for cp in copies:
            cp.start()
        for cp in copies:
            cp.wait_recv()
        for cp in copies:
            cp.wait_send()

    return _multi_exchange("grad_layer_exchange", body, gs, [jax.ShapeDtypeStruct(g.shape[1:], g.dtype) for g in gs], (n, n))


def grad_chip_exchange(ps):
    n = len(ps)

    def body(*refs):
        src, out = refs[:n], refs[n:2 * n]
        send_sems, recv_sems = refs[2 * n:]
        x, y, c = _coords()
        copies = [pltpu.make_async_remote_copy(src[i].at[2 * px + py], out[i].at[k], send_sems.at[3 * i + k], recv_sems.at[3 * i + k],
                                               device_id=(px, py, c), device_id_type=MESH)
                  for k, (px, py) in enumerate(_chip_peers(x, y)) for i in range(n)]
        for cp in copies:
            cp.start()
        for cp in copies:
            cp.wait_recv()
        for cp in copies:
            cp.wait_send()

    return _multi_exchange("grad_chip_exchange", body, ps, [jax.ShapeDtypeStruct((3,) + p.shape[1:], p.dtype) for p in ps], (3 * n, 3 * n))


def grad_reduced_exchange(rs):
    n = len(rs)

    def body(*refs):
        src, out = refs[:n], refs[n:2 * n]
        send_sems, recv_sems = refs[2 * n:]
        x, y, c = _coords()
        copies = [pltpu.make_async_remote_copy(src[i], out[i], send_sems.at[i], recv_sems.at[i],
                                               device_id=(x, y, 1 - c), device_id_type=MESH) for i in range(n)]
        for cp in copies:
            cp.start()
        for cp in copies:
            cp.wait_recv()
        for cp in copies:
            cp.wait_send()

    return _multi_exchange("grad_reduced_exchange", body, rs, [jax.ShapeDtypeStruct(r.shape, r.dtype) for r in rs], (n, n))


def device_allgather(name, src):
    def body(src_ref, out_ref, send_sems, recv_sems, local_sem):
        x, y, c = _coords()
        me = 4 * x + 2 * y + c
        peers = [(x ^ (m >> 2), y ^ ((m >> 1) & 1), c ^ (m & 1)) for m in range(1, 8)]

        def copy(k, dev):
            return pltpu.make_async_remote_copy(src_ref, out_ref.at[dev], send_sems.at[k], recv_sems.at[k],
                                                device_id=peers[k], device_id_type=MESH)

        local = pltpu.make_async_copy(src_ref, out_ref.at[me], local_sem)
        local.start()
        sends = [copy(k, me) for k in range(7)]
        for s in sends:
            s.start()
        for k, (px, py, pc) in enumerate(peers):
            copy(k, 4 * px + 2 * py + pc).wait_recv()
        for s in sends:
            s.wait_send()
        local.wait()

    return _exchange_call(name, body, src, jax.ShapeDtypeStruct((8,) + src.shape, src.dtype), 7)


BLOCK_ELEMS = 1 << 19


def add_pair(name, g, recv, c):
    _, n, R, C = g.shape
    tr = _fit_rows(R, max(16, BLOCK_ELEMS // C))

    def body(c_ref, g_ref, r_ref, o_ref):
        o_ref[...] = (g_ref[...].astype(F32) + r_ref[...].astype(F32)).astype(o_ref.dtype)

    blk = pl.BlockSpec((None, tr, C), lambda j, i, c_ref: (j, i, 0))
    return pl.pallas_call(
        body, name=name, out_shape=jax.ShapeDtypeStruct((n, R, C), g.dtype),
        grid_spec=pltpu.PrefetchScalarGridSpec(
            num_scalar_prefetch=1, grid=(n, R // tr),
            in_specs=[pl.BlockSpec((None, None, tr, C), lambda j, i, c_ref: (c_ref[0], j, i, 0)), blk], out_specs=blk),
        compiler_params=_cp("parallel", "parallel"),
    )(c, g, recv)


def sum_chips(name, own, recv, me):
    _, R, C = own.shape
    tr = _fit_rows(R, max(16, BLOCK_ELEMS // (2 * C)))

    def body(me_ref, o_ref, r_ref, out_ref):
        acc = o_ref[...].astype(F32)
        for k in range(3):
            acc = acc + r_ref[k].astype(F32)
        out_ref[...] = acc

    return pl.pallas_call(
        body, name=name, out_shape=jax.ShapeDtypeStruct((R, C), F32),
        grid_spec=pltpu.PrefetchScalarGridSpec(
            num_scalar_prefetch=1, grid=(R // tr,),
            in_specs=[pl.BlockSpec((None, tr, C), lambda i, me_ref: (me_ref[0], i, 0)), pl.BlockSpec((3, tr, C), lambda i, me_ref: (0, i, 0))],
            out_specs=pl.BlockSpec((tr, C), lambda i, me_ref: (i, 0))),
        compiler_params=_cp("parallel"),
    )(me, own, recv)


def sum_devices(parts):
    _, R, C = parts.shape

    def body(p_ref, o_ref):
        acc = p_ref[0]
        for k in range(1, 8):
            acc = acc + p_ref[k]
        o_ref[...] = acc

    return pl.pallas_call(body, name="sum_devices", out_shape=jax.ShapeDtypeStruct((R, C), F32))(parts)


def _fit_rows(R, pref):
    for t in range(min(pref, R), 0, -1):
        if R % t == 0 and (t % 16 == 0 or t == R):
            return t
    raise ValueError((R, pref))


def adamw(name, w, g, m, v):
    shape = w.shape
    C = shape[-1]
    w2, g2, m2, v2 = (a.reshape(-1, C) for a in (w, g, m, v))
    R = w2.shape[0]
    tr = _fit_rows(R, max(8, (1 << 18) // C)) if R * C > (1 << 18) else R
    c1 = 1.0 - ADAM_B1 ** ADAM_STEP
    c2 = 1.0 - ADAM_B2 ** ADAM_STEP

    def body(w_ref, g_ref, m_ref, v_ref, d_ref, mo_ref, vo_ref):
        gg = g_ref[...]
        mn = ADAM_B1 * m_ref[...] + (1.0 - ADAM_B1) * gg
        vn = ADAM_B2 * v_ref[...] + (1.0 - ADAM_B2) * jnp.square(gg)
        d_ref[...] = -ADAM_LR * ((mn / c1) / (jnp.sqrt(vn / c2) + ADAM_EPS) + ADAM_WD * w_ref[...])
        mo_ref[...] = mn
        vo_ref[...] = vn

    blk = pl.BlockSpec((tr, C), lambda i: (i, 0))
    outs = pl.pallas_call(
        body, name=name, grid=(R // tr,), in_specs=[blk] * 4, out_specs=[blk] * 3,
        out_shape=[jax.ShapeDtypeStruct((R, C), F32)] * 3, compiler_params=_cp("parallel"),
    )(w2, g2, m2, v2)
    return tuple(o.reshape(shape) for o in outs)


def _unshard(name, parts):
    n, L, r, c = parts.shape
    if name in COL_SHARDED:
        return parts.transpose(1, 2, 0, 3).reshape(L, r, n * c)
    return parts.transpose(1, 0, 2, 3).reshape(L, n * r, c)


def _shard(name, full):
    L, R, C = full.shape
    if name in COL_SHARDED:
        return full.reshape(L, R, 4, C // 4).transpose(0, 2, 1, 3)
    return full.reshape(L, 4, R // 4, C)


def _to_local(name, w):
    if name == "w_in":
        return _w_in_to_local(w)
    if name == "mla_w_uq":
        return _w_uq_to_local(w)
    return w


def _from_local(name, g):
    if name == "w_in":
        return _w_in_from_local(g)
    if name == "mla_w_uq":
        return _w_uq_from_local(g)
    return g


TWIN_WEIGHTS = ("ffn1_norm", "ffn1_wi", "ffn1_wo", "mix_norm", "w_in", "mla_q_norm", "mla_w_uq", "mla_kv_norm", "mla_w_ukv",
                "hgrn_lb_logits", "hgrn_norm", "ssm_conv_w", "ssm_conv_b", "ssm_a_log", "ssm_dt_bias", "ssm_d", "ssm_norm",
                "w_o_mla", "w_o_hgrn", "w_o_ssm", "w_out", "ffn2_norm", "ffn2_wi", "ffn2_wo", "final_norm")
SMALL_PACK = SMALL + ("ssm_conv_w", "final_norm")


def _pad_rows(flat, cols):
    n = flat.shape[0]
    rows = -(-n // cols)
    rows = -(-rows // 8) * 8
    return jnp.concatenate([flat, jnp.zeros((rows * cols - n,), flat.dtype)]).reshape(rows, cols)


def kernel(x, ffn1_norm, ffn1_wi, ffn1_wo, mix_norm, w_in, mla_q_norm, mla_w_uq, mla_kv_norm, mla_w_ukv, hgrn_lb_logits, hgrn_norm, ssm_conv_w, ssm_conv_b, ssm_a_log, ssm_dt_bias, ssm_d, ssm_norm, w_o_mla, w_o_hgrn, w_o_ssm, w_out, ffn2_norm, ffn2_wi, ffn2_wo, final_norm, loss_target, m_ffn1_norm, m_ffn1_wi, m_ffn1_wo, m_mix_norm, m_w_in, m_mla_q_norm, m_mla_w_uq, m_mla_kv_norm, m_mla_w_ukv, m_hgrn_lb_logits, m_hgrn_norm, m_ssm_conv_w, m_ssm_conv_b, m_ssm_a_log, m_ssm_dt_bias, m_ssm_d, m_ssm_norm, m_w_o_mla, m_w_o_hgrn, m_w_o_ssm, m_w_out, m_ffn2_norm, m_ffn2_wi, m_ffn2_wo, m_final_norm, v_ffn1_norm, v_ffn1_wi, v_ffn1_wo, v_mix_norm, v_w_in, v_mla_q_norm, v_mla_w_uq, v_mla_kv_norm, v_mla_w_ukv, v_hgrn_lb_logits, v_hgrn_norm, v_ssm_conv_w, v_ssm_conv_b, v_ssm_a_log, v_ssm_dt_bias, v_ssm_d, v_ssm_norm, v_w_o_mla, v_w_o_hgrn, v_w_o_ssm, v_w_out, v_ffn2_norm, v_ffn2_wi, v_ffn2_wo, v_final_norm):
    args = dict(locals())
    w = {n: args[n] for n in TWIN_WEIGHTS}
    m = {n: args["m_" + n] for n in TWIN_WEIGHTS}
    v = {n: args["v_" + n] for n in TWIN_WEIGHTS}
    xi, yi, ci = _coords()
    chip = 2 * xi + yi

    shards = [w[n].astype(BF16) for n in BIG]
    gathered = [lax.dynamic_update_slice(parts, own[None], (chip, 0, 0, 0)) for parts, own in zip(weights_allgather(shards), shards)]
    full = {n: _to_local(n, _unshard(n, parts)) for n, parts in zip(BIG, gathered)}
    conv_parts = device_allgather("conv_allgather", _pad_rows(w["ssm_conv_w"].reshape(-1), 128))
    conv_full = jnp.concatenate(
        [conv_parts[2 * j].reshape(-1)[:w["ssm_conv_w"].size].reshape(w["ssm_conv_w"].shape) for j in range(4)], axis=-1)
    Wl = [dict({n: full[n][l] for n in BIG}, ssm_conv_w=conv_full[l]) for l in range(DEPTH)]
    small = {n: w[n] for n in SMALL}

    loss, grad_x, grads, dlogits, dfinal = local_step(x[0], loss_target[0], Wl, small, w["final_norm"])
    loss = lax.psum(loss[0, 0], ("x", "y", "c"))

    c_idx, chip_idx = ci.astype(jnp.int32).reshape(1), chip.astype(jnp.int32).reshape(1)
    gs = [_shard(n, _from_local(n, jnp.stack([grads[l][n] for l in range(DEPTH)]))).astype(BF16) for n in BIG]
    from_sibling = grad_layer_exchange(gs)
    pair_sums = [add_pair("add_pair_" + n, a, b, c_idx) for n, a, b in zip(BIG, gs, from_sibling)]
    from_chips = grad_chip_exchange(pair_sums)
    reduced = [sum_chips("sum_chips_" + n, a, b, chip_idx) for n, a, b in zip(BIG, pair_sums, from_chips)]
    others = grad_reduced_exchange(reduced)
    g = {n: jnp.where(ci == 0, jnp.stack([mine, other]), jnp.stack([other, mine])) for n, mine, other in zip(BIG, reduced, others)}

    sg = {n: jnp.concatenate([grads[l][n] for l in range(DEPTH)], axis=0) for n in SMALL if n != "hgrn_lb_logits"}
    sg["hgrn_lb_logits"] = dlogits
    sg["ssm_conv_w"] = jnp.stack([grads[l]["ssm_conv_w"] for l in range(DEPTH)])
    sg["final_norm"] = dfinal
    spack = _pad_rows(jnp.concatenate([sg[n].reshape(-1) for n in SMALL_PACK]), 128)
    ssum = sum_devices(device_allgather("small_grads_allgather", spack)).reshape(-1)
    off = 0
    for n in SMALL_PACK:
        size = sg[n].size
        g[n] = ssum[off:off + size].reshape(sg[n].shape)
        off += size
    shard_cols = w["ssm_conv_w"].shape[-1]
    g["ssm_conv_w"] = lax.dynamic_slice_in_dim(g["ssm_conv_w"], chip * shard_cols, shard_cols, axis=2)
    g = {n: g[n].reshape(w[n].shape) for n in TWIN_WEIGHTS}

    upd = {n: adamw("adamw_" + n, w[n], g[n], m[n], v[n]) for n in TWIN_WEIGHTS}
    return (loss, grad_x[None], *[g[n] for n in TWIN_WEIGHTS], *[upd[n][0] for n in TWIN_WEIGHTS],
            *[upd[n][1] for n in TWIN_WEIGHTS], *[upd[n][2] for n in TWIN_WEIGHTS])
```

```python
import functools
import math

import jax
import jax.numpy as jnp
import numpy as np
from jax import lax
from jax.experimental import pallas as pl
from jax.experimental.pallas import tpu as pltpu

F32 = jnp.float32
BF16 = jnp.bfloat16
MESH = pl.DeviceIdType.MESH

D_MODEL = 2048
DEPTH = 2
CHUNK = 64
EPS = 1e-6
MLA_HEADS, MLA_Q_RANK, MLA_KV_RANK, MLA_NOPE, MLA_ROPE, MLA_V = 16, 512, 512, 128, 64, 128
ROPE_THETA = 10000.0
HG_HEADS, HG_DK = 16, 128
HG_WIDTH = HG_HEADS * HG_DK
SSM_INNER, SSM_HEADDIM, SSM_HEADS, SSM_GROUPS, SSM_STATE, SSM_CONV = 4096, 64, 64, 8, 128, 4
SSM_CONV_DIM = SSM_INNER + 2 * SSM_GROUPS * SSM_STATE
D_FF = 5632
IN_DIM = 25728
ADAM_LR, ADAM_B1, ADAM_B2, ADAM_EPS, ADAM_WD, ADAM_STEP = 0.001, 0.9, 0.999, 1e-08, 0.01, 10

P_QLAT, P_KVLAT, P_HQ, P_HF, P_HI, P_HGATE, P_Z, P_XBC, P_GATES, P_KPEDT = (
    0, 512, 1024, 3072, 5120, 7168, 9216, 13312, 19456, 25600)
IN_PAD = 26624

VMEM_LIMIT_V7X = 48 << 20
SSD_Q = 256
HG_HB = 8
assert all(off % (HG_HB * HG_DK) == 0 for off in (P_HQ, P_HF, P_HI)) and HG_HEADS % HG_HB == 0
NEG = -1e30


def _cp(*sem):
    return pltpu.CompilerParams(dimension_semantics=sem, vmem_limit_bytes=VMEM_LIMIT_V7X)


def _fit(n, pref):
    if n <= pref:
        return n
    for t in range(pref, 0, -128):
        if n % t == 0:
            return t
    raise ValueError((n, pref))


_DIMS = {"nn": (((1,), (0,)), ((), ())), "nt": (((1,), (1,)), ((), ())), "tn": (((0,), (0,)), ((), ()))}


def _dot(a, b, mode):
    return lax.dot_general(a.astype(BF16), b.astype(BF16), _DIMS[mode], preferred_element_type=F32)


def matmul(name, a, b, mode="nn", out_dtype=F32, alpha=1.0, res=None, tm=1024, tn=1024, tk=2048):
    if mode == "nn":
        (M, K), (K2, N) = a.shape, b.shape
    elif mode == "nt":
        (M, K), (N, K2) = a.shape, b.shape
    else:
        (K, M), (K2, N) = a.shape, b.shape
    assert K == K2, (name, a.shape, b.shape, mode)
    tm, tn, tk = _fit(M, tm), _fit(N, tn), _fit(K, tk)
    nk = K // tk
    a_spec = pl.BlockSpec((tk, tm), lambda i, j, k: (k, i)) if mode == "tn" else pl.BlockSpec((tm, tk), lambda i, j, k: (i, k))
    b_spec = pl.BlockSpec((tn, tk), lambda i, j, k: (j, k)) if mode == "nt" else pl.BlockSpec((tk, tn), lambda i, j, k: (k, j))
    o_spec = pl.BlockSpec((tm, tn), lambda i, j, k: (i, j))
    has_res = res is not None

    def body(*refs):
        a_ref, b_ref = refs[0], refs[1]
        o_ref = refs[3] if has_res else refs[2]

        def finish(v):
            if alpha != 1.0:
                v = v * alpha
            if has_res:
                v = v + refs[2][...].astype(F32)
            o_ref[...] = v.astype(o_ref.dtype)

        if nk == 1:
            finish(_dot(a_ref[...], b_ref[...], mode))
            return
        acc = refs[-1]
        k = pl.program_id(2)

        @pl.when(k == 0)
        def _():
            acc[...] = _dot(a_ref[...], b_ref[...], mode)

        @pl.when(jnp.logical_and(k > 0, k < nk - 1))
        def _():
            acc[...] += _dot(a_ref[...], b_ref[...], mode)

        @pl.when(k == nk - 1)
        def _():
            finish(acc[...] + _dot(a_ref[...], b_ref[...], mode))

    ins = [a, b] + ([res] if has_res else [])
    return pl.pallas_call(
        body, name=name, grid=(M // tm, N // tn, nk),
        in_specs=[a_spec, b_spec] + ([o_spec] if has_res else []), out_specs=o_spec,
        out_shape=jax.ShapeDtypeStruct((M, N), out_dtype),
        scratch_shapes=[pltpu.VMEM((tm, tn), F32)] if nk > 1 else [],
        compiler_params=_cp("parallel", "parallel", "arbitrary"),
    )(*ins)


def _row_specs(rows, consts, tm):
    specs = []
    for arr, w, off in rows:
        specs.append(pl.BlockSpec((tm, w), functools.partial(lambda j, i, off: (i, off + j), off=off)))
    for arr, w, off in consts:
        specs.append(pl.BlockSpec((arr.shape[0], w), functools.partial(lambda j, i, off: (0, off + j), off=off)))
    return specs


def rowwise(name, fn, rows, consts, outs, n_groups=1, tm=512):
    S = rows[0][0].shape[0]
    tm = _fit(S, tm)
    n_in = len(rows) + len(consts)

    def body(*refs):
        vals = fn(*[r[...].astype(F32) for r in refs[:n_in]])
        for o_ref, v in zip(refs[n_in:], vals):
            o_ref[...] = v.astype(o_ref.dtype)

    return pl.pallas_call(
        body, name=name, grid=(n_groups, S // tm),
        in_specs=_row_specs(rows, consts, tm),
        out_specs=[pl.BlockSpec((tm, w), lambda j, i: (i, j)) for _, w, _ in outs],
        out_shape=[jax.ShapeDtypeStruct((S, W), dt) for W, _, dt in outs],
        compiler_params=_cp("parallel", "parallel"),
    )(*[r[0] for r in rows], *[c[0] for c in consts])


def rowwise_bwd(name, fn, rows, consts, cts, row_grads, const_grads, adds=None, n_groups=1, tm=512):
    S = rows[0][0].shape[0]
    tm = _fit(S, tm)
    adds = adds or {}
    add_idx = list(adds)
    n_r, n_c, n_ct, n_add = len(rows), len(consts), len(cts), len(adds)

    def body(*refs):
        ins = [r[...].astype(F32) for r in refs[:n_r + n_c]]
        ct = tuple(r[...].astype(F32) for r in refs[n_r + n_c:n_r + n_c + n_ct])
        add_refs = refs[n_r + n_c + n_ct:n_r + n_c + n_ct + n_add]
        out_refs = refs[n_r + n_c + n_ct + n_add:]
        _, vjp = jax.vjp(fn, *ins)
        g = list(vjp(ct))
        for a_ref, idx in zip(add_refs, add_idx):
            g[idx] = g[idx] + a_ref[...].astype(F32)
        for q, (idx, _, _) in enumerate(row_grads):
            out_refs[q][...] = g[idx].astype(out_refs[q].dtype)
        first = pl.program_id(1) == 0
        for q, idx in enumerate(const_grads):
            o_ref = out_refs[len(row_grads) + q]

            @pl.when(first)
            def _(o_ref=o_ref):
                o_ref[...] = jnp.zeros_like(o_ref)

            o_ref[...] += g[n_r + idx]

    in_specs = _row_specs(rows, consts, tm)
    in_specs += [pl.BlockSpec((tm, w), lambda j, i: (i, j)) for _, w in cts]
    in_specs += [pl.BlockSpec((tm, rows[idx][1]), lambda j, i: (i, j)) for idx in add_idx]
    out_specs = [pl.BlockSpec((tm, rows[idx][1]), lambda j, i: (i, j)) for idx, _, _ in row_grads]
    out_specs += [pl.BlockSpec((consts[idx][0].shape[0], consts[idx][1]), lambda j, i: (0, j)) for idx in const_grads]
    out_shape = [jax.ShapeDtypeStruct((S, W), dt) for _, W, dt in row_grads]
    out_shape += [jax.ShapeDtypeStruct((consts[idx][0].shape[0], consts[idx][1] * n_groups), F32) for idx in const_grads]
    return pl.pallas_call(
        body, name=name, grid=(n_groups, S // tm), in_specs=in_specs, out_specs=out_specs, out_shape=out_shape,
        compiler_params=_cp("parallel", "arbitrary"),
    )(*[r[0] for r in rows], *[c[0] for c in consts], *[c[0] for c in cts], *adds.values())


def f_rmsnorm(x, w):
    return (x * lax.rsqrt(jnp.mean(x * x, axis=-1, keepdims=True) + EPS) * w,)


def f_swiglu(g, u):
    return (jax.nn.silu(g) * u,)


def f_hgrn_out(o, g, w):
    return (o * lax.rsqrt(jnp.mean(o * o, axis=-1, keepdims=True) + EPS) * w * jax.nn.silu(g),)


def f_ssm_out(y, z, w):
    y = y * jax.nn.silu(z)
    return (y * lax.rsqrt(jnp.mean(y * y, axis=-1, keepdims=True) + EPS) * w,)


def f_merge(ga, gb, gc, ya, yb, yc):
    return (jax.nn.sigmoid(ga) * ya + jax.nn.sigmoid(gb) * yb + jax.nn.sigmoid(gc) * yc,)


def f_lower_bounds(logits):
    p = jax.nn.softmax(logits, axis=0)
    rows = [jnp.zeros_like(p[0:1])]
    for l in range(1, DEPTH):
        rows.append(rows[-1] + p[l:l + 1])
    return (jnp.concatenate(rows, axis=0),)


def loss_head(x, target, w, tm=512):
    S, D = x.shape
    tm = _fit(S, tm)

    def loss_fn(xb, wb, tb):
        (y,) = f_rmsnorm(xb, wb)
        return 0.5 * jnp.sum(jnp.mean(jnp.square(y - tb), axis=-1))

    def body(x_ref, t_ref, w_ref, dx_ref, dxb_ref, dw_ref, loss_ref):
        @pl.when(pl.program_id(0) == 0)
        def _():
            dw_ref[...] = jnp.zeros_like(dw_ref)
            loss_ref[...] = jnp.zeros_like(loss_ref)

        l, (dx, dw) = jax.value_and_grad(loss_fn, argnums=(0, 1))(x_ref[...], w_ref[...], t_ref[...])
        dx_ref[...] = dx
        dxb_ref[...] = dx.astype(BF16)
        dw_ref[...] += dw
        loss_ref[...] += jnp.full(loss_ref.shape, l, F32)

    row = pl.BlockSpec((tm, D), lambda i: (i, 0))
    vec = pl.BlockSpec((1, D), lambda i: (0, 0))
    return pl.pallas_call(
        body, name="loss_head", grid=(S // tm,), in_specs=[row, row, vec],
        out_specs=[row, row, vec, pl.BlockSpec((1, 128), lambda i: (0, 0))],
        out_shape=[jax.ShapeDtypeStruct((S, D), F32), jax.ShapeDtypeStruct((S, D), BF16), jax.ShapeDtypeStruct((1, D), F32),
                   jax.ShapeDtypeStruct((1, 128), F32)],
        compiler_params=_cp("arbitrary"),
    )(x, target, w)


def rope_tables(S):
    inv = 1.0 / (ROPE_THETA ** (jnp.arange(0, MLA_ROPE, 2, dtype=F32) / MLA_ROPE))
    ang = jnp.arange(S, dtype=F32)[:, None] * inv[None, :]
    c, s = jnp.cos(ang), jnp.sin(ang)
    return jnp.tile(c, (1, 4)), jnp.concatenate([-s, s, -s, s], axis=1)


def _rope128(x, cosf, sinf):
    lane = lax.broadcasted_iota(jnp.int32, x.shape, 1)
    swapped = jnp.where((lane & 32) == 0, pltpu.roll(x, 96, 1), pltpu.roll(x, 32, 1))
    return x * cosf + swapped * sinf


def mla_q_prep(q, cosf, sinf, tm=512):
    S = q.shape[0]
    tm = _fit(S, tm)

    def body(q_ref, c_ref, s_ref, o_ref):
        x = q_ref[...]
        r = _rope128(x[:, 256:384], c_ref[...], s_ref[...])
        lane = lax.broadcasted_iota(jnp.int32, r.shape, 1)
        z = jnp.zeros_like(r)
        o_ref[...] = jnp.concatenate(
            [x[:, 0:128], jnp.where(lane < 64, r, z), x[:, 128:256], jnp.where(lane >= 64, r, z)], axis=1).astype(BF16)

    tab = pl.BlockSpec((tm, 128), lambda j, i: (i, 0))
    return pl.pallas_call(
        body, name="mla_q_prep", grid=(8, S // tm),
        in_specs=[pl.BlockSpec((tm, 384), lambda j, i: (i, j)), tab, tab],
        out_specs=pl.BlockSpec((tm, 512), lambda j, i: (i, j)),
        out_shape=jax.ShapeDtypeStruct((S, 4096), BF16), compiler_params=_cp("parallel", "parallel"),
    )(q, cosf, sinf)


def mla_q_prep_bwd(dqc, cosf, sinf, tm=512):
    S = dqc.shape[0]
    tm = _fit(S, tm)

    def body(d_ref, c_ref, s_ref, o_ref):
        d = d_ref[...]
        lane = lax.broadcasted_iota(jnp.int32, (tm, 128), 1)
        dr = jnp.where(lane < 64, d[:, 128:256], d[:, 384:512])
        o_ref[...] = jnp.concatenate([d[:, 0:128], d[:, 256:384], _rope128(dr, c_ref[...], -s_ref[...])], axis=1).astype(BF16)

    tab = pl.BlockSpec((tm, 128), lambda j, i: (i, 0))
    return pl.pallas_call(
        body, name="mla_q_prep_bwd", grid=(8, S // tm),
        in_specs=[pl.BlockSpec((tm, 512), lambda j, i: (i, j)), tab, tab],
        out_specs=pl.BlockSpec((tm, 384), lambda j, i: (i, j)),
        out_shape=jax.ShapeDtypeStruct((S, 3072), BF16), compiler_params=_cp("parallel", "parallel"),
    )(dqc, cosf, sinf)


def mla_k_prep(kv, proj, cosf, sinf, tm=512):
    S = kv.shape[0]
    tm = _fit(S, tm)

    def body(kv_ref, pe_ref, c_ref, s_ref, k_ref, v_ref):
        x = kv_ref[...]
        r = _rope128(pe_ref[...], c_ref[...], s_ref[...])
        lane = lax.broadcasted_iota(jnp.int32, r.shape, 1)
        r2 = jnp.where(lane < 64, r, pltpu.roll(r, 64, 1))
        k_ref[...] = jnp.concatenate([x[:, 0:128], r2, x[:, 256:384], r2], axis=1).astype(BF16)
        v_ref[...] = jnp.concatenate([x[:, 128:256], x[:, 384:512]], axis=1).astype(BF16)

    tab = pl.BlockSpec((tm, 128), lambda j, i: (i, 0))
    return pl.pallas_call(
        body, name="mla_k_prep", grid=(8, S // tm),
        in_specs=[pl.BlockSpec((tm, 512), lambda j, i: (i, j)), pl.BlockSpec((tm, 128), lambda j, i: (i, P_KPEDT // 128)), tab, tab],
        out_specs=[pl.BlockSpec((tm, 512), lambda j, i: (i, j)), pl.BlockSpec((tm, 256), lambda j, i: (i, j))],
        out_shape=[jax.ShapeDtypeStruct((S, 4096), BF16), jax.ShapeDtypeStruct((S, 2048), BF16)],
        compiler_params=_cp("parallel", "parallel"),
    )(kv, proj, cosf, sinf)


def mla_k_prep_bwd(dkc, dv, cosf, sinf, tm=512):
    S = dkc.shape[0]
    tm = _fit(S, tm)

    def body(dk_ref, dv_ref, c_ref, s_ref, dkv_ref, dpe_ref):
        dk, dvv = dk_ref[...], dv_ref[...]
        dkv_ref[...] = jnp.concatenate([dk[:, 0:128], dvv[:, 0:128], dk[:, 256:384], dvv[:, 128:256]], axis=1).astype(BF16)
        d2 = dk[:, 128:256] + dk[:, 384:512]
        lane = lax.broadcasted_iota(jnp.int32, d2.shape, 1)
        dr = jnp.where(lane < 64, d2 + pltpu.roll(d2, 64, 1), 0.0)
        dpe = jnp.where(lane < 64, _rope128(dr, c_ref[...], -s_ref[...]), 0.0)

        @pl.when(pl.program_id(1) == 0)
        def _():
            dpe_ref[...] = jnp.zeros_like(dpe_ref)

        dpe_ref[...] += dpe

    tab = pl.BlockSpec((tm, 128), lambda i, j: (i, 0))
    return pl.pallas_call(
        body, name="mla_k_prep_bwd", grid=(S // tm, 8),
        in_specs=[pl.BlockSpec((tm, 512), lambda i, j: (i, j)), pl.BlockSpec((tm, 256), lambda i, j: (i, j)), tab, tab],
        out_specs=[pl.BlockSpec((tm, 512), lambda i, j: (i, j)), tab],
        out_shape=[jax.ShapeDtypeStruct((S, 4096), BF16), jax.ShapeDtypeStruct((S, 128), F32)],
        compiler_params=_cp("parallel", "arbitrary"),
    )(dkc, dv, cosf, sinf)


ATT_SCALE = (MLA_NOPE + MLA_ROPE) ** -0.5


def _att_scores(q, k, qi, ki, t):
    s = _dot(q, k, "nt") * ATT_SCALE
    rows = qi * t + lax.broadcasted_iota(jnp.int32, (t, t), 0)
    cols = ki * t + lax.broadcasted_iota(jnp.int32, (t, t), 1)
    shift = CHUNK.bit_length() - 1
    return jnp.where((cols >> shift) <= (rows >> shift), s, NEG)


def attention_fwd(qc, kc, vb, t=2048):
    S = qc.shape[0]
    t = _fit(S, t)
    n = S // t

    def body(q_ref, k_ref, v_ref, o_ref, lse_ref, m_s, l_s, acc_s):
        qi, ki = pl.program_id(1), pl.program_id(2)

        @pl.when(ki == 0)
        def _():
            m_s[...] = jnp.full_like(m_s, NEG)
            l_s[...] = jnp.zeros_like(l_s)
            acc_s[...] = jnp.zeros_like(acc_s)

        @pl.when(ki <= qi)
        def _():
            s = _att_scores(q_ref[...], k_ref[...], qi, ki, t)
            m_prev = m_s[...]
            m_new = jnp.maximum(m_prev, jnp.max(s, axis=1, keepdims=True))
            alpha = jnp.exp(m_prev - m_new)
            p = jnp.exp(s - m_new)
            l_s[...] = alpha * l_s[...] + jnp.sum(p, axis=1, keepdims=True)
            acc_s[...] = alpha * acc_s[...] + _dot(p, v_ref[...], "nn")
            m_s[...] = m_new

        @pl.when(ki == qi)
        def _():
            o_ref[...] = (acc_s[...] / l_s[...]).astype(o_ref.dtype)
            lse_ref[...] = m_s[...] + jnp.log(l_s[...])

    return pl.pallas_call(
        body, name="attention_fwd", grid=(MLA_HEADS, n, n),
        in_specs=[pl.BlockSpec((t, 256), lambda h, i, j: (i, h)),
                  pl.BlockSpec((t, 256), lambda h, i, j: (jnp.minimum(i, j), h)),
                  pl.BlockSpec((t, 128), lambda h, i, j: (jnp.minimum(i, j), h))],
        out_specs=[pl.BlockSpec((t, 128), lambda h, i, j: (i, h)), pl.BlockSpec((None, t, 1), lambda h, i, j: (h, i, 0))],
        out_shape=[jax.ShapeDtypeStruct((S, 2048), BF16), jax.ShapeDtypeStruct((MLA_HEADS, S, 1), F32)],
        scratch_shapes=[pltpu.VMEM((t, 1), F32), pltpu.VMEM((t, 1), F32), pltpu.VMEM((t, 128), F32)],
        compiler_params=_cp("parallel", "parallel", "arbitrary"),
    )(qc, kc, vb)


def attention_delta(o, do, t=1024):
    S = o.shape[0]
    t = _fit(S, t)

    def body(o_ref, do_ref, d_ref):
        d_ref[...] = jnp.sum(o_ref[...].astype(F32) * do_ref[...].astype(F32), axis=1, keepdims=True)

    blk = pl.BlockSpec((t, 128), lambda h, i: (i, h))
    return pl.pallas_call(
        body, name="attention_delta", grid=(MLA_HEADS, S // t), in_specs=[blk, blk],
        out_specs=pl.BlockSpec((None, t, 1), lambda h, i: (h, i, 0)),
        out_shape=jax.ShapeDtypeStruct((MLA_HEADS, S, 1), F32), compiler_params=_cp("parallel", "parallel"),
    )(o, do)


def attention_bwd_dq(qc, kc, vb, do, lse, delta, t=1024):
    S = qc.shape[0]
    t = _fit(S, t)
    n = S // t

    def body(q_ref, k_ref, v_ref, do_ref, lse_ref, dl_ref, dq_ref, acc):
        qi, ki = pl.program_id(1), pl.program_id(2)

        @pl.when(ki == 0)
        def _():
            acc[...] = jnp.zeros_like(acc)

        @pl.when(ki <= qi)
        def _():
            p = jnp.exp(_att_scores(q_ref[...], k_ref[...], qi, ki, t) - lse_ref[...])
            dp = _dot(do_ref[...], v_ref[...], "nt")
            ds = p * (dp - dl_ref[...])
            acc[...] += _dot(ds, k_ref[...], "nn")

        @pl.when(ki == qi)
        def _():
            dq_ref[...] = acc[...] * ATT_SCALE

    stat = pl.BlockSpec((None, t, 1), lambda h, i, j: (h, i, 0))
    return pl.pallas_call(
        body, name="attention_bwd_dq", grid=(MLA_HEADS, n, n),
        in_specs=[pl.BlockSpec((t, 256), lambda h, i, j: (i, h)),
                  pl.BlockSpec((t, 256), lambda h, i, j: (jnp.minimum(i, j), h)),
                  pl.BlockSpec((t, 128), lambda h, i, j: (jnp.minimum(i, j), h)),
                  pl.BlockSpec((t, 128), lambda h, i, j: (i, h)), stat, stat],
        out_specs=pl.BlockSpec((t, 256), lambda h, i, j: (i, h)),
        out_shape=jax.ShapeDtypeStruct((S, 4096), F32),
        scratch_shapes=[pltpu.VMEM((t, 256), F32)],
        compiler_params=_cp("parallel", "parallel", "arbitrary"),
    )(qc, kc, vb, do, lse, delta)


def attention_bwd_dkv(qc, kc, vb, do, lse, delta, t=2048):
    S = qc.shape[0]
    t = _fit(S, t)
    n = S // t

    def body(q_ref, k_ref, v_ref, do_ref, lse_ref, dl_ref, dk_ref, dv_ref, dk_acc, dv_acc):
        ki, qi = pl.program_id(1), pl.program_id(2)

        @pl.when(qi == 0)
        def _():
            dk_acc[...] = jnp.zeros_like(dk_acc)
            dv_acc[...] = jnp.zeros_like(dv_acc)

        @pl.when(qi >= ki)
        def _():
            s = _dot(k_ref[...], q_ref[...], "nt") * ATT_SCALE
            krow = ki * t + lax.broadcasted_iota(jnp.int32, (t, t), 0)
            qcol = qi * t + lax.broadcasted_iota(jnp.int32, (t, t), 1)
            shift = CHUNK.bit_length() - 1
            p = jnp.exp(jnp.where((krow >> shift) <= (qcol >> shift), s, NEG) - lse_ref[...])
            dv_acc[...] += _dot(p, do_ref[...], "nn")
            dp = _dot(v_ref[...], do_ref[...], "nt")
            ds = p * (dp - dl_ref[...])
            dk_acc[...] += _dot(ds, q_ref[...], "nn")

        @pl.when(qi == n - 1)
        def _():
            dk_ref[...] = dk_acc[...] * ATT_SCALE
            dv_ref[...] = dv_acc[...]

    stat = pl.BlockSpec((None, 1, t), lambda h, j, i: (h, 0, jnp.maximum(i, j)))
    return pl.pallas_call(
        body, name="attention_bwd_dkv", grid=(MLA_HEADS, n, n),
        in_specs=[pl.BlockSpec((t, 256), lambda h, j, i: (jnp.maximum(i, j), h)),
                  pl.BlockSpec((t, 256), lambda h, j, i: (j, h)),
                  pl.BlockSpec((t, 128), lambda h, j, i: (j, h)),
                  pl.BlockSpec((t, 128), lambda h, j, i: (jnp.maximum(i, j), h)), stat, stat],
        out_specs=[pl.BlockSpec((t, 256), lambda h, j, i: (j, h)), pl.BlockSpec((t, 128), lambda h, j, i: (j, h))],
        out_shape=[jax.ShapeDtypeStruct((S, 4096), F32), jax.ShapeDtypeStruct((S, 2048), F32)],
        scratch_shapes=[pltpu.VMEM((t, 256), F32), pltpu.VMEM((t, 128), F32)],
        compiler_params=_cp("parallel", "parallel", "arbitrary"),
    )(qc, kc, vb, do, lse.reshape(MLA_HEADS, 1, S), delta.reshape(MLA_HEADS, 1, S))


def _scan_rows(x, reverse):
    n = x.shape[0]
    row = lax.broadcasted_iota(jnp.int32, x.shape, 0)
    d = 1
    while d < n:
        if reverse:
            x = x + jnp.where(row < n - d, pltpu.roll(x, n - d, 0), 0.0)
        else:
            x = x + jnp.where(row >= d, pltpu.roll(x, d, 0), 0.0)
        d *= 2
    return x


@jax.custom_vjp
def cumsum_rows(x):
    return _scan_rows(x, False)


cumsum_rows.defvjp(lambda x: (_scan_rows(x, False), None), lambda _, g: (_scan_rows(g, True),))


def hgrn_chunk(q_in, f_in, v, lb, state_t):
    f = lb + (1.0 - lb) * jax.nn.sigmoid(f_in)
    q = jax.nn.silu(q_in) * HG_DK ** -0.5
    k = 1.0 - f
    b = cumsum_rows(jnp.log(f))
    b_last = b[CHUNK - 1:CHUNK]
    b_mid = b[CHUNK // 2 - 1:CHUNK // 2]
    r = lax.broadcasted_iota(jnp.int32, (CHUNK, CHUNK), 0)
    c = lax.broadcasted_iota(jnp.int32, (CHUNK, CHUNK), 1)
    att = jnp.where(c <= r, _dot(q * jnp.exp(b - b_mid), k * jnp.exp(b_mid - b), "nt"), 0.0)
    o = _dot(q * jnp.exp(b), state_t, "nt") + _dot(att, v, "nn")
    new_state_t = state_t * jnp.exp(b_last) + _dot(v, k * jnp.exp(b_last - b), "tn")
    return o, new_state_t


def hgrn_scan_fwd(proj, lb):
    S = proj.shape[0]
    nc = S // CHUNK
    W = HG_HB * 128

    def body(q_ref, f_ref, v_ref, lb_ref, o_ref, hst_ref, st):
        @pl.when(pl.program_id(1) == 0)
        def _():
            st[...] = jnp.zeros_like(st)

        for h in range(HG_HB):
            cs = slice(h * 128, (h + 1) * 128)
            hst_ref[h] = st[h]
            o, new = hgrn_chunk(q_ref[:, cs], f_ref[:, cs], v_ref[:, cs], lb_ref[:, cs], st[h])
            o_ref[:, cs] = o
            st[h] = new

    def seg(off):
        return pl.BlockSpec((CHUNK, W), functools.partial(lambda g, c, off: (c, off + g), off=off // W))

    return pl.pallas_call(
        body, name="hgrn_scan_fwd", grid=(HG_HEADS // HG_HB, nc),
        in_specs=[seg(P_HQ), seg(P_HF), seg(P_HI), pl.BlockSpec((1, W), lambda g, c: (0, g))],
        out_specs=[pl.BlockSpec((CHUNK, W), lambda g, c: (c, g)), pl.BlockSpec((None, HG_HB, 128, 128), lambda g, c: (c, g, 0, 0))],
        out_shape=[jax.ShapeDtypeStruct((S, HG_WIDTH), F32), jax.ShapeDtypeStruct((nc, HG_HEADS, 128, 128), F32)],
        scratch_shapes=[pltpu.VMEM((HG_HB, 128, 128), F32)],
        compiler_params=_cp("parallel", "arbitrary"),
    )(proj, proj, proj, lb)


def hgrn_scan_bwd(proj, lb, hst, do):
    S = proj.shape[0]
    nc = S // CHUNK
    W = HG_HB * 128

    def body(q_ref, f_ref, v_ref, lb_ref, hst_ref, do_ref, dq_ref, df_ref, dv_ref, dlb_ref, dst):
        @pl.when(pl.program_id(1) == 0)
        def _():
            dst[...] = jnp.zeros_like(dst)
            dlb_ref[...] = jnp.zeros_like(dlb_ref)

        for h in range(HG_HB):
            cs = slice(h * 128, (h + 1) * 128)
            _, vjp = jax.vjp(hgrn_chunk, q_ref[:, cs], f_ref[:, cs], v_ref[:, cs], lb_ref[:, cs], hst_ref[h])
            dq, df, dv, dlb, dstate = vjp((do_ref[:, cs], dst[h]))
            dq_ref[:, cs] = dq.astype(dq_ref.dtype)
            df_ref[:, cs] = df.astype(df_ref.dtype)
            dv_ref[:, cs] = dv.astype(dv_ref.dtype)
            dlb_ref[:, cs] += dlb
            dst[h] = dstate

    def seg(off):
        return pl.BlockSpec((CHUNK, W), functools.partial(lambda g, c, off: (nc - 1 - c, off + g), off=off // W))

    row = pl.BlockSpec((CHUNK, W), lambda g, c: (nc - 1 - c, g))
    vec = pl.BlockSpec((1, W), lambda g, c: (0, g))
    return pl.pallas_call(
        body, name="hgrn_scan_bwd", grid=(HG_HEADS // HG_HB, nc),
        in_specs=[seg(P_HQ), seg(P_HF), seg(P_HI), vec,
                  pl.BlockSpec((None, HG_HB, 128, 128), lambda g, c: (nc - 1 - c, g, 0, 0)), row],
        out_specs=[row, row, row, vec],
        out_shape=[jax.ShapeDtypeStruct((S, HG_WIDTH), BF16)] * 3 + [jax.ShapeDtypeStruct((1, HG_WIDTH), F32)],
        scratch_shapes=[pltpu.VMEM((HG_HB, 128, 128), F32)],
        compiler_params=_cp("parallel", "arbitrary"),
    )(proj, proj, proj, lb, hst, do)


def _silu_grad(x):
    s = jax.nn.sigmoid(x)
    return s * (1.0 + x * (1.0 - s))


def conv_fwd(proj, w, b, tm=512):
    S = proj.shape[0]
    tm = _fit(S, tm)
    G = 512
    off = P_XBC // G

    def body(cur_ref, prev_ref, w_ref, b_ref, act_ref, pre_ref):
        prev = prev_ref[...] * (pl.program_id(1) > 0).astype(F32)
        ext = jnp.concatenate([prev, cur_ref[...]], axis=0)
        n = tm + 8
        acc = b_ref[...] + jnp.zeros((tm, G), F32)
        for j in range(SSM_CONV):
            acc = acc + w_ref[j:j + 1, :] * pltpu.roll(ext, (n - 5 - j) % n, 0)[0:tm]
        pre_ref[...] = acc
        act_ref[...] = jax.nn.silu(acc)

    out = pl.BlockSpec((tm, G), lambda j, i: (i, j))
    return pl.pallas_call(
        body, name="conv_fwd", grid=(SSM_CONV_DIM // G, S // tm),
        in_specs=[pl.BlockSpec((tm, G), lambda j, i: (i, off + j)),
                  pl.BlockSpec((8, G), lambda j, i: (jnp.maximum(i * (tm // 8) - 1, 0), off + j)),
                  pl.BlockSpec((SSM_CONV, G), lambda j, i: (0, j)), pl.BlockSpec((1, G), lambda j, i: (0, j))],
        out_specs=[out, out], out_shape=[jax.ShapeDtypeStruct((S, SSM_CONV_DIM), F32)] * 2,
        compiler_params=_cp("parallel", "parallel"),
    )(proj, proj, w, b)


def conv_bwd(proj, pre, dact, w, tm=512):
    S = proj.shape[0]
    tm = _fit(S, tm)
    G = 512
    off = P_XBC // G
    nb = S // tm

    def body(x_ref, xp_ref, pre_ref, pren_ref, d_ref, dn_ref, w_ref, dx_ref, dw_ref, db_ref):
        i = pl.program_id(1)
        n = tm + 8
        dpre = d_ref[...] * _silu_grad(pre_ref[...])
        dpre_next = dn_ref[...] * _silu_grad(pren_ref[...]) * (i < nb - 1).astype(F32)
        dext = jnp.concatenate([dpre, dpre_next], axis=0)
        xext = jnp.concatenate([xp_ref[...] * (i > 0).astype(F32), x_ref[...]], axis=0)
        dx = jnp.zeros((tm, G), F32)
        dws = []
        for j in range(SSM_CONV):
            dx = dx + w_ref[j:j + 1, :] * pltpu.roll(dext, (n - (3 - j)) % n, 0)[0:tm]
            dws.append(jnp.sum(dpre * pltpu.roll(xext, (n - 5 - j) % n, 0)[0:tm], axis=0, keepdims=True))
        dx_ref[...] = dx.astype(dx_ref.dtype)

        @pl.when(i == 0)
        def _():
            dw_ref[...] = jnp.zeros_like(dw_ref)
            db_ref[...] = jnp.zeros_like(db_ref)

        dw_ref[...] += jnp.concatenate(dws, axis=0)
        db_ref[...] += jnp.sum(dpre, axis=0, keepdims=True)

    cur = pl.BlockSpec((tm, G), lambda j, i: (i, j))
    nxt = pl.BlockSpec((8, G), lambda j, i: (jnp.minimum((i + 1) * (tm // 8), S // 8 - 1), j))
    return pl.pallas_call(
        body, name="conv_bwd", grid=(SSM_CONV_DIM // G, nb),
        in_specs=[pl.BlockSpec((tm, G), lambda j, i: (i, off + j)),
                  pl.BlockSpec((8, G), lambda j, i: (jnp.maximum(i * (tm // 8) - 1, 0), off + j)),
                  cur, nxt, cur, nxt, pl.BlockSpec((SSM_CONV, G), lambda j, i: (0, j))],
        out_specs=[cur, pl.BlockSpec((SSM_CONV, G), lambda j, i: (0, j)), pl.BlockSpec((1, G), lambda j, i: (0, j))],
        out_shape=[jax.ShapeDtypeStruct((S, SSM_CONV_DIM), BF16), jax.ShapeDtypeStruct((SSM_CONV, SSM_CONV_DIM), F32),
                   jax.ShapeDtypeStruct((1, SSM_CONV_DIM), F32)],
        compiler_params=_cp("parallel", "arbitrary"),
    )(proj, proj, pre, pre, dact, dact, w)


def _eye_dot(a, mode):
    Q = a.shape[0] if mode == "tn" else a.shape[1]
    eye = (lax.broadcasted_iota(jnp.int32, (Q, Q), 0) == lax.broadcasted_iota(jnp.int32, (Q, Q), 1)).astype(BF16)
    hi = a.astype(BF16)
    r1 = a - hi.astype(F32)
    mid = r1.astype(BF16)
    lo = (r1 - mid.astype(F32)).astype(BF16)
    if mode == "tn":
        return sum(lax.dot_general(p, eye, _DIMS["tn"], preferred_element_type=F32) for p in (hi, mid, lo))
    return sum(lax.dot_general(eye, p, _DIMS["nt"], preferred_element_type=F32) for p in (hi, mid, lo))


@jax.custom_vjp
def _transpose_exact(a):
    return _eye_dot(a, "tn")


_transpose_exact.defvjp(lambda a: (_eye_dot(a, "tn"), None), lambda _, g: (_eye_dot(g, "nt"),))


def ssd_decay_inputs(dtr, bias, alog):
    dt = jax.nn.softplus(dtr + bias)
    acum = cumsum_rows(dt * -jnp.exp(alog))
    return dt, acum, _transpose_exact(acum)


def ssd_head(x, cb, bm, cm, dt, acum, a_s, dsk, h_prev):
    Q = x.shape[0]
    r = lax.broadcasted_iota(jnp.int32, (Q, Q), 0)
    c = lax.broadcasted_iota(jnp.int32, (Q, Q), 1)
    a_l = jnp.broadcast_to(acum, (Q, Q))
    decay = jnp.where(c <= r, jnp.exp(jnp.minimum(a_l - a_s, 0.0)), 0.0)
    xdt = x * dt
    y_diag = _dot(cb * decay, xdt, "nn")
    a_last = acum[Q - 1:Q]
    states = _dot(xdt * jnp.exp(a_last - acum), bm, "tn")
    h_new = h_prev * jnp.exp(a_last) + states
    y_off = _dot(cm, h_prev, "nt") * jnp.exp(acum)
    return y_diag + y_off + x * dsk, h_new


def _ssd_specs(S, rev):
    Q = _fit(S, SSD_Q)
    nc = S // Q
    ci = (lambda c: nc - 1 - c) if rev else (lambda c: c)
    x = pl.BlockSpec((Q, 512), lambda g, c: (ci(c), g))
    bm = pl.BlockSpec((Q, 128), lambda g, c: (ci(c), SSM_INNER // 128 + g))
    cm = pl.BlockSpec((Q, 128), lambda g, c: (ci(c), SSM_INNER // 128 + SSM_GROUPS + g))
    dtr = pl.BlockSpec((None, Q, 8), lambda g, c: (g, ci(c), 0))
    par = pl.BlockSpec((None, 1, 8), lambda g, c: (g, 0, 0))
    hs = pl.BlockSpec((None, 8, SSM_HEADDIM, SSM_STATE), lambda g, c: (ci(c), g, 0, 0))
    return Q, nc, x, bm, cm, dtr, par, hs


def ssd_scan_fwd(act, dtr, bias, alog, dsk):
    S = act.shape[0]
    Q, nc, x_s, bm_s, cm_s, dtr_s, par_s, hs_s = _ssd_specs(S, False)

    def body(x_ref, bm_ref, cm_ref, dtr_ref, b_ref, a_ref, d_ref, y_ref, hs_ref, st, dt_s, ac_s, act_s):
        @pl.when(pl.program_id(1) == 0)
        def _():
            st[...] = jnp.zeros_like(st)

        dt_s[...], ac_s[...], act_s[...] = ssd_decay_inputs(dtr_ref[...], b_ref[...], a_ref[...])
        bm, cm = bm_ref[...], cm_ref[...]
        cb = _dot(cm, bm, "nt")
        for j in range(8):
            hs_ref[j] = st[j]
            y, h_new = ssd_head(x_ref[:, j * 64:(j + 1) * 64], cb, bm, cm, dt_s[:, j:j + 1], ac_s[:, j:j + 1], act_s[j:j + 1, :],
                                d_ref[:, j:j + 1], st[j])
            y_ref[:, j * 64:(j + 1) * 64] = y
            st[j] = h_new

    return pl.pallas_call(
        body, name="ssd_scan_fwd", grid=(SSM_GROUPS, nc),
        in_specs=[x_s, bm_s, cm_s, dtr_s, par_s, par_s, par_s], out_specs=[x_s, hs_s],
        out_shape=[jax.ShapeDtypeStruct((S, SSM_INNER), F32), jax.ShapeDtypeStruct((nc, SSM_HEADS, SSM_HEADDIM, SSM_STATE), F32)],
        scratch_shapes=[pltpu.VMEM((8, SSM_HEADDIM, SSM_STATE), F32), pltpu.VMEM((Q, 8), F32), pltpu.VMEM((Q, 8), F32),
                        pltpu.VMEM((8, Q), F32)],
        compiler_params=_cp("parallel", "arbitrary"),
    )(act, act, act, dtr, bias, alog, dsk)


def ssd_scan_bwd(act, dtr, bias, alog, dsk, hs, dy):
    S = act.shape[0]
    Q, nc, x_s, bm_s, cm_s, dtr_s, par_s, hs_s = _ssd_specs(S, True)
    g_s = pl.BlockSpec((Q, 128), lambda g, c: (nc - 1 - c, g))

    def body(x_ref, bm_ref, cm_ref, dtr_ref, b_ref, a_ref, d_ref, hs_ref, dy_ref,
             dx_ref, dbm_ref, dcm_ref, ddtr_ref, db_ref, da_ref, dd_ref, dst, dt_s, ac_s, ddt_s, dac_s, act_s, dact_s):
        @pl.when(pl.program_id(1) == 0)
        def _():
            dst[...] = jnp.zeros_like(dst)
            db_ref[...] = jnp.zeros_like(db_ref)
            da_ref[...] = jnp.zeros_like(da_ref)
            dd_ref[...] = jnp.zeros_like(dd_ref)

        (dt_s[...], ac_s[...], act_s[...]), decay_vjp = jax.vjp(ssd_decay_inputs, dtr_ref[...], b_ref[...], a_ref[...])
        bm, cm = bm_ref[...], cm_ref[...]
        cb = _dot(cm, bm, "nt")
        dcb = jnp.zeros((Q, Q), F32)
        dbm = jnp.zeros((Q, 128), F32)
        dcm = jnp.zeros((Q, 128), F32)
        for j in range(8):
            cs = slice(j * 64, (j + 1) * 64)
            one = slice(j, j + 1)
            _, vjp = jax.vjp(ssd_head, x_ref[:, cs], cb, bm, cm, dt_s[:, one], ac_s[:, one], act_s[one, :], d_ref[:, one], hs_ref[j])
            dx, gcb, gb, gc, gdt, gac, gact, gdsk, gh = vjp((dy_ref[:, cs], dst[j]))
            dact_s[one, :] = gact
            dx_ref[:, cs] = dx
            dcb = dcb + gcb
            dbm = dbm + gb
            dcm = dcm + gc
            ddt_s[:, one] = gdt
            dac_s[:, one] = gac
            dd_ref[:, one] += gdsk
            dst[j] = gh
        dbm_ref[...] = dbm + _dot(dcb, cm, "tn")
        dcm_ref[...] = dcm + _dot(dcb, bm, "nn")
        ddtr, dbias, dalog = decay_vjp((ddt_s[...], dac_s[...], dact_s[...]))
        ddtr_ref[...] = ddtr
        db_ref[...] += dbias
        da_ref[...] += dalog

    return pl.pallas_call(
        body, name="ssd_scan_bwd", grid=(SSM_GROUPS, nc),
        in_specs=[x_s, bm_s, cm_s, dtr_s, par_s, par_s, par_s, hs_s, x_s],
        out_specs=[x_s, g_s, g_s, dtr_s, par_s, par_s, par_s],
        out_shape=[jax.ShapeDtypeStruct((S, SSM_INNER), F32), jax.ShapeDtypeStruct((S, 1024), F32), jax.ShapeDtypeStruct((S, 1024), F32),
                   jax.ShapeDtypeStruct((SSM_GROUPS, S, 8), F32)] + [jax.ShapeDtypeStruct((SSM_GROUPS, 1, 8), F32)] * 3,
        scratch_shapes=[pltpu.VMEM((8, SSM_HEADDIM, SSM_STATE), F32)] + [pltpu.VMEM((Q, 8), F32)] * 4 + [pltpu.VMEM((8, Q), F32)] * 2,
        compiler_params=_cp("parallel", "arbitrary"),
    )(act, act, act, dtr, bias, alog, dsk, hs, dy)


def _w_in_to_local(w):
    parts = [w[..., 0:1024], w[..., 1088:19520], w[..., 19584:25728], w[..., 1024:1088], w[..., 19520:19584],
             jnp.zeros(w.shape[:-1] + (IN_PAD - IN_DIM,), w.dtype)]
    return jnp.concatenate(parts, axis=-1)


def _w_in_from_local(g):
    return jnp.concatenate([g[..., 0:1024], g[..., 25600:25664], g[..., 1024:19456], g[..., 25664:25728], g[..., 19456:25600]], axis=-1)


def _w_uq_to_local(w):
    lead = w.shape[:-1]
    w = w.reshape(lead + (MLA_HEADS, 192))
    nope = w[..., :128].reshape(lead + (8, 256))
    rope = w[..., 128:].reshape(lead + (8, 128))
    return jnp.concatenate([nope, rope], axis=-1).reshape(lead + (3072,))


def _w_uq_from_local(g):
    lead = g.shape[:-1]
    g = g.reshape(lead + (8, 384))
    nope = g[..., :256].reshape(lead + (MLA_HEADS, 128))
    rope = g[..., 256:].reshape(lead + (MLA_HEADS, 64))
    return jnp.concatenate([nope, rope], axis=-1).reshape(lead + (3072,))


BIG = ("ffn1_wi", "ffn1_wo", "w_in", "mla_w_uq", "mla_w_ukv", "w_o_mla", "w_o_hgrn", "w_o_ssm", "w_out", "ffn2_wi", "ffn2_wo")
COL_SHARDED = ("ffn1_wi", "w_in", "mla_w_uq", "mla_w_ukv", "ffn2_wi")
SMALL = ("ffn1_norm", "mix_norm", "mla_q_norm", "mla_kv_norm", "hgrn_lb_logits", "hgrn_norm", "ssm_conv_b", "ssm_a_log",
         "ssm_dt_bias", "ssm_d", "ssm_norm", "ffn2_norm")


def _ffn_fwd(tag, x, norm_w, wi, wo):
    (h,) = rowwise(tag + "_norm", f_rmsnorm, [(x, D_MODEL, 0)], [(norm_w, D_MODEL, 0)], [(D_MODEL, D_MODEL, BF16)])
    gu = matmul(tag + "_wi", h, wi, out_dtype=BF16)
    (a,) = rowwise(tag + "_act", f_swiglu, [(gu, 512, 0), (gu, 512, D_FF // 512)], [], [(D_FF, 512, BF16)], n_groups=D_FF // 512)
    out = matmul(tag + "_wo", a, wo, alpha=0.5, res=x)
    return out, (x, h, gu, a)


def _ffn_bwd(tag, dx, dxb, saved, norm_w, wi, wo):
    x, h, gu, a = saved
    da = matmul(tag + "_da", dxb, wo, "nt", out_dtype=BF16, alpha=0.5)
    dwo = matmul(tag + "_dwo", a, dxb, "tn", alpha=0.5)
    dg, du = rowwise_bwd(tag + "_act_bwd", f_swiglu, [(gu, 512, 0), (gu, 512, D_FF // 512)], [], [(da, 512)],
                         [(0, D_FF, BF16), (1, D_FF, BF16)], [], n_groups=D_FF // 512)
    dgu = jnp.concatenate([dg, du], axis=1)
    dh = matmul(tag + "_dh", dgu, wi, "nt")
    dwi = matmul(tag + "_dwi", h, dgu, "tn")
    dx_in, dxb_in, dnorm = rowwise_bwd(tag + "_norm_bwd", f_rmsnorm, [(x, D_MODEL, 0)], [(norm_w, D_MODEL, 0)], [(dh, D_MODEL)],
                                       [(0, D_MODEL, F32), (0, D_MODEL, BF16)], [0], adds={0: dx}, tm=256)
    return dx_in, dxb_in, dnorm, dwi, dwo


def _ssm_params(W):
    return [W[k].reshape(SSM_GROUPS, 1, 8) for k in ("ssm_dt_bias", "ssm_a_log", "ssm_d")]


def _dt_cols(proj):
    S = proj.shape[0]
    return proj[:, P_KPEDT + 64:P_KPEDT + 128].reshape(S, SSM_GROUPS, 8).transpose(1, 0, 2)


def _mix_fwd(x, W, lb, tabs):
    cosf, sinf = tabs
    (h,) = rowwise("mix_norm", f_rmsnorm, [(x, D_MODEL, 0)], [(W["mix_norm"], D_MODEL, 0)], [(D_MODEL, D_MODEL, BF16)])
    proj = matmul("w_in", h, W["w_in"])
    (qn,) = rowwise("q_norm", f_rmsnorm, [(proj, 512, 0)], [(W["mla_q_norm"], 512, 0)], [(512, 512, BF16)])
    (kvn,) = rowwise("kv_norm", f_rmsnorm, [(proj, 512, 1)], [(W["mla_kv_norm"], 512, 0)], [(512, 512, BF16)])
    q = matmul("w_uq", qn, W["mla_w_uq"])
    kv = matmul("w_ukv", kvn, W["mla_w_ukv"])
    qc = mla_q_prep(q, cosf, sinf)
    kc, vb = mla_k_prep(kv, proj, cosf, sinf)
    o_a, lse = attention_fwd(qc, kc, vb)
    y_a = matmul("w_o_mla", o_a, W["w_o_mla"])
    o_h, hst = hgrn_scan_fwd(proj, lb)
    (pre_b,) = rowwise("hgrn_out", f_hgrn_out, [(o_h, 128, 0), (proj, 128, P_HGATE // 128)], [(W["hgrn_norm"], 128, 0)],
                       [(HG_WIDTH, 128, BF16)], n_groups=HG_HEADS, tm=1024)
    y_b = matmul("w_o_hgrn", pre_b, W["w_o_hgrn"])
    act, pre = conv_fwd(proj, W["ssm_conv_w"], W["ssm_conv_b"])
    dtr = _dt_cols(proj)
    y_s, hs = ssd_scan_fwd(act, dtr, *_ssm_params(W))
    (pre_c,) = rowwise("ssm_out", f_ssm_out, [(y_s, 512, 0), (proj, 512, P_Z // 512)], [(W["ssm_norm"], 512, 0)],
                       [(SSM_INNER, 512, BF16)], n_groups=SSM_GROUPS)
    y_c = matmul("w_o_ssm", pre_c, W["w_o_ssm"])
    g0 = P_GATES // 512
    (merged,) = rowwise("merge", f_merge, [(proj, 512, g0), (proj, 512, g0 + 4), (proj, 512, g0 + 8), (y_a, 512, 0), (y_b, 512, 0), (y_c, 512, 0)],
                        [], [(D_MODEL, 512, BF16)], n_groups=4)
    out = matmul("w_out", merged, W["w_out"], res=x)
    return out, (x, h, proj, qn, kvn, qc, kc, vb, o_a, lse, y_a, o_h, hst, pre_b, y_b, act, pre, dtr, y_s, hs, pre_c, y_c, merged)


def _mix_bwd(dx, dxb, saved, W, lb, tabs):
    cosf, sinf = tabs
    (x, h, proj, qn, kvn, qc, kc, vb, o_a, lse, y_a, o_h, hst, pre_b, y_b, act, pre, dtr, y_s, hs, pre_c, y_c, merged) = saved
    S = x.shape[0]
    g = {}
    dmerged = matmul("d_merged", dxb, W["w_out"], "nt")
    g["w_out"] = matmul("dw_out", merged, dxb, "tn")
    g0 = P_GATES // 512
    dga, dgb, dgc, dya, dyb, dyc = rowwise_bwd(
        "merge_bwd", f_merge, [(proj, 512, g0), (proj, 512, g0 + 4), (proj, 512, g0 + 8), (y_a, 512, 0), (y_b, 512, 0), (y_c, 512, 0)],
        [], [(dmerged, 512)], [(k, D_MODEL, BF16) for k in range(6)], [], n_groups=4)
    do_a = matmul("d_o_mla", dya, W["w_o_mla"], "nt", out_dtype=BF16)
    g["w_o_mla"] = matmul("dw_o_mla", o_a, dya, "tn")
    delta = attention_delta(o_a, do_a)
    dqc = attention_bwd_dq(qc, kc, vb, do_a, lse, delta)
    dkc, dv = attention_bwd_dkv(qc, kc, vb, do_a, lse, delta)
    dq = mla_q_prep_bwd(dqc, cosf, sinf)
    dkv, dpe = mla_k_prep_bwd(dkc, dv, cosf, sinf)
    dqn = matmul("d_qn", dq, W["mla_w_uq"], "nt")
    g["mla_w_uq"] = matmul("dw_uq", qn, dq, "tn")
    dkvn = matmul("d_kvn", dkv, W["mla_w_ukv"], "nt")
    g["mla_w_ukv"] = matmul("dw_ukv", kvn, dkv, "tn")
    dq_lat, g["mla_q_norm"] = rowwise_bwd("q_norm_bwd", f_rmsnorm, [(proj, 512, 0)], [(W["mla_q_norm"], 512, 0)], [(dqn, 512)],
                                          [(0, 512, BF16)], [0])
    dkv_lat, g["mla_kv_norm"] = rowwise_bwd("kv_norm_bwd", f_rmsnorm, [(proj, 512, 1)], [(W["mla_kv_norm"], 512, 0)], [(dkvn, 512)],
                                            [(0, 512, BF16)], [0])
    do_b = matmul("d_o_hgrn", dyb, W["w_o_hgrn"], "nt")
    g["w_o_hgrn"] = matmul("dw_o_hgrn", pre_b, dyb, "tn")
    do_h, dhgate, g["hgrn_norm"] = rowwise_bwd(
        "hgrn_out_bwd", f_hgrn_out, [(o_h, 128, 0), (proj, 128, P_HGATE // 128)], [(W["hgrn_norm"], 128, 0)], [(do_b, 128)],
        [(0, HG_WIDTH, F32), (1, HG_WIDTH, BF16)], [0], n_groups=HG_HEADS, tm=1024)
    dhq, dhf, dhi, dlb = hgrn_scan_bwd(proj, lb, hst, do_h)
    do_c = matmul("d_o_ssm", dyc, W["w_o_ssm"], "nt")
    g["w_o_ssm"] = matmul("dw_o_ssm", pre_c, dyc, "tn")
    dy_s, dz, g["ssm_norm"] = rowwise_bwd(
        "ssm_out_bwd", f_ssm_out, [(y_s, 512, 0), (proj, 512, P_Z // 512)], [(W["ssm_norm"], 512, 0)], [(do_c, 512)],
        [(0, SSM_INNER, F32), (1, SSM_INNER, BF16)], [0], n_groups=SSM_GROUPS)
    dxs, dbm, dcm, ddtr, dbias, dalog, ddsk = ssd_scan_bwd(act, dtr, *_ssm_params(W), hs, dy_s)
    g["ssm_dt_bias"], g["ssm_a_log"], g["ssm_d"] = (v.reshape(1, SSM_HEADS) for v in (dbias, dalog, ddsk))
    dxbc, g["ssm_conv_w"], g["ssm_conv_b"] = conv_bwd(proj, pre, jnp.concatenate([dxs, dbm, dcm], axis=1), W["ssm_conv_w"])
    ddt = ddtr.transpose(1, 0, 2).reshape(S, SSM_HEADS)
    dproj = jnp.concatenate([dq_lat, dkv_lat, dhq, dhf, dhi, dhgate, dz, dxbc, dga, dgb, dgc, dpe[:, :64].astype(BF16),
                             ddt.astype(BF16), jnp.zeros((S, IN_PAD - IN_DIM), BF16)], axis=1)
    dh = matmul("d_h_mix", dproj, W["w_in"], "nt")
    g["w_in"] = matmul("dw_in", h, dproj, "tn")
    dx_in, dxb_in, g["mix_norm"] = rowwise_bwd("mix_norm_bwd", f_rmsnorm, [(x, D_MODEL, 0)], [(W["mix_norm"], D_MODEL, 0)], [(dh, D_MODEL)],
                                               [(0, D_MODEL, F32), (0, D_MODEL, BF16)], [0], adds={0: dx}, tm=256)
    return dx_in, dxb_in, g, dlb


def local_step(x, target, Wl, small, final_norm):
    S = x.shape[0]
    tabs = rope_tables(S)
    (lbs,) = rowwise("lower_bounds", f_lower_bounds, [(small["hgrn_lb_logits"], HG_WIDTH, 0)], [], [(HG_WIDTH, HG_WIDTH, F32)])
    saved = []
    for l in range(DEPTH):
        W = dict(Wl[l])
        for k in SMALL:
            W[k] = small[k][l:l + 1]
        lb = lbs[l:l + 1]
        x, s1 = _ffn_fwd("ffn1", x, W["ffn1_norm"], W["ffn1_wi"], W["ffn1_wo"])
        x, s2 = _mix_fwd(x, W, lb, tabs)
        x, s3 = _ffn_fwd("ffn2", x, W["ffn2_norm"], W["ffn2_wi"], W["ffn2_wo"])
        saved.append((W, lb, s1, s2, s3))
    dx, dxb, dfinal, loss = loss_head(x, target, final_norm.reshape(1, D_MODEL))
    grads = [None] * DEPTH
    dlbs = [None] * DEPTH
    for l in reversed(range(DEPTH)):
        W, lb, s1, s2, s3 = saved[l]
        g = {}
        dx, dxb, g["ffn2_norm"], g["ffn2_wi"], g["ffn2_wo"] = _ffn_bwd("ffn2", dx, dxb, s3, W["ffn2_norm"], W["ffn2_wi"], W["ffn2_wo"])
        dx, dxb, gm, dlbs[l] = _mix_bwd(dx, dxb, s2, W, lb, tabs)
        g.update(gm)
        dx, dxb, g["ffn1_norm"], g["ffn1_wi"], g["ffn1_wo"] = _ffn_bwd("ffn1", dx, dxb, s1, W["ffn1_norm"], W["ffn1_wi"], W["ffn1_wo"])
        grads[l] = g
    (dlogits,) = rowwise_bwd("lower_bounds_bwd", f_lower_bounds, [(small["hgrn_lb_logits"], HG_WIDTH, 0)], [],
                             [(jnp.concatenate(dlbs, axis=0), HG_WIDTH)], [(0, HG_WIDTH, F32)], [])
    return loss, dx, grads, dlogits, dfinal


ANY = pl.BlockSpec(memory_space=pl.ANY)


def _coords():
    return lax.axis_index("x"), lax.axis_index("y"), lax.axis_index("c")


def _exchange_call(name, body, src, out_shape, n_copies):
    return pl.pallas_call(
        body, name=name, in_specs=[ANY], out_specs=ANY, out_shape=out_shape,
        scratch_shapes=[pltpu.SemaphoreType.DMA((n_copies,)), pltpu.SemaphoreType.DMA((n_copies,)), pltpu.SemaphoreType.DMA],
    )(src)


def _multi_exchange(name, body, srcs, out_shapes, sem_counts):
    n = len(srcs)
    return pl.pallas_call(
        body, name=name, in_specs=[ANY] * n, out_specs=[ANY] * len(out_shapes), out_shape=out_shapes,
        scratch_shapes=[pltpu.SemaphoreType.DMA((k,)) for k in sem_counts],
    )(*srcs)


def _chip_peers(x, y):
    return [(1 - x, y), (x, 1 - y), (1 - x, 1 - y)]


def weights_allgather(srcs):
    n = len(srcs)

    def body(*refs):
        src, out = refs[:n], refs[n:2 * n]
        send_sems, recv_sems, fsend_sems, frecv_sems = refs[2 * n:]
        x, y, c = _coords()
        me = 2 * x + y
        peers = _chip_peers(x, y)

        def ici(i, k, chip):
            return pltpu.make_async_remote_copy(src[i].at[c], out[i].at[chip, c], send_sems.at[3 * i + k], recv_sems.at[3 * i + k],
                                                device_id=(*peers[k], c), device_id_type=MESH)

        def fwd(i, k, chip, layer):
            return pltpu.make_async_remote_copy(out[i].at[chip, layer], out[i].at[chip, layer], fsend_sems.at[3 * i + k],
                                                frecv_sems.at[3 * i + k], device_id=(x, y, 1 - c), device_id_type=MESH)

        sends = [ici(i, k, me) for k in range(3) for i in range(n)]
        for cp in sends:
            cp.start()
        passed = []
        for k, (px, py) in enumerate(peers):
            for i in range(n):
                ici(i, k, 2 * px + py).wait_recv()
                passed.append(fwd(i, k, 2 * px + py, c))
                passed[-1].start()
        for k, (px, py) in enumerate(peers):
            for i in range(n):
                fwd(i, k, 2 * px + py, 1 - c).wait_recv()
        for cp in sends + passed:
            cp.wait_send()

    out_shapes = [jax.ShapeDtypeStruct((4,) + s.shape, s.dtype) for s in srcs]
    return _multi_exchange("weights_allgather", body, srcs, out_shapes, (3 * n, 3 * n, 3 * n, 3 * n))


def grad_layer_exchange(gs):
    n = len(gs)

    def body(*refs):
        src, out = refs[:n], refs[n:2 * n]
        send_sems, recv_sems = refs[2 * n:]
        x, y, c = _coords()
        copies = [pltpu.make_async_remote_copy(src[i].at[1 - c], out[i], send_sems.at[i], recv_sems.at[i],
                                               device_id=(x, y, 1 - c), device_id_type=MESH) for i in range(n)]
        for cp in copies:
            cp.start()
        for cp in copies:
            cp.wait_recv()
        for cp in copies:
            cp.wait_send()

    return _multi_exchange("grad_layer_exchange", body, gs, [jax.ShapeDtypeStruct(g.shape[1:], g.dtype) for g in gs], (n, n))


def grad_chip_exchange(ps):
    n = len(ps)

    def body(*refs):
        src, out = refs[:n], refs[n:2 * n]
        send_sems, recv_sems = refs[2 * n:]
        x, y, c = _coords()
        copies = [pltpu.make_async_remote_copy(src[i].at[2 * px + py], out[i].at[k], send_sems.at[3 * i + k], recv_sems.at[3 * i + k],
                                               device_id=(px, py, c), device_id_type=MESH)
                  for k, (px, py) in enumerate(_chip_peers(x, y)) for i in range(n)]
        for cp in copies:
            cp.start()
        for cp in copies:
            cp.wait_recv()
        for cp in copies:
            cp.wait_send()

    return _multi_exchange("grad_chip_exchange", body, ps, [jax.ShapeDtypeStruct((3,) + p.shape[1:], p.dtype) for p in ps], (3 * n, 3 * n))


def grad_reduced_exchange(rs):
    n = len(rs)

    def body(*refs):
        src, out = refs[:n], refs[n:2 * n]
        send_sems, recv_sems = refs[2 * n:]
        x, y, c = _coords()
        copies = [pltpu.make_async_remote_copy(src[i], out[i], send_sems.at[i], recv_sems.at[i],
                                               device_id=(x, y, 1 - c), device_id_type=MESH) for i in range(n)]
        for cp in copies:
            cp.start()
        for cp in copies:
            cp.wait_recv()
        for cp in copies:
            cp.wait_send()

    return _multi_exchange("grad_reduced_exchange", body, rs, [jax.ShapeDtypeStruct(r.shape, r.dtype) for r in rs], (n, n))


def device_allgather(name, src):
    def body(src_ref, out_ref, send_sems, recv_sems, local_sem):
        x, y, c = _coords()
        me = 4 * x + 2 * y + c
        peers = [(x ^ (m >> 2), y ^ ((m >> 1) & 1), c ^ (m & 1)) for m in range(1, 8)]

        def copy(k, dev):
            return pltpu.make_async_remote_copy(src_ref, out_ref.at[dev], send_sems.at[k], recv_sems.at[k],
                                                device_id=peers[k], device_id_type=MESH)

        local = pltpu.make_async_copy(src_ref, out_ref.at[me], local_sem)
        local.start()
        sends = [copy(k, me) for k in range(7)]
        for s in sends:
            s.start()
        for k, (px, py, pc) in enumerate(peers):
            copy(k, 4 * px + 2 * py + pc).wait_recv()
        for s in sends:
            s.wait_send()
        local.wait()

    return _exchange_call(name, body, src, jax.ShapeDtypeStruct((8,) + src.shape, src.dtype), 7)


BLOCK_ELEMS = 1 << 19


def add_pair(name, g, recv, c):
    _, n, R, C = g.shape
    tr = _fit_rows(R, max(16, BLOCK_ELEMS // C))

    def body(c_ref, g_ref, r_ref, o_ref):
        o_ref[...] = (g_ref[...].astype(F32) + r_ref[...].astype(F32)).astype(o_ref.dtype)

    blk = pl.BlockSpec((None, tr, C), lambda j, i, c_ref: (j, i, 0))
    return pl.pallas_call(
        body, name=name, out_shape=jax.ShapeDtypeStruct((n, R, C), g.dtype),
        grid_spec=pltpu.PrefetchScalarGridSpec(
            num_scalar_prefetch=1, grid=(n, R // tr),
            in_specs=[pl.BlockSpec((None, None, tr, C), lambda j, i, c_ref: (c_ref[0], j, i, 0)), blk], out_specs=blk),
        compiler_params=_cp("parallel", "parallel"),
    )(c, g, recv)


def sum_chips(name, own, recv, me):
    _, R, C = own.shape
    tr = _fit_rows(R, max(16, BLOCK_ELEMS // (2 * C)))

    def body(me_ref, o_ref, r_ref, out_ref):
        acc = o_ref[...].astype(F32)
        for k in range(3):
            acc = acc + r_ref[k].astype(F32)
        out_ref[...] = acc

    return pl.pallas_call(
        body, name=name, out_shape=jax.ShapeDtypeStruct((R, C), F32),
        grid_spec=pltpu.PrefetchScalarGridSpec(
            num_scalar_prefetch=1, grid=(R // tr,),
            in_specs=[pl.BlockSpec((None, tr, C), lambda i, me_ref: (me_ref[0], i, 0)), pl.BlockSpec((3, tr, C), lambda i, me_ref: (0, i, 0))],
            out_specs=pl.BlockSpec((tr, C), lambda i, me_ref: (i, 0))),
        compiler_params=_cp("parallel"),
    )(me, own, recv)


def sum_devices(parts):
    _, R, C = parts.shape

    def body(p_ref, o_ref):
        acc = p_ref[0]
        for k in range(1, 8):
            acc = acc + p_ref[k]
        o_ref[...] = acc

    return pl.pallas_call(body, name="sum_devices", out_shape=jax.ShapeDtypeStruct((R, C), F32))(parts)


def _fit_rows(R, pref):
    for t in range(min(pref, R), 0, -1):
        if R % t == 0 and (t % 16 == 0 or t == R):
            return t
    raise ValueError((R, pref))


def adamw(name, w, g, m, v):
    shape = w.shape
    C = shape[-1]
    w2, g2, m2, v2 = (a.reshape(-1, C) for a in (w, g, m, v))
    R = w2.shape[0]
    tr = _fit_rows(R, max(8, (1 << 18) // C)) if R * C > (1 << 18) else R
    c1 = 1.0 - ADAM_B1 ** ADAM_STEP
    c2 = 1.0 - ADAM_B2 ** ADAM_STEP

    def body(w_ref, g_ref, m_ref, v_ref, d_ref, mo_ref, vo_ref):
        gg = g_ref[...]
        mn = ADAM_B1 * m_ref[...] + (1.0 - ADAM_B1) * gg
        vn = ADAM_B2 * v_ref[...] + (1.0 - ADAM_B2) * jnp.square(gg)
        d_ref[...] = -ADAM_LR * ((mn / c1) / (jnp.sqrt(vn / c2) + ADAM_EPS) + ADAM_WD * w_ref[...])
        mo_ref[...] = mn
        vo_ref[...] = vn

    blk = pl.BlockSpec((tr, C), lambda i: (i, 0))
    outs = pl.pallas_call(
        body, name=name, grid=(R // tr,), in_specs=[blk] * 4, out_specs=[blk] * 3,
        out_shape=[jax.ShapeDtypeStruct((R, C), F32)] * 3, compiler_params=_cp("parallel"),
    )(w2, g2, m2, v2)
    return tuple(o.reshape(shape) for o in outs)


def _unshard(name, parts):
    n, L, r, c = parts.shape
    if name in COL_SHARDED:
        return parts.transpose(1, 2, 0, 3).reshape(L, r, n * c)
    return parts.transpose(1, 0, 2, 3).reshape(L, n * r, c)


def _shard(name, full):
    L, R, C = full.shape
    if name in COL_SHARDED:
        return full.reshape(L, R, 4, C // 4).transpose(0, 2, 1, 3)
    return full.reshape(L, 4, R // 4, C)


def _to_local(name, w):
    if name == "w_in":
        return _w_in_to_local(w)
    if name == "mla_w_uq":
        return _w_uq_to_local(w)
    return w


def _from_local(name, g):
    if name == "w_in":
        return _w_in_from_local(g)
    if name == "mla_w_uq":
        return _w_uq_from_local(g)
    return g


TWIN_WEIGHTS = ("ffn1_norm", "ffn1_wi", "ffn1_wo", "mix_norm", "w_in", "mla_q_norm", "mla_w_uq", "mla_kv_norm", "mla_w_ukv",
                "hgrn_lb_logits", "hgrn_norm", "ssm_conv_w", "ssm_conv_b", "ssm_a_log", "ssm_dt_bias", "ssm_d", "ssm_norm",
                "w_o_mla", "w_o_hgrn", "w_o_ssm", "w_out", "ffn2_norm", "ffn2_wi", "ffn2_wo", "final_norm")
SMALL_PACK = SMALL + ("ssm_conv_w", "final_norm")


def _pad_rows(flat, cols):
    n = flat.shape[0]
    rows = -(-n // cols)
    rows = -(-rows // 8) * 8
    return jnp.concatenate([flat, jnp.zeros((rows * cols - n,), flat.dtype)]).reshape(rows, cols)


def kernel(x, ffn1_norm, ffn1_wi, ffn1_wo, mix_norm, w_in, mla_q_norm, mla_w_uq, mla_kv_norm, mla_w_ukv, hgrn_lb_logits, hgrn_norm, ssm_conv_w, ssm_conv_b, ssm_a_log, ssm_dt_bias, ssm_d, ssm_norm, w_o_mla, w_o_hgrn, w_o_ssm, w_out, ffn2_norm, ffn2_wi, ffn2_wo, final_norm, loss_target, m_ffn1_norm, m_ffn1_wi, m_ffn1_wo, m_mix_norm, m_w_in, m_mla_q_norm, m_mla_w_uq, m_mla_kv_norm, m_mla_w_ukv, m_hgrn_lb_logits, m_hgrn_norm, m_ssm_conv_w, m_ssm_conv_b, m_ssm_a_log, m_ssm_dt_bias, m_ssm_d, m_ssm_norm, m_w_o_mla, m_w_o_hgrn, m_w_o_ssm, m_w_out, m_ffn2_norm, m_ffn2_wi, m_ffn2_wo, m_final_norm, v_ffn1_norm, v_ffn1_wi, v_ffn1_wo, v_mix_norm, v_w_in, v_mla_q_norm, v_mla_w_uq, v_mla_kv_norm, v_mla_w_ukv, v_hgrn_lb_logits, v_hgrn_norm, v_ssm_conv_w, v_ssm_conv_b, v_ssm_a_log, v_ssm_dt_bias, v_ssm_d, v_ssm_norm, v_w_o_mla, v_w_o_hgrn, v_w_o_ssm, v_w_out, v_ffn2_norm, v_ffn2_wi, v_ffn2_wo, v_final_norm):
    args = dict(locals())
    w = {n: args[n] for n in TWIN_WEIGHTS}
    m = {n: args["m_" + n] for n in TWIN_WEIGHTS}
    v = {n: args["v_" + n] for n in TWIN_WEIGHTS}
    xi, yi, ci = _coords()
    chip = 2 * xi + yi

    shards = [w[n].astype(BF16) for n in BIG]
    gathered = [lax.dynamic_update_slice(parts, own[None], (chip, 0, 0, 0)) for parts, own in zip(weights_allgather(shards), shards)]
    full = {n: _to_local(n, _unshard(n, parts)) for n, parts in zip(BIG, gathered)}
    conv_parts = device_allgather("conv_allgather", _pad_rows(w["ssm_conv_w"].reshape(-1), 128))
    conv_full = jnp.concatenate(
        [conv_parts[2 * j].reshape(-1)[:w["ssm_conv_w"].size].reshape(w["ssm_conv_w"].shape) for j in range(4)], axis=-1)
    Wl = [dict({n: full[n][l] for n in BIG}, ssm_conv_w=conv_full[l]) for l in range(DEPTH)]
    small = {n: w[n] for n in SMALL}

    loss, grad_x, grads, dlogits, dfinal = local_step(x[0], loss_target[0], Wl, small, w["final_norm"])
    loss = lax.psum(loss[0, 0], ("x", "y", "c"))

    c_idx, chip_idx = ci.astype(jnp.int32).reshape(1), chip.astype(jnp.int32).reshape(1)
    gs = [_shard(n, _from_local(n, jnp.stack([grads[l][n] for l in range(DEPTH)]))).astype(BF16) for n in BIG]
    from_sibling = grad_layer_exchange(gs)
    pair_sums = [add_pair("add_pair_" + n, a, b, c_idx) for n, a, b in zip(BIG, gs, from_sibling)]
    from_chips = grad_chip_exchange(pair_sums)
    reduced = [sum_chips("sum_chips_" + n, a, b, chip_idx) for n, a, b in zip(BIG, pair_sums, from_chips)]
    others = grad_reduced_exchange(reduced)
    g = {n: jnp.where(ci == 0, jnp.stack([mine, other]), jnp.stack([other, mine])) for n, mine, other in zip(BIG, reduced, others)}

    sg = {n: jnp.concatenate([grads[l][n] for l in range(DEPTH)], axis=0) for n in SMALL if n != "hgrn_lb_logits"}
    sg["hgrn_lb_logits"] = dlogits
    sg["ssm_conv_w"] = jnp.stack([grads[l]["ssm_conv_w"] for l in range(DEPTH)])
    sg["final_norm"] = dfinal
    spack = _pad_rows(jnp.concatenate([sg[n].reshape(-1) for n in SMALL_PACK]), 128)
    ssum = sum_devices(device_allgather("small_grads_allgather", spack)).reshape(-1)
    off = 0
    for n in SMALL_PACK:
        size = sg[n].size
        g[n] = ssum[off:off + size].reshape(sg[n].shape)
        off += size
    shard_cols = w["ssm_conv_w"].shape[-1]
    g["ssm_conv_w"] = lax.dynamic_slice_in_dim(g["ssm_conv_w"], chip * shard_cols, shard_cols, axis=2)
    g = {n: g[n].reshape(w[n].shape) for n in TWIN_WEIGHTS}

    upd = {n: adamw("adamw_" + n, w[n], g[n], m[n], v[n]) for n in TWIN_WEIGHTS}
    return (loss, grad_x[None], *[g[n] for n in TWIN_WEIGHTS], *[upd[n][0] for n in TWIN_WEIGHTS],
            *[upd[n][1] for n in TWIN_WEIGHTS], *[upd[n][2] for n in TWIN_WEIGHTS])
```

```python
import functools
import math
from typing import Callable, NamedTuple

import jax
import jax.numpy as jnp
import numpy as np
from jax import lax
from jax.experimental import pallas as pl
from jax.experimental.pallas import tpu as pltpu

F32 = jnp.float32
BF16 = jnp.bfloat16
MESH = pl.DeviceIdType.MESH

D_MODEL = 2048
DEPTH = 2
CHUNK = 64
EPS = 1e-6
MLA_HEADS, MLA_Q_RANK, MLA_KV_RANK, MLA_NOPE, MLA_ROPE, MLA_V = 16, 512, 512, 128, 64, 128
ROPE_THETA = 10000.0
HG_HEADS, HG_DK = 16, 128
HG_WIDTH = HG_HEADS * HG_DK
SSM_INNER, SSM_HEADDIM, SSM_HEADS, SSM_GROUPS, SSM_STATE, SSM_CONV = 4096, 64, 64, 8, 128, 4
SSM_CONV_DIM = SSM_INNER + 2 * SSM_GROUPS * SSM_STATE
D_FF = 5632
IN_DIM = 25728
ADAM_LR, ADAM_B1, ADAM_B2, ADAM_EPS, ADAM_WD, ADAM_STEP = 0.001, 0.9, 0.999, 1e-08, 0.01, 10

P_QLAT, P_KVLAT, P_HQ, P_HF, P_HI, P_HGATE, P_Z, P_XBC, P_GATES, P_KPEDT = (
    0, 512, 1024, 3072, 5120, 7168, 9216, 13312, 19456, 25600)
IN_PAD = 26624

VMEM_LIMIT_V7X = 48 << 20
SSD_Q = 256
HG_HB = 8
assert all(off % (HG_HB * HG_DK) == 0 for off in (P_HQ, P_HF, P_HI)) and HG_HEADS % HG_HB == 0
NEG = -1e30


def _cp(*sem):
    return pltpu.CompilerParams(dimension_semantics=sem, vmem_limit_bytes=VMEM_LIMIT_V7X)


def _fit(n, pref):
    if n <= pref:
        return n
    for t in range(pref, 0, -128):
        if n % t == 0:
            return t
    raise ValueError((n, pref))


_DIMS = {"nn": (((1,), (0,)), ((), ())), "nt": (((1,), (1,)), ((), ())), "tn": (((0,), (0,)), ((), ()))}


def _dot(a, b, mode):
    return lax.dot_general(a.astype(BF16), b.astype(BF16), _DIMS[mode], preferred_element_type=F32)


def matmul(name, a, b, mode="nn", out_dtype=F32, alpha=1.0, res=None, tm=1024, tn=1024, tk=2048, side=None):
    if mode == "nn":
        (M, K), (K2, N) = a.shape, b.shape
    elif mode == "nt":
        (M, K), (N, K2) = a.shape, b.shape
    else:
        (K, M), (K2, N) = a.shape, b.shape
    assert K == K2, (name, a.shape, b.shape, mode)
    tm, tn, tk = _fit(M, tm), _fit(N, tn), _fit(K, tk)
    nk = K // tk
    a_spec = pl.BlockSpec((tk, tm), lambda i, j, k: (k, i)) if mode == "tn" else pl.BlockSpec((tm, tk), lambda i, j, k: (i, k))
    b_spec = pl.BlockSpec((tn, tk), lambda i, j, k: (j, k)) if mode == "nt" else pl.BlockSpec((tk, tn), lambda i, j, k: (k, j))
    o_spec = pl.BlockSpec((tm, tn), lambda i, j, k: (i, j))
    has_res = res is not None

    def body(*refs):
        a_ref, b_ref = refs[0], refs[1]
        o_ref = refs[3] if has_res else refs[2]

        def finish(v):
            if alpha != 1.0:
                v = v * alpha
            if has_res:
                v = v + refs[2][...].astype(F32)
            o_ref[...] = v.astype(o_ref.dtype)

        if nk == 1:
            finish(_dot(a_ref[...], b_ref[...], mode))
            return
        acc = refs[-1]
        k = pl.program_id(2)

        @pl.when(k == 0)
        def _():
            acc[...] = _dot(a_ref[...], b_ref[...], mode)

        @pl.when(jnp.logical_and(k > 0, k < nk - 1))
        def _():
            acc[...] += _dot(a_ref[...], b_ref[...], mode)

        @pl.when(k == nk - 1)
        def _():
            finish(acc[...] + _dot(a_ref[...], b_ref[...], mode))

    ins = [a, b] + ([res] if has_res else [])
    return _pcall(name, body, (M // tm, N // tn, nk), [a_spec, b_spec] + ([o_spec] if has_res else []), o_spec,
                  jax.ShapeDtypeStruct((M, N), out_dtype), [pltpu.VMEM((tm, tn), F32)] if nk > 1 else [],
                  ("parallel", "parallel", "arbitrary"), ins, side)


def _row_specs(rows, consts, tm):
    specs = []
    for arr, w, off in rows:
        specs.append(pl.BlockSpec((tm, w), functools.partial(lambda j, i, off: (i, off + j), off=off)))
    for arr, w, off in consts:
        specs.append(pl.BlockSpec((arr.shape[0], w), functools.partial(lambda j, i, off: (0, off + j), off=off)))
    return specs


def rowwise(name, fn, rows, consts, outs, n_groups=1, tm=512):
    S = rows[0][0].shape[0]
    tm = _fit(S, tm)
    n_in = len(rows) + len(consts)

    def body(*refs):
        vals = fn(*[r[...].astype(F32) for r in refs[:n_in]])
        for o_ref, v in zip(refs[n_in:], vals):
            o_ref[...] = v.astype(o_ref.dtype)

    return pl.pallas_call(
        body, name=name, grid=(n_groups, S // tm),
        in_specs=_row_specs(rows, consts, tm),
        out_specs=[pl.BlockSpec((tm, w), lambda j, i: (i, j)) for _, w, _ in outs],
        out_shape=[jax.ShapeDtypeStruct((S, W), dt) for W, _, dt in outs],
        compiler_params=_cp("parallel", "parallel"),
    )(*[r[0] for r in rows], *[c[0] for c in consts])


def rowwise_bwd(name, fn, rows, consts, cts, row_grads, const_grads, adds=None, n_groups=1, tm=512):
    S = rows[0][0].shape[0]
    tm = _fit(S, tm)
    adds = adds or {}
    add_idx = list(adds)
    n_r, n_c, n_ct, n_add = len(rows), len(consts), len(cts), len(adds)

    def body(*refs):
        ins = [r[...].astype(F32) for r in refs[:n_r + n_c]]
        ct = tuple(r[...].astype(F32) for r in refs[n_r + n_c:n_r + n_c + n_ct])
        add_refs = refs[n_r + n_c + n_ct:n_r + n_c + n_ct + n_add]
        out_refs = refs[n_r + n_c + n_ct + n_add:]
        _, vjp = jax.vjp(fn, *ins)
        g = list(vjp(ct))
        for a_ref, idx in zip(add_refs, add_idx):
            g[idx] = g[idx] + a_ref[...].astype(F32)
        for q, (idx, _, _) in enumerate(row_grads):
            out_refs[q][...] = g[idx].astype(out_refs[q].dtype)
        first = pl.program_id(1) == 0
        for q, idx in enumerate(const_grads):
            o_ref = out_refs[len(row_grads) + q]

            @pl.when(first)
            def _(o_ref=o_ref):
                o_ref[...] = jnp.zeros_like(o_ref)

            o_ref[...] += g[n_r + idx]

    in_specs = _row_specs(rows, consts, tm)
    in_specs += [pl.BlockSpec((tm, w), lambda j, i: (i, j)) for _, w in cts]
    in_specs += [pl.BlockSpec((tm, rows[idx][1]), lambda j, i: (i, j)) for idx in add_idx]
    out_specs = [pl.BlockSpec((tm, rows[idx][1]), lambda j, i: (i, j)) for idx, _, _ in row_grads]
    out_specs += [pl.BlockSpec((consts[idx][0].shape[0], consts[idx][1]), lambda j, i: (0, j)) for idx in const_grads]
    out_shape = [jax.ShapeDtypeStruct((S, W), dt) for _, W, dt in row_grads]
    out_shape += [jax.ShapeDtypeStruct((consts[idx][0].shape[0], consts[idx][1] * n_groups), F32) for idx in const_grads]
    return pl.pallas_call(
        body, name=name, grid=(n_groups, S // tm), in_specs=in_specs, out_specs=out_specs, out_shape=out_shape,
        compiler_params=_cp("parallel", "arbitrary"),
    )(*[r[0] for r in rows], *[c[0] for c in consts], *[c[0] for c in cts], *adds.values())


def f_rmsnorm(x, w):
    return (x * lax.rsqrt(jnp.mean(x * x, axis=-1, keepdims=True) + EPS) * w,)


def f_swiglu(g, u):
    return (jax.nn.silu(g) * u,)


def f_hgrn_out(o, g, w):
    return (o * lax.rsqrt(jnp.mean(o * o, axis=-1, keepdims=True) + EPS) * w * jax.nn.silu(g),)


def f_ssm_out(y, z, w):
    y = y * jax.nn.silu(z)
    return (y * lax.rsqrt(jnp.mean(y * y, axis=-1, keepdims=True) + EPS) * w,)


def f_merge(ga, gb, gc, ya, yb, yc):
    return (jax.nn.sigmoid(ga) * ya + jax.nn.sigmoid(gb) * yb + jax.nn.sigmoid(gc) * yc,)


def f_lower_bounds(logits):
    p = jax.nn.softmax(logits, axis=0)
    rows = [jnp.zeros_like(p[0:1])]
    for l in range(1, DEPTH):
        rows.append(rows[-1] + p[l:l + 1])
    return (jnp.concatenate(rows, axis=0),)


def loss_head(x, target, w, tm=512):
    S, D = x.shape
    tm = _fit(S, tm)

    def loss_fn(xb, wb, tb):
        (y,) = f_rmsnorm(xb, wb)
        return 0.5 * jnp.sum(jnp.mean(jnp.square(y - tb), axis=-1))

    def body(x_ref, t_ref, w_ref, dx_ref, dxb_ref, dw_ref, loss_ref):
        @pl.when(pl.program_id(0) == 0)
        def _():
            dw_ref[...] = jnp.zeros_like(dw_ref)
            loss_ref[...] = jnp.zeros_like(loss_ref)

        l, (dx, dw) = jax.value_and_grad(loss_fn, argnums=(0, 1))(x_ref[...], w_ref[...], t_ref[...])
        dx_ref[...] = dx
        dxb_ref[...] = dx.astype(BF16)
        dw_ref[...] += dw
        loss_ref[...] += jnp.full(loss_ref.shape, l, F32)

    row = pl.BlockSpec((tm, D), lambda i: (i, 0))
    vec = pl.BlockSpec((1, D), lambda i: (0, 0))
    return pl.pallas_call(
        body, name="loss_head", grid=(S // tm,), in_specs=[row, row, vec],
        out_specs=[row, row, vec, pl.BlockSpec((1, 128), lambda i: (0, 0))],
        out_shape=[jax.ShapeDtypeStruct((S, D), F32), jax.ShapeDtypeStruct((S, D), BF16), jax.ShapeDtypeStruct((1, D), F32),
                   jax.ShapeDtypeStruct((1, 128), F32)],
        compiler_params=_cp("arbitrary"),
    )(x, target, w)


def rope_tables(S):
    inv = 1.0 / (ROPE_THETA ** (jnp.arange(0, MLA_ROPE, 2, dtype=F32) / MLA_ROPE))
    ang = jnp.arange(S, dtype=F32)[:, None] * inv[None, :]
    c, s = jnp.cos(ang), jnp.sin(ang)
    return jnp.tile(c, (1, 4)), jnp.concatenate([-s, s, -s, s], axis=1)


def _rope128(x, cosf, sinf):
    lane = lax.broadcasted_iota(jnp.int32, x.shape, 1)
    swapped = jnp.where((lane & 32) == 0, pltpu.roll(x, 96, 1), pltpu.roll(x, 32, 1))
    return x * cosf + swapped * sinf


def mla_q_prep(q, cosf, sinf, tm=512):
    S = q.shape[0]
    tm = _fit(S, tm)

    def body(q_ref, c_ref, s_ref, o_ref):
        x = q_ref[...]
        r = _rope128(x[:, 256:384], c_ref[...], s_ref[...])
        lane = lax.broadcasted_iota(jnp.int32, r.shape, 1)
        z = jnp.zeros_like(r)
        o_ref[...] = jnp.concatenate(
            [x[:, 0:128], jnp.where(lane < 64, r, z), x[:, 128:256], jnp.where(lane >= 64, r, z)], axis=1).astype(BF16)

    tab = pl.BlockSpec((tm, 128), lambda j, i: (i, 0))
    return pl.pallas_call(
        body, name="mla_q_prep", grid=(8, S // tm),
        in_specs=[pl.BlockSpec((tm, 384), lambda j, i: (i, j)), tab, tab],
        out_specs=pl.BlockSpec((tm, 512), lambda j, i: (i, j)),
        out_shape=jax.ShapeDtypeStruct((S, 4096), BF16), compiler_params=_cp("parallel", "parallel"),
    )(q, cosf, sinf)


def mla_q_prep_bwd(dqc, cosf, sinf, tm=512):
    S = dqc.shape[0]
    tm = _fit(S, tm)

    def body(d_ref, c_ref, s_ref, o_ref):
        d = d_ref[...]
        lane = lax.broadcasted_iota(jnp.int32, (tm, 128), 1)
        dr = jnp.where(lane < 64, d[:, 128:256], d[:, 384:512])
        o_ref[...] = jnp.concatenate([d[:, 0:128], d[:, 256:384], _rope128(dr, c_ref[...], -s_ref[...])], axis=1).astype(BF16)

    tab = pl.BlockSpec((tm, 128), lambda j, i: (i, 0))
    return pl.pallas_call(
        body, name="mla_q_prep_bwd", grid=(8, S // tm),
        in_specs=[pl.BlockSpec((tm, 512), lambda j, i: (i, j)), tab, tab],
        out_specs=pl.BlockSpec((tm, 384), lambda j, i: (i, j)),
        out_shape=jax.ShapeDtypeStruct((S, 3072), BF16), compiler_params=_cp("parallel", "parallel"),
    )(dqc, cosf, sinf)


def mla_k_prep(kv, proj, cosf, sinf, tm=512):
    S = kv.shape[0]
    tm = _fit(S, tm)

    def body(kv_ref, pe_ref, c_ref, s_ref, k_ref, v_ref):
        x = kv_ref[...]
        r = _rope128(pe_ref[...], c_ref[...], s_ref[...])
        lane = lax.broadcasted_iota(jnp.int32, r.shape, 1)
        r2 = jnp.where(lane < 64, r, pltpu.roll(r, 64, 1))
        k_ref[...] = jnp.concatenate([x[:, 0:128], r2, x[:, 256:384], r2], axis=1).astype(BF16)
        v_ref[...] = jnp.concatenate([x[:, 128:256], x[:, 384:512]], axis=1).astype(BF16)

    tab = pl.BlockSpec((tm, 128), lambda j, i: (i, 0))
    return pl.pallas_call(
        body, name="mla_k_prep", grid=(8, S // tm),
        in_specs=[pl.BlockSpec((tm, 512), lambda j, i: (i, j)), pl.BlockSpec((tm, 128), lambda j, i: (i, P_KPEDT // 128)), tab, tab],
        out_specs=[pl.BlockSpec((tm, 512), lambda j, i: (i, j)), pl.BlockSpec((tm, 256), lambda j, i: (i, j))],
        out_shape=[jax.ShapeDtypeStruct((S, 4096), BF16), jax.ShapeDtypeStruct((S, 2048), BF16)],
        compiler_params=_cp("parallel", "parallel"),
    )(kv, proj, cosf, sinf)


def mla_k_prep_bwd(dkc, dv, cosf, sinf, tm=512):
    S = dkc.shape[0]
    tm = _fit(S, tm)

    def body(dk_ref, dv_ref, c_ref, s_ref, dkv_ref, dpe_ref):
        dk, dvv = dk_ref[...], dv_ref[...]
        dkv_ref[...] = jnp.concatenate([dk[:, 0:128], dvv[:, 0:128], dk[:, 256:384], dvv[:, 128:256]], axis=1).astype(BF16)
        d2 = dk[:, 128:256] + dk[:, 384:512]
        lane = lax.broadcasted_iota(jnp.int32, d2.shape, 1)
        dr = jnp.where(lane < 64, d2 + pltpu.roll(d2, 64, 1), 0.0)
        dpe = jnp.where(lane < 64, _rope128(dr, c_ref[...], -s_ref[...]), 0.0)

        @pl.when(pl.program_id(1) == 0)
        def _():
            dpe_ref[...] = jnp.zeros_like(dpe_ref)

        dpe_ref[...] += dpe

    tab = pl.BlockSpec((tm, 128), lambda i, j: (i, 0))
    return pl.pallas_call(
        body, name="mla_k_prep_bwd", grid=(S // tm, 8),
        in_specs=[pl.BlockSpec((tm, 512), lambda i, j: (i, j)), pl.BlockSpec((tm, 256), lambda i, j: (i, j)), tab, tab],
        out_specs=[pl.BlockSpec((tm, 512), lambda i, j: (i, j)), tab],
        out_shape=[jax.ShapeDtypeStruct((S, 4096), BF16), jax.ShapeDtypeStruct((S, 128), F32)],
        compiler_params=_cp("parallel", "arbitrary"),
    )(dkc, dv, cosf, sinf)


ATT_SCALE = (MLA_NOPE + MLA_ROPE) ** -0.5


def _att_scores(q, k, qi, ki, t):
    s = _dot(q, k, "nt") * ATT_SCALE
    rows = qi * t + lax.broadcasted_iota(jnp.int32, (t, t), 0)
    cols = ki * t + lax.broadcasted_iota(jnp.int32, (t, t), 1)
    shift = CHUNK.bit_length() - 1
    return jnp.where((cols >> shift) <= (rows >> shift), s, NEG)


def attention_fwd(qc, kc, vb, t=2048, side=None):
    S = qc.shape[0]
    t = _fit(S, t)
    n = S // t

    def body(q_ref, k_ref, v_ref, o_ref, lse_ref, m_s, l_s, acc_s):
        qi, ki = pl.program_id(1), pl.program_id(2)

        @pl.when(ki == 0)
        def _():
            m_s[...] = jnp.full_like(m_s, NEG)
            l_s[...] = jnp.zeros_like(l_s)
            acc_s[...] = jnp.zeros_like(acc_s)

        @pl.when(ki <= qi)
        def _():
            s = _att_scores(q_ref[...], k_ref[...], qi, ki, t)
            m_prev = m_s[...]
            m_new = jnp.maximum(m_prev, jnp.max(s, axis=1, keepdims=True))
            alpha = jnp.exp(m_prev - m_new)
            p = jnp.exp(s - m_new)
            l_s[...] = alpha * l_s[...] + jnp.sum(p, axis=1, keepdims=True)
            acc_s[...] = alpha * acc_s[...] + _dot(p, v_ref[...], "nn")
            m_s[...] = m_new

        @pl.when(ki == qi)
        def _():
            o_ref[...] = (acc_s[...] / l_s[...]).astype(o_ref.dtype)
            lse_ref[...] = m_s[...] + jnp.log(l_s[...])

    return _pcall(
        "attention_fwd", body, (MLA_HEADS, n, n),
        [pl.BlockSpec((t, 256), lambda h, i, j: (i, h)),
         pl.BlockSpec((t, 256), lambda h, i, j: (jnp.minimum(i, j), h)),
         pl.BlockSpec((t, 128), lambda h, i, j: (jnp.minimum(i, j), h))],
        [pl.BlockSpec((t, 128), lambda h, i, j: (i, h)), pl.BlockSpec((None, t, 1), lambda h, i, j: (h, i, 0))],
        [jax.ShapeDtypeStruct((S, 2048), BF16), jax.ShapeDtypeStruct((MLA_HEADS, S, 1), F32)],
        [pltpu.VMEM((t, 1), F32), pltpu.VMEM((t, 1), F32), pltpu.VMEM((t, 128), F32)],
        ("parallel", "parallel", "arbitrary"), (qc, kc, vb), side)


def attention_delta(o, do, t=1024):
    S = o.shape[0]
    t = _fit(S, t)

    def body(o_ref, do_ref, d_ref):
        d_ref[...] = jnp.sum(o_ref[...].astype(F32) * do_ref[...].astype(F32), axis=1, keepdims=True)

    blk = pl.BlockSpec((t, 128), lambda h, i: (i, h))
    return pl.pallas_call(
        body, name="attention_delta", grid=(MLA_HEADS, S // t), in_specs=[blk, blk],
        out_specs=pl.BlockSpec((None, t, 1), lambda h, i: (h, i, 0)),
        out_shape=jax.ShapeDtypeStruct((MLA_HEADS, S, 1), F32), compiler_params=_cp("parallel", "parallel"),
    )(o, do)


def attention_bwd_dq(qc, kc, vb, do, lse, delta, t=1024, side=None):
    S = qc.shape[0]
    t = _fit(S, t)
    n = S // t

    def body(q_ref, k_ref, v_ref, do_ref, lse_ref, dl_ref, dq_ref, acc):
        qi, ki = pl.program_id(1), pl.program_id(2)

        @pl.when(ki == 0)
        def _():
            acc[...] = jnp.zeros_like(acc)

        @pl.when(ki <= qi)
        def _():
            p = jnp.exp(_att_scores(q_ref[...], k_ref[...], qi, ki, t) - lse_ref[...])
            dp = _dot(do_ref[...], v_ref[...], "nt")
            ds = p * (dp - dl_ref[...])
            acc[...] += _dot(ds, k_ref[...], "nn")

        @pl.when(ki == qi)
        def _():
            dq_ref[...] = acc[...] * ATT_SCALE

    stat = pl.BlockSpec((None, t, 1), lambda h, i, j: (h, i, 0))
    return _pcall(
        "attention_bwd_dq", body, (MLA_HEADS, n, n),
        [pl.BlockSpec((t, 256), lambda h, i, j: (i, h)),
         pl.BlockSpec((t, 256), lambda h, i, j: (jnp.minimum(i, j), h)),
         pl.BlockSpec((t, 128), lambda h, i, j: (jnp.minimum(i, j), h)),
         pl.BlockSpec((t, 128), lambda h, i, j: (i, h)), stat, stat],
        pl.BlockSpec((t, 256), lambda h, i, j: (i, h)), jax.ShapeDtypeStruct((S, 4096), F32),
        [pltpu.VMEM((t, 256), F32)], ("parallel", "parallel", "arbitrary"), (qc, kc, vb, do, lse, delta), side)


def attention_bwd_dkv(qc, kc, vb, do, lse, delta, t=2048):
    S = qc.shape[0]
    t = _fit(S, t)
    n = S // t

    def body(q_ref, k_ref, v_ref, do_ref, lse_ref, dl_ref, dk_ref, dv_ref, dk_acc, dv_acc):
        ki, qi = pl.program_id(1), pl.program_id(2)

        @pl.when(qi == 0)
        def _():
            dk_acc[...] = jnp.zeros_like(dk_acc)
            dv_acc[...] = jnp.zeros_like(dv_acc)

        @pl.when(qi >= ki)
        def _():
            s = _dot(k_ref[...], q_ref[...], "nt") * ATT_SCALE
            krow = ki * t + lax.broadcasted_iota(jnp.int32, (t, t), 0)
            qcol = qi * t + lax.broadcasted_iota(jnp.int32, (t, t), 1)
            shift = CHUNK.bit_length() - 1
            p = jnp.exp(jnp.where((krow >> shift) <= (qcol >> shift), s, NEG) - lse_ref[...])
            dv_acc[...] += _dot(p, do_ref[...], "nn")
            dp = _dot(v_ref[...], do_ref[...], "nt")
            ds = p * (dp - dl_ref[...])
            dk_acc[...] += _dot(ds, q_ref[...], "nn")

        @pl.when(qi == n - 1)
        def _():
            dk_ref[...] = dk_acc[...] * ATT_SCALE
            dv_ref[...] = dv_acc[...]

    stat = pl.BlockSpec((None, 1, t), lambda h, j, i: (h, 0, jnp.maximum(i, j)))
    return pl.pallas_call(
        body, name="attention_bwd_dkv", grid=(MLA_HEADS, n, n),
        in_specs=[pl.BlockSpec((t, 256), lambda h, j, i: (jnp.maximum(i, j), h)),
                  pl.BlockSpec((t, 256), lambda h, j, i: (j, h)),
                  pl.BlockSpec((t, 128), lambda h, j, i: (j, h)),
                  pl.BlockSpec((t, 128), lambda h, j, i: (jnp.maximum(i, j), h)), stat, stat],
        out_specs=[pl.BlockSpec((t, 256), lambda h, j, i: (j, h)), pl.BlockSpec((t, 128), lambda h, j, i: (j, h))],
        out_shape=[jax.ShapeDtypeStruct((S, 4096), F32), jax.ShapeDtypeStruct((S, 2048), F32)],
        scratch_shapes=[pltpu.VMEM((t, 256), F32), pltpu.VMEM((t, 128), F32)],
        compiler_params=_cp("parallel", "parallel", "arbitrary"),
    )(qc, kc, vb, do, lse.reshape(MLA_HEADS, 1, S), delta.reshape(MLA_HEADS, 1, S))


def _scan_rows(x, reverse):
    n = x.shape[0]
    row = lax.broadcasted_iota(jnp.int32, x.shape, 0)
    d = 1
    while d < n:
        if reverse:
            x = x + jnp.where(row < n - d, pltpu.roll(x, n - d, 0), 0.0)
        else:
            x = x + jnp.where(row >= d, pltpu.roll(x, d, 0), 0.0)
        d *= 2
    return x


@jax.custom_vjp
def cumsum_rows(x):
    return _scan_rows(x, False)


cumsum_rows.defvjp(lambda x: (_scan_rows(x, False), None), lambda _, g: (_scan_rows(g, True),))


def hgrn_chunk(q_in, f_in, v, lb, state_t):
    f = lb + (1.0 - lb) * jax.nn.sigmoid(f_in)
    q = jax.nn.silu(q_in) * HG_DK ** -0.5
    k = 1.0 - f
    b = cumsum_rows(jnp.log(f))
    b_last = b[CHUNK - 1:CHUNK]
    b_mid = b[CHUNK // 2 - 1:CHUNK // 2]
    r = lax.broadcasted_iota(jnp.int32, (CHUNK, CHUNK), 0)
    c = lax.broadcasted_iota(jnp.int32, (CHUNK, CHUNK), 1)
    att = jnp.where(c <= r, _dot(q * jnp.exp(b - b_mid), k * jnp.exp(b_mid - b), "nt"), 0.0)
    o = _dot(q * jnp.exp(b), state_t, "nt") + _dot(att, v, "nn")
    new_state_t = state_t * jnp.exp(b_last) + _dot(v, k * jnp.exp(b_last - b), "tn")
    return o, new_state_t


def hgrn_scan_fwd(proj, lb):
    S = proj.shape[0]
    nc = S // CHUNK
    W = HG_HB * 128

    def body(q_ref, f_ref, v_ref, lb_ref, o_ref, hst_ref, st):
        @pl.when(pl.program_id(1) == 0)
        def _():
            st[...] = jnp.zeros_like(st)

        for h in range(HG_HB):
            cs = slice(h * 128, (h + 1) * 128)
            hst_ref[h] = st[h]
            o, new = hgrn_chunk(q_ref[:, cs], f_ref[:, cs], v_ref[:, cs], lb_ref[:, cs], st[h])
            o_ref[:, cs] = o
            st[h] = new

    def seg(off):
        return pl.BlockSpec((CHUNK, W), functools.partial(lambda g, c, off: (c, off + g), off=off // W))

    return pl.pallas_call(
        body, name="hgrn_scan_fwd", grid=(HG_HEADS // HG_HB, nc),
        in_specs=[seg(P_HQ), seg(P_HF), seg(P_HI), pl.BlockSpec((1, W), lambda g, c: (0, g))],
        out_specs=[pl.BlockSpec((CHUNK, W), lambda g, c: (c, g)), pl.BlockSpec((None, HG_HB, 128, 128), lambda g, c: (c, g, 0, 0))],
        out_shape=[jax.ShapeDtypeStruct((S, HG_WIDTH), F32), jax.ShapeDtypeStruct((nc, HG_HEADS, 128, 128), F32)],
        scratch_shapes=[pltpu.VMEM((HG_HB, 128, 128), F32)],
        compiler_params=_cp("parallel", "arbitrary"),
    )(proj, proj, proj, lb)


def hgrn_scan_bwd(proj, lb, hst, do):
    S = proj.shape[0]
    nc = S // CHUNK
    W = HG_HB * 128

    def body(q_ref, f_ref, v_ref, lb_ref, hst_ref, do_ref, dq_ref, df_ref, dv_ref, dlb_ref, dst):
        @pl.when(pl.program_id(1) == 0)
        def _():
            dst[...] = jnp.zeros_like(dst)
            dlb_ref[...] = jnp.zeros_like(dlb_ref)

        for h in range(HG_HB):
            cs = slice(h * 128, (h + 1) * 128)
            _, vjp = jax.vjp(hgrn_chunk, q_ref[:, cs], f_ref[:, cs], v_ref[:, cs], lb_ref[:, cs], hst_ref[h])
            dq, df, dv, dlb, dstate = vjp((do_ref[:, cs], dst[h]))
            dq_ref[:, cs] = dq.astype(dq_ref.dtype)
            df_ref[:, cs] = df.astype(df_ref.dtype)
            dv_ref[:, cs] = dv.astype(dv_ref.dtype)
            dlb_ref[:, cs] += dlb
            dst[h] = dstate

    def seg(off):
        return pl.BlockSpec((CHUNK, W), functools.partial(lambda g, c, off: (nc - 1 - c, off + g), off=off // W))

    row = pl.BlockSpec((CHUNK, W), lambda g, c: (nc - 1 - c, g))
    vec = pl.BlockSpec((1, W), lambda g, c: (0, g))
    return pl.pallas_call(
        body, name="hgrn_scan_bwd", grid=(HG_HEADS // HG_HB, nc),
        in_specs=[seg(P_HQ), seg(P_HF), seg(P_HI), vec,
                  pl.BlockSpec((None, HG_HB, 128, 128), lambda g, c: (nc - 1 - c, g, 0, 0)), row],
        out_specs=[row, row, row, vec],
        out_shape=[jax.ShapeDtypeStruct((S, HG_WIDTH), BF16)] * 3 + [jax.ShapeDtypeStruct((1, HG_WIDTH), F32)],
        scratch_shapes=[pltpu.VMEM((HG_HB, 128, 128), F32)],
        compiler_params=_cp("parallel", "arbitrary"),
    )(proj, proj, proj, lb, hst, do)


def _silu_grad(x):
    s = jax.nn.sigmoid(x)
    return s * (1.0 + x * (1.0 - s))


def conv_fwd(proj, w, b, tm=512):
    S = proj.shape[0]
    tm = _fit(S, tm)
    G = 512
    off = P_XBC // G

    def body(cur_ref, prev_ref, w_ref, b_ref, act_ref, pre_ref):
        prev = prev_ref[...] * (pl.program_id(1) > 0).astype(F32)
        ext = jnp.concatenate([prev, cur_ref[...]], axis=0)
        n = tm + 8
        acc = b_ref[...] + jnp.zeros((tm, G), F32)
        for j in range(SSM_CONV):
            acc = acc + w_ref[j:j + 1, :] * pltpu.roll(ext, (n - 5 - j) % n, 0)[0:tm]
        pre_ref[...] = acc
        act_ref[...] = jax.nn.silu(acc)

    out = pl.BlockSpec((tm, G), lambda j, i: (i, j))
    return pl.pallas_call(
        body, name="conv_fwd", grid=(SSM_CONV_DIM // G, S // tm),
        in_specs=[pl.BlockSpec((tm, G), lambda j, i: (i, off + j)),
                  pl.BlockSpec((8, G), lambda j, i: (jnp.maximum(i * (tm // 8) - 1, 0), off + j)),
                  pl.BlockSpec((SSM_CONV, G), lambda j, i: (0, j)), pl.BlockSpec((1, G), lambda j, i: (0, j))],
        out_specs=[out, out], out_shape=[jax.ShapeDtypeStruct((S, SSM_CONV_DIM), F32)] * 2,
        compiler_params=_cp("parallel", "parallel"),
    )(proj, proj, w, b)


def conv_bwd(proj, pre, dact, w, tm=512):
    S = proj.shape[0]
    tm = _fit(S, tm)
    G = 512
    off = P_XBC // G
    nb = S // tm

    def body(x_ref, xp_ref, pre_ref, pren_ref, d_ref, dn_ref, w_ref, dx_ref, dw_ref, db_ref):
        i = pl.program_id(1)
        n = tm + 8
        dpre = d_ref[...] * _silu_grad(pre_ref[...])
        dpre_next = dn_ref[...] * _silu_grad(pren_ref[...]) * (i < nb - 1).astype(F32)
        dext = jnp.concatenate([dpre, dpre_next], axis=0)
        xext = jnp.concatenate([xp_ref[...] * (i > 0).astype(F32), x_ref[...]], axis=0)
        dx = jnp.zeros((tm, G), F32)
        dws = []
        for j in range(SSM_CONV):
            dx = dx + w_ref[j:j + 1, :] * pltpu.roll(dext, (n - (3 - j)) % n, 0)[0:tm]
            dws.append(jnp.sum(dpre * pltpu.roll(xext, (n - 5 - j) % n, 0)[0:tm], axis=0, keepdims=True))
        dx_ref[...] = dx.astype(dx_ref.dtype)

        @pl.when(i == 0)
        def _():
            dw_ref[...] = jnp.zeros_like(dw_ref)
            db_ref[...] = jnp.zeros_like(db_ref)

        dw_ref[...] += jnp.concatenate(dws, axis=0)
        db_ref[...] += jnp.sum(dpre, axis=0, keepdims=True)

    cur = pl.BlockSpec((tm, G), lambda j, i: (i, j))
    nxt = pl.BlockSpec((8, G), lambda j, i: (jnp.minimum((i + 1) * (tm // 8), S // 8 - 1), j))
    return pl.pallas_call(
        body, name="conv_bwd", grid=(SSM_CONV_DIM // G, nb),
        in_specs=[pl.BlockSpec((tm, G), lambda j, i: (i, off + j)),
                  pl.BlockSpec((8, G), lambda j, i: (jnp.maximum(i * (tm // 8) - 1, 0), off + j)),
                  cur, nxt, cur, nxt, pl.BlockSpec((SSM_CONV, G), lambda j, i: (0, j))],
        out_specs=[cur, pl.BlockSpec((SSM_CONV, G), lambda j, i: (0, j)), pl.BlockSpec((1, G), lambda j, i: (0, j))],
        out_shape=[jax.ShapeDtypeStruct((S, SSM_CONV_DIM), BF16), jax.ShapeDtypeStruct((SSM_CONV, SSM_CONV_DIM), F32),
                   jax.ShapeDtypeStruct((1, SSM_CONV_DIM), F32)],
        compiler_params=_cp("parallel", "arbitrary"),
    )(proj, proj, pre, pre, dact, dact, w)


def _eye_dot(a, mode):
    Q = a.shape[0] if mode == "tn" else a.shape[1]
    eye = (lax.broadcasted_iota(jnp.int32, (Q, Q), 0) == lax.broadcasted_iota(jnp.int32, (Q, Q), 1)).astype(BF16)
    hi = a.astype(BF16)
    r1 = a - hi.astype(F32)
    mid = r1.astype(BF16)
    lo = (r1 - mid.astype(F32)).astype(BF16)
    if mode == "tn":
        return sum(lax.dot_general(p, eye, _DIMS["tn"], preferred_element_type=F32) for p in (hi, mid, lo))
    return sum(lax.dot_general(eye, p, _DIMS["nt"], preferred_element_type=F32) for p in (hi, mid, lo))


@jax.custom_vjp
def _transpose_exact(a):
    return _eye_dot(a, "tn")


_transpose_exact.defvjp(lambda a: (_eye_dot(a, "tn"), None), lambda _, g: (_eye_dot(g, "nt"),))


def ssd_decay_inputs(dtr, bias, alog):
    dt = jax.nn.softplus(dtr + bias)
    acum = cumsum_rows(dt * -jnp.exp(alog))
    return dt, acum, _transpose_exact(acum)


def ssd_head(x, cb, bm, cm, dt, acum, a_s, dsk, h_prev):
    Q = x.shape[0]
    r = lax.broadcasted_iota(jnp.int32, (Q, Q), 0)
    c = lax.broadcasted_iota(jnp.int32, (Q, Q), 1)
    a_l = jnp.broadcast_to(acum, (Q, Q))
    decay = jnp.where(c <= r, jnp.exp(jnp.minimum(a_l - a_s, 0.0)), 0.0)
    xdt = x * dt
    y_diag = _dot(cb * decay, xdt, "nn")
    a_last = acum[Q - 1:Q]
    states = _dot(xdt * jnp.exp(a_last - acum), bm, "tn")
    h_new = h_prev * jnp.exp(a_last) + states
    y_off = _dot(cm, h_prev, "nt") * jnp.exp(acum)
    return y_diag + y_off + x * dsk, h_new


def _ssd_specs(S, rev):
    Q = _fit(S, SSD_Q)
    nc = S // Q
    ci = (lambda c: nc - 1 - c) if rev else (lambda c: c)
    x = pl.BlockSpec((Q, 512), lambda g, c: (ci(c), g))
    bm = pl.BlockSpec((Q, 128), lambda g, c: (ci(c), SSM_INNER // 128 + g))
    cm = pl.BlockSpec((Q, 128), lambda g, c: (ci(c), SSM_INNER // 128 + SSM_GROUPS + g))
    dtr = pl.BlockSpec((None, Q, 8), lambda g, c: (g, ci(c), 0))
    par = pl.BlockSpec((None, 1, 8), lambda g, c: (g, 0, 0))
    hs = pl.BlockSpec((None, 8, SSM_HEADDIM, SSM_STATE), lambda g, c: (ci(c), g, 0, 0))
    return Q, nc, x, bm, cm, dtr, par, hs


def ssd_scan_fwd(act, dtr, bias, alog, dsk):
    S = act.shape[0]
    Q, nc, x_s, bm_s, cm_s, dtr_s, par_s, hs_s = _ssd_specs(S, False)

    def body(x_ref, bm_ref, cm_ref, dtr_ref, b_ref, a_ref, d_ref, y_ref, hs_ref, st, dt_s, ac_s, act_s):
        @pl.when(pl.program_id(1) == 0)
        def _():
            st[...] = jnp.zeros_like(st)

        dt_s[...], ac_s[...], act_s[...] = ssd_decay_inputs(dtr_ref[...], b_ref[...], a_ref[...])
        bm, cm = bm_ref[...], cm_ref[...]
        cb = _dot(cm, bm, "nt")
        for j in range(8):
            hs_ref[j] = st[j]
            y, h_new = ssd_head(x_ref[:, j * 64:(j + 1) * 64], cb, bm, cm, dt_s[:, j:j + 1], ac_s[:, j:j + 1], act_s[j:j + 1, :],
                                d_ref[:, j:j + 1], st[j])
            y_ref[:, j * 64:(j + 1) * 64] = y
            st[j] = h_new

    return pl.pallas_call(
        body, name="ssd_scan_fwd", grid=(SSM_GROUPS, nc),
        in_specs=[x_s, bm_s, cm_s, dtr_s, par_s, par_s, par_s], out_specs=[x_s, hs_s],
        out_shape=[jax.ShapeDtypeStruct((S, SSM_INNER), F32), jax.ShapeDtypeStruct((nc, SSM_HEADS, SSM_HEADDIM, SSM_STATE), F32)],
        scratch_shapes=[pltpu.VMEM((8, SSM_HEADDIM, SSM_STATE), F32), pltpu.VMEM((Q, 8), F32), pltpu.VMEM((Q, 8), F32),
                        pltpu.VMEM((8, Q), F32)],
        compiler_params=_cp("parallel", "arbitrary"),
    )(act, act, act, dtr, bias, alog, dsk)


def ssd_scan_bwd(act, dtr, bias, alog, dsk, hs, dy):
    S = act.shape[0]
    Q, nc, x_s, bm_s, cm_s, dtr_s, par_s, hs_s = _ssd_specs(S, True)
    g_s = pl.BlockSpec((Q, 128), lambda g, c: (nc - 1 - c, g))

    def body(x_ref, bm_ref, cm_ref, dtr_ref, b_ref, a_ref, d_ref, hs_ref, dy_ref,
             dx_ref, dbm_ref, dcm_ref, ddtr_ref, db_ref, da_ref, dd_ref, dst, dt_s, ac_s, ddt_s, dac_s, act_s, dact_s):
        @pl.when(pl.program_id(1) == 0)
        def _():
            dst[...] = jnp.zeros_like(dst)
            db_ref[...] = jnp.zeros_like(db_ref)
            da_ref[...] = jnp.zeros_like(da_ref)
            dd_ref[...] = jnp.zeros_like(dd_ref)

        (dt_s[...], ac_s[...], act_s[...]), decay_vjp = jax.vjp(ssd_decay_inputs, dtr_ref[...], b_ref[...], a_ref[...])
        bm, cm = bm_ref[...], cm_ref[...]
        cb = _dot(cm, bm, "nt")
        dcb = jnp.zeros((Q, Q), F32)
        dbm = jnp.zeros((Q, 128), F32)
        dcm = jnp.zeros((Q, 128), F32)
        for j in range(8):
            cs = slice(j * 64, (j + 1) * 64)
            one = slice(j, j + 1)
            _, vjp = jax.vjp(ssd_head, x_ref[:, cs], cb, bm, cm, dt_s[:, one], ac_s[:, one], act_s[one, :], d_ref[:, one], hs_ref[j])
            dx, gcb, gb, gc, gdt, gac, gact, gdsk, gh = vjp((dy_ref[:, cs], dst[j]))
            dact_s[one, :] = gact
            dx_ref[:, cs] = dx
            dcb = dcb + gcb
            dbm = dbm + gb
            dcm = dcm + gc
            ddt_s[:, one] = gdt
            dac_s[:, one] = gac
            dd_ref[:, one] += gdsk
            dst[j] = gh
        dbm_ref[...] = dbm + _dot(dcb, cm, "tn")
        dcm_ref[...] = dcm + _dot(dcb, bm, "nn")
        ddtr, dbias, dalog = decay_vjp((ddt_s[...], dac_s[...], dact_s[...]))
        ddtr_ref[...] = ddtr
        db_ref[...] += dbias
        da_ref[...] += dalog

    return pl.pallas_call(
        body, name="ssd_scan_bwd", grid=(SSM_GROUPS, nc),
        in_specs=[x_s, bm_s, cm_s, dtr_s, par_s, par_s, par_s, hs_s, x_s],
        out_specs=[x_s, g_s, g_s, dtr_s, par_s, par_s, par_s],
        out_shape=[jax.ShapeDtypeStruct((S, SSM_INNER), F32), jax.ShapeDtypeStruct((S, 1024), F32), jax.ShapeDtypeStruct((S, 1024), F32),
                   jax.ShapeDtypeStruct((SSM_GROUPS, S, 8), F32)] + [jax.ShapeDtypeStruct((SSM_GROUPS, 1, 8), F32)] * 3,
        scratch_shapes=[pltpu.VMEM((8, SSM_HEADDIM, SSM_STATE), F32)] + [pltpu.VMEM((Q, 8), F32)] * 4 + [pltpu.VMEM((8, Q), F32)] * 2,
        compiler_params=_cp("parallel", "arbitrary"),
    )(act, act, act, dtr, bias, alog, dsk, hs, dy)


def _w_in_to_local(w):
    parts = [w[..., 0:1024], w[..., 1088:19520], w[..., 19584:25728], w[..., 1024:1088], w[..., 19520:19584],
             jnp.zeros(w.shape[:-1] + (IN_PAD - IN_DIM,), w.dtype)]
    return jnp.concatenate(parts, axis=-1)


def _w_in_from_local(g):
    return jnp.concatenate([g[..., 0:1024], g[..., 25600:25664], g[..., 1024:19456], g[..., 25664:25728], g[..., 19456:25600]], axis=-1)


def _w_uq_to_local(w):
    lead = w.shape[:-1]
    w = w.reshape(lead + (MLA_HEADS, 192))
    nope = w[..., :128].reshape(lead + (8, 256))
    rope = w[..., 128:].reshape(lead + (8, 128))
    return jnp.concatenate([nope, rope], axis=-1).reshape(lead + (3072,))


def _w_uq_from_local(g):
    lead = g.shape[:-1]
    g = g.reshape(lead + (8, 384))
    nope = g[..., :256].reshape(lead + (MLA_HEADS, 128))
    rope = g[..., 256:].reshape(lead + (MLA_HEADS, 64))
    return jnp.concatenate([nope, rope], axis=-1).reshape(lead + (3072,))


BIG = ("ffn1_wi", "ffn1_wo", "w_in", "mla_w_uq", "mla_w_ukv", "w_o_mla", "w_o_hgrn", "w_o_ssm", "w_out", "ffn2_wi", "ffn2_wo")
COL_SHARDED = ("ffn1_wi", "w_in", "mla_w_uq", "mla_w_ukv", "ffn2_wi")
SMALL = ("ffn1_norm", "mix_norm", "mla_q_norm", "mla_kv_norm", "hgrn_lb_logits", "hgrn_norm", "ssm_conv_b", "ssm_a_log",
         "ssm_dt_bias", "ssm_d", "ssm_norm", "ffn2_norm")


def _ffn_fwd(tag, x, norm_w, wi, wo):
    (h,) = rowwise(tag + "_norm", f_rmsnorm, [(x, D_MODEL, 0)], [(norm_w, D_MODEL, 0)], [(D_MODEL, D_MODEL, BF16)])
    gu = matmul(tag + "_wi", h, wi, out_dtype=BF16)
    (a,) = rowwise(tag + "_act", f_swiglu, [(gu, 512, 0), (gu, 512, D_FF // 512)], [], [(D_FF, 512, BF16)], n_groups=D_FF // 512)
    out = matmul(tag + "_wo", a, wo, alpha=0.5, res=x)
    return out, (x, h, gu, a)


def _carry(hosts, key, fn, *args, **kw):
    if hosts and key in hosts:
        make, take = hosts[key]
        out, side_out = fn(*args, side=make(), **kw)
        take(side_out)
        return out
    return fn(*args, **kw)


def _ffn_bwd(tag, dx, dxb, saved, norm_w, wi, wo, hosts=None):
    x, h, gu, a = saved
    da = matmul(tag + "_da", dxb, wo, "nt", out_dtype=BF16, alpha=0.5)
    dwo = matmul(tag + "_dwo", a, dxb, "tn", alpha=0.5)
    dg, du = rowwise_bwd(tag + "_act_bwd", f_swiglu, [(gu, 512, 0), (gu, 512, D_FF // 512)], [], [(da, 512)],
                         [(0, D_FF, BF16), (1, D_FF, BF16)], [], n_groups=D_FF // 512)
    dgu = jnp.concatenate([dg, du], axis=1)
    dh = _carry(hosts, tag + "_dh", matmul, tag + "_dh", dgu, wi, "nt")
    dwi = matmul(tag + "_dwi", h, dgu, "tn")
    dx_in, dxb_in, dnorm = rowwise_bwd(tag + "_norm_bwd", f_rmsnorm, [(x, D_MODEL, 0)], [(norm_w, D_MODEL, 0)], [(dh, D_MODEL)],
                                       [(0, D_MODEL, F32), (0, D_MODEL, BF16)], [0], adds={0: dx}, tm=256)
    return dx_in, dxb_in, dnorm, dwi, dwo


def _ssm_params(W):
    return [W[k].reshape(SSM_GROUPS, 1, 8) for k in ("ssm_dt_bias", "ssm_a_log", "ssm_d")]


def _dt_cols(proj):
    S = proj.shape[0]
    return proj[:, P_KPEDT + 64:P_KPEDT + 128].reshape(S, SSM_GROUPS, 8).transpose(1, 0, 2)


def _mix_fwd(x, W, lb, tabs, hosts=None):
    cosf, sinf = tabs
    (h,) = rowwise("mix_norm", f_rmsnorm, [(x, D_MODEL, 0)], [(W["mix_norm"], D_MODEL, 0)], [(D_MODEL, D_MODEL, BF16)])
    proj = matmul("w_in", h, W["w_in"])
    (qn,) = rowwise("q_norm", f_rmsnorm, [(proj, 512, 0)], [(W["mla_q_norm"], 512, 0)], [(512, 512, BF16)])
    (kvn,) = rowwise("kv_norm", f_rmsnorm, [(proj, 512, 1)], [(W["mla_kv_norm"], 512, 0)], [(512, 512, BF16)])
    q = matmul("w_uq", qn, W["mla_w_uq"])
    kv = matmul("w_ukv", kvn, W["mla_w_ukv"])
    qc = mla_q_prep(q, cosf, sinf)
    kc, vb = mla_k_prep(kv, proj, cosf, sinf)
    o_a, lse = _carry(hosts, "attention_fwd", attention_fwd, qc, kc, vb)
    y_a = matmul("w_o_mla", o_a, W["w_o_mla"])
    o_h, hst = hgrn_scan_fwd(proj, lb)
    (pre_b,) = rowwise("hgrn_out", f_hgrn_out, [(o_h, 128, 0), (proj, 128, P_HGATE // 128)], [(W["hgrn_norm"], 128, 0)],
                       [(HG_WIDTH, 128, BF16)], n_groups=HG_HEADS, tm=1024)
    y_b = matmul("w_o_hgrn", pre_b, W["w_o_hgrn"])
    act, pre = conv_fwd(proj, W["ssm_conv_w"], W["ssm_conv_b"])
    dtr = _dt_cols(proj)
    y_s, hs = ssd_scan_fwd(act, dtr, *_ssm_params(W))
    (pre_c,) = rowwise("ssm_out", f_ssm_out, [(y_s, 512, 0), (proj, 512, P_Z // 512)], [(W["ssm_norm"], 512, 0)],
                       [(SSM_INNER, 512, BF16)], n_groups=SSM_GROUPS)
    y_c = matmul("w_o_ssm", pre_c, W["w_o_ssm"])
    g0 = P_GATES // 512
    (merged,) = rowwise("merge", f_merge, [(proj, 512, g0), (proj, 512, g0 + 4), (proj, 512, g0 + 8), (y_a, 512, 0), (y_b, 512, 0), (y_c, 512, 0)],
                        [], [(D_MODEL, 512, BF16)], n_groups=4)
    out = matmul("w_out", merged, W["w_out"], res=x)
    return out, (x, h, proj, qn, kvn, qc, kc, vb, o_a, lse, y_a, o_h, hst, pre_b, y_b, act, pre, dtr, y_s, hs, pre_c, y_c, merged)


def _mix_bwd(dx, dxb, saved, W, lb, tabs, hosts=None):
    cosf, sinf = tabs
    (x, h, proj, qn, kvn, qc, kc, vb, o_a, lse, y_a, o_h, hst, pre_b, y_b, act, pre, dtr, y_s, hs, pre_c, y_c, merged) = saved
    S = x.shape[0]
    g = {}
    dmerged = matmul("d_merged", dxb, W["w_out"], "nt")
    g["w_out"] = matmul("dw_out", merged, dxb, "tn")
    g0 = P_GATES // 512
    dga, dgb, dgc, dya, dyb, dyc = rowwise_bwd(
        "merge_bwd", f_merge, [(proj, 512, g0), (proj, 512, g0 + 4), (proj, 512, g0 + 8), (y_a, 512, 0), (y_b, 512, 0), (y_c, 512, 0)],
        [], [(dmerged, 512)], [(k, D_MODEL, BF16) for k in range(6)], [], n_groups=4)
    do_a = matmul("d_o_mla", dya, W["w_o_mla"], "nt", out_dtype=BF16)
    g["w_o_mla"] = matmul("dw_o_mla", o_a, dya, "tn")
    delta = attention_delta(o_a, do_a)
    dqc = _carry(hosts, "attention_bwd_dq", attention_bwd_dq, qc, kc, vb, do_a, lse, delta)
    dkc, dv = attention_bwd_dkv(qc, kc, vb, do_a, lse, delta)
    dq = mla_q_prep_bwd(dqc, cosf, sinf)
    dkv, dpe = mla_k_prep_bwd(dkc, dv, cosf, sinf)
    dqn = matmul("d_qn", dq, W["mla_w_uq"], "nt")
    g["mla_w_uq"] = matmul("dw_uq", qn, dq, "tn")
    dkvn = matmul("d_kvn", dkv, W["mla_w_ukv"], "nt")
    g["mla_w_ukv"] = matmul("dw_ukv", kvn, dkv, "tn")
    dq_lat, g["mla_q_norm"] = rowwise_bwd("q_norm_bwd", f_rmsnorm, [(proj, 512, 0)], [(W["mla_q_norm"], 512, 0)], [(dqn, 512)],
                                          [(0, 512, BF16)], [0])
    dkv_lat, g["mla_kv_norm"] = rowwise_bwd("kv_norm_bwd", f_rmsnorm, [(proj, 512, 1)], [(W["mla_kv_norm"], 512, 0)], [(dkvn, 512)],
                                            [(0, 512, BF16)], [0])
    do_b = matmul("d_o_hgrn", dyb, W["w_o_hgrn"], "nt")
    g["w_o_hgrn"] = matmul("dw_o_hgrn", pre_b, dyb, "tn")
    do_h, dhgate, g["hgrn_norm"] = rowwise_bwd(
        "hgrn_out_bwd", f_hgrn_out, [(o_h, 128, 0), (proj, 128, P_HGATE // 128)], [(W["hgrn_norm"], 128, 0)], [(do_b, 128)],
        [(0, HG_WIDTH, F32), (1, HG_WIDTH, BF16)], [0], n_groups=HG_HEADS, tm=1024)
    dhq, dhf, dhi, dlb = hgrn_scan_bwd(proj, lb, hst, do_h)
    do_c = matmul("d_o_ssm", dyc, W["w_o_ssm"], "nt")
    g["w_o_ssm"] = matmul("dw_o_ssm", pre_c, dyc, "tn")
    dy_s, dz, g["ssm_norm"] = rowwise_bwd(
        "ssm_out_bwd", f_ssm_out, [(y_s, 512, 0), (proj, 512, P_Z // 512)], [(W["ssm_norm"], 512, 0)], [(do_c, 512)],
        [(0, SSM_INNER, F32), (1, SSM_INNER, BF16)], [0], n_groups=SSM_GROUPS)
    dxs, dbm, dcm, ddtr, dbias, dalog, ddsk = ssd_scan_bwd(act, dtr, *_ssm_params(W), hs, dy_s)
    g["ssm_dt_bias"], g["ssm_a_log"], g["ssm_d"] = (v.reshape(1, SSM_HEADS) for v in (dbias, dalog, ddsk))
    dxbc, g["ssm_conv_w"], g["ssm_conv_b"] = conv_bwd(proj, pre, jnp.concatenate([dxs, dbm, dcm], axis=1), W["ssm_conv_w"])
    ddt = ddtr.transpose(1, 0, 2).reshape(S, SSM_HEADS)
    dproj = jnp.concatenate([dq_lat, dkv_lat, dhq, dhf, dhi, dhgate, dz, dxbc, dga, dgb, dgc, dpe[:, :64].astype(BF16),
                             ddt.astype(BF16), jnp.zeros((S, IN_PAD - IN_DIM), BF16)], axis=1)
    dh = _carry(hosts, "d_h_mix", matmul, "d_h_mix", dproj, W["w_in"], "nt")
    g["w_in"] = matmul("dw_in", h, dproj, "tn")
    dx_in, dxb_in, g["mix_norm"] = rowwise_bwd("mix_norm_bwd", f_rmsnorm, [(x, D_MODEL, 0)], [(W["mix_norm"], D_MODEL, 0)], [(dh, D_MODEL)],
                                               [(0, D_MODEL, F32), (0, D_MODEL, BF16)], [0], adds={0: dx}, tm=256)
    return dx_in, dxb_in, g, dlb


def local_step(x, target, weights, small, final_norm, fwd_hosts=None, bwd_hosts=None):
    S = x.shape[0]
    tabs = rope_tables(S)
    (lbs,) = rowwise("lower_bounds", f_lower_bounds, [(small["hgrn_lb_logits"], HG_WIDTH, 0)], [], [(HG_WIDTH, HG_WIDTH, F32)])
    saved = []
    for l in range(DEPTH):
        W = dict(weights(l))
        for k in SMALL:
            W[k] = small[k][l:l + 1]
        lb = lbs[l:l + 1]
        x, s1 = _ffn_fwd("ffn1", x, W["ffn1_norm"], W["ffn1_wi"], W["ffn1_wo"])
        x, s2 = _mix_fwd(x, W, lb, tabs, fwd_hosts(l) if fwd_hosts else None)
        x, s3 = _ffn_fwd("ffn2", x, W["ffn2_norm"], W["ffn2_wi"], W["ffn2_wo"])
        saved.append((W, lb, s1, s2, s3))
    dx, dxb, dfinal, loss = loss_head(x, target, final_norm.reshape(1, D_MODEL))
    grads = [None] * DEPTH
    dlbs = [None] * DEPTH
    for l in reversed(range(DEPTH)):
        W, lb, s1, s2, s3 = saved[l]
        hosts = bwd_hosts(l, grads) if bwd_hosts else None
        g = {}
        dx, dxb, g["ffn2_norm"], g["ffn2_wi"], g["ffn2_wo"] = _ffn_bwd("ffn2", dx, dxb, s3, W["ffn2_norm"], W["ffn2_wi"], W["ffn2_wo"], hosts)
        dx, dxb, gm, dlbs[l] = _mix_bwd(dx, dxb, s2, W, lb, tabs, hosts)
        g.update(gm)
        dx, dxb, g["ffn1_norm"], g["ffn1_wi"], g["ffn1_wo"] = _ffn_bwd("ffn1", dx, dxb, s1, W["ffn1_norm"], W["ffn1_wi"], W["ffn1_wo"])
        grads[l] = g
    (dlogits,) = rowwise_bwd("lower_bounds_bwd", f_lower_bounds, [(small["hgrn_lb_logits"], HG_WIDTH, 0)], [],
                             [(jnp.concatenate(dlbs, axis=0), HG_WIDTH)], [(0, HG_WIDTH, F32)], [])
    return loss, dx, grads, dlogits, dfinal


ANY = pl.BlockSpec(memory_space=pl.ANY)


def _coords():
    return lax.axis_index("x"), lax.axis_index("y"), lax.axis_index("c")


def _exchange_call(name, body, src, out_shape, n_copies):
    return pl.pallas_call(
        body, name=name, in_specs=[ANY], out_specs=ANY, out_shape=out_shape,
        scratch_shapes=[pltpu.SemaphoreType.DMA((n_copies,)), pltpu.SemaphoreType.DMA((n_copies,)), pltpu.SemaphoreType.DMA],
    )(src)


class Exchange(NamedTuple):
    srcs: list
    out_shapes: list
    sem_counts: tuple
    start: Callable
    wait: Callable


def run_exchange(name, ex):
    n = len(ex.srcs)

    def body(*refs):
        src, out, sems = refs[:n], refs[n:n + len(ex.out_shapes)], refs[n + len(ex.out_shapes):]
        ex.start(src, out, sems)
        ex.wait(src, out, sems)

    return pl.pallas_call(
        body, name=name, in_specs=[ANY] * n, out_specs=[ANY] * len(ex.out_shapes), out_shape=ex.out_shapes,
        scratch_shapes=[pltpu.SemaphoreType.DMA((k,)) for k in ex.sem_counts],
    )(*ex.srcs)


def _pcall(name, body, grid, in_specs, out_specs, out_shape, scratch_shapes, sem, args, side=None):
    if side is None:
        return pl.pallas_call(body, name=name, grid=grid, in_specs=in_specs, out_specs=out_specs, out_shape=out_shape,
                              scratch_shapes=scratch_shapes, compiler_params=_cp(*sem))(*args)
    single = not isinstance(out_shape, (list, tuple))
    out_specs_l = [out_specs] if single else list(out_specs)
    out_shape_l = [out_shape] if single else list(out_shape)
    n_in, n_out, n_scr, n_s, n_so = len(in_specs), len(out_specs_l), len(scratch_shapes), len(side.srcs), len(side.out_shapes)

    def hosted(*refs):
        ins, s_in = refs[:n_in], refs[n_in:n_in + n_s]
        o0 = n_in + n_s
        outs, s_out = refs[o0:o0 + n_out], refs[o0 + n_out:o0 + n_out + n_so]
        scr, sems = refs[o0 + n_out + n_so:o0 + n_out + n_so + n_scr], refs[o0 + n_out + n_so + n_scr:]
        ids = [pl.program_id(d) for d in range(len(grid))]
        first = functools.reduce(jnp.logical_and, [i == 0 for i in ids])
        last = functools.reduce(jnp.logical_and, [i == g - 1 for i, g in zip(ids, grid)])

        @pl.when(first)
        def _():
            side.start(s_in, s_out, sems)

        body(*ins, *outs, *scr)

        @pl.when(last)
        def _():
            side.wait(s_in, s_out, sems)

    res = pl.pallas_call(
        hosted, name=name, grid=grid, in_specs=list(in_specs) + [ANY] * n_s, out_specs=out_specs_l + [ANY] * n_so,
        out_shape=out_shape_l + list(side.out_shapes),
        scratch_shapes=list(scratch_shapes) + [pltpu.SemaphoreType.DMA((k,)) for k in side.sem_counts],
        compiler_params=_cp(*["arbitrary"] * len(grid)),
    )(*args, *side.srcs)
    return (res[0] if single else res[:n_out]), res[n_out:]


def _chip_peers(x, y):
    return [(1 - x, y), (x, 1 - y), (1 - x, 1 - y)]


def weights_fetch(shards, layer):
    n = len(shards)
    half = [s.shape[1] // 2 for s in shards]

    def copies(src, out, sems):
        send_sems, recv_sems, fsend_sems, frecv_sems = sems
        x, y, c = _coords()
        peers = _chip_peers(x, y)

        def ici(i, k, chip):
            rows = pl.ds(c * half[i], half[i])
            return pltpu.make_async_remote_copy(src[i].at[layer, rows], out[i].at[chip, rows], send_sems.at[3 * i + k],
                                                recv_sems.at[3 * i + k], device_id=(*peers[k], c), device_id_type=MESH)

        def fwd(i, k, chip, h):
            rows = pl.ds(h * half[i], half[i])
            return pltpu.make_async_remote_copy(out[i].at[chip, rows], out[i].at[chip, rows], fsend_sems.at[3 * i + k],
                                                frecv_sems.at[3 * i + k], device_id=(x, y, 1 - c), device_id_type=MESH)

        return 2 * x + y, c, peers, ici, fwd

    def start(src, out, sems):
        me, c, peers, ici, fwd = copies(src, out, sems)
        for k in range(3):
            for i in range(n):
                ici(i, k, me).start()

    def wait(src, out, sems):
        me, c, peers, ici, fwd = copies(src, out, sems)
        passed = []
        for k, (px, py) in enumerate(peers):
            for i in range(n):
                ici(i, k, 2 * px + py).wait_recv()
                passed.append(fwd(i, k, 2 * px + py, c))
                passed[-1].start()
        for k, (px, py) in enumerate(peers):
            for i in range(n):
                fwd(i, k, 2 * px + py, 1 - c).wait_recv()
        for k in range(3):
            for i in range(n):
                ici(i, k, me).wait_send()
        for cp in passed:
            cp.wait_send()

    out_shapes = [jax.ShapeDtypeStruct((4,) + s.shape[1:], s.dtype) for s in shards]
    return Exchange(shards, out_shapes, (3 * n, 3 * n, 3 * n, 3 * n), start, wait)


def _pairwise(srcs, out_shapes, n_sems, make):
    def start(src, out, sems):
        for cp in make(src, out, *sems):
            cp.start()

    def wait(src, out, sems):
        cps = make(src, out, *sems)
        for cp in cps:
            cp.wait_recv()
        for cp in cps:
            cp.wait_send()

    return Exchange(srcs, out_shapes, (n_sems, n_sems), start, wait)


def grad_half_exchange(gs):
    n = len(gs)
    half = [g.shape[1] // 2 for g in gs]

    def make(src, out, send_sems, recv_sems):
        x, y, c = _coords()
        return [pltpu.make_async_remote_copy(src[i].at[:, pl.ds((1 - c) * half[i], half[i])], out[i], send_sems.at[i], recv_sems.at[i],
                                             device_id=(x, y, 1 - c), device_id_type=MESH) for i in range(n)]

    return _pairwise(gs, [jax.ShapeDtypeStruct((4, h) + g.shape[2:], g.dtype) for g, h in zip(gs, half)], n, make)


def grad_chip_exchange(ps):
    n = len(ps)

    def make(src, out, send_sems, recv_sems):
        x, y, c = _coords()
        return [pltpu.make_async_remote_copy(src[i].at[2 * px + py], out[i].at[k], send_sems.at[3 * i + k], recv_sems.at[3 * i + k],
                                             device_id=(px, py, c), device_id_type=MESH)
                for k, (px, py) in enumerate(_chip_peers(x, y)) for i in range(n)]

    return _pairwise(ps, [jax.ShapeDtypeStruct((3,) + p.shape[1:], p.dtype) for p in ps], 3 * n, make)


def grad_reduced_exchange(rs):
    n = len(rs)

    def make(src, out, send_sems, recv_sems):
        x, y, c = _coords()
        return [pltpu.make_async_remote_copy(src[i], out[i], send_sems.at[i], recv_sems.at[i],
                                             device_id=(x, y, 1 - c), device_id_type=MESH) for i in range(n)]

    return _pairwise(rs, [jax.ShapeDtypeStruct(r.shape, r.dtype) for r in rs], n, make)


def device_allgather(name, src):
    def body(src_ref, out_ref, send_sems, recv_sems, local_sem):
        x, y, c = _coords()
        me = 4 * x + 2 * y + c
        peers = [(x ^ (m >> 2), y ^ ((m >> 1) & 1), c ^ (m & 1)) for m in range(1, 8)]

        def copy(k, dev):
            return pltpu.make_async_remote_copy(src_ref, out_ref.at[dev], send_sems.at[k], recv_sems.at[k],
                                                device_id=peers[k], device_id_type=MESH)

        local = pltpu.make_async_copy(src_ref, out_ref.at[me], local_sem)
        local.start()
        sends = [copy(k, me) for k in range(7)]
        for s in sends:
            s.start()
        for k, (px, py, pc) in enumerate(peers):
            copy(k, 4 * px + 2 * py + pc).wait_recv()
        for s in sends:
            s.wait_send()
        local.wait()

    return _exchange_call(name, body, src, jax.ShapeDtypeStruct((8,) + src.shape, src.dtype), 7)


BLOCK_ELEMS = 1 << 19


def add_pair(name, g, recv, c):
    n, R, C = recv.shape
    tr = _fit_rows(R, max(16, BLOCK_ELEMS // C))
    nb = R // tr

    def body(c_ref, g_ref, r_ref, o_ref):
        o_ref[...] = (g_ref[...].astype(F32) + r_ref[...].astype(F32)).astype(o_ref.dtype)

    blk = pl.BlockSpec((None, tr, C), lambda j, i, c_ref: (j, i, 0))
    return pl.pallas_call(
        body, name=name, out_shape=jax.ShapeDtypeStruct((n, R, C), g.dtype),
        grid_spec=pltpu.PrefetchScalarGridSpec(
            num_scalar_prefetch=1, grid=(n, nb),
            in_specs=[pl.BlockSpec((None, tr, C), lambda j, i, c_ref: (j, c_ref[0] * nb + i, 0)), blk], out_specs=blk),
        compiler_params=_cp("parallel", "parallel"),
    )(c, g, recv)


def sum_chips(name, own, recv, me):
    _, R, C = own.shape
    tr = _fit_rows(R, max(16, BLOCK_ELEMS // (2 * C)))

    def body(me_ref, o_ref, r_ref, out_ref):
        acc = o_ref[...].astype(F32)
        for k in range(3):
            acc = acc + r_ref[k].astype(F32)
        out_ref[...] = acc

    return pl.pallas_call(
        body, name=name, out_shape=jax.ShapeDtypeStruct((R, C), F32),
        grid_spec=pltpu.PrefetchScalarGridSpec(
            num_scalar_prefetch=1, grid=(R // tr,),
            in_specs=[pl.BlockSpec((None, tr, C), lambda i, me_ref: (me_ref[0], i, 0)), pl.BlockSpec((3, tr, C), lambda i, me_ref: (0, i, 0))],
            out_specs=pl.BlockSpec((tr, C), lambda i, me_ref: (i, 0))),
        compiler_params=_cp("parallel"),
    )(me, own, recv)


def sum_devices(parts):
    _, R, C = parts.shape

    def body(p_ref, o_ref):
        acc = p_ref[0]
        for k in range(1, 8):
            acc = acc + p_ref[k]
        o_ref[...] = acc

    return pl.pallas_call(body, name="sum_devices", out_shape=jax.ShapeDtypeStruct((R, C), F32))(parts)


def _fit_rows(R, pref):
    for t in range(min(pref, R), 0, -1):
        if R % t == 0 and (t % 16 == 0 or t == R):
            return t
    raise ValueError((R, pref))


def adamw(name, w, g, m, v):
    shape = w.shape
    C = shape[-1]
    w2, g2, m2, v2 = (a.reshape(-1, C) for a in (w, g, m, v))
    R = w2.shape[0]
    tr = _fit_rows(R, max(8, (1 << 18) // C)) if R * C > (1 << 18) else R
    c1 = 1.0 - ADAM_B1 ** ADAM_STEP
    c2 = 1.0 - ADAM_B2 ** ADAM_STEP

    def body(w_ref, g_ref, m_ref, v_ref, d_ref, mo_ref, vo_ref):
        gg = g_ref[...]
        mn = ADAM_B1 * m_ref[...] + (1.0 - ADAM_B1) * gg
        vn = ADAM_B2 * v_ref[...] + (1.0 - ADAM_B2) * jnp.square(gg)
        d_ref[...] = -ADAM_LR * ((mn / c1) / (jnp.sqrt(vn / c2) + ADAM_EPS) + ADAM_WD * w_ref[...])
        mo_ref[...] = mn
        vo_ref[...] = vn

    blk = pl.BlockSpec((tr, C), lambda i: (i, 0))
    outs = pl.pallas_call(
        body, name=name, grid=(R // tr,), in_specs=[blk] * 4, out_specs=[blk] * 3,
        out_shape=[jax.ShapeDtypeStruct((R, C), F32)] * 3, compiler_params=_cp("parallel"),
    )(w2, g2, m2, v2)
    return tuple(o.reshape(shape) for o in outs)


def _unshard(name, parts):
    n, L, r, c = parts.shape
    if name in COL_SHARDED:
        return parts.transpose(1, 2, 0, 3).reshape(L, r, n * c)
    return parts.transpose(1, 0, 2, 3).reshape(L, n * r, c)


def _shard(name, full):
    L, R, C = full.shape
    if name in COL_SHARDED:
        return full.reshape(L, R, 4, C // 4).transpose(0, 2, 1, 3)
    return full.reshape(L, 4, R // 4, C)


def _to_local(name, w):
    if name == "w_in":
        return _w_in_to_local(w)
    if name == "mla_w_uq":
        return _w_uq_to_local(w)
    return w


def _from_local(name, g):
    if name == "w_in":
        return _w_in_from_local(g)
    if name == "mla_w_uq":
        return _w_uq_from_local(g)
    return g


TWIN_WEIGHTS = ("ffn1_norm", "ffn1_wi", "ffn1_wo", "mix_norm", "w_in", "mla_q_norm", "mla_w_uq", "mla_kv_norm", "mla_w_ukv",
                "hgrn_lb_logits", "hgrn_norm", "ssm_conv_w", "ssm_conv_b", "ssm_a_log", "ssm_dt_bias", "ssm_d", "ssm_norm",
                "w_o_mla", "w_o_hgrn", "w_o_ssm", "w_out", "ffn2_norm", "ffn2_wi", "ffn2_wo", "final_norm")
SMALL_PACK = SMALL + ("ssm_conv_w", "final_norm")


def _pad_rows(flat, cols):
    n = flat.shape[0]
    rows = -(-n // cols)
    rows = -(-rows // 8) * 8
    return jnp.concatenate([flat, jnp.zeros((rows * cols - n,), flat.dtype)]).reshape(rows, cols)


def kernel(x, ffn1_norm, ffn1_wi, ffn1_wo, mix_norm, w_in, mla_q_norm, mla_w_uq, mla_kv_norm, mla_w_ukv, hgrn_lb_logits, hgrn_norm, ssm_conv_w, ssm_conv_b, ssm_a_log, ssm_dt_bias, ssm_d, ssm_norm, w_o_mla, w_o_hgrn, w_o_ssm, w_out, ffn2_norm, ffn2_wi, ffn2_wo, final_norm, loss_target, m_ffn1_norm, m_ffn1_wi, m_ffn1_wo, m_mix_norm, m_w_in, m_mla_q_norm, m_mla_w_uq, m_mla_kv_norm, m_mla_w_ukv, m_hgrn_lb_logits, m_hgrn_norm, m_ssm_conv_w, m_ssm_conv_b, m_ssm_a_log, m_ssm_dt_bias, m_ssm_d, m_ssm_norm, m_w_o_mla, m_w_o_hgrn, m_w_o_ssm, m_w_out, m_ffn2_norm, m_ffn2_wi, m_ffn2_wo, m_final_norm, v_ffn1_norm, v_ffn1_wi, v_ffn1_wo, v_mix_norm, v_w_in, v_mla_q_norm, v_mla_w_uq, v_mla_kv_norm, v_mla_w_ukv, v_hgrn_lb_logits, v_hgrn_norm, v_ssm_conv_w, v_ssm_conv_b, v_ssm_a_log, v_ssm_dt_bias, v_ssm_d, v_ssm_norm, v_w_o_mla, v_w_o_hgrn, v_w_o_ssm, v_w_out, v_ffn2_norm, v_ffn2_wi, v_ffn2_wo, v_final_norm):
    args = dict(locals())
    w = {n: args[n] for n in TWIN_WEIGHTS}
    m = {n: args["m_" + n] for n in TWIN_WEIGHTS}
    v = {n: args["v_" + n] for n in TWIN_WEIGHTS}
    xi, yi, ci = _coords()
    chip = 2 * xi + yi

    shards = [w[n].astype(BF16) for n in BIG]
    conv_parts = device_allgather("conv_allgather", _pad_rows(w["ssm_conv_w"].reshape(-1), 128))
    conv_full = jnp.concatenate(
        [conv_parts[2 * j].reshape(-1)[:w["ssm_conv_w"].size].reshape(w["ssm_conv_w"].shape) for j in range(4)], axis=-1)
    fetched = {0: run_exchange("weights_fetch", weights_fetch(shards, 0))}

    def layer_weights(l):
        W = {"ssm_conv_w": conv_full[l]}
        for n, parts, own in zip(BIG, fetched[l], shards):
            parts = lax.dynamic_update_slice(parts, own[l][None], (chip, 0, 0))
            W[n] = _to_local(n, _unshard(n, parts[:, None])[0])
        return W

    def fwd_hosts(l):
        return {"attention_fwd": (lambda: weights_fetch(shards, 1), lambda out: fetched.update({1: out}))} if l == 0 else None

    c_idx, chip_idx = ci.astype(jnp.int32).reshape(1), chip.astype(jnp.int32).reshape(1)
    reduced_layers = {}

    def reduction_stages(l, layer_grads):
        st = {}
        tag = "_l%d_" % l

        def half_exchange():
            st["gs"] = [_shard(n, _from_local(n, layer_grads[n][None]))[0].astype(BF16) for n in BIG]
            return grad_half_exchange(st["gs"])

        def chip_exchange():
            st["pair"] = [add_pair("add_pair" + tag + n, a, b, c_idx) for n, a, b in zip(BIG, st["gs"], st["recv"])]
            return grad_chip_exchange(st["pair"])

        def reduced_exchange():
            st["red"] = [sum_chips("sum_chips" + tag + n, a, b, chip_idx) for n, a, b in zip(BIG, st["pair"], st["from_chips"])]
            return grad_reduced_exchange(st["red"])

        def finish(others):
            reduced_layers[l] = [jnp.where(ci == 0, jnp.concatenate([mine, other]), jnp.concatenate([other, mine]))
                                 for mine, other in zip(st["red"], others)]

        return [(half_exchange, lambda out: st.update(recv=out)), (chip_exchange, lambda out: st.update(from_chips=out)),
                (reduced_exchange, finish)]

    def bwd_hosts(l, grads_so_far):
        if l != 0:
            return None
        return dict(zip(("ffn2_dh", "attention_bwd_dq", "d_h_mix"), reduction_stages(1, grads_so_far[1])))

    small = {n: w[n] for n in SMALL}
    loss, grad_x, grads, dlogits, dfinal = local_step(x[0], loss_target[0], layer_weights, small, w["final_norm"], fwd_hosts, bwd_hosts)
    loss = lax.psum(loss[0, 0], ("x", "y", "c"))
    for name, (make, take) in zip(("grad_half_exchange", "grad_chip_exchange", "grad_reduced_exchange"), reduction_stages(0, grads[0])):
        take(run_exchange(name, make()))
    g = {n: jnp.stack([reduced_layers[l][i] for l in range(DEPTH)]) for i, n in enumerate(BIG)}

    sg = {n: jnp.concatenate([grads[l][n] for l in range(DEPTH)], axis=0) for n in SMALL if n != "hgrn_lb_logits"}
    sg["hgrn_lb_logits"] = dlogits
    sg["ssm_conv_w"] = jnp.stack([grads[l]["ssm_conv_w"] for l in range(DEPTH)])
    sg["final_norm"] = dfinal
    spack = _pad_rows(jnp.concatenate([sg[n].reshape(-1) for n in SMALL_PACK]), 128)
    ssum = sum_devices(device_allgather("small_grads_allgather", spack)).reshape(-1)
    off = 0
    for n in SMALL_PACK:
        size = sg[n].size
        g[n] = ssum[off:off + size].reshape(sg[n].shape)
        off += size
    shard_cols = w["ssm_conv_w"].shape[-1]
    g["ssm_conv_w"] = lax.dynamic_slice_in_dim(g["ssm_conv_w"], chip * shard_cols, shard_cols, axis=2)
    g = {n: g[n].reshape(w[n].shape) for n in TWIN_WEIGHTS}

    upd = {n: adamw("adamw_" + n, w[n], g[n], m[n], v[n]) for n in TWIN_WEIGHTS}
    return (loss, grad_x[None], *[g[n] for n in TWIN_WEIGHTS], *[upd[n][0] for n in TWIN_WEIGHTS],
            *[upd[n][1] for n in TWIN_WEIGHTS], *[upd[n][2] for n in TWIN_WEIGHTS])
```

```python
import functools
import math
from typing import Callable, NamedTuple

import jax
import jax.numpy as jnp
import numpy as np
from jax import lax
from jax.experimental import pallas as pl
from jax.experimental.pallas import tpu as pltpu

F32 = jnp.float32
BF16 = jnp.bfloat16
MESH = pl.DeviceIdType.MESH

D_MODEL = 2048
DEPTH = 2
CHUNK = 64
EPS = 1e-6
MLA_HEADS, MLA_Q_RANK, MLA_KV_RANK, MLA_NOPE, MLA_ROPE, MLA_V = 16, 512, 512, 128, 64, 128
ROPE_THETA = 10000.0
HG_HEADS, HG_DK = 16, 128
HG_WIDTH = HG_HEADS * HG_DK
SSM_INNER, SSM_HEADDIM, SSM_HEADS, SSM_GROUPS, SSM_STATE, SSM_CONV = 4096, 64, 64, 8, 128, 4
SSM_CONV_DIM = SSM_INNER + 2 * SSM_GROUPS * SSM_STATE
D_FF = 5632
IN_DIM = 25728
ADAM_LR, ADAM_B1, ADAM_B2, ADAM_EPS, ADAM_WD, ADAM_STEP = 0.001, 0.9, 0.999, 1e-08, 0.01, 10

P_QLAT, P_KVLAT, P_HQ, P_HF, P_HI, P_HGATE, P_Z, P_XBC, P_GATES, P_KPEDT = (
    0, 512, 1024, 3072, 5120, 7168, 9216, 13312, 19456, 25600)
IN_PAD = 26624

VMEM_LIMIT_V7X = 48 << 20
SSD_Q = 256
HG_HB = 8
assert all(off % (HG_HB * HG_DK) == 0 for off in (P_HQ, P_HF, P_HI)) and HG_HEADS % HG_HB == 0
NEG = -1e30


def _cp(*sem):
    return pltpu.CompilerParams(dimension_semantics=sem, vmem_limit_bytes=VMEM_LIMIT_V7X)


def _fit(n, pref):
    if n <= pref:
        return n
    for t in range(pref, 0, -128):
        if n % t == 0:
            return t
    raise ValueError((n, pref))


_DIMS = {"nn": (((1,), (0,)), ((), ())), "nt": (((1,), (1,)), ((), ())), "tn": (((0,), (0,)), ((), ()))}


def _dot(a, b, mode):
    return lax.dot_general(a.astype(BF16), b.astype(BF16), _DIMS[mode], preferred_element_type=F32)


def matmul(name, a, b, mode="nn", out_dtype=F32, alpha=1.0, res=None, tm=1024, tn=1024, tk=2048, side=None):
    if mode == "nn":
        (M, K), (K2, N) = a.shape, b.shape
    elif mode == "nt":
        (M, K), (N, K2) = a.shape, b.shape
    else:
        (K, M), (K2, N) = a.shape, b.shape
    assert K == K2, (name, a.shape, b.shape, mode)
    tm, tn, tk = _fit(M, tm), _fit(N, tn), _fit(K, tk)
    nk = K // tk
    a_spec = pl.BlockSpec((tk, tm), lambda i, j, k: (k, i)) if mode == "tn" else pl.BlockSpec((tm, tk), lambda i, j, k: (i, k))
    b_spec = pl.BlockSpec((tn, tk), lambda i, j, k: (j, k)) if mode == "nt" else pl.BlockSpec((tk, tn), lambda i, j, k: (k, j))
    o_spec = pl.BlockSpec((tm, tn), lambda i, j, k: (i, j))
    has_res = res is not None

    def body(*refs):
        a_ref, b_ref = refs[0], refs[1]
        o_ref = refs[3] if has_res else refs[2]

        def finish(v):
            if alpha != 1.0:
                v = v * alpha
            if has_res:
                v = v + refs[2][...].astype(F32)
            o_ref[...] = v.astype(o_ref.dtype)

        if nk == 1:
            finish(_dot(a_ref[...], b_ref[...], mode))
            return
        acc = refs[-1]
        k = pl.program_id(2)

        @pl.when(k == 0)
        def _():
            acc[...] = _dot(a_ref[...], b_ref[...], mode)

        @pl.when(jnp.logical_and(k > 0, k < nk - 1))
        def _():
            acc[...] += _dot(a_ref[...], b_ref[...], mode)

        @pl.when(k == nk - 1)
        def _():
            finish(acc[...] + _dot(a_ref[...], b_ref[...], mode))

    ins = [a, b] + ([res] if has_res else [])
    return _pcall(name, body, (M // tm, N // tn, nk), [a_spec, b_spec] + ([o_spec] if has_res else []), o_spec,
                  jax.ShapeDtypeStruct((M, N), out_dtype), [pltpu.VMEM((tm, tn), F32)] if nk > 1 else [],
                  ("parallel", "parallel", "arbitrary"), ins, side)


def _row_specs(rows, consts, tm):
    specs = []
    for arr, w, off in rows:
        specs.append(pl.BlockSpec((tm, w), functools.partial(lambda j, i, off: (i, off + j), off=off)))
    for arr, w, off in consts:
        specs.append(pl.BlockSpec((arr.shape[0], w), functools.partial(lambda j, i, off: (0, off + j), off=off)))
    return specs


def rowwise(name, fn, rows, consts, outs, n_groups=1, tm=512):
    S = rows[0][0].shape[0]
    tm = _fit(S, tm)
    n_in = len(rows) + len(consts)

    def body(*refs):
        vals = fn(*[r[...].astype(F32) for r in refs[:n_in]])
        for o_ref, v in zip(refs[n_in:], vals):
            o_ref[...] = v.astype(o_ref.dtype)

    return pl.pallas_call(
        body, name=name, grid=(n_groups, S // tm),
        in_specs=_row_specs(rows, consts, tm),
        out_specs=[pl.BlockSpec((tm, w), lambda j, i: (i, j)) for _, w, _ in outs],
        out_shape=[jax.ShapeDtypeStruct((S, W), dt) for W, _, dt in outs],
        compiler_params=_cp("parallel", "parallel"),
    )(*[r[0] for r in rows], *[c[0] for c in consts])


def rowwise_bwd(name, fn, rows, consts, cts, row_grads, const_grads, adds=None, n_groups=1, tm=512):
    S = rows[0][0].shape[0]
    tm = _fit(S, tm)
    adds = adds or {}
    add_idx = list(adds)
    n_r, n_c, n_ct, n_add = len(rows), len(consts), len(cts), len(adds)

    def body(*refs):
        ins = [r[...].astype(F32) for r in refs[:n_r + n_c]]
        ct = tuple(r[...].astype(F32) for r in refs[n_r + n_c:n_r + n_c + n_ct])
        add_refs = refs[n_r + n_c + n_ct:n_r + n_c + n_ct + n_add]
        out_refs = refs[n_r + n_c + n_ct + n_add:]
        _, vjp = jax.vjp(fn, *ins)
        g = list(vjp(ct))
        for a_ref, idx in zip(add_refs, add_idx):
            g[idx] = g[idx] + a_ref[...].astype(F32)
        for q, (idx, _, _) in enumerate(row_grads):
            out_refs[q][...] = g[idx].astype(out_refs[q].dtype)
        first = pl.program_id(1) == 0
        for q, idx in enumerate(const_grads):
            o_ref = out_refs[len(row_grads) + q]

            @pl.when(first)
            def _(o_ref=o_ref):
                o_ref[...] = jnp.zeros_like(o_ref)

            o_ref[...] += g[n_r + idx]

    in_specs = _row_specs(rows, consts, tm)
    in_specs += [pl.BlockSpec((tm, w), lambda j, i: (i, j)) for _, w in cts]
    in_specs += [pl.BlockSpec((tm, rows[idx][1]), lambda j, i: (i, j)) for idx in add_idx]
    out_specs = [pl.BlockSpec((tm, rows[idx][1]), lambda j, i: (i, j)) for idx, _, _ in row_grads]
    out_specs += [pl.BlockSpec((consts[idx][0].shape[0], consts[idx][1]), lambda j, i: (0, j)) for idx in const_grads]
    out_shape = [jax.ShapeDtypeStruct((S, W), dt) for _, W, dt in row_grads]
    out_shape += [jax.ShapeDtypeStruct((consts[idx][0].shape[0], consts[idx][1] * n_groups), F32) for idx in const_grads]
    return pl.pallas_call(
        body, name=name, grid=(n_groups, S // tm), in_specs=in_specs, out_specs=out_specs, out_shape=out_shape,
        compiler_params=_cp("parallel", "arbitrary"),
    )(*[r[0] for r in rows], *[c[0] for c in consts], *[c[0] for c in cts], *adds.values())


def f_rmsnorm(x, w):
    return (x * lax.rsqrt(jnp.mean(x * x, axis=-1, keepdims=True) + EPS) * w,)


def f_swiglu(g, u):
    return (jax.nn.silu(g) * u,)


def f_hgrn_out(o, g, w):
    return (o * lax.rsqrt(jnp.mean(o * o, axis=-1, keepdims=True) + EPS) * w * jax.nn.silu(g),)


def f_ssm_out(y, z, w):
    y = y * jax.nn.silu(z)
    return (y * lax.rsqrt(jnp.mean(y * y, axis=-1, keepdims=True) + EPS) * w,)


def f_merge(ga, gb, gc, ya, yb, yc):
    return (jax.nn.sigmoid(ga) * ya + jax.nn.sigmoid(gb) * yb + jax.nn.sigmoid(gc) * yc,)


def f_lower_bounds(logits):
    p = jax.nn.softmax(logits, axis=0)
    rows = [jnp.zeros_like(p[0:1])]
    for l in range(1, DEPTH):
        rows.append(rows[-1] + p[l:l + 1])
    return (jnp.concatenate(rows, axis=0),)


def loss_head(x, target, w, tm=512):
    S, D = x.shape
    tm = _fit(S, tm)

    def loss_fn(xb, wb, tb):
        (y,) = f_rmsnorm(xb, wb)
        return 0.5 * jnp.sum(jnp.mean(jnp.square(y - tb), axis=-1))

    def body(x_ref, t_ref, w_ref, dx_ref, dxb_ref, dw_ref, loss_ref):
        @pl.when(pl.program_id(0) == 0)
        def _():
            dw_ref[...] = jnp.zeros_like(dw_ref)
            loss_ref[...] = jnp.zeros_like(loss_ref)

        l, (dx, dw) = jax.value_and_grad(loss_fn, argnums=(0, 1))(x_ref[...], w_ref[...], t_ref[...])
        dx_ref[...] = dx
        dxb_ref[...] = dx.astype(BF16)
        dw_ref[...] += dw
        loss_ref[...] += jnp.full(loss_ref.shape, l, F32)

    row = pl.BlockSpec((tm, D), lambda i: (i, 0))
    vec = pl.BlockSpec((1, D), lambda i: (0, 0))
    return pl.pallas_call(
        body, name="loss_head", grid=(S // tm,), in_specs=[row, row, vec],
        out_specs=[row, row, vec, pl.BlockSpec((1, 128), lambda i: (0, 0))],
        out_shape=[jax.ShapeDtypeStruct((S, D), F32), jax.ShapeDtypeStruct((S, D), BF16), jax.ShapeDtypeStruct((1, D), F32),
                   jax.ShapeDtypeStruct((1, 128), F32)],
        compiler_params=_cp("arbitrary"),
    )(x, target, w)


def rope_tables(S):
    inv = 1.0 / (ROPE_THETA ** (jnp.arange(0, MLA_ROPE, 2, dtype=F32) / MLA_ROPE))
    ang = jnp.arange(S, dtype=F32)[:, None] * inv[None, :]
    c, s = jnp.cos(ang), jnp.sin(ang)
    return jnp.tile(c, (1, 4)), jnp.concatenate([-s, s, -s, s], axis=1)


def _rope128(x, cosf, sinf):
    lane = lax.broadcasted_iota(jnp.int32, x.shape, 1)
    swapped = jnp.where((lane & 32) == 0, pltpu.roll(x, 96, 1), pltpu.roll(x, 32, 1))
    return x * cosf + swapped * sinf


def mla_q_prep(q, cosf, sinf, tm=512):
    S = q.shape[0]
    tm = _fit(S, tm)

    def body(q_ref, c_ref, s_ref, o_ref):
        x = q_ref[...]
        r = _rope128(x[:, 256:384], c_ref[...], s_ref[...])
        lane = lax.broadcasted_iota(jnp.int32, r.shape, 1)
        z = jnp.zeros_like(r)
        o_ref[...] = jnp.concatenate(
            [x[:, 0:128], jnp.where(lane < 64, r, z), x[:, 128:256], jnp.where(lane >= 64, r, z)], axis=1).astype(BF16)

    tab = pl.BlockSpec((tm, 128), lambda j, i: (i, 0))
    return pl.pallas_call(
        body, name="mla_q_prep", grid=(8, S // tm),
        in_specs=[pl.BlockSpec((tm, 384), lambda j, i: (i, j)), tab, tab],
        out_specs=pl.BlockSpec((tm, 512), lambda j, i: (i, j)),
        out_shape=jax.ShapeDtypeStruct((S, 4096), BF16), compiler_params=_cp("parallel", "parallel"),
    )(q, cosf, sinf)


def mla_q_prep_bwd(dqc, cosf, sinf, tm=512):
    S = dqc.shape[0]
    tm = _fit(S, tm)

    def body(d_ref, c_ref, s_ref, o_ref):
        d = d_ref[...]
        lane = lax.broadcasted_iota(jnp.int32, (tm, 128), 1)
        dr = jnp.where(lane < 64, d[:, 128:256], d[:, 384:512])
        o_ref[...] = jnp.concatenate([d[:, 0:128], d[:, 256:384], _rope128(dr, c_ref[...], -s_ref[...])], axis=1).astype(BF16)

    tab = pl.BlockSpec((tm, 128), lambda j, i: (i, 0))
    return pl.pallas_call(
        body, name="mla_q_prep_bwd", grid=(8, S // tm),
        in_specs=[pl.BlockSpec((tm, 512), lambda j, i: (i, j)), tab, tab],
        out_specs=pl.BlockSpec((tm, 384), lambda j, i: (i, j)),
        out_shape=jax.ShapeDtypeStruct((S, 3072), BF16), compiler_params=_cp("parallel", "parallel"),
    )(dqc, cosf, sinf)


def mla_k_prep(kv, proj, cosf, sinf, tm=512):
    S = kv.shape[0]
    tm = _fit(S, tm)

    def body(kv_ref, pe_ref, c_ref, s_ref, k_ref, v_ref):
        x = kv_ref[...]
        r = _rope128(pe_ref[...], c_ref[...], s_ref[...])
        lane = lax.broadcasted_iota(jnp.int32, r.shape, 1)
        r2 = jnp.where(lane < 64, r, pltpu.roll(r, 64, 1))
        k_ref[...] = jnp.concatenate([x[:, 0:128], r2, x[:, 256:384], r2], axis=1).astype(BF16)
        v_ref[...] = jnp.concatenate([x[:, 128:256], x[:, 384:512]], axis=1).astype(BF16)

    tab = pl.BlockSpec((tm, 128), lambda j, i: (i, 0))
    return pl.pallas_call(
        body, name="mla_k_prep", grid=(8, S // tm),
        in_specs=[pl.BlockSpec((tm, 512), lambda j, i: (i, j)), pl.BlockSpec((tm, 128), lambda j, i: (i, P_KPEDT // 128)), tab, tab],
        out_specs=[pl.BlockSpec((tm, 512), lambda j, i: (i, j)), pl.BlockSpec((tm, 256), lambda j, i: (i, j))],
        out_shape=[jax.ShapeDtypeStruct((S, 4096), BF16), jax.ShapeDtypeStruct((S, 2048), BF16)],
        compiler_params=_cp("parallel", "parallel"),
    )(kv, proj, cosf, sinf)


def mla_k_prep_bwd(dkc, dv, cosf, sinf, tm=512):
    S = dkc.shape[0]
    tm = _fit(S, tm)

    def body(dk_ref, dv_ref, c_ref, s_ref, dkv_ref, dpe_ref):
        dk, dvv = dk_ref[...], dv_ref[...]
        dkv_ref[...] = jnp.concatenate([dk[:, 0:128], dvv[:, 0:128], dk[:, 256:384], dvv[:, 128:256]], axis=1).astype(BF16)
        d2 = dk[:, 128:256] + dk[:, 384:512]
        lane = lax.broadcasted_iota(jnp.int32, d2.shape, 1)
        dr = jnp.where(lane < 64, d2 + pltpu.roll(d2, 64, 1), 0.0)
        dpe = jnp.where(lane < 64, _rope128(dr, c_ref[...], -s_ref[...]), 0.0)

        @pl.when(pl.program_id(1) == 0)
        def _():
            dpe_ref[...] = jnp.zeros_like(dpe_ref)

        dpe_ref[...] += dpe

    tab = pl.BlockSpec((tm, 128), lambda i, j: (i, 0))
    return pl.pallas_call(
        body, name="mla_k_prep_bwd", grid=(S // tm, 8),
        in_specs=[pl.BlockSpec((tm, 512), lambda i, j: (i, j)), pl.BlockSpec((tm, 256), lambda i, j: (i, j)), tab, tab],
        out_specs=[pl.BlockSpec((tm, 512), lambda i, j: (i, j)), tab],
        out_shape=[jax.ShapeDtypeStruct((S, 4096), BF16), jax.ShapeDtypeStruct((S, 128), F32)],
        compiler_params=_cp("parallel", "arbitrary"),
    )(dkc, dv, cosf, sinf)


ATT_SCALE = (MLA_NOPE + MLA_ROPE) ** -0.5


def _att_scores(q, k, qi, ki, t):
    s = _dot(q, k, "nt") * ATT_SCALE
    rows = qi * t + lax.broadcasted_iota(jnp.int32, (t, t), 0)
    cols = ki * t + lax.broadcasted_iota(jnp.int32, (t, t), 1)
    shift = CHUNK.bit_length() - 1
    return jnp.where((cols >> shift) <= (rows >> shift), s, NEG)


def attention_fwd(qc, kc, vb, t=2048, side=None):
    S = qc.shape[0]
    t = _fit(S, t)
    n = S // t

    def body(q_ref, k_ref, v_ref, o_ref, lse_ref, m_s, l_s, acc_s):
        qi, ki = pl.program_id(1), pl.program_id(2)

        @pl.when(ki == 0)
        def _():
            m_s[...] = jnp.full_like(m_s, NEG)
            l_s[...] = jnp.zeros_like(l_s)
            acc_s[...] = jnp.zeros_like(acc_s)

        @pl.when(ki <= qi)
        def _():
            s = _att_scores(q_ref[...], k_ref[...], qi, ki, t)
            m_prev = m_s[...]
            m_new = jnp.maximum(m_prev, jnp.max(s, axis=1, keepdims=True))
            alpha = jnp.exp(m_prev - m_new)
            p = jnp.exp(s - m_new)
            l_s[...] = alpha * l_s[...] + jnp.sum(p, axis=1, keepdims=True)
            acc_s[...] = alpha * acc_s[...] + _dot(p, v_ref[...], "nn")
            m_s[...] = m_new

        @pl.when(ki == qi)
        def _():
            o_ref[...] = (acc_s[...] / l_s[...]).astype(o_ref.dtype)
            lse_ref[...] = m_s[...] + jnp.log(l_s[...])

    return _pcall(
        "attention_fwd", body, (MLA_HEADS, n, n),
        [pl.BlockSpec((t, 256), lambda h, i, j: (i, h)),
         pl.BlockSpec((t, 256), lambda h, i, j: (jnp.minimum(i, j), h)),
         pl.BlockSpec((t, 128), lambda h, i, j: (jnp.minimum(i, j), h))],
        [pl.BlockSpec((t, 128), lambda h, i, j: (i, h)), pl.BlockSpec((None, t, 1), lambda h, i, j: (h, i, 0))],
        [jax.ShapeDtypeStruct((S, 2048), BF16), jax.ShapeDtypeStruct((MLA_HEADS, S, 1), F32)],
        [pltpu.VMEM((t, 1), F32), pltpu.VMEM((t, 1), F32), pltpu.VMEM((t, 128), F32)],
        ("parallel", "parallel", "arbitrary"), (qc, kc, vb), side)


def attention_delta(o, do, t=1024):
    S = o.shape[0]
    t = _fit(S, t)

    def body(o_ref, do_ref, d_ref):
        d_ref[...] = jnp.sum(o_ref[...].astype(F32) * do_ref[...].astype(F32), axis=1, keepdims=True)

    blk = pl.BlockSpec((t, 128), lambda h, i: (i, h))
    return pl.pallas_call(
        body, name="attention_delta", grid=(MLA_HEADS, S // t), in_specs=[blk, blk],
        out_specs=pl.BlockSpec((None, t, 1), lambda h, i: (h, i, 0)),
        out_shape=jax.ShapeDtypeStruct((MLA_HEADS, S, 1), F32), compiler_params=_cp("parallel", "parallel"),
    )(o, do)


def attention_bwd_dq(qc, kc, vb, do, lse, delta, t=1024, side=None):
    S = qc.shape[0]
    t = _fit(S, t)
    n = S // t

    def body(q_ref, k_ref, v_ref, do_ref, lse_ref, dl_ref, dq_ref, acc):
        qi, ki = pl.program_id(1), pl.program_id(2)

        @pl.when(ki == 0)
        def _():
            acc[...] = jnp.zeros_like(acc)

        @pl.when(ki <= qi)
        def _():
            p = jnp.exp(_att_scores(q_ref[...], k_ref[...], qi, ki, t) - lse_ref[...])
            dp = _dot(do_ref[...], v_ref[...], "nt")
            ds = p * (dp - dl_ref[...])
            acc[...] += _dot(ds, k_ref[...], "nn")

        @pl.when(ki == qi)
        def _():
            dq_ref[...] = acc[...] * ATT_SCALE

    stat = pl.BlockSpec((None, t, 1), lambda h, i, j: (h, i, 0))
    return _pcall(
        "attention_bwd_dq", body, (MLA_HEADS, n, n),
        [pl.BlockSpec((t, 256), lambda h, i, j: (i, h)),
         pl.BlockSpec((t, 256), lambda h, i, j: (jnp.minimum(i, j), h)),
         pl.BlockSpec((t, 128), lambda h, i, j: (jnp.minimum(i, j), h)),
         pl.BlockSpec((t, 128), lambda h, i, j: (i, h)), stat, stat],
        pl.BlockSpec((t, 256), lambda h, i, j: (i, h)), jax.ShapeDtypeStruct((S, 4096), F32),
        [pltpu.VMEM((t, 256), F32)], ("parallel", "parallel", "arbitrary"), (qc, kc, vb, do, lse, delta), side)


def attention_bwd_dkv(qc, kc, vb, do, lse, delta, t=2048):
    S = qc.shape[0]
    t = _fit(S, t)
    n = S // t

    def body(q_ref, k_ref, v_ref, do_ref, lse_ref, dl_ref, dk_ref, dv_ref, dk_acc, dv_acc):
        ki, qi = pl.program_id(1), pl.program_id(2)

        @pl.when(qi == 0)
        def _():
            dk_acc[...] = jnp.zeros_like(dk_acc)
            dv_acc[...] = jnp.zeros_like(dv_acc)

        @pl.when(qi >= ki)
        def _():
            s = _dot(k_ref[...], q_ref[...], "nt") * ATT_SCALE
            krow = ki * t + lax.broadcasted_iota(jnp.int32, (t, t), 0)
            qcol = qi * t + lax.broadcasted_iota(jnp.int32, (t, t), 1)
            shift = CHUNK.bit_length() - 1
            p = jnp.exp(jnp.where((krow >> shift) <= (qcol >> shift), s, NEG) - lse_ref[...])
            dv_acc[...] += _dot(p, do_ref[...], "nn")
            dp = _dot(v_ref[...], do_ref[...], "nt")
            ds = p * (dp - dl_ref[...])
            dk_acc[...] += _dot(ds, q_ref[...], "nn")

        @pl.when(qi == n - 1)
        def _():
            dk_ref[...] = dk_acc[...] * ATT_SCALE
            dv_ref[...] = dv_acc[...]

    stat = pl.BlockSpec((None, 1, t), lambda h, j, i: (h, 0, jnp.maximum(i, j)))
    return pl.pallas_call(
        body, name="attention_bwd_dkv", grid=(MLA_HEADS, n, n),
        in_specs=[pl.BlockSpec((t, 256), lambda h, j, i: (jnp.maximum(i, j), h)),
                  pl.BlockSpec((t, 256), lambda h, j, i: (j, h)),
                  pl.BlockSpec((t, 128), lambda h, j, i: (j, h)),
                  pl.BlockSpec((t, 128), lambda h, j, i: (jnp.maximum(i, j), h)), stat, stat],
        out_specs=[pl.BlockSpec((t, 256), lambda h, j, i: (j, h)), pl.BlockSpec((t, 128), lambda h, j, i: (j, h))],
        out_shape=[jax.ShapeDtypeStruct((S, 4096), F32), jax.ShapeDtypeStruct((S, 2048), F32)],
        scratch_shapes=[pltpu.VMEM((t, 256), F32), pltpu.VMEM((t, 128), F32)],
        compiler_params=_cp("parallel", "parallel", "arbitrary"),
    )(qc, kc, vb, do, lse.reshape(MLA_HEADS, 1, S), delta.reshape(MLA_HEADS, 1, S))


def _scan_rows(x, reverse):
    n = x.shape[0]
    row = lax.broadcasted_iota(jnp.int32, x.shape, 0)
    d = 1
    while d < n:
        if reverse:
            x = x + jnp.where(row < n - d, pltpu.roll(x, n - d, 0), 0.0)
        else:
            x = x + jnp.where(row >= d, pltpu.roll(x, d, 0), 0.0)
        d *= 2
    return x


@jax.custom_vjp
def cumsum_rows(x):
    return _scan_rows(x, False)


cumsum_rows.defvjp(lambda x: (_scan_rows(x, False), None), lambda _, g: (_scan_rows(g, True),))


def hgrn_chunk(q_in, f_in, v, lb, state_t):
    f = lb + (1.0 - lb) * jax.nn.sigmoid(f_in)
    q = jax.nn.silu(q_in) * HG_DK ** -0.5
    k = 1.0 - f
    b = cumsum_rows(jnp.log(f))
    b_last = b[CHUNK - 1:CHUNK]
    b_mid = b[CHUNK // 2 - 1:CHUNK // 2]
    r = lax.broadcasted_iota(jnp.int32, (CHUNK, CHUNK), 0)
    c = lax.broadcasted_iota(jnp.int32, (CHUNK, CHUNK), 1)
    att = jnp.where(c <= r, _dot(q * jnp.exp(b - b_mid), k * jnp.exp(b_mid - b), "nt"), 0.0)
    o = _dot(q * jnp.exp(b), state_t, "nt") + _dot(att, v, "nn")
    new_state_t = state_t * jnp.exp(b_last) + _dot(v, k * jnp.exp(b_last - b), "tn")
    return o, new_state_t


def hgrn_scan_fwd(proj, lb):
    S = proj.shape[0]
    nc = S // CHUNK
    W = HG_HB * 128

    def body(q_ref, f_ref, v_ref, lb_ref, o_ref, hst_ref, st):
        @pl.when(pl.program_id(1) == 0)
        def _():
            st[...] = jnp.zeros_like(st)

        for h in range(HG_HB):
            cs = slice(h * 128, (h + 1) * 128)
            hst_ref[h] = st[h]
            o, new = hgrn_chunk(q_ref[:, cs], f_ref[:, cs], v_ref[:, cs], lb_ref[:, cs], st[h])
            o_ref[:, cs] = o
            st[h] = new

    def seg(off):
        return pl.BlockSpec((CHUNK, W), functools.partial(lambda g, c, off: (c, off + g), off=off // W))

    return pl.pallas_call(
        body, name="hgrn_scan_fwd", grid=(HG_HEADS // HG_HB, nc),
        in_specs=[seg(P_HQ), seg(P_HF), seg(P_HI), pl.BlockSpec((1, W), lambda g, c: (0, g))],
        out_specs=[pl.BlockSpec((CHUNK, W), lambda g, c: (c, g)), pl.BlockSpec((None, HG_HB, 128, 128), lambda g, c: (c, g, 0, 0))],
        out_shape=[jax.ShapeDtypeStruct((S, HG_WIDTH), F32), jax.ShapeDtypeStruct((nc, HG_HEADS, 128, 128), F32)],
        scratch_shapes=[pltpu.VMEM((HG_HB, 128, 128), F32)],
        compiler_params=_cp("parallel", "arbitrary"),
    )(proj, proj, proj, lb)


def hgrn_scan_bwd(proj, lb, hst, do):
    S = proj.shape[0]
    nc = S // CHUNK
    W = HG_HB * 128

    def body(q_ref, f_ref, v_ref, lb_ref, hst_ref, do_ref, dq_ref, df_ref, dv_ref, dlb_ref, dst):
        @pl.when(pl.program_id(1) == 0)
        def _():
            dst[...] = jnp.zeros_like(dst)
            dlb_ref[...] = jnp.zeros_like(dlb_ref)

        for h in range(HG_HB):
            cs = slice(h * 128, (h + 1) * 128)
            _, vjp = jax.vjp(hgrn_chunk, q_ref[:, cs], f_ref[:, cs], v_ref[:, cs], lb_ref[:, cs], hst_ref[h])
            dq, df, dv, dlb, dstate = vjp((do_ref[:, cs], dst[h]))
            dq_ref[:, cs] = dq.astype(dq_ref.dtype)
            df_ref[:, cs] = df.astype(df_ref.dtype)
            dv_ref[:, cs] = dv.astype(dv_ref.dtype)
            dlb_ref[:, cs] += dlb
            dst[h] = dstate

    def seg(off):
        return pl.BlockSpec((CHUNK, W), functools.partial(lambda g, c, off: (nc - 1 - c, off + g), off=off // W))

    row = pl.BlockSpec((CHUNK, W), lambda g, c: (nc - 1 - c, g))
    vec = pl.BlockSpec((1, W), lambda g, c: (0, g))
    return pl.pallas_call(
        body, name="hgrn_scan_bwd", grid=(HG_HEADS // HG_HB, nc),
        in_specs=[seg(P_HQ), seg(P_HF), seg(P_HI), vec,
                  pl.BlockSpec((None, HG_HB, 128, 128), lambda g, c: (nc - 1 - c, g, 0, 0)), row],
        out_specs=[row, row, row, vec],
        out_shape=[jax.ShapeDtypeStruct((S, HG_WIDTH), BF16)] * 3 + [jax.ShapeDtypeStruct((1, HG_WIDTH), F32)],
        scratch_shapes=[pltpu.VMEM((HG_HB, 128, 128), F32)],
        compiler_params=_cp("parallel", "arbitrary"),
    )(proj, proj, proj, lb, hst, do)


def _silu_grad(x):
    s = jax.nn.sigmoid(x)
    return s * (1.0 + x * (1.0 - s))


def conv_fwd(proj, w, b, tm=512):
    S = proj.shape[0]
    tm = _fit(S, tm)
    G = 512
    off = P_XBC // G

    def body(cur_ref, prev_ref, w_ref, b_ref, act_ref, pre_ref):
        prev = prev_ref[...] * (pl.program_id(1) > 0).astype(F32)
        ext = jnp.concatenate([prev, cur_ref[...]], axis=0)
        n = tm + 8
        acc = b_ref[...] + jnp.zeros((tm, G), F32)
        for j in range(SSM_CONV):
            acc = acc + w_ref[j:j + 1, :] * pltpu.roll(ext, (n - 5 - j) % n, 0)[0:tm]
        pre_ref[...] = acc
        act_ref[...] = jax.nn.silu(acc)

    out = pl.BlockSpec((tm, G), lambda j, i: (i, j))
    return pl.pallas_call(
        body, name="conv_fwd", grid=(SSM_CONV_DIM // G, S // tm),
        in_specs=[pl.BlockSpec((tm, G), lambda j, i: (i, off + j)),
                  pl.BlockSpec((8, G), lambda j, i: (jnp.maximum(i * (tm // 8) - 1, 0), off + j)),
                  pl.BlockSpec((SSM_CONV, G), lambda j, i: (0, j)), pl.BlockSpec((1, G), lambda j, i: (0, j))],
        out_specs=[out, out], out_shape=[jax.ShapeDtypeStruct((S, SSM_CONV_DIM), F32)] * 2,
        compiler_params=_cp("parallel", "parallel"),
    )(proj, proj, w, b)


def conv_bwd(proj, pre, dact, w, tm=512):
    S = proj.shape[0]
    tm = _fit(S, tm)
    G = 512
    off = P_XBC // G
    nb = S // tm

    def body(x_ref, xp_ref, pre_ref, pren_ref, d_ref, dn_ref, w_ref, dx_ref, dw_ref, db_ref):
        i = pl.program_id(1)
        n = tm + 8
        dpre = d_ref[...] * _silu_grad(pre_ref[...])
        dpre_next = dn_ref[...] * _silu_grad(pren_ref[...]) * (i < nb - 1).astype(F32)
        dext = jnp.concatenate([dpre, dpre_next], axis=0)
        xext = jnp.concatenate([xp_ref[...] * (i > 0).astype(F32), x_ref[...]], axis=0)
        dx = jnp.zeros((tm, G), F32)
        dws = []
        for j in range(SSM_CONV):
            dx = dx + w_ref[j:j + 1, :] * pltpu.roll(dext, (n - (3 - j)) % n, 0)[0:tm]
            dws.append(jnp.sum(dpre * pltpu.roll(xext, (n - 5 - j) % n, 0)[0:tm], axis=0, keepdims=True))
        dx_ref[...] = dx.astype(dx_ref.dtype)

        @pl.when(i == 0)
        def _():
            dw_ref[...] = jnp.zeros_like(dw_ref)
            db_ref[...] = jnp.zeros_like(db_ref)

        dw_ref[...] += jnp.concatenate(dws, axis=0)
        db_ref[...] += jnp.sum(dpre, axis=0, keepdims=True)

    cur = pl.BlockSpec((tm, G), lambda j, i: (i, j))
    nxt = pl.BlockSpec((8, G), lambda j, i: (jnp.minimum((i + 1) * (tm // 8), S // 8 - 1), j))
    return pl.pallas_call(
        body, name="conv_bwd", grid=(SSM_CONV_DIM // G, nb),
        in_specs=[pl.BlockSpec((tm, G), lambda j, i: (i, off + j)),
                  pl.BlockSpec((8, G), lambda j, i: (jnp.maximum(i * (tm // 8) - 1, 0), off + j)),
                  cur, nxt, cur, nxt, pl.BlockSpec((SSM_CONV, G), lambda j, i: (0, j))],
        out_specs=[cur, pl.BlockSpec((SSM_CONV, G), lambda j, i: (0, j)), pl.BlockSpec((1, G), lambda j, i: (0, j))],
        out_shape=[jax.ShapeDtypeStruct((S, SSM_CONV_DIM), BF16), jax.ShapeDtypeStruct((SSM_CONV, SSM_CONV_DIM), F32),
                   jax.ShapeDtypeStruct((1, SSM_CONV_DIM), F32)],
        compiler_params=_cp("parallel", "arbitrary"),
    )(proj, proj, pre, pre, dact, dact, w)


def _eye_dot(a, mode):
    Q = a.shape[0] if mode == "tn" else a.shape[1]
    eye = (lax.broadcasted_iota(jnp.int32, (Q, Q), 0) == lax.broadcasted_iota(jnp.int32, (Q, Q), 1)).astype(BF16)
    hi = a.astype(BF16)
    r1 = a - hi.astype(F32)
    mid = r1.astype(BF16)
    lo = (r1 - mid.astype(F32)).astype(BF16)
    if mode == "tn":
        return sum(lax.dot_general(p, eye, _DIMS["tn"], preferred_element_type=F32) for p in (hi, mid, lo))
    return sum(lax.dot_general(eye, p, _DIMS["nt"], preferred_element_type=F32) for p in (hi, mid, lo))


@jax.custom_vjp
def _transpose_exact(a):
    return _eye_dot(a, "tn")


_transpose_exact.defvjp(lambda a: (_eye_dot(a, "tn"), None), lambda _, g: (_eye_dot(g, "nt"),))


def ssd_decay_inputs(dtr, bias, alog):
    dt = jax.nn.softplus(dtr + bias)
    acum = cumsum_rows(dt * -jnp.exp(alog))
    return dt, acum, _transpose_exact(acum)


def ssd_head(x, cb, bm, cm, dt, acum, a_s, dsk, h_prev):
    Q = x.shape[0]
    r = lax.broadcasted_iota(jnp.int32, (Q, Q), 0)
    c = lax.broadcasted_iota(jnp.int32, (Q, Q), 1)
    a_l = jnp.broadcast_to(acum, (Q, Q))
    decay = jnp.where(c <= r, jnp.exp(jnp.minimum(a_l - a_s, 0.0)), 0.0)
    xdt = x * dt
    y_diag = _dot(cb * decay, xdt, "nn")
    a_last = acum[Q - 1:Q]
    states = _dot(xdt * jnp.exp(a_last - acum), bm, "tn")
    h_new = h_prev * jnp.exp(a_last) + states
    y_off = _dot(cm, h_prev, "nt") * jnp.exp(acum)
    return y_diag + y_off + x * dsk, h_new


def _ssd_specs(S, rev):
    Q = _fit(S, SSD_Q)
    nc = S // Q
    ci = (lambda c: nc - 1 - c) if rev else (lambda c: c)
    x = pl.BlockSpec((Q, 512), lambda g, c: (ci(c), g))
    bm = pl.BlockSpec((Q, 128), lambda g, c: (ci(c), SSM_INNER // 128 + g))
    cm = pl.BlockSpec((Q, 128), lambda g, c: (ci(c), SSM_INNER // 128 + SSM_GROUPS + g))
    dtr = pl.BlockSpec((None, Q, 8), lambda g, c: (g, ci(c), 0))
    par = pl.BlockSpec((None, 1, 8), lambda g, c: (g, 0, 0))
    hs = pl.BlockSpec((None, 8, SSM_HEADDIM, SSM_STATE), lambda g, c: (ci(c), g, 0, 0))
    return Q, nc, x, bm, cm, dtr, par, hs


def ssd_scan_fwd(act, dtr, bias, alog, dsk):
    S = act.shape[0]
    Q, nc, x_s, bm_s, cm_s, dtr_s, par_s, hs_s = _ssd_specs(S, False)

    def body(x_ref, bm_ref, cm_ref, dtr_ref, b_ref, a_ref, d_ref, y_ref, hs_ref, st, dt_s, ac_s, act_s):
        @pl.when(pl.program_id(1) == 0)
        def _():
            st[...] = jnp.zeros_like(st)

        dt_s[...], ac_s[...], act_s[...] = ssd_decay_inputs(dtr_ref[...], b_ref[...], a_ref[...])
        bm, cm = bm_ref[...], cm_ref[...]
        cb = _dot(cm, bm, "nt")
        for j in range(8):
            hs_ref[j] = st[j]
            y, h_new = ssd_head(x_ref[:, j * 64:(j + 1) * 64], cb, bm, cm, dt_s[:, j:j + 1], ac_s[:, j:j + 1], act_s[j:j + 1, :],
                                d_ref[:, j:j + 1], st[j])
            y_ref[:, j * 64:(j + 1) * 64] = y
            st[j] = h_new

    return pl.pallas_call(
        body, name="ssd_scan_fwd", grid=(SSM_GROUPS, nc),
        in_specs=[x_s, bm_s, cm_s, dtr_s, par_s, par_s, par_s], out_specs=[x_s, hs_s],
        out_shape=[jax.ShapeDtypeStruct((S, SSM_INNER), F32), jax.ShapeDtypeStruct((nc, SSM_HEADS, SSM_HEADDIM, SSM_STATE), F32)],
        scratch_shapes=[pltpu.VMEM((8, SSM_HEADDIM, SSM_STATE), F32), pltpu.VMEM((Q, 8), F32), pltpu.VMEM((Q, 8), F32),
                        pltpu.VMEM((8, Q), F32)],
        compiler_params=_cp("parallel", "arbitrary"),
    )(act, act, act, dtr, bias, alog, dsk)


def ssd_scan_bwd(act, dtr, bias, alog, dsk, hs, dy):
    S = act.shape[0]
    Q, nc, x_s, bm_s, cm_s, dtr_s, par_s, hs_s = _ssd_specs(S, True)
    g_s = pl.BlockSpec((Q, 128), lambda g, c: (nc - 1 - c, g))

    def body(x_ref, bm_ref, cm_ref, dtr_ref, b_ref, a_ref, d_ref, hs_ref, dy_ref,
             dx_ref, dbm_ref, dcm_ref, ddtr_ref, db_ref, da_ref, dd_ref, dst, dt_s, ac_s, ddt_s, dac_s, act_s, dact_s):
        @pl.when(pl.program_id(1) == 0)
        def _():
            dst[...] = jnp.zeros_like(dst)
            db_ref[...] = jnp.zeros_like(db_ref)
            da_ref[...] = jnp.zeros_like(da_ref)
            dd_ref[...] = jnp.zeros_like(dd_ref)

        (dt_s[...], ac_s[...], act_s[...]), decay_vjp = jax.vjp(ssd_decay_inputs, dtr_ref[...], b_ref[...], a_ref[...])
        bm, cm = bm_ref[...], cm_ref[...]
        cb = _dot(cm, bm, "nt")
        dcb = jnp.zeros((Q, Q), F32)
        dbm = jnp.zeros((Q, 128), F32)
        dcm = jnp.zeros((Q, 128), F32)
        for j in range(8):
            cs = slice(j * 64, (j + 1) * 64)
            one = slice(j, j + 1)
            _, vjp = jax.vjp(ssd_head, x_ref[:, cs], cb, bm, cm, dt_s[:, one], ac_s[:, one], act_s[one, :], d_ref[:, one], hs_ref[j])
            dx, gcb, gb, gc, gdt, gac, gact, gdsk, gh = vjp((dy_ref[:, cs], dst[j]))
            dact_s[one, :] = gact
            dx_ref[:, cs] = dx
            dcb = dcb + gcb
            dbm = dbm + gb
            dcm = dcm + gc
            ddt_s[:, one] = gdt
            dac_s[:, one] = gac
            dd_ref[:, one] += gdsk
            dst[j] = gh
        dbm_ref[...] = dbm + _dot(dcb, cm, "tn")
        dcm_ref[...] = dcm + _dot(dcb, bm, "nn")
        ddtr, dbias, dalog = decay_vjp((ddt_s[...], dac_s[...], dact_s[...]))
        ddtr_ref[...] = ddtr
        db_ref[...] += dbias
        da_ref[...] += dalog

    return pl.pallas_call(
        body, name="ssd_scan_bwd", grid=(SSM_GROUPS, nc),
        in_specs=[x_s, bm_s, cm_s, dtr_s, par_s, par_s, par_s, hs_s, x_s],
        out_specs=[x_s, g_s, g_s, dtr_s, par_s, par_s, par_s],
        out_shape=[jax.ShapeDtypeStruct((S, SSM_INNER), F32), jax.ShapeDtypeStruct((S, 1024), F32), jax.ShapeDtypeStruct((S, 1024), F32),
                   jax.ShapeDtypeStruct((SSM_GROUPS, S, 8), F32)] + [jax.ShapeDtypeStruct((SSM_GROUPS, 1, 8), F32)] * 3,
        scratch_shapes=[pltpu.VMEM((8, SSM_HEADDIM, SSM_STATE), F32)] + [pltpu.VMEM((Q, 8), F32)] * 4 + [pltpu.VMEM((8, Q), F32)] * 2,
        compiler_params=_cp("parallel", "arbitrary"),
    )(act, act, act, dtr, bias, alog, dsk, hs, dy)


def _w_in_to_local(w):
    parts = [w[..., 0:1024], w[..., 1088:19520], w[..., 19584:25728], w[..., 1024:1088], w[..., 19520:19584],
             jnp.zeros(w.shape[:-1] + (IN_PAD - IN_DIM,), w.dtype)]
    return jnp.concatenate(parts, axis=-1)


def _w_in_from_local(g):
    return jnp.concatenate([g[..., 0:1024], g[..., 25600:25664], g[..., 1024:19456], g[..., 25664:25728], g[..., 19456:25600]], axis=-1)


def _w_uq_to_local(w):
    lead = w.shape[:-1]
    w = w.reshape(lead + (MLA_HEADS, 192))
    nope = w[..., :128].reshape(lead + (8, 256))
    rope = w[..., 128:].reshape(lead + (8, 128))
    return jnp.concatenate([nope, rope], axis=-1).reshape(lead + (3072,))


def _w_uq_from_local(g):
    lead = g.shape[:-1]
    g = g.reshape(lead + (8, 384))
    nope = g[..., :256].reshape(lead + (MLA_HEADS, 128))
    rope = g[..., 256:].reshape(lead + (MLA_HEADS, 64))
    return jnp.concatenate([nope, rope], axis=-1).reshape(lead + (3072,))


BIG = ("ffn1_wi", "ffn1_wo", "w_in", "mla_w_uq", "mla_w_ukv", "w_o_mla", "w_o_hgrn", "w_o_ssm", "w_out", "ffn2_wi", "ffn2_wo")
COL_SHARDED = ("ffn1_wi", "w_in", "mla_w_uq", "mla_w_ukv", "ffn2_wi")
FFN1, FFN2 = ("ffn1_wi", "ffn1_wo"), ("ffn2_wi", "ffn2_wo")
SMALL = ("ffn1_norm", "mix_norm", "mla_q_norm", "mla_kv_norm", "hgrn_lb_logits", "hgrn_norm", "ssm_conv_b", "ssm_a_log",
         "ssm_dt_bias", "ssm_d", "ssm_norm", "ffn2_norm")


def _carry(hosts, key, fn, *args, **kw):
    if hosts and key in hosts:
        make, take = hosts[key]
        out, side_out = fn(*args, side=make(), **kw)
        take(side_out)
        return out
    return fn(*args, **kw)


def _ffn_fwd(tag, x, norm_w, wi, wo, hosts=None):
    (h,) = rowwise(tag + "_norm", f_rmsnorm, [(x, D_MODEL, 0)], [(norm_w, D_MODEL, 0)], [(D_MODEL, D_MODEL, BF16)])
    gu = _carry(hosts, tag + "_wi", matmul, tag + "_wi", h, wi, out_dtype=BF16)
    (a,) = rowwise(tag + "_act", f_swiglu, [(gu, 512, 0), (gu, 512, D_FF // 512)], [], [(D_FF, 512, BF16)], n_groups=D_FF // 512)
    out = matmul(tag + "_wo", a, wo, alpha=0.5, res=x)
    return out, (x, h, gu, a)


def _ffn_bwd(tag, dx, dxb, saved, norm_w, wi, wo, hosts=None):
    x, h, gu, a = saved
    da = _carry(hosts, tag + "_da", matmul, tag + "_da", dxb, wo, "nt", out_dtype=BF16, alpha=0.5)
    dwo = _carry(hosts, tag + "_dwo", matmul, tag + "_dwo", a, dxb, "tn", alpha=0.5)
    dg, du = rowwise_bwd(tag + "_act_bwd", f_swiglu, [(gu, 512, 0), (gu, 512, D_FF // 512)], [], [(da, 512)],
                         [(0, D_FF, BF16), (1, D_FF, BF16)], [], n_groups=D_FF // 512)
    dgu = jnp.concatenate([dg, du], axis=1)
    dh = _carry(hosts, tag + "_dh", matmul, tag + "_dh", dgu, wi, "nt")
    dwi = _carry(hosts, tag + "_dwi", matmul, tag + "_dwi", h, dgu, "tn")
    dx_in, dxb_in, dnorm = rowwise_bwd(tag + "_norm_bwd", f_rmsnorm, [(x, D_MODEL, 0)], [(norm_w, D_MODEL, 0)], [(dh, D_MODEL)],
                                       [(0, D_MODEL, F32), (0, D_MODEL, BF16)], [0], adds={0: dx}, tm=256)
    return dx_in, dxb_in, dnorm, dwi, dwo


def _ssm_params(W):
    return [W[k].reshape(SSM_GROUPS, 1, 8) for k in ("ssm_dt_bias", "ssm_a_log", "ssm_d")]


def _dt_cols(proj):
    S = proj.shape[0]
    return proj[:, P_KPEDT + 64:P_KPEDT + 128].reshape(S, SSM_GROUPS, 8).transpose(1, 0, 2)


def _mix_fwd(x, W, lb, tabs, hosts=None):
    cosf, sinf = tabs
    (h,) = rowwise("mix_norm", f_rmsnorm, [(x, D_MODEL, 0)], [(W["mix_norm"], D_MODEL, 0)], [(D_MODEL, D_MODEL, BF16)])
    proj = _carry(hosts, "w_in", matmul, "w_in", h, W["w_in"])
    (qn,) = rowwise("q_norm", f_rmsnorm, [(proj, 512, 0)], [(W["mla_q_norm"], 512, 0)], [(512, 512, BF16)])
    (kvn,) = rowwise("kv_norm", f_rmsnorm, [(proj, 512, 1)], [(W["mla_kv_norm"], 512, 0)], [(512, 512, BF16)])
    q = matmul("w_uq", qn, W["mla_w_uq"])
    kv = matmul("w_ukv", kvn, W["mla_w_ukv"])
    qc = mla_q_prep(q, cosf, sinf)
    kc, vb = mla_k_prep(kv, proj, cosf, sinf)
    o_a, lse = _carry(hosts, "attention_fwd", attention_fwd, qc, kc, vb)
    y_a = matmul("w_o_mla", o_a, W["w_o_mla"])
    o_h, hst = hgrn_scan_fwd(proj, lb)
    (pre_b,) = rowwise("hgrn_out", f_hgrn_out, [(o_h, 128, 0), (proj, 128, P_HGATE // 128)], [(W["hgrn_norm"], 128, 0)],
                       [(HG_WIDTH, 128, BF16)], n_groups=HG_HEADS, tm=1024)
    y_b = matmul("w_o_hgrn", pre_b, W["w_o_hgrn"])
    act, pre = conv_fwd(proj, W["ssm_conv_w"], W["ssm_conv_b"])
    dtr = _dt_cols(proj)
    y_s, hs = ssd_scan_fwd(act, dtr, *_ssm_params(W))
    (pre_c,) = rowwise("ssm_out", f_ssm_out, [(y_s, 512, 0), (proj, 512, P_Z // 512)], [(W["ssm_norm"], 512, 0)],
                       [(SSM_INNER, 512, BF16)], n_groups=SSM_GROUPS)
    y_c = matmul("w_o_ssm", pre_c, W["w_o_ssm"])
    g0 = P_GATES // 512
    (merged,) = rowwise("merge", f_merge, [(proj, 512, g0), (proj, 512, g0 + 4), (proj, 512, g0 + 8), (y_a, 512, 0), (y_b, 512, 0), (y_c, 512, 0)],
                        [], [(D_MODEL, 512, BF16)], n_groups=4)
    out = matmul("w_out", merged, W["w_out"], res=x)
    return out, (x, h, proj, qn, kvn, qc, kc, vb, o_a, lse, y_a, o_h, hst, pre_b, y_b, act, pre, dtr, y_s, hs, pre_c, y_c, merged)


def _mix_bwd(dx, dxb, saved, W, lb, tabs, hosts=None):
    cosf, sinf = tabs
    (x, h, proj, qn, kvn, qc, kc, vb, o_a, lse, y_a, o_h, hst, pre_b, y_b, act, pre, dtr, y_s, hs, pre_c, y_c, merged) = saved
    S = x.shape[0]
    g = {}
    dmerged = _carry(hosts, "d_merged", matmul, "d_merged", dxb, W["w_out"], "nt")
    g["w_out"] = matmul("dw_out", merged, dxb, "tn")
    g0 = P_GATES // 512
    dga, dgb, dgc, dya, dyb, dyc = rowwise_bwd(
        "merge_bwd", f_merge, [(proj, 512, g0), (proj, 512, g0 + 4), (proj, 512, g0 + 8), (y_a, 512, 0), (y_b, 512, 0), (y_c, 512, 0)],
        [], [(dmerged, 512)], [(k, D_MODEL, BF16) for k in range(6)], [], n_groups=4)
    do_a = matmul("d_o_mla", dya, W["w_o_mla"], "nt", out_dtype=BF16)
    g["w_o_mla"] = matmul("dw_o_mla", o_a, dya, "tn")
    delta = attention_delta(o_a, do_a)
    dqc = _carry(hosts, "attention_bwd_dq", attention_bwd_dq, qc, kc, vb, do_a, lse, delta)
    dkc, dv = attention_bwd_dkv(qc, kc, vb, do_a, lse, delta)
    dq = mla_q_prep_bwd(dqc, cosf, sinf)
    dkv, dpe = mla_k_prep_bwd(dkc, dv, cosf, sinf)
    dqn = matmul("d_qn", dq, W["mla_w_uq"], "nt")
    g["mla_w_uq"] = matmul("dw_uq", qn, dq, "tn")
    dkvn = matmul("d_kvn", dkv, W["mla_w_ukv"], "nt")
    g["mla_w_ukv"] = matmul("dw_ukv", kvn, dkv, "tn")
    dq_lat, g["mla_q_norm"] = rowwise_bwd("q_norm_bwd", f_rmsnorm, [(proj, 512, 0)], [(W["mla_q_norm"], 512, 0)], [(dqn, 512)],
                                          [(0, 512, BF16)], [0])
    dkv_lat, g["mla_kv_norm"] = rowwise_bwd("kv_norm_bwd", f_rmsnorm, [(proj, 512, 1)], [(W["mla_kv_norm"], 512, 0)], [(dkvn, 512)],
                                            [(0, 512, BF16)], [0])
    do_b = matmul("d_o_hgrn", dyb, W["w_o_hgrn"], "nt")
    g["w_o_hgrn"] = matmul("dw_o_hgrn", pre_b, dyb, "tn")
    do_h, dhgate, g["hgrn_norm"] = rowwise_bwd(
        "hgrn_out_bwd", f_hgrn_out, [(o_h, 128, 0), (proj, 128, P_HGATE // 128)], [(W["hgrn_norm"], 128, 0)], [(do_b, 128)],
        [(0, HG_WIDTH, F32), (1, HG_WIDTH, BF16)], [0], n_groups=HG_HEADS, tm=1024)
    dhq, dhf, dhi, dlb = hgrn_scan_bwd(proj, lb, hst, do_h)
    do_c = matmul("d_o_ssm", dyc, W["w_o_ssm"], "nt")
    g["w_o_ssm"] = matmul("dw_o_ssm", pre_c, dyc, "tn")
    dy_s, dz, g["ssm_norm"] = rowwise_bwd(
        "ssm_out_bwd", f_ssm_out, [(y_s, 512, 0), (proj, 512, P_Z // 512)], [(W["ssm_norm"], 512, 0)], [(do_c, 512)],
        [(0, SSM_INNER, F32), (1, SSM_INNER, BF16)], [0], n_groups=SSM_GROUPS)
    dxs, dbm, dcm, ddtr, dbias, dalog, ddsk = ssd_scan_bwd(act, dtr, *_ssm_params(W), hs, dy_s)
    g["ssm_dt_bias"], g["ssm_a_log"], g["ssm_d"] = (v.reshape(1, SSM_HEADS) for v in (dbias, dalog, ddsk))
    dxbc, g["ssm_conv_w"], g["ssm_conv_b"] = conv_bwd(proj, pre, jnp.concatenate([dxs, dbm, dcm], axis=1), W["ssm_conv_w"])
    ddt = ddtr.transpose(1, 0, 2).reshape(S, SSM_HEADS)
    dproj = jnp.concatenate([dq_lat, dkv_lat, dhq, dhf, dhi, dhgate, dz, dxbc, dga, dgb, dgc, dpe[:, :64].astype(BF16),
                             ddt.astype(BF16), jnp.zeros((S, IN_PAD - IN_DIM), BF16)], axis=1)
    dh = _carry(hosts, "d_h_mix", matmul, "d_h_mix", dproj, W["w_in"], "nt")
    g["w_in"] = _carry(hosts, "dw_in", matmul, "dw_in", h, dproj, "tn")
    dx_in, dxb_in, g["mix_norm"] = rowwise_bwd("mix_norm_bwd", f_rmsnorm, [(x, D_MODEL, 0)], [(W["mix_norm"], D_MODEL, 0)], [(dh, D_MODEL)],
                                               [(0, D_MODEL, F32), (0, D_MODEL, BF16)], [0], adds={0: dx}, tm=256)
    return dx_in, dxb_in, g, dlb


def local_step(x, target, weights, small, final_norm, fwd_hosts=None, bwd_hosts=None):
    S = x.shape[0]
    tabs = rope_tables(S)
    (lbs,) = rowwise("lower_bounds", f_lower_bounds, [(small["hgrn_lb_logits"], HG_WIDTH, 0)], [], [(HG_WIDTH, HG_WIDTH, F32)])

    class LayerWeights(dict):
        def __init__(self, l):
            super().__init__({k: small[k][l:l + 1] for k in SMALL})
            self.layer = l

        def __missing__(self, name):
            self[name] = weights(self.layer, name)
            return self[name]

    saved = []
    for l in range(DEPTH):
        W = LayerWeights(l)
        lb = lbs[l:l + 1]
        hosts = fwd_hosts(l) if fwd_hosts else None
        x, s1 = _ffn_fwd("ffn1", x, W["ffn1_norm"], W["ffn1_wi"], W["ffn1_wo"], hosts)
        x, s2 = _mix_fwd(x, W, lb, tabs, hosts)
        x, s3 = _ffn_fwd("ffn2", x, W["ffn2_norm"], W["ffn2_wi"], W["ffn2_wo"])
        saved.append((W, lb, s1, s2, s3))
    dx, dxb, dfinal, loss = loss_head(x, target, final_norm.reshape(1, D_MODEL))
    grads = [None] * DEPTH
    dlbs = [None] * DEPTH
    for l in reversed(range(DEPTH)):
        W, lb, s1, s2, s3 = saved[l]
        g = {}
        hosts = bwd_hosts(l, grads, g) if bwd_hosts else None
        dx, dxb, g["ffn2_norm"], g["ffn2_wi"], g["ffn2_wo"] = _ffn_bwd("ffn2", dx, dxb, s3, W["ffn2_norm"], W["ffn2_wi"], W["ffn2_wo"], hosts)
        dx, dxb, gm, dlbs[l] = _mix_bwd(dx, dxb, s2, W, lb, tabs, hosts)
        g.update(gm)
        dx, dxb, g["ffn1_norm"], g["ffn1_wi"], g["ffn1_wo"] = _ffn_bwd("ffn1", dx, dxb, s1, W["ffn1_norm"], W["ffn1_wi"], W["ffn1_wo"], hosts)
        grads[l] = g
    (dlogits,) = rowwise_bwd("lower_bounds_bwd", f_lower_bounds, [(small["hgrn_lb_logits"], HG_WIDTH, 0)], [],
                             [(jnp.concatenate(dlbs, axis=0), HG_WIDTH)], [(0, HG_WIDTH, F32)], [])
    return loss, dx, grads, dlogits, dfinal


ANY = pl.BlockSpec(memory_space=pl.ANY)


def _coords():
    return lax.axis_index("x"), lax.axis_index("y"), lax.axis_index("c")


def _exchange_call(name, body, src, out_shape, n_copies):
    return pl.pallas_call(
        body, name=name, in_specs=[ANY], out_specs=ANY, out_shape=out_shape,
        scratch_shapes=[pltpu.SemaphoreType.DMA((n_copies,)), pltpu.SemaphoreType.DMA((n_copies,)), pltpu.SemaphoreType.DMA],
    )(src)


class Exchange(NamedTuple):
    srcs: list
    out_shapes: list
    sem_counts: tuple
    start: Callable
    wait: Callable


def run_exchange(name, ex):
    n = len(ex.srcs)

    def body(*refs):
        src, out, sems = refs[:n], refs[n:n + len(ex.out_shapes)], refs[n + len(ex.out_shapes):]
        ex.start(src, out, sems)
        ex.wait(src, out, sems)

    return pl.pallas_call(
        body, name=name, in_specs=[ANY] * n, out_specs=[ANY] * len(ex.out_shapes), out_shape=ex.out_shapes,
        scratch_shapes=[pltpu.SemaphoreType.DMA((k,)) for k in ex.sem_counts],
    )(*ex.srcs)


def _pcall(name, body, grid, in_specs, out_specs, out_shape, scratch_shapes, sem, args, side=None):
    if side is None:
        return pl.pallas_call(body, name=name, grid=grid, in_specs=in_specs, out_specs=out_specs, out_shape=out_shape,
                              scratch_shapes=scratch_shapes, compiler_params=_cp(*sem))(*args)
    single = not isinstance(out_shape, (list, tuple))
    out_specs_l = [out_specs] if single else list(out_specs)
    out_shape_l = [out_shape] if single else list(out_shape)
    n_in, n_out, n_scr, n_s, n_so = len(in_specs), len(out_specs_l), len(scratch_shapes), len(side.srcs), len(side.out_shapes)

    def hosted(*refs):
        ins, s_in = refs[:n_in], refs[n_in:n_in + n_s]
        o0 = n_in + n_s
        outs, s_out = refs[o0:o0 + n_out], refs[o0 + n_out:o0 + n_out + n_so]
        scr, sems = refs[o0 + n_out + n_so:o0 + n_out + n_so + n_scr], refs[o0 + n_out + n_so + n_scr:]
        ids = [pl.program_id(d) for d in range(len(grid))]
        first = functools.reduce(jnp.logical_and, [i == 0 for i in ids])
        last = functools.reduce(jnp.logical_and, [i == g - 1 for i, g in zip(ids, grid)])

        @pl.when(first)
        def _():
            side.start(s_in, s_out, sems)

        body(*ins, *outs, *scr)

        @pl.when(last)
        def _():
            side.wait(s_in, s_out, sems)

    res = pl.pallas_call(
        hosted, name=name, grid=grid, in_specs=list(in_specs) + [ANY] * n_s, out_specs=out_specs_l + [ANY] * n_so,
        out_shape=out_shape_l + list(side.out_shapes),
        scratch_shapes=list(scratch_shapes) + [pltpu.SemaphoreType.DMA((k,)) for k in side.sem_counts],
        compiler_params=_cp(*["arbitrary"] * len(grid)),
    )(*args, *side.srcs)
    return (res[0] if single else res[:n_out]), res[n_out:]


def _chip_peers(x, y):
    return [(1 - x, y), (x, 1 - y), (1 - x, 1 - y)]


def weights_fetch(shards, layer):
    n = len(shards)
    half = [s.shape[1] // 2 for s in shards]

    def copies(src, out, sems):
        send_sems, recv_sems, fsend_sems, frecv_sems = sems
        x, y, c = _coords()
        peers = _chip_peers(x, y)

        def ici(i, k, chip):
            rows = pl.ds(c * half[i], half[i])
            return pltpu.make_async_remote_copy(src[i].at[layer, rows], out[i].at[chip, rows], send_sems.at[3 * i + k],
                                                recv_sems.at[3 * i + k], device_id=(*peers[k], c), device_id_type=MESH)

        def fwd(i, k, chip, h):
            rows = pl.ds(h * half[i], half[i])
            return pltpu.make_async_remote_copy(out[i].at[chip, rows], out[i].at[chip, rows], fsend_sems.at[3 * i + k],
                                                frecv_sems.at[3 * i + k], device_id=(x, y, 1 - c), device_id_type=MESH)

        return 2 * x + y, c, peers, ici, fwd

    def start(src, out, sems):
        me, c, peers, ici, fwd = copies(src, out, sems)
        for k in range(3):
            for i in range(n):
                ici(i, k, me).start()

    def wait(src, out, sems):
        me, c, peers, ici, fwd = copies(src, out, sems)
        passed = []
        for k, (px, py) in enumerate(peers):
            for i in range(n):
                ici(i, k, 2 * px + py).wait_recv()
                passed.append(fwd(i, k, 2 * px + py, c))
                passed[-1].start()
        for k, (px, py) in enumerate(peers):
            for i in range(n):
                fwd(i, k, 2 * px + py, 1 - c).wait_recv()
        for k in range(3):
            for i in range(n):
                ici(i, k, me).wait_send()
        for cp in passed:
            cp.wait_send()

    out_shapes = [jax.ShapeDtypeStruct((4,) + s.shape[1:], s.dtype) for s in shards]
    return Exchange(shards, out_shapes, (3 * n, 3 * n, 3 * n, 3 * n), start, wait)


def _pairwise(srcs, out_shapes, n_sems, make):
    def start(src, out, sems):
        for cp in make(src, out, *sems):
            cp.start()

    def wait(src, out, sems):
        cps = make(src, out, *sems)
        for cp in cps:
            cp.wait_recv()
        for cp in cps:
            cp.wait_send()

    return Exchange(srcs, out_shapes, (n_sems, n_sems), start, wait)


def grad_half_exchange(gs):
    n = len(gs)
    half = [g.shape[1] // 2 for g in gs]

    def make(src, out, send_sems, recv_sems):
        x, y, c = _coords()
        return [pltpu.make_async_remote_copy(src[i].at[:, pl.ds((1 - c) * half[i], half[i])], out[i], send_sems.at[i], recv_sems.at[i],
                                             device_id=(x, y, 1 - c), device_id_type=MESH) for i in range(n)]

    return _pairwise(gs, [jax.ShapeDtypeStruct((4, h) + g.shape[2:], g.dtype) for g, h in zip(gs, half)], n, make)


def grad_chip_exchange(ps):
    n = len(ps)

    def make(src, out, send_sems, recv_sems):
        x, y, c = _coords()
        return [pltpu.make_async_remote_copy(src[i].at[2 * px + py], out[i].at[k], send_sems.at[3 * i + k], recv_sems.at[3 * i + k],
                                             device_id=(px, py, c), device_id_type=MESH)
                for k, (px, py) in enumerate(_chip_peers(x, y)) for i in range(n)]

    return _pairwise(ps, [jax.ShapeDtypeStruct((3,) + p.shape[1:], p.dtype) for p in ps], 3 * n, make)


def grad_reduced_exchange(rs):
    n = len(rs)

    def make(src, out, send_sems, recv_sems):
        x, y, c = _coords()
        return [pltpu.make_async_remote_copy(src[i], out[i], send_sems.at[i], recv_sems.at[i],
                                             device_id=(x, y, 1 - c), device_id_type=MESH) for i in range(n)]

    return _pairwise(rs, [jax.ShapeDtypeStruct(r.shape, r.dtype) for r in rs], n, make)


def device_allgather(name, src):
    def body(src_ref, out_ref, send_sems, recv_sems, local_sem):
        x, y, c = _coords()
        me = 4 * x + 2 * y + c
        peers = [(x ^ (m >> 2), y ^ ((m >> 1) & 1), c ^ (m & 1)) for m in range(1, 8)]

        def copy(k, dev):
            return pltpu.make_async_remote_copy(src_ref, out_ref.at[dev], send_sems.at[k], recv_sems.at[k],
                                                device_id=peers[k], device_id_type=MESH)

        local = pltpu.make_async_copy(src_ref, out_ref.at[me], local_sem)
        local.start()
        sends = [copy(k, me) for k in range(7)]
        for s in sends:
            s.start()
        for k, (px, py, pc) in enumerate(peers):
            copy(k, 4 * px + 2 * py + pc).wait_recv()
        for s in sends:
            s.wait_send()
        local.wait()

    return _exchange_call(name, body, src, jax.ShapeDtypeStruct((8,) + src.shape, src.dtype), 7)


BLOCK_ELEMS = 1 << 19


def add_pair(name, g, recv, c):
    n, R, C = recv.shape
    tr = _fit_rows(R, max(16, BLOCK_ELEMS // C))
    nb = R // tr

    def body(c_ref, g_ref, r_ref, o_ref):
        o_ref[...] = (g_ref[...].astype(F32) + r_ref[...].astype(F32)).astype(o_ref.dtype)

    blk = pl.BlockSpec((None, tr, C), lambda j, i, c_ref: (j, i, 0))
    return pl.pallas_call(
        body, name=name, out_shape=jax.ShapeDtypeStruct((n, R, C), g.dtype),
        grid_spec=pltpu.PrefetchScalarGridSpec(
            num_scalar_prefetch=1, grid=(n, nb),
            in_specs=[pl.BlockSpec((None, tr, C), lambda j, i, c_ref: (j, c_ref[0] * nb + i, 0)), blk], out_specs=blk),
        compiler_params=_cp("parallel", "parallel"),
    )(c, g, recv)


def sum_chips(name, own, recv, me):
    _, R, C = own.shape
    tr = _fit_rows(R, max(16, BLOCK_ELEMS // (2 * C)))

    def body(me_ref, o_ref, r_ref, out_ref):
        acc = o_ref[...].astype(F32)
        for k in range(3):
            acc = acc + r_ref[k].astype(F32)
        out_ref[...] = acc

    return pl.pallas_call(
        body, name=name, out_shape=jax.ShapeDtypeStruct((R, C), F32),
        grid_spec=pltpu.PrefetchScalarGridSpec(
            num_scalar_prefetch=1, grid=(R // tr,),
            in_specs=[pl.BlockSpec((None, tr, C), lambda i, me_ref: (me_ref[0], i, 0)), pl.BlockSpec((3, tr, C), lambda i, me_ref: (0, i, 0))],
            out_specs=pl.BlockSpec((tr, C), lambda i, me_ref: (i, 0))),
        compiler_params=_cp("parallel"),
    )(me, own, recv)


def sum_devices(parts):
    _, R, C = parts.shape

    def body(p_ref, o_ref):
        acc = p_ref[0]
        for k in range(1, 8):
            acc = acc + p_ref[k]
        o_ref[...] = acc

    return pl.pallas_call(body, name="sum_devices", out_shape=jax.ShapeDtypeStruct((R, C), F32))(parts)


def _fit_rows(R, pref):
    for t in range(min(pref, R), 0, -1):
        if R % t == 0 and (t % 16 == 0 or t == R):
            return t
    raise ValueError((R, pref))


def adamw(name, w, g, m, v):
    shape = w.shape
    C = shape[-1]
    w2, g2, m2, v2 = (a.reshape(-1, C) for a in (w, g, m, v))
    R = w2.shape[0]
    tr = _fit_rows(R, max(8, (1 << 18) // C)) if R * C > (1 << 18) else R
    c1 = 1.0 - ADAM_B1 ** ADAM_STEP
    c2 = 1.0 - ADAM_B2 ** ADAM_STEP

    def body(w_ref, g_ref, m_ref, v_ref, d_ref, mo_ref, vo_ref):
        gg = g_ref[...]
        mn = ADAM_B1 * m_ref[...] + (1.0 - ADAM_B1) * gg
        vn = ADAM_B2 * v_ref[...] + (1.0 - ADAM_B2) * jnp.square(gg)
        d_ref[...] = -ADAM_LR * ((mn / c1) / (jnp.sqrt(vn / c2) + ADAM_EPS) + ADAM_WD * w_ref[...])
        mo_ref[...] = mn
        vo_ref[...] = vn

    blk = pl.BlockSpec((tr, C), lambda i: (i, 0))
    outs = pl.pallas_call(
        body, name=name, grid=(R // tr,), in_specs=[blk] * 4, out_specs=[blk] * 3,
        out_shape=[jax.ShapeDtypeStruct((R, C), F32)] * 3, compiler_params=_cp("parallel"),
    )(w2, g2, m2, v2)
    return tuple(o.reshape(shape) for o in outs)


def _unshard(name, parts):
    n, L, r, c = parts.shape
    if name in COL_SHARDED:
        return parts.transpose(1, 2, 0, 3).reshape(L, r, n * c)
    return parts.transpose(1, 0, 2, 3).reshape(L, n * r, c)


def _shard(name, full):
    L, R, C = full.shape
    if name in COL_SHARDED:
        return full.reshape(L, R, 4, C // 4).transpose(0, 2, 1, 3)
    return full.reshape(L, 4, R // 4, C)


def _to_local(name, w):
    if name == "w_in":
        return _w_in_to_local(w)
    if name == "mla_w_uq":
        return _w_uq_to_local(w)
    return w


def _from_local(name, g):
    if name == "w_in":
        return _w_in_from_local(g)
    if name == "mla_w_uq":
        return _w_uq_from_local(g)
    return g


TWIN_WEIGHTS = ("ffn1_norm", "ffn1_wi", "ffn1_wo", "mix_norm", "w_in", "mla_q_norm", "mla_w_uq", "mla_kv_norm", "mla_w_ukv",
                "hgrn_lb_logits", "hgrn_norm", "ssm_conv_w", "ssm_conv_b", "ssm_a_log", "ssm_dt_bias", "ssm_d", "ssm_norm",
                "w_o_mla", "w_o_hgrn", "w_o_ssm", "w_out", "ffn2_norm", "ffn2_wi", "ffn2_wo", "final_norm")
SMALL_PACK = SMALL + ("ssm_conv_w", "final_norm")


def _pad_rows(flat, cols):
    n = flat.shape[0]
    rows = -(-n // cols)
    rows = -(-rows // 8) * 8
    return jnp.concatenate([flat, jnp.zeros((rows * cols - n,), flat.dtype)]).reshape(rows, cols)


def kernel(x, ffn1_norm, ffn1_wi, ffn1_wo, mix_norm, w_in, mla_q_norm, mla_w_uq, mla_kv_norm, mla_w_ukv, hgrn_lb_logits, hgrn_norm, ssm_conv_w, ssm_conv_b, ssm_a_log, ssm_dt_bias, ssm_d, ssm_norm, w_o_mla, w_o_hgrn, w_o_ssm, w_out, ffn2_norm, ffn2_wi, ffn2_wo, final_norm, loss_target, m_ffn1_norm, m_ffn1_wi, m_ffn1_wo, m_mix_norm, m_w_in, m_mla_q_norm, m_mla_w_uq, m_mla_kv_norm, m_mla_w_ukv, m_hgrn_lb_logits, m_hgrn_norm, m_ssm_conv_w, m_ssm_conv_b, m_ssm_a_log, m_ssm_dt_bias, m_ssm_d, m_ssm_norm, m_w_o_mla, m_w_o_hgrn, m_w_o_ssm, m_w_out, m_ffn2_norm, m_ffn2_wi, m_ffn2_wo, m_final_norm, v_ffn1_norm, v_ffn1_wi, v_ffn1_wo, v_mix_norm, v_w_in, v_mla_q_norm, v_mla_w_uq, v_mla_kv_norm, v_mla_w_ukv, v_hgrn_lb_logits, v_hgrn_norm, v_ssm_conv_w, v_ssm_conv_b, v_ssm_a_log, v_ssm_dt_bias, v_ssm_d, v_ssm_norm, v_w_o_mla, v_w_o_hgrn, v_w_o_ssm, v_w_out, v_ffn2_norm, v_ffn2_wi, v_ffn2_wo, v_final_norm):
    args = dict(locals())
    w = {n: args[n] for n in TWIN_WEIGHTS}
    m = {n: args["m_" + n] for n in TWIN_WEIGHTS}
    v = {n: args["v_" + n] for n in TWIN_WEIGHTS}
    xi, yi, ci = _coords()
    chip = 2 * xi + yi

    shards = {n: w[n].astype(BF16) for n in BIG}
    conv_parts = device_allgather("conv_allgather", _pad_rows(w["ssm_conv_w"].reshape(-1), 128))
    conv_full = jnp.concatenate(
        [conv_parts[2 * j].reshape(-1)[:w["ssm_conv_w"].size].reshape(w["ssm_conv_w"].shape) for j in range(4)], axis=-1)
    fetched = {}

    def fetch(names, l):
        return (lambda: weights_fetch([shards[n] for n in names], l),
                lambda out: fetched.update({(l, n): parts for n, parts in zip(names, out)}))

    def weight(l, name):
        if name == "ssm_conv_w":
            return conv_full[l]
        parts = lax.dynamic_update_slice(fetched[l, name], shards[name][l][None], (chip, 0, 0))
        return _to_local(name, _unshard(name, parts[:, None])[0])

    make, take = fetch(FFN1, 0)
    take(run_exchange("weights_fetch", make()))

    def fwd_hosts(l):
        if l != 0:
            return None
        return {"ffn1_wi": fetch(("w_in",), 0), "w_in": fetch(tuple(n for n in BIG if n not in FFN1 + ("w_in",)), 0),
                "attention_fwd": fetch(BIG, 1)}

    c_idx, chip_idx = ci.astype(jnp.int32).reshape(1), chip.astype(jnp.int32).reshape(1)
    reduced = {}

    def reduction_stages(l, names, layer_grads):
        st = {}
        tag = "_l%d_" % l

        def half_exchange():
            st["gs"] = [_shard(n, _from_local(n, layer_grads[n][None]))[0].astype(BF16) for n in names]
            return grad_half_exchange(st["gs"])

        def chip_exchange():
            st["pair"] = [add_pair("add_pair" + tag + n, a, b, c_idx) for n, a, b in zip(names, st["gs"], st["recv"])]
            return grad_chip_exchange(st["pair"])

        def reduced_exchange():
            st["red"] = [sum_chips("sum_chips" + tag + n, a, b, chip_idx) for n, a, b in zip(names, st["pair"], st["from_chips"])]
            return grad_reduced_exchange(st["red"])

        def finish(others):
            for n, mine, other in zip(names, st["red"], others):
                reduced[l, n] = jnp.where(ci == 0, jnp.concatenate([mine, other]), jnp.concatenate([other, mine]))

        return [(half_exchange, lambda out: st.update(recv=out)), (chip_exchange, lambda out: st.update(from_chips=out)),
                (reduced_exchange, finish)]

    middle = tuple(n for n in BIG if n not in FFN1 + FFN2)

    def bwd_hosts(l, earlier, current):
        if l != 0:
            return None
        hosts = dict(zip(("ffn2_dh", "attention_bwd_dq", "d_h_mix"), reduction_stages(1, BIG, earlier[1])))
        hosts.update(zip(("d_merged", "dw_in", "ffn1_dwo"), reduction_stages(0, FFN2, current)))
        hosts.update(zip(("ffn1_da", "ffn1_dh", "ffn1_dwi"), reduction_stages(0, middle, current)))
        return hosts

    small = {n: w[n] for n in SMALL}
    loss, grad_x, grads, dlogits, dfinal = local_step(x[0], loss_target[0], weight, small, w["final_norm"], fwd_hosts, bwd_hosts)
    loss = lax.psum(loss[0, 0], ("x", "y", "c"))
    for name, (make, take) in zip(("grad_half_exchange", "grad_chip_exchange", "grad_reduced_exchange"), reduction_stages(0, FFN1, grads[0])):
        take(run_exchange(name, make()))
    g = {n: jnp.stack([reduced[l, n] for l in range(DEPTH)]) for n in BIG}

    sg = {n: jnp.concatenate([grads[l][n] for l in range(DEPTH)], axis=0) for n in SMALL if n != "hgrn_lb_logits"}
    sg["hgrn_lb_logits"] = dlogits
    sg["ssm_conv_w"] = jnp.stack([grads[l]["ssm_conv_w"] for l in range(DEPTH)])
    sg["final_norm"] = dfinal
    spack = _pad_rows(jnp.concatenate([sg[n].reshape(-1) for n in SMALL_PACK]), 128)
    ssum = sum_devices(device_allgather("small_grads_allgather", spack)).reshape(-1)
    off = 0
    for n in SMALL_PACK:
        size = sg[n].size
        g[n] = ssum[off:off + size].reshape(sg[n].shape)
        off += size
    shard_cols = w["ssm_conv_w"].shape[-1]
    g["ssm_conv_w"] = lax.dynamic_slice_in_dim(g["ssm_conv_w"], chip * shard_cols, shard_cols, axis=2)
    g = {n: g[n].reshape(w[n].shape) for n in TWIN_WEIGHTS}

    upd = {n: adamw("adamw_" + n, w[n], g[n], m[n], v[n]) for n in TWIN_WEIGHTS}
    return (loss, grad_x[None], *[g[n] for n in TWIN_WEIGHTS], *[upd[n][0] for n in TWIN_WEIGHTS],
            *[upd[n][1] for n in TWIN_WEIGHTS], *[upd[n][2] for n in TWIN_WEIGHTS])
```

```python
import functools
import math
from typing import Callable, NamedTuple

import jax
import jax.numpy as jnp
import numpy as np
from jax import lax
from jax.experimental import pallas as pl
from jax.experimental.pallas import tpu as pltpu

F32 = jnp.float32
BF16 = jnp.bfloat16
MESH = pl.DeviceIdType.MESH

D_MODEL = 2048
DEPTH = 2
CHUNK = 64
EPS = 1e-6
MLA_HEADS, MLA_Q_RANK, MLA_KV_RANK, MLA_NOPE, MLA_ROPE, MLA_V = 16, 512, 512, 128, 64, 128
ROPE_THETA = 10000.0
HG_HEADS, HG_DK = 16, 128
HG_WIDTH = HG_HEADS * HG_DK
SSM_INNER, SSM_HEADDIM, SSM_HEADS, SSM_GROUPS, SSM_STATE, SSM_CONV = 4096, 64, 64, 8, 128, 4
SSM_CONV_DIM = SSM_INNER + 2 * SSM_GROUPS * SSM_STATE
D_FF = 5632
IN_DIM = 25728
ADAM_LR, ADAM_B1, ADAM_B2, ADAM_EPS, ADAM_WD, ADAM_STEP = 0.001, 0.9, 0.999, 1e-08, 0.01, 10

P_QLAT, P_KVLAT, P_HQ, P_HF, P_HI, P_HGATE, P_Z, P_XBC, P_GATES, P_KPEDT = (
    0, 512, 1024, 3072, 5120, 7168, 9216, 13312, 19456, 25600)
IN_PAD = 26624

VMEM_LIMIT_V7X = 48 << 20
SSD_Q = 256
HG_HB = 8
assert all(off % (HG_HB * HG_DK) == 0 for off in (P_HQ, P_HF, P_HI)) and HG_HEADS % HG_HB == 0
NEG = -1e30


def _cp(*sem):
    return pltpu.CompilerParams(dimension_semantics=sem, vmem_limit_bytes=VMEM_LIMIT_V7X)


def _fit(n, pref):
    if n <= pref:
        return n
    for t in range(pref, 0, -128):
        if n % t == 0:
            return t
    raise ValueError((n, pref))


_DIMS = {"nn": (((1,), (0,)), ((), ())), "nt": (((1,), (1,)), ((), ())), "tn": (((0,), (0,)), ((), ()))}


def _dot(a, b, mode):
    return lax.dot_general(a.astype(BF16), b.astype(BF16), _DIMS[mode], preferred_element_type=F32)


def matmul(name, a, b, mode="nn", out_dtype=F32, alpha=1.0, res=None, tm=1024, tn=1024, tk=2048, side=None):
    if mode == "nn":
        (M, K), (K2, N) = a.shape, b.shape
    elif mode == "nt":
        (M, K), (N, K2) = a.shape, b.shape
    else:
        (K, M), (K2, N) = a.shape, b.shape
    assert K == K2, (name, a.shape, b.shape, mode)
    tm, tn, tk = _fit(M, tm), _fit(N, tn), _fit(K, tk)
    nk = K // tk
    a_spec = pl.BlockSpec((tk, tm), lambda i, j, k: (k, i)) if mode == "tn" else pl.BlockSpec((tm, tk), lambda i, j, k: (i, k))
    b_spec = pl.BlockSpec((tn, tk), lambda i, j, k: (j, k)) if mode == "nt" else pl.BlockSpec((tk, tn), lambda i, j, k: (k, j))
    o_spec = pl.BlockSpec((tm, tn), lambda i, j, k: (i, j))
    has_res = res is not None

    def body(*refs):
        a_ref, b_ref = refs[0], refs[1]
        o_ref = refs[3] if has_res else refs[2]

        def finish(v):
            if alpha != 1.0:
                v = v * alpha
            if has_res:
                v = v + refs[2][...].astype(F32)
            o_ref[...] = v.astype(o_ref.dtype)

        if nk == 1:
            finish(_dot(a_ref[...], b_ref[...], mode))
            return
        acc = refs[-1]
        k = pl.program_id(2)

        @pl.when(k == 0)
        def _():
            acc[...] = _dot(a_ref[...], b_ref[...], mode)

        @pl.when(jnp.logical_and(k > 0, k < nk - 1))
        def _():
            acc[...] += _dot(a_ref[...], b_ref[...], mode)

        @pl.when(k == nk - 1)
        def _():
            finish(acc[...] + _dot(a_ref[...], b_ref[...], mode))

    ins = [a, b] + ([res] if has_res else [])
    return _pcall(name, body, (M // tm, N // tn, nk), [a_spec, b_spec] + ([o_spec] if has_res else []), o_spec,
                  jax.ShapeDtypeStruct((M, N), out_dtype), [pltpu.VMEM((tm, tn), F32)] if nk > 1 else [],
                  ("parallel", "parallel", "arbitrary"), ins, side)


def _row_specs(rows, consts, tm):
    specs = []
    for arr, w, off in rows:
        specs.append(pl.BlockSpec((tm, w), functools.partial(lambda j, i, off: (i, off + j), off=off)))
    for arr, w, off in consts:
        specs.append(pl.BlockSpec((arr.shape[0], w), functools.partial(lambda j, i, off: (0, off + j), off=off)))
    return specs


def rowwise(name, fn, rows, consts, outs, n_groups=1, tm=512):
    S = rows[0][0].shape[0]
    tm = _fit(S, tm)
    n_in = len(rows) + len(consts)

    def body(*refs):
        vals = fn(*[r[...].astype(F32) for r in refs[:n_in]])
        for o_ref, v in zip(refs[n_in:], vals):
            o_ref[...] = v.astype(o_ref.dtype)

    return pl.pallas_call(
        body, name=name, grid=(n_groups, S // tm),
        in_specs=_row_specs(rows, consts, tm),
        out_specs=[pl.BlockSpec((tm, w), lambda j, i: (i, j)) for _, w, _ in outs],
        out_shape=[jax.ShapeDtypeStruct((S, W), dt) for W, _, dt in outs],
        compiler_params=_cp("parallel", "parallel"),
    )(*[r[0] for r in rows], *[c[0] for c in consts])


def rowwise_bwd(name, fn, rows, consts, cts, row_grads, const_grads, adds=None, n_groups=1, tm=512):
    S = rows[0][0].shape[0]
    tm = _fit(S, tm)
    adds = adds or {}
    add_idx = list(adds)
    n_r, n_c, n_ct, n_add = len(rows), len(consts), len(cts), len(adds)

    def body(*refs):
        ins = [r[...].astype(F32) for r in refs[:n_r + n_c]]
        ct = tuple(r[...].astype(F32) for r in refs[n_r + n_c:n_r + n_c + n_ct])
        add_refs = refs[n_r + n_c + n_ct:n_r + n_c + n_ct + n_add]
        out_refs = refs[n_r + n_c + n_ct + n_add:]
        _, vjp = jax.vjp(fn, *ins)
        g = list(vjp(ct))
        for a_ref, idx in zip(add_refs, add_idx):
            g[idx] = g[idx] + a_ref[...].astype(F32)
        for q, (idx, _, _) in enumerate(row_grads):
            out_refs[q][...] = g[idx].astype(out_refs[q].dtype)
        first = pl.program_id(1) == 0
        for q, idx in enumerate(const_grads):
            o_ref = out_refs[len(row_grads) + q]

            @pl.when(first)
            def _(o_ref=o_ref):
                o_ref[...] = jnp.zeros_like(o_ref)

            o_ref[...] += g[n_r + idx]

    in_specs = _row_specs(rows, consts, tm)
    in_specs += [pl.BlockSpec((tm, w), lambda j, i: (i, j)) for _, w in cts]
    in_specs += [pl.BlockSpec((tm, rows[idx][1]), lambda j, i: (i, j)) for idx in add_idx]
    out_specs = [pl.BlockSpec((tm, rows[idx][1]), lambda j, i: (i, j)) for idx, _, _ in row_grads]
    out_specs += [pl.BlockSpec((consts[idx][0].shape[0], consts[idx][1]), lambda j, i: (0, j)) for idx in const_grads]
    out_shape = [jax.ShapeDtypeStruct((S, W), dt) for _, W, dt in row_grads]
    out_shape += [jax.ShapeDtypeStruct((consts[idx][0].shape[0], consts[idx][1] * n_groups), F32) for idx in const_grads]
    return pl.pallas_call(
        body, name=name, grid=(n_groups, S // tm), in_specs=in_specs, out_specs=out_specs, out_shape=out_shape,
        compiler_params=_cp("parallel", "arbitrary"),
    )(*[r[0] for r in rows], *[c[0] for c in consts], *[c[0] for c in cts], *adds.values())


def f_rmsnorm(x, w):
    return (x * lax.rsqrt(jnp.mean(x * x, axis=-1, keepdims=True) + EPS) * w,)


def f_swiglu(g, u):
    return (jax.nn.silu(g) * u,)


def f_hgrn_out(o, g, w):
    return (o * lax.rsqrt(jnp.mean(o * o, axis=-1, keepdims=True) + EPS) * w * jax.nn.silu(g),)


def f_ssm_out(y, z, w):
    y = y * jax.nn.silu(z)
    return (y * lax.rsqrt(jnp.mean(y * y, axis=-1, keepdims=True) + EPS) * w,)


def f_merge(ga, gb, gc, ya, yb, yc):
    return (jax.nn.sigmoid(ga) * ya + jax.nn.sigmoid(gb) * yb + jax.nn.sigmoid(gc) * yc,)


def f_lower_bounds(logits):
    p = jax.nn.softmax(logits, axis=0)
    rows = [jnp.zeros_like(p[0:1])]
    for l in range(1, DEPTH):
        rows.append(rows[-1] + p[l:l + 1])
    return (jnp.concatenate(rows, axis=0),)


def loss_head(x, target, w, tm=512):
    S, D = x.shape
    tm = _fit(S, tm)

    def loss_fn(xb, wb, tb):
        (y,) = f_rmsnorm(xb, wb)
        return 0.5 * jnp.sum(jnp.mean(jnp.square(y - tb), axis=-1))

    def body(x_ref, t_ref, w_ref, dx_ref, dxb_ref, dw_ref, loss_ref):
        @pl.when(pl.program_id(0) == 0)
        def _():
            dw_ref[...] = jnp.zeros_like(dw_ref)
            loss_ref[...] = jnp.zeros_like(loss_ref)

        l, (dx, dw) = jax.value_and_grad(loss_fn, argnums=(0, 1))(x_ref[...], w_ref[...], t_ref[...])
        dx_ref[...] = dx
        dxb_ref[...] = dx.astype(BF16)
        dw_ref[...] += dw
        loss_ref[...] += jnp.full(loss_ref.shape, l, F32)

    row = pl.BlockSpec((tm, D), lambda i: (i, 0))
    vec = pl.BlockSpec((1, D), lambda i: (0, 0))
    return pl.pallas_call(
        body, name="loss_head", grid=(S // tm,), in_specs=[row, row, vec],
        out_specs=[row, row, vec, pl.BlockSpec((1, 128), lambda i: (0, 0))],
        out_shape=[jax.ShapeDtypeStruct((S, D), F32), jax.ShapeDtypeStruct((S, D), BF16), jax.ShapeDtypeStruct((1, D), F32),
                   jax.ShapeDtypeStruct((1, 128), F32)],
        compiler_params=_cp("arbitrary"),
    )(x, target, w)


def rope_tables(S):
    inv = 1.0 / (ROPE_THETA ** (jnp.arange(0, MLA_ROPE, 2, dtype=F32) / MLA_ROPE))
    ang = jnp.arange(S, dtype=F32)[:, None] * inv[None, :]
    c, s = jnp.cos(ang), jnp.sin(ang)
    return jnp.tile(c, (1, 4)), jnp.concatenate([-s, s, -s, s], axis=1)


def _rope128(x, cosf, sinf):
    lane = lax.broadcasted_iota(jnp.int32, x.shape, 1)
    swapped = jnp.where((lane & 32) == 0, pltpu.roll(x, 96, 1), pltpu.roll(x, 32, 1))
    return x * cosf + swapped * sinf


def mla_q_prep(q, cosf, sinf, tm=512):
    S = q.shape[0]
    tm = _fit(S, tm)

    def body(q_ref, c_ref, s_ref, o_ref):
        x = q_ref[...]
        r = _rope128(x[:, 256:384], c_ref[...], s_ref[...])
        lane = lax.broadcasted_iota(jnp.int32, r.shape, 1)
        z = jnp.zeros_like(r)
        o_ref[...] = jnp.concatenate(
            [x[:, 0:128], jnp.where(lane < 64, r, z), x[:, 128:256], jnp.where(lane >= 64, r, z)], axis=1).astype(BF16)

    tab = pl.BlockSpec((tm, 128), lambda j, i: (i, 0))
    return pl.pallas_call(
        body, name="mla_q_prep", grid=(8, S // tm),
        in_specs=[pl.BlockSpec((tm, 384), lambda j, i: (i, j)), tab, tab],
        out_specs=pl.BlockSpec((tm, 512), lambda j, i: (i, j)),
        out_shape=jax.ShapeDtypeStruct((S, 4096), BF16), compiler_params=_cp("parallel", "parallel"),
    )(q, cosf, sinf)


def mla_q_prep_bwd(dqc, cosf, sinf, tm=512):
    S = dqc.shape[0]
    tm = _fit(S, tm)

    def body(d_ref, c_ref, s_ref, o_ref):
        d = d_ref[...]
        lane = lax.broadcasted_iota(jnp.int32, (tm, 128), 1)
        dr = jnp.where(lane < 64, d[:, 128:256], d[:, 384:512])
        o_ref[...] = jnp.concatenate([d[:, 0:128], d[:, 256:384], _rope128(dr, c_ref[...], -s_ref[...])], axis=1).astype(BF16)

    tab = pl.BlockSpec((tm, 128), lambda j, i: (i, 0))
    return pl.pallas_call(
        body, name="mla_q_prep_bwd", grid=(8, S // tm),
        in_specs=[pl.BlockSpec((tm, 512), lambda j, i: (i, j)), tab, tab],
        out_specs=pl.BlockSpec((tm, 384), lambda j, i: (i, j)),
        out_shape=jax.ShapeDtypeStruct((S, 3072), BF16), compiler_params=_cp("parallel", "parallel"),
    )(dqc, cosf, sinf)


def mla_k_prep(kv, proj, cosf, sinf, tm=512):
    S = kv.shape[0]
    tm = _fit(S, tm)

    def body(kv_ref, pe_ref, c_ref, s_ref, k_ref, v_ref):
        x = kv_ref[...]
        r = _rope128(pe_ref[...], c_ref[...], s_ref[...])
        lane = lax.broadcasted_iota(jnp.int32, r.shape, 1)
        r2 = jnp.where(lane < 64, r, pltpu.roll(r, 64, 1))
        k_ref[...] = jnp.concatenate([x[:, 0:128], r2, x[:, 256:384], r2], axis=1).astype(BF16)
        v_ref[...] = jnp.concatenate([x[:, 128:256], x[:, 384:512]], axis=1).astype(BF16)

    tab = pl.BlockSpec((tm, 128), lambda j, i: (i, 0))
    return pl.pallas_call(
        body, name="mla_k_prep", grid=(8, S // tm),
        in_specs=[pl.BlockSpec((tm, 512), lambda j, i: (i, j)), pl.BlockSpec((tm, 128), lambda j, i: (i, P_KPEDT // 128)), tab, tab],
        out_specs=[pl.BlockSpec((tm, 512), lambda j, i: (i, j)), pl.BlockSpec((tm, 256), lambda j, i: (i, j))],
        out_shape=[jax.ShapeDtypeStruct((S, 4096), BF16), jax.ShapeDtypeStruct((S, 2048), BF16)],
        compiler_params=_cp("parallel", "parallel"),
    )(kv, proj, cosf, sinf)


def mla_k_prep_bwd(dkc, dv, cosf, sinf, tm=512):
    S = dkc.shape[0]
    tm = _fit(S, tm)

    def body(dk_ref, dv_ref, c_ref, s_ref, dkv_ref, dpe_ref):
        dk, dvv = dk_ref[...], dv_ref[...]
        dkv_ref[...] = jnp.concatenate([dk[:, 0:128], dvv[:, 0:128], dk[:, 256:384], dvv[:, 128:256]], axis=1).astype(BF16)
        d2 = dk[:, 128:256] + dk[:, 384:512]
        lane = lax.broadcasted_iota(jnp.int32, d2.shape, 1)
        dr = jnp.where(lane < 64, d2 + pltpu.roll(d2, 64, 1), 0.0)
        dpe = jnp.where(lane < 64, _rope128(dr, c_ref[...], -s_ref[...]), 0.0)

        @pl.when(pl.program_id(1) == 0)
        def _():
            dpe_ref[...] = jnp.zeros_like(dpe_ref)

        dpe_ref[...] += dpe

    tab = pl.BlockSpec((tm, 128), lambda i, j: (i, 0))
    return pl.pallas_call(
        body, name="mla_k_prep_bwd", grid=(S // tm, 8),
        in_specs=[pl.BlockSpec((tm, 512), lambda i, j: (i, j)), pl.BlockSpec((tm, 256), lambda i, j: (i, j)), tab, tab],
        out_specs=[pl.BlockSpec((tm, 512), lambda i, j: (i, j)), tab],
        out_shape=[jax.ShapeDtypeStruct((S, 4096), BF16), jax.ShapeDtypeStruct((S, 128), F32)],
        compiler_params=_cp("parallel", "arbitrary"),
    )(dkc, dv, cosf, sinf)


ATT_SCALE = (MLA_NOPE + MLA_ROPE) ** -0.5


def _att_scores(q, k, qi, ki, t):
    s = _dot(q, k, "nt") * ATT_SCALE
    rows = qi * t + lax.broadcasted_iota(jnp.int32, (t, t), 0)
    cols = ki * t + lax.broadcasted_iota(jnp.int32, (t, t), 1)
    shift = CHUNK.bit_length() - 1
    return jnp.where((cols >> shift) <= (rows >> shift), s, NEG)


def attention_fwd(qc, kc, vb, t=2048, side=None):
    S = qc.shape[0]
    t = _fit(S, t)
    n = S // t

    def body(q_ref, k_ref, v_ref, o_ref, lse_ref, m_s, l_s, acc_s):
        qi, ki = pl.program_id(1), pl.program_id(2)

        @pl.when(ki == 0)
        def _():
            m_s[...] = jnp.full_like(m_s, NEG)
            l_s[...] = jnp.zeros_like(l_s)
            acc_s[...] = jnp.zeros_like(acc_s)

        @pl.when(ki <= qi)
        def _():
            s = _att_scores(q_ref[...], k_ref[...], qi, ki, t)
            m_prev = m_s[...]
            m_new = jnp.maximum(m_prev, jnp.max(s, axis=1, keepdims=True))
            alpha = jnp.exp(m_prev - m_new)
            p = jnp.exp(s - m_new)
            l_s[...] = alpha * l_s[...] + jnp.sum(p, axis=1, keepdims=True)
            acc_s[...] = alpha * acc_s[...] + _dot(p, v_ref[...], "nn")
            m_s[...] = m_new

        @pl.when(ki == qi)
        def _():
            o_ref[...] = (acc_s[...] / l_s[...]).astype(o_ref.dtype)
            lse_ref[...] = m_s[...] + jnp.log(l_s[...])

    return _pcall(
        "attention_fwd", body, (MLA_HEADS, n, n),
        [pl.BlockSpec((t, 256), lambda h, i, j: (i, h)),
         pl.BlockSpec((t, 256), lambda h, i, j: (jnp.minimum(i, j), h)),
         pl.BlockSpec((t, 128), lambda h, i, j: (jnp.minimum(i, j), h))],
        [pl.BlockSpec((t, 128), lambda h, i, j: (i, h)), pl.BlockSpec((None, t, 1), lambda h, i, j: (h, i, 0))],
        [jax.ShapeDtypeStruct((S, 2048), BF16), jax.ShapeDtypeStruct((MLA_HEADS, S, 1), F32)],
        [pltpu.VMEM((t, 1), F32), pltpu.VMEM((t, 1), F32), pltpu.VMEM((t, 128), F32)],
        ("parallel", "parallel", "arbitrary"), (qc, kc, vb), side)


def attention_delta(o, do, t=1024):
    S = o.shape[0]
    t = _fit(S, t)

    def body(o_ref, do_ref, d_ref):
        d_ref[...] = jnp.sum(o_ref[...].astype(F32) * do_ref[...].astype(F32), axis=1, keepdims=True)

    blk = pl.BlockSpec((t, 128), lambda h, i: (i, h))
    return pl.pallas_call(
        body, name="attention_delta", grid=(MLA_HEADS, S // t), in_specs=[blk, blk],
        out_specs=pl.BlockSpec((None, t, 1), lambda h, i: (h, i, 0)),
        out_shape=jax.ShapeDtypeStruct((MLA_HEADS, S, 1), F32), compiler_params=_cp("parallel", "parallel"),
    )(o, do)


def attention_bwd_dq(qc, kc, vb, do, lse, delta, t=2048, side=None):
    S = qc.shape[0]
    t = _fit(S, t)
    n = S // t

    def body(q_ref, k_ref, v_ref, do_ref, lse_ref, dl_ref, dq_ref, acc):
        qi, ki = pl.program_id(1), pl.program_id(2)

        @pl.when(ki == 0)
        def _():
            acc[...] = jnp.zeros_like(acc)

        @pl.when(ki <= qi)
        def _():
            p = jnp.exp(_att_scores(q_ref[...], k_ref[...], qi, ki, t) - lse_ref[...])
            dp = _dot(do_ref[...], v_ref[...], "nt")
            ds = p * (dp - dl_ref[...])
            acc[...] += _dot(ds, k_ref[...], "nn")

        @pl.when(ki == qi)
        def _():
            dq_ref[...] = acc[...] * ATT_SCALE

    stat = pl.BlockSpec((None, t, 1), lambda h, i, j: (h, i, 0))
    return _pcall(
        "attention_bwd_dq", body, (MLA_HEADS, n, n),
        [pl.BlockSpec((t, 256), lambda h, i, j: (i, h)),
         pl.BlockSpec((t, 256), lambda h, i, j: (jnp.minimum(i, j), h)),
         pl.BlockSpec((t, 128), lambda h, i, j: (jnp.minimum(i, j), h)),
         pl.BlockSpec((t, 128), lambda h, i, j: (i, h)), stat, stat],
        pl.BlockSpec((t, 256), lambda h, i, j: (i, h)), jax.ShapeDtypeStruct((S, 4096), F32),
        [pltpu.VMEM((t, 256), F32)], ("parallel", "parallel", "arbitrary"), (qc, kc, vb, do, lse, delta), side)


def attention_bwd_dkv(qc, kc, vb, do, lse, delta, t=2048):
    S = qc.shape[0]
    t = _fit(S, t)
    n = S // t

    def body(q_ref, k_ref, v_ref, do_ref, lse_ref, dl_ref, dk_ref, dv_ref, dk_acc, dv_acc):
        ki, qi = pl.program_id(1), pl.program_id(2)

        @pl.when(qi == 0)
        def _():
            dk_acc[...] = jnp.zeros_like(dk_acc)
            dv_acc[...] = jnp.zeros_like(dv_acc)

        @pl.when(qi >= ki)
        def _():
            s = _dot(k_ref[...], q_ref[...], "nt") * ATT_SCALE
            krow = ki * t + lax.broadcasted_iota(jnp.int32, (t, t), 0)
            qcol = qi * t + lax.broadcasted_iota(jnp.int32, (t, t), 1)
            shift = CHUNK.bit_length() - 1
            p = jnp.exp(jnp.where((krow >> shift) <= (qcol >> shift), s, NEG) - lse_ref[...])
            dv_acc[...] += _dot(p, do_ref[...], "nn")
            dp = _dot(v_ref[...], do_ref[...], "nt")
            ds = p * (dp - dl_ref[...])
            dk_acc[...] += _dot(ds, q_ref[...], "nn")

        @pl.when(qi == n - 1)
        def _():
            dk_ref[...] = dk_acc[...] * ATT_SCALE
            dv_ref[...] = dv_acc[...]

    stat = pl.BlockSpec((None, 1, t), lambda h, j, i: (h, 0, jnp.maximum(i, j)))
    return pl.pallas_call(
        body, name="attention_bwd_dkv", grid=(MLA_HEADS, n, n),
        in_specs=[pl.BlockSpec((t, 256), lambda h, j, i: (jnp.maximum(i, j), h)),
                  pl.BlockSpec((t, 256), lambda h, j, i: (j, h)),
                  pl.BlockSpec((t, 128), lambda h, j, i: (j, h)),
                  pl.BlockSpec((t, 128), lambda h, j, i: (jnp.maximum(i, j), h)), stat, stat],
        out_specs=[pl.BlockSpec((t, 256), lambda h, j, i: (j, h)), pl.BlockSpec((t, 128), lambda h, j, i: (j, h))],
        out_shape=[jax.ShapeDtypeStruct((S, 4096), F32), jax.ShapeDtypeStruct((S, 2048), F32)],
        scratch_shapes=[pltpu.VMEM((t, 256), F32), pltpu.VMEM((t, 128), F32)],
        compiler_params=_cp("parallel", "parallel", "arbitrary"),
    )(qc, kc, vb, do, lse.reshape(MLA_HEADS, 1, S), delta.reshape(MLA_HEADS, 1, S))


def _scan_rows(x, reverse):
    n = x.shape[0]
    row = lax.broadcasted_iota(jnp.int32, x.shape, 0)
    d = 1
    while d < n:
        if reverse:
            x = x + jnp.where(row < n - d, pltpu.roll(x, n - d, 0), 0.0)
        else:
            x = x + jnp.where(row >= d, pltpu.roll(x, d, 0), 0.0)
        d *= 2
    return x


@jax.custom_vjp
def cumsum_rows(x):
    return _scan_rows(x, False)


cumsum_rows.defvjp(lambda x: (_scan_rows(x, False), None), lambda _, g: (_scan_rows(g, True),))


def hgrn_chunk(q_in, f_in, v, lb, state_t):
    f = lb + (1.0 - lb) * jax.nn.sigmoid(f_in)
    q = jax.nn.silu(q_in) * HG_DK ** -0.5
    k = 1.0 - f
    b = cumsum_rows(jnp.log(f))
    b_last = b[CHUNK - 1:CHUNK]
    b_mid = b[CHUNK // 2 - 1:CHUNK // 2]
    r = lax.broadcasted_iota(jnp.int32, (CHUNK, CHUNK), 0)
    c = lax.broadcasted_iota(jnp.int32, (CHUNK, CHUNK), 1)
    att = jnp.where(c <= r, _dot(q * jnp.exp(b - b_mid), k * jnp.exp(b_mid - b), "nt"), 0.0)
    o = _dot(q * jnp.exp(b), state_t, "nt") + _dot(att, v, "nn")
    new_state_t = state_t * jnp.exp(b_last) + _dot(v, k * jnp.exp(b_last - b), "tn")
    return o, new_state_t


def hgrn_scan_fwd(proj, lb):
    S = proj.shape[0]
    nc = S // CHUNK
    W = HG_HB * 128

    def body(q_ref, f_ref, v_ref, lb_ref, o_ref, hst_ref, st):
        @pl.when(pl.program_id(1) == 0)
        def _():
            st[...] = jnp.zeros_like(st)

        for h in range(HG_HB):
            cs = slice(h * 128, (h + 1) * 128)
            hst_ref[h] = st[h]
            o, new = hgrn_chunk(q_ref[:, cs], f_ref[:, cs], v_ref[:, cs], lb_ref[:, cs], st[h])
            o_ref[:, cs] = o
            st[h] = new

    def seg(off):
        return pl.BlockSpec((CHUNK, W), functools.partial(lambda g, c, off: (c, off + g), off=off // W))

    return pl.pallas_call(
        body, name="hgrn_scan_fwd", grid=(HG_HEADS // HG_HB, nc),
        in_specs=[seg(P_HQ), seg(P_HF), seg(P_HI), pl.BlockSpec((1, W), lambda g, c: (0, g))],
        out_specs=[pl.BlockSpec((CHUNK, W), lambda g, c: (c, g)), pl.BlockSpec((None, HG_HB, 128, 128), lambda g, c: (c, g, 0, 0))],
        out_shape=[jax.ShapeDtypeStruct((S, HG_WIDTH), F32), jax.ShapeDtypeStruct((nc, HG_HEADS, 128, 128), F32)],
        scratch_shapes=[pltpu.VMEM((HG_HB, 128, 128), F32)],
        compiler_params=_cp("parallel", "arbitrary"),
    )(proj, proj, proj, lb)


def hgrn_scan_bwd(proj, lb, hst, do):
    S = proj.shape[0]
    nc = S // CHUNK
    W = HG_HB * 128

    def body(q_ref, f_ref, v_ref, lb_ref, hst_ref, do_ref, dq_ref, df_ref, dv_ref, dlb_ref, dst):
        @pl.when(pl.program_id(1) == 0)
        def _():
            dst[...] = jnp.zeros_like(dst)
            dlb_ref[...] = jnp.zeros_like(dlb_ref)

        for h in range(HG_HB):
            cs = slice(h * 128, (h + 1) * 128)
            _, vjp = jax.vjp(hgrn_chunk, q_ref[:, cs], f_ref[:, cs], v_ref[:, cs], lb_ref[:, cs], hst_ref[h])
            dq, df, dv, dlb, dstate = vjp((do_ref[:, cs], dst[h]))
            dq_ref[:, cs] = dq.astype(dq_ref.dtype)
            df_ref[:, cs] = df.astype(df_ref.dtype)
            dv_ref[:, cs] = dv.astype(dv_ref.dtype)
            dlb_ref[:, cs] += dlb
            dst[h] = dstate

    def seg(off):
        return pl.BlockSpec((CHUNK, W), functools.partial(lambda g, c, off: (nc - 1 - c, off + g), off=off // W))

    row = pl.BlockSpec((CHUNK, W), lambda g, c: (nc - 1 - c, g))
    vec = pl.BlockSpec((1, W), lambda g, c: (0, g))
    return pl.pallas_call(
        body, name="hgrn_scan_bwd", grid=(HG_HEADS // HG_HB, nc),
        in_specs=[seg(P_HQ), seg(P_HF), seg(P_HI), vec,
                  pl.BlockSpec((None, HG_HB, 128, 128), lambda g, c: (nc - 1 - c, g, 0, 0)), row],
        out_specs=[row, row, row, vec],
        out_shape=[jax.ShapeDtypeStruct((S, HG_WIDTH), BF16)] * 3 + [jax.ShapeDtypeStruct((1, HG_WIDTH), F32)],
        scratch_shapes=[pltpu.VMEM((HG_HB, 128, 128), F32)],
        compiler_params=_cp("parallel", "arbitrary"),
    )(proj, proj, proj, lb, hst, do)


def _silu_grad(x):
    s = jax.nn.sigmoid(x)
    return s * (1.0 + x * (1.0 - s))


def conv_fwd(proj, w, b, tm=512):
    S = proj.shape[0]
    tm = _fit(S, tm)
    G = 512
    off = P_XBC // G

    def body(cur_ref, prev_ref, w_ref, b_ref, act_ref, pre_ref):
        prev = prev_ref[...] * (pl.program_id(1) > 0).astype(F32)
        ext = jnp.concatenate([prev, cur_ref[...]], axis=0)
        n = tm + 8
        acc = b_ref[...] + jnp.zeros((tm, G), F32)
        for j in range(SSM_CONV):
            acc = acc + w_ref[j:j + 1, :] * pltpu.roll(ext, (n - 5 - j) % n, 0)[0:tm]
        pre_ref[...] = acc
        act_ref[...] = jax.nn.silu(acc)

    out = pl.BlockSpec((tm, G), lambda j, i: (i, j))
    return pl.pallas_call(
        body, name="conv_fwd", grid=(SSM_CONV_DIM // G, S // tm),
        in_specs=[pl.BlockSpec((tm, G), lambda j, i: (i, off + j)),
                  pl.BlockSpec((8, G), lambda j, i: (jnp.maximum(i * (tm // 8) - 1, 0), off + j)),
                  pl.BlockSpec((SSM_CONV, G), lambda j, i: (0, j)), pl.BlockSpec((1, G), lambda j, i: (0, j))],
        out_specs=[out, out], out_shape=[jax.ShapeDtypeStruct((S, SSM_CONV_DIM), F32)] * 2,
        compiler_params=_cp("parallel", "parallel"),
    )(proj, proj, w, b)


def conv_bwd(proj, pre, dact, w, tm=512):
    S = proj.shape[0]
    tm = _fit(S, tm)
    G = 512
    off = P_XBC // G
    nb = S // tm

    def body(x_ref, xp_ref, pre_ref, pren_ref, d_ref, dn_ref, w_ref, dx_ref, dw_ref, db_ref):
        i = pl.program_id(1)
        n = tm + 8
        dpre = d_ref[...] * _silu_grad(pre_ref[...])
        dpre_next = dn_ref[...] * _silu_grad(pren_ref[...]) * (i < nb - 1).astype(F32)
        dext = jnp.concatenate([dpre, dpre_next], axis=0)
        xext = jnp.concatenate([xp_ref[...] * (i > 0).astype(F32), x_ref[...]], axis=0)
        dx = jnp.zeros((tm, G), F32)
        dws = []
        for j in range(SSM_CONV):
            dx = dx + w_ref[j:j + 1, :] * pltpu.roll(dext, (n - (3 - j)) % n, 0)[0:tm]
            dws.append(jnp.sum(dpre * pltpu.roll(xext, (n - 5 - j) % n, 0)[0:tm], axis=0, keepdims=True))
        dx_ref[...] = dx.astype(dx_ref.dtype)

        @pl.when(i == 0)
        def _():
            dw_ref[...] = jnp.zeros_like(dw_ref)
            db_ref[...] = jnp.zeros_like(db_ref)

        dw_ref[...] += jnp.concatenate(dws, axis=0)
        db_ref[...] += jnp.sum(dpre, axis=0, keepdims=True)

    cur = pl.BlockSpec((tm, G), lambda j, i: (i, j))
    nxt = pl.BlockSpec((8, G), lambda j, i: (jnp.minimum((i + 1) * (tm // 8), S // 8 - 1), j))
    return pl.pallas_call(
        body, name="conv_bwd", grid=(SSM_CONV_DIM // G, nb),
        in_specs=[pl.BlockSpec((tm, G), lambda j, i: (i, off + j)),
                  pl.BlockSpec((8, G), lambda j, i: (jnp.maximum(i * (tm // 8) - 1, 0), off + j)),
                  cur, nxt, cur, nxt, pl.BlockSpec((SSM_CONV, G), lambda j, i: (0, j))],
        out_specs=[cur, pl.BlockSpec((SSM_CONV, G), lambda j, i: (0, j)), pl.BlockSpec((1, G), lambda j, i: (0, j))],
        out_shape=[jax.ShapeDtypeStruct((S, SSM_CONV_DIM), BF16), jax.ShapeDtypeStruct((SSM_CONV, SSM_CONV_DIM), F32),
                   jax.ShapeDtypeStruct((1, SSM_CONV_DIM), F32)],
        compiler_params=_cp("parallel", "arbitrary"),
    )(proj, proj, pre, pre, dact, dact, w)


def _eye_dot(a, mode):
    Q = a.shape[0] if mode == "tn" else a.shape[1]
    eye = (lax.broadcasted_iota(jnp.int32, (Q, Q), 0) == lax.broadcasted_iota(jnp.int32, (Q, Q), 1)).astype(BF16)
    hi = a.astype(BF16)
    r1 = a - hi.astype(F32)
    mid = r1.astype(BF16)
    lo = (r1 - mid.astype(F32)).astype(BF16)
    if mode == "tn":
        return sum(lax.dot_general(p, eye, _DIMS["tn"], preferred_element_type=F32) for p in (hi, mid, lo))
    return sum(lax.dot_general(eye, p, _DIMS["nt"], preferred_element_type=F32) for p in (hi, mid, lo))


@jax.custom_vjp
def _transpose_exact(a):
    return _eye_dot(a, "tn")


_transpose_exact.defvjp(lambda a: (_eye_dot(a, "tn"), None), lambda _, g: (_eye_dot(g, "nt"),))


def ssd_decay_inputs(dtr, bias, alog):
    dt = jax.nn.softplus(dtr + bias)
    acum = cumsum_rows(dt * -jnp.exp(alog))
    return dt, acum, _transpose_exact(acum)


def ssd_head(x, cb, bm, cm, dt, acum, a_s, dsk, h_prev):
    Q = x.shape[0]
    r = lax.broadcasted_iota(jnp.int32, (Q, Q), 0)
    c = lax.broadcasted_iota(jnp.int32, (Q, Q), 1)
    a_l = jnp.broadcast_to(acum, (Q, Q))
    decay = jnp.where(c <= r, jnp.exp(jnp.minimum(a_l - a_s, 0.0)), 0.0)
    xdt = x * dt
    y_diag = _dot(cb * decay, xdt, "nn")
    a_last = acum[Q - 1:Q]
    states = _dot(xdt * jnp.exp(a_last - acum), bm, "tn")
    h_new = h_prev * jnp.exp(a_last) + states
    y_off = _dot(cm, h_prev, "nt") * jnp.exp(acum)
    return y_diag + y_off + x * dsk, h_new


def _ssd_specs(S, rev):
    Q = _fit(S, SSD_Q)
    nc = S // Q
    ci = (lambda c: nc - 1 - c) if rev else (lambda c: c)
    x = pl.BlockSpec((Q, 512), lambda g, c: (ci(c), g))
    bm = pl.BlockSpec((Q, 128), lambda g, c: (ci(c), SSM_INNER // 128 + g))
    cm = pl.BlockSpec((Q, 128), lambda g, c: (ci(c), SSM_INNER // 128 + SSM_GROUPS + g))
    dtr = pl.BlockSpec((None, Q, 8), lambda g, c: (g, ci(c), 0))
    par = pl.BlockSpec((None, 1, 8), lambda g, c: (g, 0, 0))
    hs = pl.BlockSpec((None, 8, SSM_HEADDIM, SSM_STATE), lambda g, c: (ci(c), g, 0, 0))
    return Q, nc, x, bm, cm, dtr, par, hs


def ssd_scan_fwd(act, dtr, bias, alog, dsk):
    S = act.shape[0]
    Q, nc, x_s, bm_s, cm_s, dtr_s, par_s, hs_s = _ssd_specs(S, False)

    def body(x_ref, bm_ref, cm_ref, dtr_ref, b_ref, a_ref, d_ref, y_ref, hs_ref, st, dt_s, ac_s, act_s):
        @pl.when(pl.program_id(1) == 0)
        def _():
            st[...] = jnp.zeros_like(st)

        dt_s[...], ac_s[...], act_s[...] = ssd_decay_inputs(dtr_ref[...], b_ref[...], a_ref[...])
        bm, cm = bm_ref[...], cm_ref[...]
        cb = _dot(cm, bm, "nt")
        for j in range(8):
            hs_ref[j] = st[j]
            y, h_new = ssd_head(x_ref[:, j * 64:(j + 1) * 64], cb, bm, cm, dt_s[:, j:j + 1], ac_s[:, j:j + 1], act_s[j:j + 1, :],
                                d_ref[:, j:j + 1], st[j])
            y_ref[:, j * 64:(j + 1) * 64] = y
            st[j] = h_new

    return pl.pallas_call(
        body, name="ssd_scan_fwd", grid=(SSM_GROUPS, nc),
        in_specs=[x_s, bm_s, cm_s, dtr_s, par_s, par_s, par_s], out_specs=[x_s, hs_s],
        out_shape=[jax.ShapeDtypeStruct((S, SSM_INNER), F32), jax.ShapeDtypeStruct((nc, SSM_HEADS, SSM_HEADDIM, SSM_STATE), F32)],
        scratch_shapes=[pltpu.VMEM((8, SSM_HEADDIM, SSM_STATE), F32), pltpu.VMEM((Q, 8), F32), pltpu.VMEM((Q, 8), F32),
                        pltpu.VMEM((8, Q), F32)],
        compiler_params=_cp("parallel", "arbitrary"),
    )(act, act, act, dtr, bias, alog, dsk)


def ssd_scan_bwd(act, dtr, bias, alog, dsk, hs, dy):
    S = act.shape[0]
    Q, nc, x_s, bm_s, cm_s, dtr_s, par_s, hs_s = _ssd_specs(S, True)
    g_s = pl.BlockSpec((Q, 128), lambda g, c: (nc - 1 - c, g))

    def body(x_ref, bm_ref, cm_ref, dtr_ref, b_ref, a_ref, d_ref, hs_ref, dy_ref,
             dx_ref, dbm_ref, dcm_ref, ddtr_ref, db_ref, da_ref, dd_ref, dst, dt_s, ac_s, ddt_s, dac_s, act_s, dact_s):
        @pl.when(pl.program_id(1) == 0)
        def _():
            dst[...] = jnp.zeros_like(dst)
            db_ref[...] = jnp.zeros_like(db_ref)
            da_ref[...] = jnp.zeros_like(da_ref)
            dd_ref[...] = jnp.zeros_like(dd_ref)

        (dt_s[...], ac_s[...], act_s[...]), decay_vjp = jax.vjp(ssd_decay_inputs, dtr_ref[...], b_ref[...], a_ref[...])
        bm, cm = bm_ref[...], cm_ref[...]
        cb = _dot(cm, bm, "nt")
        dcb = jnp.zeros((Q, Q), F32)
        dbm = jnp.zeros((Q, 128), F32)
        dcm = jnp.zeros((Q, 128), F32)
        for j in range(8):
            cs = slice(j * 64, (j + 1) * 64)
            one = slice(j, j + 1)
            _, vjp = jax.vjp(ssd_head, x_ref[:, cs], cb, bm, cm, dt_s[:, one], ac_s[:, one], act_s[one, :], d_ref[:, one], hs_ref[j])
            dx, gcb, gb, gc, gdt, gac, gact, gdsk, gh = vjp((dy_ref[:, cs], dst[j]))
            dact_s[one, :] = gact
            dx_ref[:, cs] = dx
            dcb = dcb + gcb
            dbm = dbm + gb
            dcm = dcm + gc
            ddt_s[:, one] = gdt
            dac_s[:, one] = gac
            dd_ref[:, one] += gdsk
            dst[j] = gh
        dbm_ref[...] = dbm + _dot(dcb, cm, "tn")
        dcm_ref[...] = dcm + _dot(dcb, bm, "nn")
        ddtr, dbias, dalog = decay_vjp((ddt_s[...], dac_s[...], dact_s[...]))
        ddtr_ref[...] = ddtr
        db_ref[...] += dbias
        da_ref[...] += dalog

    return pl.pallas_call(
        body, name="ssd_scan_bwd", grid=(SSM_GROUPS, nc),
        in_specs=[x_s, bm_s, cm_s, dtr_s, par_s, par_s, par_s, hs_s, x_s],
        out_specs=[x_s, g_s, g_s, dtr_s, par_s, par_s, par_s],
        out_shape=[jax.ShapeDtypeStruct((S, SSM_INNER), F32), jax.ShapeDtypeStruct((S, 1024), F32), jax.ShapeDtypeStruct((S, 1024), F32),
                   jax.ShapeDtypeStruct((SSM_GROUPS, S, 8), F32)] + [jax.ShapeDtypeStruct((SSM_GROUPS, 1, 8), F32)] * 3,
        scratch_shapes=[pltpu.VMEM((8, SSM_HEADDIM, SSM_STATE), F32)] + [pltpu.VMEM((Q, 8), F32)] * 4 + [pltpu.VMEM((8, Q), F32)] * 2,
        compiler_params=_cp("parallel", "arbitrary"),
    )(act, act, act, dtr, bias, alog, dsk, hs, dy)


def _w_in_to_local(w):
    parts = [w[..., 0:1024], w[..., 1088:19520], w[..., 19584:25728], w[..., 1024:1088], w[..., 19520:19584],
             jnp.zeros(w.shape[:-1] + (IN_PAD - IN_DIM,), w.dtype)]
    return jnp.concatenate(parts, axis=-1)


def _w_in_from_local(g):
    return jnp.concatenate([g[..., 0:1024], g[..., 25600:25664], g[..., 1024:19456], g[..., 25664:25728], g[..., 19456:25600]], axis=-1)


def _w_uq_to_local(w):
    lead = w.shape[:-1]
    w = w.reshape(lead + (MLA_HEADS, 192))
    nope = w[..., :128].reshape(lead + (8, 256))
    rope = w[..., 128:].reshape(lead + (8, 128))
    return jnp.concatenate([nope, rope], axis=-1).reshape(lead + (3072,))


def _w_uq_from_local(g):
    lead = g.shape[:-1]
    g = g.reshape(lead + (8, 384))
    nope = g[..., :256].reshape(lead + (MLA_HEADS, 128))
    rope = g[..., 256:].reshape(lead + (MLA_HEADS, 64))
    return jnp.concatenate([nope, rope], axis=-1).reshape(lead + (3072,))


BIG = ("ffn1_wi", "ffn1_wo", "w_in", "mla_w_uq", "mla_w_ukv", "w_o_mla", "w_o_hgrn", "w_o_ssm", "w_out", "ffn2_wi", "ffn2_wo")
COL_SHARDED = ("ffn1_wi", "w_in", "mla_w_uq", "mla_w_ukv", "ffn2_wi")
FFN1, FFN2 = ("ffn1_wi", "ffn1_wo"), ("ffn2_wi", "ffn2_wo")
SMALL = ("ffn1_norm", "mix_norm", "mla_q_norm", "mla_kv_norm", "hgrn_lb_logits", "hgrn_norm", "ssm_conv_b", "ssm_a_log",
         "ssm_dt_bias", "ssm_d", "ssm_norm", "ffn2_norm")


def _carry(hosts, key, fn, *args, **kw):
    if hosts and key in hosts:
        make, take = hosts[key]
        out, side_out = fn(*args, side=make(), **kw)
        take(side_out)
        return out
    return fn(*args, **kw)


def _ffn_fwd(tag, x, norm_w, wi, wo, hosts=None):
    (h,) = rowwise(tag + "_norm", f_rmsnorm, [(x, D_MODEL, 0)], [(norm_w, D_MODEL, 0)], [(D_MODEL, D_MODEL, BF16)])
    gu = _carry(hosts, tag + "_wi", matmul, tag + "_wi", h, wi, out_dtype=BF16)
    (a,) = rowwise(tag + "_act", f_swiglu, [(gu, 512, 0), (gu, 512, D_FF // 512)], [], [(D_FF, 512, BF16)], n_groups=D_FF // 512)
    out = matmul(tag + "_wo", a, wo, alpha=0.5, res=x)
    return out, (x, h, gu, a)


def _ffn_bwd(tag, dx, dxb, saved, norm_w, wi, wo, hosts=None):
    x, h, gu, a = saved
    da = _carry(hosts, tag + "_da", matmul, tag + "_da", dxb, wo, "nt", out_dtype=BF16, alpha=0.5)
    dwo = _carry(hosts, tag + "_dwo", matmul, tag + "_dwo", a, dxb, "tn", alpha=0.5)
    dg, du = rowwise_bwd(tag + "_act_bwd", f_swiglu, [(gu, 512, 0), (gu, 512, D_FF // 512)], [], [(da, 512)],
                         [(0, D_FF, BF16), (1, D_FF, BF16)], [], n_groups=D_FF // 512)
    dgu = jnp.concatenate([dg, du], axis=1)
    dh = _carry(hosts, tag + "_dh", matmul, tag + "_dh", dgu, wi, "nt")
    dwi = _carry(hosts, tag + "_dwi", matmul, tag + "_dwi", h, dgu, "tn")
    dx_in, dxb_in, dnorm = rowwise_bwd(tag + "_norm_bwd", f_rmsnorm, [(x, D_MODEL, 0)], [(norm_w, D_MODEL, 0)], [(dh, D_MODEL)],
                                       [(0, D_MODEL, F32), (0, D_MODEL, BF16)], [0], adds={0: dx}, tm=256)
    return dx_in, dxb_in, dnorm, dwi, dwo


def _ssm_params(W):
    return [W[k].reshape(SSM_GROUPS, 1, 8) for k in ("ssm_dt_bias", "ssm_a_log", "ssm_d")]


def _dt_cols(proj):
    S = proj.shape[0]
    return proj[:, P_KPEDT + 64:P_KPEDT + 128].reshape(S, SSM_GROUPS, 8).transpose(1, 0, 2)


def _mix_fwd(x, W, lb, tabs, hosts=None):
    cosf, sinf = tabs
    (h,) = rowwise("mix_norm", f_rmsnorm, [(x, D_MODEL, 0)], [(W["mix_norm"], D_MODEL, 0)], [(D_MODEL, D_MODEL, BF16)])
    proj = _carry(hosts, "w_in", matmul, "w_in", h, W["w_in"])
    (qn,) = rowwise("q_norm", f_rmsnorm, [(proj, 512, 0)], [(W["mla_q_norm"], 512, 0)], [(512, 512, BF16)])
    (kvn,) = rowwise("kv_norm", f_rmsnorm, [(proj, 512, 1)], [(W["mla_kv_norm"], 512, 0)], [(512, 512, BF16)])
    q = matmul("w_uq", qn, W["mla_w_uq"])
    kv = matmul("w_ukv", kvn, W["mla_w_ukv"])
    qc = mla_q_prep(q, cosf, sinf)
    kc, vb = mla_k_prep(kv, proj, cosf, sinf)
    o_a, lse = _carry(hosts, "attention_fwd", attention_fwd, qc, kc, vb)
    y_a = matmul("w_o_mla", o_a, W["w_o_mla"])
    o_h, hst = hgrn_scan_fwd(proj, lb)
    (pre_b,) = rowwise("hgrn_out", f_hgrn_out, [(o_h, 128, 0), (proj, 128, P_HGATE // 128)], [(W["hgrn_norm"], 128, 0)],
                       [(HG_WIDTH, 128, BF16)], n_groups=HG_HEADS, tm=1024)
    y_b = matmul("w_o_hgrn", pre_b, W["w_o_hgrn"])
    act, pre = conv_fwd(proj, W["ssm_conv_w"], W["ssm_conv_b"])
    dtr = _dt_cols(proj)
    y_s, hs = ssd_scan_fwd(act, dtr, *_ssm_params(W))
    (pre_c,) = rowwise("ssm_out", f_ssm_out, [(y_s, 512, 0), (proj, 512, P_Z // 512)], [(W["ssm_norm"], 512, 0)],
                       [(SSM_INNER, 512, BF16)], n_groups=SSM_GROUPS)
    y_c = matmul("w_o_ssm", pre_c, W["w_o_ssm"])
    g0 = P_GATES // 512
    (merged,) = rowwise("merge", f_merge, [(proj, 512, g0), (proj, 512, g0 + 4), (proj, 512, g0 + 8), (y_a, 512, 0), (y_b, 512, 0), (y_c, 512, 0)],
                        [], [(D_MODEL, 512, BF16)], n_groups=4)
    out = matmul("w_out", merged, W["w_out"], res=x)
    return out, (x, h, proj, qn, kvn, qc, kc, vb, o_a, lse, y_a, o_h, hst, pre_b, y_b, act, pre, dtr, y_s, hs, pre_c, y_c, merged)


def _mix_bwd(dx, dxb, saved, W, lb, tabs, hosts=None):
    cosf, sinf = tabs
    (x, h, proj, qn, kvn, qc, kc, vb, o_a, lse, y_a, o_h, hst, pre_b, y_b, act, pre, dtr, y_s, hs, pre_c, y_c, merged) = saved
    S = x.shape[0]
    g = {}
    dmerged = _carry(hosts, "d_merged", matmul, "d_merged", dxb, W["w_out"], "nt")
    g["w_out"] = matmul("dw_out", merged, dxb, "tn")
    g0 = P_GATES // 512
    dga, dgb, dgc, dya, dyb, dyc = rowwise_bwd(
        "merge_bwd", f_merge, [(proj, 512, g0), (proj, 512, g0 + 4), (proj, 512, g0 + 8), (y_a, 512, 0), (y_b, 512, 0), (y_c, 512, 0)],
        [], [(dmerged, 512)], [(k, D_MODEL, BF16) for k in range(6)], [], n_groups=4)
    do_a = matmul("d_o_mla", dya, W["w_o_mla"], "nt", out_dtype=BF16)
    g["w_o_mla"] = matmul("dw_o_mla", o_a, dya, "tn")
    delta = attention_delta(o_a, do_a)
    dqc = _carry(hosts, "attention_bwd_dq", attention_bwd_dq, qc, kc, vb, do_a, lse, delta)
    dkc, dv = attention_bwd_dkv(qc, kc, vb, do_a, lse, delta)
    dq = mla_q_prep_bwd(dqc, cosf, sinf)
    dkv, dpe = mla_k_prep_bwd(dkc, dv, cosf, sinf)
    dqn = matmul("d_qn", dq, W["mla_w_uq"], "nt")
    g["mla_w_uq"] = matmul("dw_uq", qn, dq, "tn")
    dkvn = matmul("d_kvn", dkv, W["mla_w_ukv"], "nt")
    g["mla_w_ukv"] = matmul("dw_ukv", kvn, dkv, "tn")
    dq_lat, g["mla_q_norm"] = rowwise_bwd("q_norm_bwd", f_rmsnorm, [(proj, 512, 0)], [(W["mla_q_norm"], 512, 0)], [(dqn, 512)],
                                          [(0, 512, BF16)], [0])
    dkv_lat, g["mla_kv_norm"] = rowwise_bwd("kv_norm_bwd", f_rmsnorm, [(proj, 512, 1)], [(W["mla_kv_norm"], 512, 0)], [(dkvn, 512)],
                                            [(0, 512, BF16)], [0])
    do_b = matmul("d_o_hgrn", dyb, W["w_o_hgrn"], "nt")
    g["w_o_hgrn"] = matmul("dw_o_hgrn", pre_b, dyb, "tn")
    do_h, dhgate, g["hgrn_norm"] = rowwise_bwd(
        "hgrn_out_bwd", f_hgrn_out, [(o_h, 128, 0), (proj, 128, P_HGATE // 128)], [(W["hgrn_norm"], 128, 0)], [(do_b, 128)],
        [(0, HG_WIDTH, F32), (1, HG_WIDTH, BF16)], [0], n_groups=HG_HEADS, tm=1024)
    dhq, dhf, dhi, dlb = hgrn_scan_bwd(proj, lb, hst, do_h)
    do_c = matmul("d_o_ssm", dyc, W["w_o_ssm"], "nt")
    g["w_o_ssm"] = matmul("dw_o_ssm", pre_c, dyc, "tn")
    dy_s, dz, g["ssm_norm"] = rowwise_bwd(
        "ssm_out_bwd", f_ssm_out, [(y_s, 512, 0), (proj, 512, P_Z // 512)], [(W["ssm_norm"], 512, 0)], [(do_c, 512)],
        [(0, SSM_INNER, F32), (1, SSM_INNER, BF16)], [0], n_groups=SSM_GROUPS)
    dxs, dbm, dcm, ddtr, dbias, dalog, ddsk = ssd_scan_bwd(act, dtr, *_ssm_params(W), hs, dy_s)
    g["ssm_dt_bias"], g["ssm_a_log"], g["ssm_d"] = (v.reshape(1, SSM_HEADS) for v in (dbias, dalog, ddsk))
    dxbc, g["ssm_conv_w"], g["ssm_conv_b"] = conv_bwd(proj, pre, jnp.concatenate([dxs, dbm, dcm], axis=1), W["ssm_conv_w"])
    ddt = ddtr.transpose(1, 0, 2).reshape(S, SSM_HEADS)
    dproj = jnp.concatenate([dq_lat, dkv_lat, dhq, dhf, dhi, dhgate, dz, dxbc, dga, dgb, dgc, dpe[:, :64].astype(BF16),
                             ddt.astype(BF16), jnp.zeros((S, IN_PAD - IN_DIM), BF16)], axis=1)
    dh = _carry(hosts, "d_h_mix", matmul, "d_h_mix", dproj, W["w_in"], "nt")
    g["w_in"] = _carry(hosts, "dw_in", matmul, "dw_in", h, dproj, "tn")
    dx_in, dxb_in, g["mix_norm"] = rowwise_bwd("mix_norm_bwd", f_rmsnorm, [(x, D_MODEL, 0)], [(W["mix_norm"], D_MODEL, 0)], [(dh, D_MODEL)],
                                               [(0, D_MODEL, F32), (0, D_MODEL, BF16)], [0], adds={0: dx}, tm=256)
    return dx_in, dxb_in, g, dlb


def local_step(x, target, weights, small, final_norm, fwd_hosts=None, bwd_hosts=None):
    S = x.shape[0]
    tabs = rope_tables(S)
    (lbs,) = rowwise("lower_bounds", f_lower_bounds, [(small["hgrn_lb_logits"], HG_WIDTH, 0)], [], [(HG_WIDTH, HG_WIDTH, F32)])

    class LayerWeights(dict):
        def __init__(self, l):
            super().__init__({k: small[k][l:l + 1] for k in SMALL})
            self.layer = l

        def __missing__(self, name):
            self[name] = weights(self.layer, name)
            return self[name]

    saved = []
    for l in range(DEPTH):
        W = LayerWeights(l)
        lb = lbs[l:l + 1]
        hosts = fwd_hosts(l) if fwd_hosts else None
        x, s1 = _ffn_fwd("ffn1", x, W["ffn1_norm"], W["ffn1_wi"], W["ffn1_wo"], hosts)
        x, s2 = _mix_fwd(x, W, lb, tabs, hosts)
        x, s3 = _ffn_fwd("ffn2", x, W["ffn2_norm"], W["ffn2_wi"], W["ffn2_wo"])
        saved.append((W, lb, s1, s2, s3))
    dx, dxb, dfinal, loss = loss_head(x, target, final_norm.reshape(1, D_MODEL))
    grads = [None] * DEPTH
    dlbs = [None] * DEPTH
    for l in reversed(range(DEPTH)):
        W, lb, s1, s2, s3 = saved[l]
        g = {}
        hosts = bwd_hosts(l, grads, g) if bwd_hosts else None
        dx, dxb, g["ffn2_norm"], g["ffn2_wi"], g["ffn2_wo"] = _ffn_bwd("ffn2", dx, dxb, s3, W["ffn2_norm"], W["ffn2_wi"], W["ffn2_wo"], hosts)
        dx, dxb, gm, dlbs[l] = _mix_bwd(dx, dxb, s2, W, lb, tabs, hosts)
        g.update(gm)
        dx, dxb, g["ffn1_norm"], g["ffn1_wi"], g["ffn1_wo"] = _ffn_bwd("ffn1", dx, dxb, s1, W["ffn1_norm"], W["ffn1_wi"], W["ffn1_wo"], hosts)
        grads[l] = g
    (dlogits,) = rowwise_bwd("lower_bounds_bwd", f_lower_bounds, [(small["hgrn_lb_logits"], HG_WIDTH, 0)], [],
                             [(jnp.concatenate(dlbs, axis=0), HG_WIDTH)], [(0, HG_WIDTH, F32)], [])
    return loss, dx, grads, dlogits, dfinal


ANY = pl.BlockSpec(memory_space=pl.ANY)


def _coords():
    return lax.axis_index("x"), lax.axis_index("y"), lax.axis_index("c")


def _exchange_call(name, body, src, out_shape, n_copies):
    return pl.pallas_call(
        body, name=name, in_specs=[ANY], out_specs=ANY, out_shape=out_shape,
        scratch_shapes=[pltpu.SemaphoreType.DMA((n_copies,)), pltpu.SemaphoreType.DMA((n_copies,)), pltpu.SemaphoreType.DMA],
    )(src)


class Exchange(NamedTuple):
    srcs: list
    out_shapes: list
    sem_counts: tuple
    start: Callable
    wait: Callable


def run_exchange(name, ex):
    n = len(ex.srcs)

    def body(*refs):
        src, out, sems = refs[:n], refs[n:n + len(ex.out_shapes)], refs[n + len(ex.out_shapes):]
        ex.start(src, out, sems)
        ex.wait(src, out, sems)

    return pl.pallas_call(
        body, name=name, in_specs=[ANY] * n, out_specs=[ANY] * len(ex.out_shapes), out_shape=ex.out_shapes,
        scratch_shapes=[pltpu.SemaphoreType.DMA((k,)) for k in ex.sem_counts],
    )(*ex.srcs)


def _pcall(name, body, grid, in_specs, out_specs, out_shape, scratch_shapes, sem, args, side=None):
    if side is None:
        return pl.pallas_call(body, name=name, grid=grid, in_specs=in_specs, out_specs=out_specs, out_shape=out_shape,
                              scratch_shapes=scratch_shapes, compiler_params=_cp(*sem))(*args)
    single = not isinstance(out_shape, (list, tuple))
    out_specs_l = [out_specs] if single else list(out_specs)
    out_shape_l = [out_shape] if single else list(out_shape)
    n_in, n_out, n_scr, n_s, n_so = len(in_specs), len(out_specs_l), len(scratch_shapes), len(side.srcs), len(side.out_shapes)

    def hosted(*refs):
        ins, s_in = refs[:n_in], refs[n_in:n_in + n_s]
        o0 = n_in + n_s
        outs, s_out = refs[o0:o0 + n_out], refs[o0 + n_out:o0 + n_out + n_so]
        scr, sems = refs[o0 + n_out + n_so:o0 + n_out + n_so + n_scr], refs[o0 + n_out + n_so + n_scr:]
        ids = [pl.program_id(d) for d in range(len(grid))]
        first = functools.reduce(jnp.logical_and, [i == 0 for i in ids])
        last = functools.reduce(jnp.logical_and, [i == g - 1 for i, g in zip(ids, grid)])

        @pl.when(first)
        def _():
            side.start(s_in, s_out, sems)

        body(*ins, *outs, *scr)

        @pl.when(last)
        def _():
            side.wait(s_in, s_out, sems)

    res = pl.pallas_call(
        hosted, name=name, grid=grid, in_specs=list(in_specs) + [ANY] * n_s, out_specs=out_specs_l + [ANY] * n_so,
        out_shape=out_shape_l + list(side.out_shapes),
        scratch_shapes=list(scratch_shapes) + [pltpu.SemaphoreType.DMA((k,)) for k in side.sem_counts],
        compiler_params=_cp(*["arbitrary"] * len(grid)),
    )(*args, *side.srcs)
    return (res[0] if single else res[:n_out]), res[n_out:]


def _chip_peers(x, y):
    return [(1 - x, y), (x, 1 - y), (1 - x, 1 - y)]


def weights_fetch(shards, layer):
    n = len(shards)
    half = [s.shape[1] // 2 for s in shards]

    def copies(src, out, sems):
        send_sems, recv_sems, fsend_sems, frecv_sems = sems
        x, y, c = _coords()
        peers = _chip_peers(x, y)

        def ici(i, k, chip):
            rows = pl.ds(c * half[i], half[i])
            return pltpu.make_async_remote_copy(src[i].at[layer, rows], out[i].at[chip, rows], send_sems.at[3 * i + k],
                                                recv_sems.at[3 * i + k], device_id=(*peers[k], c), device_id_type=MESH)

        def fwd(i, k, chip, h):
            rows = pl.ds(h * half[i], half[i])
            return pltpu.make_async_remote_copy(out[i].at[chip, rows], out[i].at[chip, rows], fsend_sems.at[3 * i + k],
                                                frecv_sems.at[3 * i + k], device_id=(x, y, 1 - c), device_id_type=MESH)

        return 2 * x + y, c, peers, ici, fwd

    def start(src, out, sems):
        me, c, peers, ici, fwd = copies(src, out, sems)
        for k in range(3):
            for i in range(n):
                ici(i, k, me).start()

    def wait(src, out, sems):
        me, c, peers, ici, fwd = copies(src, out, sems)
        passed = []
        for k, (px, py) in enumerate(peers):
            for i in range(n):
                ici(i, k, 2 * px + py).wait_recv()
                passed.append(fwd(i, k, 2 * px + py, c))
                passed[-1].start()
        for k, (px, py) in enumerate(peers):
            for i in range(n):
                fwd(i, k, 2 * px + py, 1 - c).wait_recv()
        for k in range(3):
            for i in range(n):
                ici(i, k, me).wait_send()
        for cp in passed:
            cp.wait_send()

    out_shapes = [jax.ShapeDtypeStruct((4,) + s.shape[1:], s.dtype) for s in shards]
    return Exchange(shards, out_shapes, (3 * n, 3 * n, 3 * n, 3 * n), start, wait)


def _pairwise(srcs, out_shapes, n_sems, make):
    def start(src, out, sems):
        for cp in make(src, out, *sems):
            cp.start()

    def wait(src, out, sems):
        cps = make(src, out, *sems)
        for cp in cps:
            cp.wait_recv()
        for cp in cps:
            cp.wait_send()

    return Exchange(srcs, out_shapes, (n_sems, n_sems), start, wait)


def grad_half_exchange(gs):
    n = len(gs)
    half = [g.shape[1] // 2 for g in gs]

    def make(src, out, send_sems, recv_sems):
        x, y, c = _coords()
        return [pltpu.make_async_remote_copy(src[i].at[:, pl.ds((1 - c) * half[i], half[i])], out[i], send_sems.at[i], recv_sems.at[i],
                                             device_id=(x, y, 1 - c), device_id_type=MESH) for i in range(n)]

    return _pairwise(gs, [jax.ShapeDtypeStruct((4, h) + g.shape[2:], g.dtype) for g, h in zip(gs, half)], n, make)


def grad_chip_exchange(ps):
    n = len(ps)

    def make(src, out, send_sems, recv_sems):
        x, y, c = _coords()
        return [pltpu.make_async_remote_copy(src[i].at[2 * px + py], out[i].at[k], send_sems.at[3 * i + k], recv_sems.at[3 * i + k],
                                             device_id=(px, py, c), device_id_type=MESH)
                for k, (px, py) in enumerate(_chip_peers(x, y)) for i in range(n)]

    return _pairwise(ps, [jax.ShapeDtypeStruct((3,) + p.shape[1:], p.dtype) for p in ps], 3 * n, make)


def grad_reduced_exchange(rs):
    n = len(rs)

    def make(src, out, send_sems, recv_sems):
        x, y, c = _coords()
        return [pltpu.make_async_remote_copy(src[i], out[i], send_sems.at[i], recv_sems.at[i],
                                             device_id=(x, y, 1 - c), device_id_type=MESH) for i in range(n)]

    return _pairwise(rs, [jax.ShapeDtypeStruct(r.shape, r.dtype) for r in rs], n, make)


def device_allgather(name, src):
    def body(src_ref, out_ref, send_sems, recv_sems, local_sem):
        x, y, c = _coords()
        me = 4 * x + 2 * y + c
        peers = [(x ^ (m >> 2), y ^ ((m >> 1) & 1), c ^ (m & 1)) for m in range(1, 8)]

        def copy(k, dev):
            return pltpu.make_async_remote_copy(src_ref, out_ref.at[dev], send_sems.at[k], recv_sems.at[k],
                                                device_id=peers[k], device_id_type=MESH)

        local = pltpu.make_async_copy(src_ref, out_ref.at[me], local_sem)
        local.start()
        sends = [copy(k, me) for k in range(7)]
        for s in sends:
            s.start()
        for k, (px, py, pc) in enumerate(peers):
            copy(k, 4 * px + 2 * py + pc).wait_recv()
        for s in sends:
            s.wait_send()
        local.wait()

    return _exchange_call(name, body, src, jax.ShapeDtypeStruct((8,) + src.shape, src.dtype), 7)


BLOCK_ELEMS = 1 << 19


def add_pair(name, g, recv, c):
    n, R, C = recv.shape
    tr = _fit_rows(R, max(16, BLOCK_ELEMS // C))
    nb = R // tr

    def body(c_ref, g_ref, r_ref, o_ref):
        o_ref[...] = (g_ref[...].astype(F32) + r_ref[...].astype(F32)).astype(o_ref.dtype)

    blk = pl.BlockSpec((None, tr, C), lambda j, i, c_ref: (j, i, 0))
    return pl.pallas_call(
        body, name=name, out_shape=jax.ShapeDtypeStruct((n, R, C), g.dtype),
        grid_spec=pltpu.PrefetchScalarGridSpec(
            num_scalar_prefetch=1, grid=(n, nb),
            in_specs=[pl.BlockSpec((None, tr, C), lambda j, i, c_ref: (j, c_ref[0] * nb + i, 0)), blk], out_specs=blk),
        compiler_params=_cp("parallel", "parallel"),
    )(c, g, recv)


def sum_chips(name, own, recv, me):
    _, R, C = own.shape
    tr = _fit_rows(R, max(16, BLOCK_ELEMS // (2 * C)))

    def body(me_ref, o_ref, r_ref, out_ref):
        acc = o_ref[...].astype(F32)
        for k in range(3):
            acc = acc + r_ref[k].astype(F32)
        out_ref[...] = acc

    return pl.pallas_call(
        body, name=name, out_shape=jax.ShapeDtypeStruct((R, C), F32),
        grid_spec=pltpu.PrefetchScalarGridSpec(
            num_scalar_prefetch=1, grid=(R // tr,),
            in_specs=[pl.BlockSpec((None, tr, C), lambda i, me_ref: (me_ref[0], i, 0)), pl.BlockSpec((3, tr, C), lambda i, me_ref: (0, i, 0))],
            out_specs=pl.BlockSpec((tr, C), lambda i, me_ref: (i, 0))),
        compiler_params=_cp("parallel"),
    )(me, own, recv)


def sum_devices(parts):
    _, R, C = parts.shape

    def body(p_ref, o_ref):
        acc = p_ref[0]
        for k in range(1, 8):
            acc = acc + p_ref[k]
        o_ref[...] = acc

    return pl.pallas_call(body, name="sum_devices", out_shape=jax.ShapeDtypeStruct((R, C), F32))(parts)


def _fit_rows(R, pref):
    for t in range(min(pref, R), 0, -1):
        if R % t == 0 and (t % 16 == 0 or t == R):
            return t
    raise ValueError((R, pref))


def adamw(name, w, g, m, v):
    shape = w.shape
    C = shape[-1]
    w2, g2, m2, v2 = (a.reshape(-1, C) for a in (w, g, m, v))
    R = w2.shape[0]
    tr = _fit_rows(R, max(8, (1 << 18) // C)) if R * C > (1 << 18) else R
    c1 = 1.0 - ADAM_B1 ** ADAM_STEP
    c2 = 1.0 - ADAM_B2 ** ADAM_STEP

    def body(w_ref, g_ref, m_ref, v_ref, d_ref, mo_ref, vo_ref):
        gg = g_ref[...]
        mn = ADAM_B1 * m_ref[...] + (1.0 - ADAM_B1) * gg
        vn = ADAM_B2 * v_ref[...] + (1.0 - ADAM_B2) * jnp.square(gg)
        d_ref[...] = -ADAM_LR * ((mn / c1) / (jnp.sqrt(vn / c2) + ADAM_EPS) + ADAM_WD * w_ref[...])
        mo_ref[...] = mn
        vo_ref[...] = vn

    blk = pl.BlockSpec((tr, C), lambda i: (i, 0))
    outs = pl.pallas_call(
        body, name=name, grid=(R // tr,), in_specs=[blk] * 4, out_specs=[blk] * 3,
        out_shape=[jax.ShapeDtypeStruct((R, C), F32)] * 3, compiler_params=_cp("parallel"),
    )(w2, g2, m2, v2)
    return tuple(o.reshape(shape) for o in outs)


def _unshard(name, parts):
    n, L, r, c = parts.shape
    if name in COL_SHARDED:
        return parts.transpose(1, 2, 0, 3).reshape(L, r, n * c)
    return parts.transpose(1, 0, 2, 3).reshape(L, n * r, c)


def _shard(name, full):
    L, R, C = full.shape
    if name in COL_SHARDED:
        return full.reshape(L, R, 4, C // 4).transpose(0, 2, 1, 3)
    return full.reshape(L, 4, R // 4, C)


def _to_local(name, w):
    if name == "w_in":
        return _w_in_to_local(w)
    if name == "mla_w_uq":
        return _w_uq_to_local(w)
    return w


def _from_local(name, g):
    if name == "w_in":
        return _w_in_from_local(g)
    if name == "mla_w_uq":
        return _w_uq_from_local(g)
    return g


TWIN_WEIGHTS = ("ffn1_norm", "ffn1_wi", "ffn1_wo", "mix_norm", "w_in", "mla_q_norm", "mla_w_uq", "mla_kv_norm", "mla_w_ukv",
                "hgrn_lb_logits", "hgrn_norm", "ssm_conv_w", "ssm_conv_b", "ssm_a_log", "ssm_dt_bias", "ssm_d", "ssm_norm",
                "w_o_mla", "w_o_hgrn", "w_o_ssm", "w_out", "ffn2_norm", "ffn2_wi", "ffn2_wo", "final_norm")
SMALL_PACK = SMALL + ("ssm_conv_w", "final_norm")


def _pad_rows(flat, cols):
    n = flat.shape[0]
    rows = -(-n // cols)
    rows = -(-rows // 8) * 8
    return jnp.concatenate([flat, jnp.zeros((rows * cols - n,), flat.dtype)]).reshape(rows, cols)


def kernel(x, ffn1_norm, ffn1_wi, ffn1_wo, mix_norm, w_in, mla_q_norm, mla_w_uq, mla_kv_norm, mla_w_ukv, hgrn_lb_logits, hgrn_norm, ssm_conv_w, ssm_conv_b, ssm_a_log, ssm_dt_bias, ssm_d, ssm_norm, w_o_mla, w_o_hgrn, w_o_ssm, w_out, ffn2_norm, ffn2_wi, ffn2_wo, final_norm, loss_target, m_ffn1_norm, m_ffn1_wi, m_ffn1_wo, m_mix_norm, m_w_in, m_mla_q_norm, m_mla_w_uq, m_mla_kv_norm, m_mla_w_ukv, m_hgrn_lb_logits, m_hgrn_norm, m_ssm_conv_w, m_ssm_conv_b, m_ssm_a_log, m_ssm_dt_bias, m_ssm_d, m_ssm_norm, m_w_o_mla, m_w_o_hgrn, m_w_o_ssm, m_w_out, m_ffn2_norm, m_ffn2_wi, m_ffn2_wo, m_final_norm, v_ffn1_norm, v_ffn1_wi, v_ffn1_wo, v_mix_norm, v_w_in, v_mla_q_norm, v_mla_w_uq, v_mla_kv_norm, v_mla_w_ukv, v_hgrn_lb_logits, v_hgrn_norm, v_ssm_conv_w, v_ssm_conv_b, v_ssm_a_log, v_ssm_dt_bias, v_ssm_d, v_ssm_norm, v_w_o_mla, v_w_o_hgrn, v_w_o_ssm, v_w_out, v_ffn2_norm, v_ffn2_wi, v_ffn2_wo, v_final_norm):
    args = dict(locals())
    w = {n: args[n] for n in TWIN_WEIGHTS}
    m = {n: args["m_" + n] for n in TWIN_WEIGHTS}
    v = {n: args["v_" + n] for n in TWIN_WEIGHTS}
    xi, yi, ci = _coords()
    chip = 2 * xi + yi

    shards = {n: w[n].astype(BF16) for n in BIG}
    conv_parts = device_allgather("conv_allgather", _pad_rows(w["ssm_conv_w"].reshape(-1), 128))
    conv_full = jnp.concatenate(
        [conv_parts[2 * j].reshape(-1)[:w["ssm_conv_w"].size].reshape(w["ssm_conv_w"].shape) for j in range(4)], axis=-1)
    fetched = {}

    def fetch(names, l):
        return (lambda: weights_fetch([shards[n] for n in names], l),
                lambda out: fetched.update({(l, n): parts for n, parts in zip(names, out)}))

    def weight(l, name):
        if name == "ssm_conv_w":
            return conv_full[l]
        parts = lax.dynamic_update_slice(fetched[l, name], shards[name][l][None], (chip, 0, 0))
        return _to_local(name, _unshard(name, parts[:, None])[0])

    make, take = fetch(FFN1, 0)
    take(run_exchange("weights_fetch", make()))

    def fwd_hosts(l):
        if l != 0:
            return None
        return {"ffn1_wi": fetch(("w_in",), 0), "w_in": fetch(tuple(n for n in BIG if n not in FFN1 + ("w_in",)), 0),
                "attention_fwd": fetch(BIG, 1)}

    c_idx, chip_idx = ci.astype(jnp.int32).reshape(1), chip.astype(jnp.int32).reshape(1)
    reduced = {}

    def reduction_stages(l, names, layer_grads):
        st = {}
        tag = "_l%d_" % l

        def half_exchange():
            st["gs"] = [_shard(n, _from_local(n, layer_grads[n][None]))[0].astype(BF16) for n in names]
            return grad_half_exchange(st["gs"])

        def chip_exchange():
            st["pair"] = [add_pair("add_pair" + tag + n, a, b, c_idx) for n, a, b in zip(names, st["gs"], st["recv"])]
            return grad_chip_exchange(st["pair"])

        def reduced_exchange():
            st["red"] = [sum_chips("sum_chips" + tag + n, a, b, chip_idx) for n, a, b in zip(names, st["pair"], st["from_chips"])]
            return grad_reduced_exchange(st["red"])

        def finish(others):
            for n, mine, other in zip(names, st["red"], others):
                reduced[l, n] = jnp.where(ci == 0, jnp.concatenate([mine, other]), jnp.concatenate([other, mine]))

        return [(half_exchange, lambda out: st.update(recv=out)), (chip_exchange, lambda out: st.update(from_chips=out)),
                (reduced_exchange, finish)]

    middle = tuple(n for n in BIG if n not in FFN1 + FFN2)

    def bwd_hosts(l, earlier, current):
        if l != 0:
            return None
        hosts = dict(zip(("ffn2_dh", "attention_bwd_dq", "d_h_mix"), reduction_stages(1, BIG, earlier[1])))
        hosts.update(zip(("d_merged", "dw_in", "ffn1_dwo"), reduction_stages(0, FFN2, current)))
        hosts.update(zip(("ffn1_da", "ffn1_dh", "ffn1_dwi"), reduction_stages(0, middle, current)))
        return hosts

    small = {n: w[n] for n in SMALL}
    loss, grad_x, grads, dlogits, dfinal = local_step(x[0], loss_target[0], weight, small, w["final_norm"], fwd_hosts, bwd_hosts)
    loss = lax.psum(loss[0, 0], ("x", "y", "c"))
    for name, (make, take) in zip(("grad_half_exchange", "grad_chip_exchange", "grad_reduced_exchange"), reduction_stages(0, FFN1, grads[0])):
        take(run_exchange(name, make()))
    g = {n: jnp.stack([reduced[l, n] for l in range(DEPTH)]) for n in BIG}

    sg = {n: jnp.concatenate([grads[l][n] for l in range(DEPTH)], axis=0) for n in SMALL if n != "hgrn_lb_logits"}
    sg["hgrn_lb_logits"] = dlogits
    sg["ssm_conv_w"] = jnp.stack([grads[l]["ssm_conv_w"] for l in range(DEPTH)])
    sg["final_norm"] = dfinal
    spack = _pad_rows(jnp.concatenate([sg[n].reshape(-1) for n in SMALL_PACK]), 128)
    ssum = sum_devices(device_allgather("small_grads_allgather", spack)).reshape(-1)
    off = 0
    for n in SMALL_PACK:
        size = sg[n].size
        g[n] = ssum[off:off + size].reshape(sg[n].shape)
        off += size
    shard_cols = w["ssm_conv_w"].shape[-1]
    g["ssm_conv_w"] = lax.dynamic_slice_in_dim(g["ssm_conv_w"], chip * shard_cols, shard_cols, axis=2)
    g = {n: g[n].reshape(w[n].shape) for n in TWIN_WEIGHTS}

    upd = {n: adamw("adamw_" + n, w[n], g[n], m[n], v[n]) for n in TWIN_WEIGHTS}
    return (loss, grad_x[None], *[g[n] for n in TWIN_WEIGHTS], *[upd[n][0] for n in TWIN_WEIGHTS],
            *[upd[n][1] for n in TWIN_WEIGHTS], *[upd[n][2] for n in TWIN_WEIGHTS])
```
